```python
import math
import jax, jax.numpy as jnp
from jax import lax
import numpy as np

D_MODEL = 1024
BATCH = 8
SEQ = 2048
DEPTH = 1
DEC_BATCH = 128
DEC_SEQ = 1
PAST_LEN = 16384
PAGE_SIZE = 128

D_CONV = D_MODEL
CONV_W = 3
D_SSM = D_MODEL
SSM_H = 16
SSM_GROUPS = D_SSM // SSM_H
STATE_N = 64
DT_MIN = 0.001
DT_MAX = 0.1
N_EXPERTS = 64
TOP_K = 8
N_ROUTE_GROUPS = 8
TOPK_GROUPS = 4
D_EXPERT = 256
D_SHARED = 256
ROUTED_SCALE = 2.5
LN_EPS = 1e-5
ALPHA = (2.0 * DEPTH) ** 0.25
BETA = (8.0 * DEPTH) ** -0.25
D_IN = 3 * D_CONV + D_SSM + 2 * D_MODEL

kernel_name = "hybrid_shortconv_s5_moe_decode_step"

F32 = jnp.float32


def layer_norm(x, g, b):
    xf = x.astype(F32)
    mu = jnp.mean(xf, axis=-1, keepdims=True)
    var = jnp.mean(jnp.square(xf - mu), axis=-1, keepdims=True)
    y = (xf - mu) * lax.rsqrt(var + LN_EPS) * g.astype(F32) + b.astype(F32)
    return y.astype(x.dtype)


def short_conv(u, prev, w):
    L = u.shape[1]
    full = jnp.concatenate([prev.astype(u.dtype), u], axis=1)
    y = full[:, 0:L] * w[0]
    for k in range(1, CONV_W):
        y = y + full[:, k:k + L] * w[k]
    return y, full[:, L:]


def ssm_discretize(a_re, a_im, log_dt):
    dt = jnp.exp(log_dt.astype(F32))[:, None]
    ar = a_re.astype(F32)
    ai = a_im.astype(F32)
    mag = jnp.exp(ar * dt)
    lr = mag * jnp.cos(ai * dt)
    li = mag * jnp.sin(ai * dt)
    den = ar * ar + ai * ai
    fr = ((lr - 1.0) * ar + li * ai) / den
    fi = (li * ar - (lr - 1.0) * ai) / den
    return lr, li, fr, fi


def ssm_combine(e1, e2):
    a1r, a1i, b1r, b1i = e1
    a2r, a2i, b2r, b2i = e2
    return (a2r * a1r - a2i * a1i,
            a2r * a1i + a2i * a1r,
            a2r * b1r - a2i * b1i + b2r,
            a2r * b1i + a2i * b1r + b2i)


def ssm_branch(u, h0r, h0i, a_re, a_im, log_dt, b_re, b_im, c_re, c_im, d_skip):
    bsz, L, _ = u.shape
    uf = u.astype(F32)
    ug = uf.reshape(bsz, L, SSM_GROUPS, SSM_H)
    lr, li, fr, fi = ssm_discretize(a_re, a_im, log_dt)
    br = b_re.astype(F32)
    bi = b_im.astype(F32)
    bbr = fr[..., None] * br - fi[..., None] * bi
    bbi = fr[..., None] * bi + fi[..., None] * br
    xr = jnp.einsum('blgh,gnh->blgn', ug, bbr)
    xi = jnp.einsum('blgh,gnh->blgn', ug, bbi)
    h0r = h0r.astype(F32)
    h0i = h0i.astype(F32)
    xr = xr.at[:, 0].add(lr * h0r - li * h0i)
    xi = xi.at[:, 0].add(lr * h0i + li * h0r)
    ar = jnp.broadcast_to(lr, (1, L) + lr.shape)
    ai = jnp.broadcast_to(li, (1, L) + li.shape)
    _, _, hr, hi = lax.associative_scan(ssm_combine, (ar, ai, xr, xi), axis=1)
    y = (jnp.einsum('blgn,ghn->blgh', hr, c_re.astype(F32))
         - jnp.einsum('blgn,ghn->blgh', hi, c_im.astype(F32)))
    y = y.reshape(bsz, L, D_SSM) + d_skip.astype(F32) * uf
    return y, hr[:, -1], hi[:, -1]


def moe_ffn(x, w_router, router_bias, w_gate, w_up, w_down, ws_gate, ws_up, ws_down):
    bsz, L, D = x.shape
    t = x.reshape(-1, D)
    scores = jax.nn.sigmoid(t.astype(F32) @ w_router.astype(F32))
    biased = scores + router_bias.astype(F32)
    grp = biased.reshape(-1, N_ROUTE_GROUPS, N_EXPERTS // N_ROUTE_GROUPS)
    grp_score = jnp.sum(lax.top_k(grp, 2)[0], axis=-1)
    _, gidx = lax.top_k(grp_score, TOPK_GROUPS)
    gmask = jnp.sum(jax.nn.one_hot(gidx, N_ROUTE_GROUPS, dtype=F32), axis=-2)
    emask = jnp.repeat(gmask, N_EXPERTS // N_ROUTE_GROUPS, axis=-1)
    _, eidx = lax.top_k(jnp.where(emask > 0, biased, -jnp.inf), TOP_K)
    w = jnp.take_along_axis(scores, eidx, axis=-1)
    w = w / jnp.sum(w, axis=-1, keepdims=True) * ROUTED_SCALE
    comb = jnp.sum(jax.nn.one_hot(eidx, N_EXPERTS, dtype=F32) * w[..., None], axis=-2)
    h = jax.nn.silu(jnp.einsum('td,edf->tef', t, w_gate)) * jnp.einsum('td,edf->tef', t, w_up)
    routed = jnp.einsum('tef,efd->td', h * comb[..., None].astype(h.dtype), w_down)
    shared = (jax.nn.silu(t @ ws_gate) * (t @ ws_up)) @ ws_down
    return (routed + shared).reshape(bsz, L, D)


def hybrid_layer(x, conv_prev, h0r, h0i, p):
    split_at = [D_CONV, 2 * D_CONV, 3 * D_CONV, 3 * D_CONV + D_SSM, 3 * D_CONV + D_SSM + D_MODEL]
    proj = x @ p['w_in'] + p['b_in']
    bg, cg, xc, us, ga, gb = jnp.split(proj, split_at, axis=-1)
    conv_out, conv_new = short_conv(cg * xc, conv_prev, p['conv_w'])
    ya = (bg * conv_out) @ p['w_conv_out']
    ys, hr, hi = ssm_branch(us, h0r, h0i, p['a_re'], p['a_im'], p['log_dt'],
                            p['ssm_b_re'], p['ssm_b_im'], p['ssm_c_re'], p['ssm_c_im'], p['ssm_d'])
    z = jax.nn.gelu(ys).astype(x.dtype)
    glu = z * jax.nn.sigmoid(z @ p['w_glu'] + p['b_glu'])
    yb = glu @ p['w_ssm_out']
    m = jax.nn.sigmoid(ga) * ya + jax.nn.sigmoid(gb) * yb
    x = layer_norm(ALPHA * x + m @ p['w_o'], p['ln1_g'], p['ln1_b'])
    f = moe_ffn(x, p['w_router'], p['router_bias'], p['w_gate'], p['w_up'], p['w_down'],
                p['ws_gate'], p['ws_up'], p['ws_down'])
    x = layer_norm(ALPHA * x + f, p['ln2_g'], p['ln2_b'])
    return x, conv_new, hr, hi


def run_trunk(x, conv_st, re_st, im_st, params):
    convs, res, ims = [], [], []
    for l in range(DEPTH):
        p = {k: v[l] for k, v in params.items()}
        x, c, r, i = hybrid_layer(x, conv_st[l], re_st[l], im_st[l], p)
        convs.append(c)
        res.append(r)
        ims.append(i)
    return x, jnp.stack(convs), jnp.stack(res), jnp.stack(ims)


def setup_inputs(seed: int = 0) -> dict:
    key = jax.random.key(seed)
    ks = jax.random.split(key, 33)
    nrm = lambda k, s, sc: jax.random.normal(k, s, F32) * sc
    L_ = DEPTH
    n_idx = jnp.arange(STATE_N, dtype=F32)
    return {
        'x_prompt': nrm(ks[0], (BATCH, SEQ, D_MODEL), 1.0),
        'x_sample': nrm(ks[1], (DEC_BATCH, DEC_SEQ, D_MODEL), 1.0),
        'state_conv': nrm(ks[2], (L_, DEC_BATCH, CONV_W - 1, D_CONV), 0.5),
        'state_ssm_re': nrm(ks[3], (L_, DEC_BATCH, SSM_GROUPS, STATE_N), 0.5),
        'state_ssm_im': nrm(ks[4], (L_, DEC_BATCH, SSM_GROUPS, STATE_N), 0.5),
        'w_in': nrm(ks[5], (L_, D_MODEL, D_IN), D_MODEL ** -0.5),
        'b_in': nrm(ks[6], (L_, D_IN), 0.02),
        'conv_w': nrm(ks[7], (L_, CONV_W, D_CONV), CONV_W ** -0.5),
        'w_conv_out': nrm(ks[8], (L_, D_CONV, D_MODEL), D_CONV ** -0.5),
        'a_re': -0.5 * jnp.exp(nrm(ks[9], (L_, SSM_GROUPS, STATE_N), 0.01)),
        'a_im': math.pi * n_idx + nrm(ks[10], (L_, SSM_GROUPS, STATE_N), 0.01),
        'log_dt': jax.random.uniform(ks[11], (L_, SSM_GROUPS), F32, math.log(DT_MIN), math.log(DT_MAX)),
        'ssm_b_re': nrm(ks[12], (L_, SSM_GROUPS, STATE_N, SSM_H), (2.0 * SSM_H) ** -0.5),
        'ssm_b_im': nrm(ks[13], (L_, SSM_GROUPS, STATE_N, SSM_H), (2.0 * SSM_H) ** -0.5),
        'ssm_c_re': nrm(ks[14], (L_, SSM_GROUPS, SSM_H, STATE_N), STATE_N ** -0.5),
        'ssm_c_im': nrm(ks[15], (L_, SSM_GROUPS, SSM_H, STATE_N), STATE_N ** -0.5),
        'ssm_d': nrm(ks[16], (L_, D_SSM), 1.0),
        'w_glu': nrm(ks[17], (L_, D_SSM, D_SSM), D_SSM ** -0.5),
        'b_glu': nrm(ks[18], (L_, D_SSM), 0.02),
        'w_ssm_out': nrm(ks[19], (L_, D_SSM, D_MODEL), D_SSM ** -0.5),
        'w_o': nrm(ks[20], (L_, D_MODEL, D_MODEL), BETA * D_MODEL ** -0.5),
        'ln1_g': 1.0 + nrm(ks[21], (L_, D_MODEL), 0.02),
        'ln1_b': nrm(ks[22], (L_, D_MODEL), 0.02),
        'w_router': nrm(ks[23], (L_, D_MODEL, N_EXPERTS), D_MODEL ** -0.5),
        'router_bias': nrm(ks[24], (L_, N_EXPERTS), 0.01),
        'w_gate': nrm(ks[25], (L_, N_EXPERTS, D_MODEL, D_EXPERT), D_MODEL ** -0.5),
        'w_up': nrm(ks[26], (L_, N_EXPERTS, D_MODEL, D_EXPERT), D_MODEL ** -0.5),
        'w_down': nrm(ks[27], (L_, N_EXPERTS, D_EXPERT, D_MODEL), BETA * D_EXPERT ** -0.5),
        'ws_gate': nrm(ks[28], (L_, D_MODEL, D_SHARED), D_MODEL ** -0.5),
        'ws_up': nrm(ks[29], (L_, D_MODEL, D_SHARED), D_MODEL ** -0.5),
        'ws_down': nrm(ks[30], (L_, D_SHARED, D_MODEL), BETA * D_SHARED ** -0.5),
        'ln2_g': 1.0 + nrm(ks[31], (L_, D_MODEL), 0.02),
        'ln2_b': nrm(ks[32], (L_, D_MODEL), 0.02),
    }


def reference(x_prompt, x_sample, state_conv, state_ssm_re, state_ssm_im,
              w_in, b_in, conv_w, w_conv_out, a_re, a_im, log_dt,
              ssm_b_re, ssm_b_im, ssm_c_re, ssm_c_im, ssm_d, w_glu, b_glu, w_ssm_out, w_o,
              ln1_g, ln1_b, w_router, router_bias, w_gate, w_up, w_down,
              ws_gate, ws_up, ws_down, ln2_g, ln2_b):
    params = {
        'w_in': w_in, 'b_in': b_in, 'conv_w': conv_w, 'w_conv_out': w_conv_out,
        'a_re': a_re, 'a_im': a_im, 'log_dt': log_dt,
        'ssm_b_re': ssm_b_re, 'ssm_b_im': ssm_b_im, 'ssm_c_re': ssm_c_re, 'ssm_c_im': ssm_c_im,
        'ssm_d': ssm_d, 'w_glu': w_glu, 'b_glu': b_glu, 'w_ssm_out': w_ssm_out, 'w_o': w_o,
        'ln1_g': ln1_g, 'ln1_b': ln1_b, 'w_router': w_router, 'router_bias': router_bias,
        'w_gate': w_gate, 'w_up': w_up, 'w_down': w_down,
        'ws_gate': ws_gate, 'ws_up': ws_up, 'ws_down': ws_down,
        'ln2_g': ln2_g, 'ln2_b': ln2_b,
    }
    bsz = x_prompt.shape[0]
    zero_conv = jnp.zeros((DEPTH, bsz, CONV_W - 1, D_CONV), x_prompt.dtype)
    zero_ssm = jnp.zeros((DEPTH, bsz, SSM_GROUPS, STATE_N), F32)
    y_prompt, conv_p, re_p, im_p = run_trunk(x_prompt, zero_conv, zero_ssm, zero_ssm, params)
    y_sample, conv_s, re_s, im_s = run_trunk(x_sample, state_conv, state_ssm_re, state_ssm_im, params)
    return (y_prompt, y_sample, conv_p, re_p, im_p, conv_s, re_s, im_s)
```

```python
import functools
import math

import jax
import jax.numpy as jnp
from jax import lax
from jax.experimental import pallas as pl
from jax.experimental.pallas import tpu as pltpu

F32 = jnp.float32
BF16 = jnp.bfloat16

LN_EPS = 1e-5
ROUTED_SCALE = 2.5
N_ROUTE_GROUPS = 8
TOPK_GROUPS = 4
TOP_K = 8

SUBLANES = 8
SSM_BLOCKS = 4
VMEM_LIMIT = 60 * 1024 * 1024


def _const_spec(shape):
    nd = len(shape)
    return pl.BlockSpec(shape, lambda *_: (0,) * nd, pipeline_mode=pl.Buffered(1))


def _ssm_prep_kernel(are_ref, aim_ref, ldt_ref, br_ref, bi_ref,
                     lr_ref, li_ref, bbr_ref, bbi_ref):
    dt = jnp.exp(ldt_ref[...])
    ar = are_ref[...]
    ai = aim_ref[...]
    mag = jnp.exp(ar * dt)
    lr = mag * jnp.cos(ai * dt)
    li = mag * jnp.sin(ai * dt)
    den = ar * ar + ai * ai
    fr = ((lr - 1.0) * ar + li * ai) / den
    fi = (li * ar - (lr - 1.0) * ai) / den
    lr_ref[...] = lr
    li_ref[...] = li
    br = br_ref[...]
    bi = bi_ref[...]
    bbr_ref[...] = fr * br - fi * bi
    bbi_ref[...] = fr * bi + fi * br


def _ssm_prep(a_re, a_im, log_dt, b_re, b_im):
    g, n = a_re.shape
    h = b_re.shape[-1]
    vec = jax.ShapeDtypeStruct((g, 1, n), F32)
    mat = jax.ShapeDtypeStruct((g, h, n), F32)
    return pl.pallas_call(
        _ssm_prep_kernel,
        out_shape=(vec, vec, mat, mat),
        name="ssm_prep",
    )(a_re.reshape(g, 1, n), a_im.reshape(g, 1, n), log_dt.reshape(g, 1, 1),
      b_re.transpose(0, 2, 1), b_im.transpose(0, 2, 1))


def _block_diag(m, nblk):
    g, p, q = m.shape
    gl = g // nblk
    eye = jnp.eye(gl, dtype=m.dtype)
    out = jnp.einsum('kapq,ab->kapbq', m.reshape(nblk, gl, p, q), eye)
    return out.reshape(nblk, gl * p, gl * q)


def _layer_norm(r, g, b):
    mu = jnp.mean(r, axis=-1, keepdims=True)
    d = r - mu
    var = jnp.mean(d * d, axis=-1, keepdims=True)
    return d * lax.rsqrt(var + LN_EPS) * g + b


def _route(scores, biased):
    n_exp, r = scores.shape
    gsz = n_exp // N_ROUTE_GROUPS
    neg = jnp.float32(-jnp.inf)
    rows = []
    for g in range(N_ROUTE_GROUPS):
        v = biased[g * gsz:(g + 1) * gsz, :]
        m1 = jnp.max(v, axis=0, keepdims=True)
        is_max = v == m1
        n_max = jnp.sum(is_max.astype(F32), axis=0, keepdims=True)
        rest = jnp.max(jnp.where(is_max, neg, v), axis=0, keepdims=True)
        rows.append(m1 + jnp.where(n_max >= 2.0, m1, rest))
    gscore = jnp.concatenate(rows, axis=0)
    gidx = lax.broadcasted_iota(jnp.int32, gscore.shape, 0)
    grank = jnp.zeros(gscore.shape, F32)
    for g in range(N_ROUTE_GROUPS):
        sg = gscore[g:g + 1, :]
        beats = (sg > gscore) | ((sg == gscore) & (gidx > g))
        grank = grank + beats.astype(F32)
    gkeep = grank < float(TOPK_GROUPS)
    masked = jnp.concatenate(
        [jnp.where(gkeep[g:g + 1, :], biased[g * gsz:(g + 1) * gsz, :], neg)
         for g in range(N_ROUTE_GROUPS)], axis=0)
    eidx = lax.broadcasted_iota(jnp.int32, masked.shape, 0)
    erank = jnp.zeros(masked.shape, F32)
    for e in range(n_exp):
        se = masked[e:e + 1, :]
        beats = (se > masked) | ((se == masked) & (eidx > e))
        erank = erank + beats.astype(F32)
    w = jnp.where(erank < float(TOP_K), scores, 0.0)
    return w / jnp.sum(w, axis=0, keepdims=True) * ROUTED_SCALE


def _mixer_kernel(alpha, batch, steps,
                  x_ref, cprev_ref, h0r_ref, h0i_ref,
                  win_ref, bin_ref, convw_ref, wco_ref,
                  lamr_ref, lami_ref, bbd_ref, cbd_ref, dskip_ref,
                  wglu_ref, bglu_ref, wso_ref, wo_ref, ln1g_ref, ln1b_ref,
                  wrt_ref, rbias_ref,
                  x1_ref, comb_ref, cnew_ref, hr_ref, hi_ref,
                  ubuf, xk_ref, ys_ref):
    rows = batch * steps
    d = x_ref.shape[1]
    d_blk = d // SSM_BLOCKS
    n_blk = lamr_ref.shape[1] // SSM_BLOCKS
    i = pl.program_id(0)

    @pl.when(i == 0)
    def _init():
        ubuf[0:2 * batch, :] = cprev_ref[...]
        hr_ref[...] = h0r_ref[...]
        hi_ref[...] = h0i_ref[...]

    x = x_ref[...]
    xb = x.astype(BF16)

    def proj(c):
        cols = slice(c * d, (c + 1) * d)
        return (jnp.dot(xb, win_ref[:, cols], preferred_element_type=F32)
                + bin_ref[:, cols])

    u = proj(1) * proj(2)
    ubuf[2 * batch:2 * batch + rows, :] = u
    conv = (convw_ref[0:1, :] * ubuf[0:rows, :]
            + convw_ref[1:2, :] * ubuf[batch:batch + rows, :]
            + convw_ref[2:3, :] * u)
    ya = jnp.dot((proj(0) * conv).astype(BF16), wco_ref[...],
                 preferred_element_type=F32)
    tail = ubuf[rows:rows + 2 * batch, :]
    ubuf[0:2 * batch, :] = tail
    cnew_ref[...] = tail

    us = proj(3)
    half = n_blk // 2
    for k in range(SSM_BLOCKS):
        usk = us[:, k * d_blk:(k + 1) * d_blk].astype(BF16)
        xk_ref[...] = jnp.dot(usk, bbd_ref[k], preferred_element_type=F32)
        for hf in range(2):
            st = slice(k * n_blk + hf * half, k * n_blk + (hf + 1) * half)
            re = slice(hf * half, (hf + 1) * half)
            im = slice(n_blk + hf * half, n_blk + (hf + 1) * half)
            if steps == 1:
                lr = lamr_ref[:, st]
                li = lami_ref[:, st]
                hr = hr_ref[:, st]
                hi = hi_ref[:, st]
                nhr = lr * hr - li * hi + xk_ref[:, re]
                nhi = lr * hi + li * hr + xk_ref[:, im]
                xk_ref[:, re] = nhr
                xk_ref[:, im] = nhi
                hr_ref[:, st] = nhr
                hi_ref[:, st] = nhi
            else:
                lr = jnp.broadcast_to(lamr_ref[:, st], (SUBLANES, half))
                li = jnp.broadcast_to(lami_ref[:, st], (SUBLANES, half))
                for s in range(batch // SUBLANES):
                    sub = slice(s * SUBLANES, (s + 1) * SUBLANES)

                    def step(t, carry, s=s, re=re, im=im, lr=lr, li=li):
                        hr, hi = carry
                        row = pl.multiple_of(t * batch + s * SUBLANES, SUBLANES)
                        rs = pl.ds(row, SUBLANES)
                        nhr = lr * hr - li * hi + xk_ref[rs, re]
                        nhi = lr * hi + li * hr + xk_ref[rs, im]
                        xk_ref[rs, re] = nhr
                        xk_ref[rs, im] = nhi
                        return nhr, nhi

                    hr, hi = lax.fori_loop(0, steps, step,
                                           (hr_ref[sub, st], hi_ref[sub, st]))
                    hr_ref[sub, st] = hr
                    hi_ref[sub, st] = hi
        ys_ref[:, k * d_blk:(k + 1) * d_blk] = jnp.dot(
            xk_ref[...].astype(BF16), cbd_ref[k], preferred_element_type=F32)
    ys = ys_ref[...] + dskip_ref[...] * us
    z = jax.nn.gelu(ys)
    gate = jnp.dot(z.astype(BF16), wglu_ref[...], preferred_element_type=F32) + bglu_ref[...]
    glu = z * jax.nn.sigmoid(gate)
    yb = jnp.dot(glu.astype(BF16), wso_ref[...], preferred_element_type=F32)

    m = jax.nn.sigmoid(proj(4)) * ya + jax.nn.sigmoid(proj(5)) * yb
    o = jnp.dot(m.astype(BF16), wo_ref[...], preferred_element_type=F32)
    x1 = _layer_norm(alpha * x + o, ln1g_ref[...], ln1b_ref[...])
    x1_ref[...] = x1

    logits = lax.dot_general(wrt_ref[...], x1, (((1,), (1,)), ((), ())),
                             precision=lax.Precision.HIGHEST,
                             preferred_element_type=F32)
    scores = jax.nn.sigmoid(logits)
    comb = _route(scores, scores + rbias_ref[...])
    comb_ref[...] = comb.T


def _mixer(x_rows, cprev, h0r, h0i, p, *, alpha, batch, steps):
    total, d = x_rows.shape
    rows = batch * steps
    n_state = h0r.shape[1]
    n_exp = p['w_router_t'].shape[0]
    grid = (total // rows,)
    consts = [p['w_in'], p['b_in'], p['conv_w'], p['w_conv_out'],
              p['lam_r'], p['lam_i'], p['bbd'], p['cbd'], p['ssm_d'],
              p['w_glu'], p['b_glu'], p['w_ssm_out'], p['w_o'], p['ln1_g'], p['ln1_b'],
              p['w_router_t'], p['router_bias']]
    in_specs = ([pl.BlockSpec((rows, d), lambda i: (i, 0)),
                 _const_spec(cprev.shape), _const_spec(h0r.shape), _const_spec(h0i.shape)]
                + [_const_spec(c.shape) for c in consts])
    out_shape = (jax.ShapeDtypeStruct((total, d), F32),
                 jax.ShapeDtypeStruct((total, n_exp), F32),
                 jax.ShapeDtypeStruct((2 * batch, d), F32),
                 jax.ShapeDtypeStruct((batch, n_state), F32),
                 jax.ShapeDtypeStruct((batch, n_state), F32))
    out_specs = (pl.BlockSpec((rows, d), lambda i: (i, 0)),
                 pl.BlockSpec((rows, n_exp), lambda i: (i, 0)),
                 pl.BlockSpec((2 * batch, d), lambda i: (0, 0)),
                 pl.BlockSpec((batch, n_state), lambda i: (0, 0)),
                 pl.BlockSpec((batch, n_state), lambda i: (0, 0)))
    scratch = [pltpu.VMEM((rows + 2 * batch, d), F32),
               pltpu.VMEM((rows, 2 * n_state // SSM_BLOCKS), F32),
               pltpu.VMEM((rows, d), F32)]
    return pl.pallas_call(
        functools.partial(_mixer_kernel, alpha, batch, steps),
        grid=grid, in_specs=in_specs, out_specs=out_specs, out_shape=out_shape,
        scratch_shapes=scratch,
        compiler_params=pltpu.CompilerParams(
            dimension_semantics=("arbitrary",), vmem_limit_bytes=VMEM_LIMIT),
        name="mixer",
    )(x_rows, cprev, h0r, h0i, *consts)


def _swiglu(xb, wg, wu):
    g = jnp.dot(xb, wg.astype(BF16), preferred_element_type=F32)
    u = jnp.dot(xb, wu.astype(BF16), preferred_element_type=F32)
    return jax.nn.silu(g) * u


def _moe_kernel(alpha, x1_ref, comb_ref, wg_ref, wu_ref, wd_ref,
                wsg_ref, wsu_ref, wsd_ref, g_ref, b_ref, out_ref, xb_ref, acc_ref):
    e = pl.program_id(1)

    @pl.when(e == 0)
    def _shared():
        xb = x1_ref[...].astype(BF16)
        xb_ref[...] = xb
        hs = _swiglu(xb, wsg_ref[...], wsu_ref[...])
        acc_ref[...] = jnp.dot(hs.astype(BF16), wsd_ref[...].astype(BF16),
                               preferred_element_type=F32)

    comb = comb_ref[...]
    lane = lax.broadcasted_iota(jnp.int32, comb.shape, 1)
    c = jnp.sum(jnp.where(lane == e, comb, 0.0), axis=1, keepdims=True)
    h = _swiglu(xb_ref[...], wg_ref[...], wu_ref[...]) * c
    acc_ref[...] += jnp.dot(h.astype(BF16), wd_ref[...].astype(BF16),
                            preferred_element_type=F32)

    @pl.when(e == pl.num_programs(1) - 1)
    def _finish():
        out_ref[...] = _layer_norm(alpha * x1_ref[...] + acc_ref[...],
                                   g_ref[...], b_ref[...])


def _moe(x1, comb, p, *, alpha, tile):
    total, d = x1.shape
    n_exp, _, f = p['w_gate'].shape
    fs = p['ws_gate'].shape[1]
    grid = (total // tile, n_exp)
    in_specs = [pl.BlockSpec((tile, d), lambda i, e: (i, 0)),
                pl.BlockSpec((tile, n_exp), lambda i, e: (i, 0)),
                pl.BlockSpec((None, d, f), lambda i, e: (e, 0, 0)),
                pl.BlockSpec((None, d, f), lambda i, e: (e, 0, 0)),
                pl.BlockSpec((None, f, d), lambda i, e: (e, 0, 0)),
                _const_spec((d, fs)), _const_spec((d, fs)), _const_spec((fs, d)),
                _const_spec((1, d)), _const_spec((1, d))]
    return pl.pallas_call(
        functools.partial(_moe_kernel, alpha),
        grid=grid, in_specs=in_specs,
        out_specs=pl.BlockSpec((tile, d), lambda i, e: (i, 0)),
        out_shape=jax.ShapeDtypeStruct((total, d), F32),
        scratch_shapes=[pltpu.VMEM((tile, d), BF16), pltpu.VMEM((tile, d), F32)],
        compiler_params=pltpu.CompilerParams(
            dimension_semantics=("arbitrary", "arbitrary"), vmem_limit_bytes=VMEM_LIMIT),
        name="moe",
    )(x1, comb, p['w_gate'], p['w_up'], p['w_down'],
      p['ws_gate'], p['ws_up'], p['ws_down'], p['ln2_g'], p['ln2_b'])


def _layer_params(l, w):
    g, n = w['a_re'].shape[1:]
    lr, li, bbr, bbi = _ssm_prep(w['a_re'][l], w['a_im'][l], w['log_dt'][l],
                                 w['ssm_b_re'][l], w['ssm_b_im'][l])
    bbd = jnp.concatenate([_block_diag(bbr, SSM_BLOCKS), _block_diag(bbi, SSM_BLOCKS)],
                          axis=-1).astype(BF16)
    c_re = w['ssm_c_re'][l].transpose(0, 2, 1)
    c_im = w['ssm_c_im'][l].transpose(0, 2, 1)
    cbd = jnp.concatenate([_block_diag(c_re, SSM_BLOCKS), _block_diag(-c_im, SSM_BLOCKS)],
                          axis=1).astype(BF16)
    row = lambda v: v.reshape(1, -1)
    return {
        'w_in': w['w_in'][l].astype(BF16), 'b_in': row(w['b_in'][l]),
        'conv_w': w['conv_w'][l], 'w_conv_out': w['w_conv_out'][l].astype(BF16),
        'lam_r': lr.reshape(1, g * n), 'lam_i': li.reshape(1, g * n),
        'bbd': bbd, 'cbd': cbd, 'ssm_d': row(w['ssm_d'][l]),
        'w_glu': w['w_glu'][l].astype(BF16), 'b_glu': row(w['b_glu'][l]),
        'w_ssm_out': w['w_ssm_out'][l].astype(BF16), 'w_o': w['w_o'][l].astype(BF16),
        'ln1_g': row(w['ln1_g'][l]), 'ln1_b': row(w['ln1_b'][l]),
        'w_router_t': w['w_router'][l].T, 'router_bias': w['router_bias'][l].reshape(-1, 1),
        'w_gate': w['w_gate'][l], 'w_up': w['w_up'][l], 'w_down': w['w_down'][l],
        'ws_gate': w['ws_gate'][l], 'ws_up': w['ws_up'][l], 'ws_down': w['ws_down'][l],
        'ln2_g': row(w['ln2_g'][l]), 'ln2_b': row(w['ln2_b'][l]),
    }


def _pick_steps(batch, seq, max_rows):
    steps = max(1, min(seq, max_rows // batch))
    while seq % steps:
        steps -= 1
    return steps


def _pick_tile(total, max_tile):
    tile = min(total, max_tile)
    while total % tile or tile % 16:
        tile -= 16
    return tile


def _run_trunk(x, conv_st, re_st, im_st, params, alpha):
    bsz, seq, d = x.shape
    assert bsz % SUBLANES == 0
    steps = _pick_steps(bsz, seq, 512)
    tile = _pick_tile(bsz * seq, 1024)
    rows = x.transpose(1, 0, 2).reshape(seq * bsz, d)
    convs, res, ims = [], [], []
    for l, p in enumerate(params):
        kw = conv_st.shape[2]
        assert kw == 2
        cprev = conv_st[l].astype(F32).transpose(1, 0, 2).reshape(kw * bsz, d)
        h0r = re_st[l].astype(F32).reshape(bsz, -1)
        h0i = im_st[l].astype(F32).reshape(bsz, -1)
        x1, comb, cnew, hr, hi = _mixer(rows, cprev, h0r, h0i, p,
                                        alpha=alpha, batch=bsz, steps=steps)
        rows = _moe(x1, comb, p, alpha=alpha, tile=tile)
        convs.append(cnew.reshape(kw, bsz, d).transpose(1, 0, 2))
        res.append(hr.reshape(re_st.shape[1:]))
        ims.append(hi.reshape(im_st.shape[1:]))
    y = rows.reshape(seq, bsz, d).transpose(1, 0, 2)
    return y, jnp.stack(convs), jnp.stack(res), jnp.stack(ims)


def kernel(x_prompt, x_sample, state_conv, state_ssm_re, state_ssm_im,
           w_in, b_in, conv_w, w_conv_out, a_re, a_im, log_dt,
           ssm_b_re, ssm_b_im, ssm_c_re, ssm_c_im, ssm_d, w_glu, b_glu, w_ssm_out, w_o,
           ln1_g, ln1_b, w_router, router_bias, w_gate, w_up, w_down,
           ws_gate, ws_up, ws_down, ln2_g, ln2_b):
    w = dict(w_in=w_in, b_in=b_in, conv_w=conv_w, w_conv_out=w_conv_out,
             a_re=a_re, a_im=a_im, log_dt=log_dt,
             ssm_b_re=ssm_b_re, ssm_b_im=ssm_b_im, ssm_c_re=ssm_c_re, ssm_c_im=ssm_c_im,
             ssm_d=ssm_d, w_glu=w_glu, b_glu=b_glu, w_ssm_out=w_ssm_out, w_o=w_o,
             ln1_g=ln1_g, ln1_b=ln1_b, w_router=w_router, router_bias=router_bias,
             w_gate=w_gate, w_up=w_up, w_down=w_down,
             ws_gate=ws_gate, ws_up=ws_up, ws_down=ws_down, ln2_g=ln2_g, ln2_b=ln2_b)
    depth = w_in.shape[0]
    alpha = (2.0 * depth) ** 0.25
    params = [_layer_params(l, w) for l in range(depth)]
    bsz = x_prompt.shape[0]
    zero_conv = jnp.zeros((depth, bsz) + state_conv.shape[2:], x_prompt.dtype)
    zero_ssm = jnp.zeros((depth, bsz) + state_ssm_re.shape[2:], F32)
    y_p, conv_p, re_p, im_p = _run_trunk(x_prompt, zero_conv, zero_ssm, zero_ssm, params, alpha)
    y_s, conv_s, re_s, im_s = _run_trunk(x_sample, state_conv, state_ssm_re, state_ssm_im,
                                         params, alpha)
    return (y_p, y_s, conv_p, re_p, im_p, conv_s, re_s, im_s)
```

```python
import functools
import math

import jax
import jax.numpy as jnp
from jax import lax
from jax.experimental import pallas as pl
from jax.experimental.pallas import tpu as pltpu

F32 = jnp.float32
BF16 = jnp.bfloat16

LN_EPS = 1e-5
ROUTED_SCALE = 2.5
N_ROUTE_GROUPS = 8
TOPK_GROUPS = 4
TOP_K = 8

SUBLANES = 8
SSM_BLOCKS = 4
VMEM_LIMIT = 60 * 1024 * 1024


def _const_spec(shape):
    nd = len(shape)
    return pl.BlockSpec(shape, lambda *_: (0,) * nd, pipeline_mode=pl.Buffered(1))


def _ssm_prep_kernel(are_ref, aim_ref, ldt_ref, br_ref, bi_ref,
                     lr_ref, li_ref, bbr_ref, bbi_ref):
    dt = jnp.exp(ldt_ref[...])
    ar = are_ref[...]
    ai = aim_ref[...]
    mag = jnp.exp(ar * dt)
    lr = mag * jnp.cos(ai * dt)
    li = mag * jnp.sin(ai * dt)
    den = ar * ar + ai * ai
    fr = ((lr - 1.0) * ar + li * ai) / den
    fi = (li * ar - (lr - 1.0) * ai) / den
    lr_ref[...] = lr
    li_ref[...] = li
    br = br_ref[...]
    bi = bi_ref[...]
    bbr_ref[...] = fr * br - fi * bi
    bbi_ref[...] = fr * bi + fi * br


def _ssm_prep(a_re, a_im, log_dt, b_re, b_im):
    g, n = a_re.shape
    h = b_re.shape[-1]
    vec = jax.ShapeDtypeStruct((g, 1, n), F32)
    mat = jax.ShapeDtypeStruct((g, h, n), F32)
    return pl.pallas_call(
        _ssm_prep_kernel,
        out_shape=(vec, vec, mat, mat),
        name="ssm_prep",
    )(a_re.reshape(g, 1, n), a_im.reshape(g, 1, n), log_dt.reshape(g, 1, 1),
      b_re.transpose(0, 2, 1), b_im.transpose(0, 2, 1))


def _block_diag(m, nblk):
    g, p, q = m.shape
    gl = g // nblk
    eye = jnp.eye(gl, dtype=m.dtype)
    out = jnp.einsum('kapq,ab->kapbq', m.reshape(nblk, gl, p, q), eye)
    return out.reshape(nblk, gl * p, gl * q)


def _layer_norm(r, g, b):
    mu = jnp.mean(r, axis=-1, keepdims=True)
    d = r - mu
    var = jnp.mean(d * d, axis=-1, keepdims=True)
    return d * lax.rsqrt(var + LN_EPS) * g + b


def _route(scores, biased):
    n_exp, r = scores.shape
    gsz = n_exp // N_ROUTE_GROUPS
    neg = jnp.float32(-jnp.inf)
    rows = []
    for g in range(N_ROUTE_GROUPS):
        v = biased[g * gsz:(g + 1) * gsz, :]
        m1 = jnp.max(v, axis=0, keepdims=True)
        is_max = v == m1
        n_max = jnp.sum(is_max.astype(F32), axis=0, keepdims=True)
        rest = jnp.max(jnp.where(is_max, neg, v), axis=0, keepdims=True)
        rows.append(m1 + jnp.where(n_max >= 2.0, m1, rest))
    gscore = jnp.concatenate(rows, axis=0)
    gidx = lax.broadcasted_iota(jnp.int32, gscore.shape, 0)
    grank = jnp.zeros(gscore.shape, F32)
    for g in range(N_ROUTE_GROUPS):
        sg = gscore[g:g + 1, :]
        beats = (sg > gscore) | ((sg == gscore) & (gidx > g))
        grank = grank + beats.astype(F32)
    gkeep = grank < float(TOPK_GROUPS)
    masked = jnp.concatenate(
        [jnp.where(gkeep[g:g + 1, :], biased[g * gsz:(g + 1) * gsz, :], neg)
         for g in range(N_ROUTE_GROUPS)], axis=0)
    eidx = lax.broadcasted_iota(jnp.int32, masked.shape, 0)
    erank = jnp.zeros(masked.shape, F32)
    for e in range(n_exp):
        se = masked[e:e + 1, :]
        beats = (se > masked) | ((se == masked) & (eidx > e))
        erank = erank + beats.astype(F32)
    w = jnp.where(erank < float(TOP_K), scores, 0.0)
    return w / jnp.sum(w, axis=0, keepdims=True) * ROUTED_SCALE


def _mixer_kernel(alpha, batch, steps, sub,
                  x_ref, cprev_ref, h0r_ref, h0i_ref,
                  win_ref, bin_ref, convw_ref, wco_ref,
                  lamr_ref, lami_ref, bbd_ref, cbd_ref, dskip_ref,
                  wglu_ref, bglu_ref, wso_ref, wo_ref, ln1g_ref, ln1b_ref,
                  wrt_ref, rbias_ref,
                  x1_ref, comb_ref, cnew_ref, hr_ref, hi_ref, *rest):
    if sub:
        combt_ref, cnt_ref, ubuf, xk_ref, ys_ref = rest
    else:
        ubuf, xk_ref, ys_ref = rest
    rows = batch * steps
    d = x_ref.shape[1]
    d_blk = d // SSM_BLOCKS
    n_blk = lamr_ref.shape[1] // SSM_BLOCKS
    i = pl.program_id(0)

    @pl.when(i == 0)
    def _init():
        ubuf[0:2 * batch, :] = cprev_ref[...]
        hr_ref[...] = h0r_ref[...]
        hi_ref[...] = h0i_ref[...]

    x = x_ref[...]
    xb = x.astype(BF16)

    def proj(c):
        cols = slice(c * d, (c + 1) * d)
        return (jnp.dot(xb, win_ref[:, cols], preferred_element_type=F32)
                + bin_ref[:, cols])

    u = proj(1) * proj(2)
    ubuf[2 * batch:2 * batch + rows, :] = u
    conv = (convw_ref[0:1, :] * ubuf[0:rows, :]
            + convw_ref[1:2, :] * ubuf[batch:batch + rows, :]
            + convw_ref[2:3, :] * u)
    ya = jnp.dot((proj(0) * conv).astype(BF16), wco_ref[...],
                 preferred_element_type=F32)
    tail = ubuf[rows:rows + 2 * batch, :]
    ubuf[0:2 * batch, :] = tail
    cnew_ref[...] = tail

    us = proj(3)
    half = n_blk // 2
    for k in range(SSM_BLOCKS):
        usk = us[:, k * d_blk:(k + 1) * d_blk].astype(BF16)
        xk_ref[...] = jnp.dot(usk, bbd_ref[k], preferred_element_type=F32)
        for hf in range(2):
            st = slice(k * n_blk + hf * half, k * n_blk + (hf + 1) * half)
            re = slice(hf * half, (hf + 1) * half)
            im = slice(n_blk + hf * half, n_blk + (hf + 1) * half)
            if steps == 1:
                lr = lamr_ref[:, st]
                li = lami_ref[:, st]
                hr = hr_ref[:, st]
                hi = hi_ref[:, st]
                nhr = lr * hr - li * hi + xk_ref[:, re]
                nhi = lr * hi + li * hr + xk_ref[:, im]
                xk_ref[:, re] = nhr
                xk_ref[:, im] = nhi
                hr_ref[:, st] = nhr
                hi_ref[:, st] = nhi
            else:
                lr = jnp.broadcast_to(lamr_ref[:, st], (SUBLANES, half))
                li = jnp.broadcast_to(lami_ref[:, st], (SUBLANES, half))
                for s in range(batch // SUBLANES):
                    grp = slice(s * SUBLANES, (s + 1) * SUBLANES)

                    def step(t, carry, s=s, re=re, im=im, lr=lr, li=li):
                        hr, hi = carry
                        row = pl.multiple_of(t * batch + s * SUBLANES, SUBLANES)
                        rs = pl.ds(row, SUBLANES)
                        nhr = lr * hr - li * hi + xk_ref[rs, re]
                        nhi = lr * hi + li * hr + xk_ref[rs, im]
                        xk_ref[rs, re] = nhr
                        xk_ref[rs, im] = nhi
                        return nhr, nhi

                    hr, hi = lax.fori_loop(0, steps, step,
                                           (hr_ref[grp, st], hi_ref[grp, st]))
                    hr_ref[grp, st] = hr
                    hi_ref[grp, st] = hi
        ys_ref[:, k * d_blk:(k + 1) * d_blk] = jnp.dot(
            xk_ref[...].astype(BF16), cbd_ref[k], preferred_element_type=F32)
    ys = ys_ref[...] + dskip_ref[...] * us
    z = jax.nn.gelu(ys)
    gate = jnp.dot(z.astype(BF16), wglu_ref[...], preferred_element_type=F32) + bglu_ref[...]
    glu = z * jax.nn.sigmoid(gate)
    yb = jnp.dot(glu.astype(BF16), wso_ref[...], preferred_element_type=F32)

    m = jax.nn.sigmoid(proj(4)) * ya + jax.nn.sigmoid(proj(5)) * yb
    o = jnp.dot(m.astype(BF16), wo_ref[...], preferred_element_type=F32)
    x1 = _layer_norm(alpha * x + o, ln1g_ref[...], ln1b_ref[...])
    x1_ref[...] = x1

    logits = lax.dot_general(wrt_ref[...], x1, (((1,), (1,)), ((), ())),
                             precision=lax.Precision.HIGHEST,
                             preferred_element_type=F32)
    scores = jax.nn.sigmoid(logits)
    comb = _route(scores, scores + rbias_ref[...])
    comb_ref[...] = comb.T
    if sub:
        combt_ref[...] = comb
        ones = jnp.ones((SUBLANES, sub), BF16)
        for s in range(rows // sub):
            sel = jnp.where(comb[:, s * sub:(s + 1) * sub] != 0.0, 1.0, 0.0).astype(BF16)
            cnt_ref[s] = lax.dot_general(ones, sel, (((1,), (1,)), ((), ())),
                                         preferred_element_type=F32)


def _mixer(x_rows, cprev, h0r, h0i, p, *, alpha, batch, steps, sub=0):
    total, d = x_rows.shape
    rows = batch * steps
    assert not sub or rows % sub == 0
    n_state = h0r.shape[1]
    n_exp = p['w_router_t'].shape[0]
    grid = (total // rows,)
    consts = [p['w_in'], p['b_in'], p['conv_w'], p['w_conv_out'],
              p['lam_r'], p['lam_i'], p['bbd'], p['cbd'], p['ssm_d'],
              p['w_glu'], p['b_glu'], p['w_ssm_out'], p['w_o'], p['ln1_g'], p['ln1_b'],
              p['w_router_t'], p['router_bias']]
    in_specs = ([pl.BlockSpec((rows, d), lambda i: (i, 0)),
                 _const_spec(cprev.shape), _const_spec(h0r.shape), _const_spec(h0i.shape)]
                + [_const_spec(c.shape) for c in consts])
    out_shape = (jax.ShapeDtypeStruct((total, d), F32),
                 jax.ShapeDtypeStruct((total, n_exp), F32),
                 jax.ShapeDtypeStruct((2 * batch, d), F32),
                 jax.ShapeDtypeStruct((batch, n_state), F32),
                 jax.ShapeDtypeStruct((batch, n_state), F32))
    out_specs = (pl.BlockSpec((rows, d), lambda i: (i, 0)),
                 pl.BlockSpec((rows, n_exp), lambda i: (i, 0)),
                 pl.BlockSpec((2 * batch, d), lambda i: (0, 0)),
                 pl.BlockSpec((batch, n_state), lambda i: (0, 0)),
                 pl.BlockSpec((batch, n_state), lambda i: (0, 0)))
    if sub:
        out_shape += (jax.ShapeDtypeStruct((n_exp, total), F32),
                      jax.ShapeDtypeStruct((total // sub, SUBLANES, n_exp), F32))
        out_specs += (pl.BlockSpec((n_exp, rows), lambda i: (0, i)),
                      pl.BlockSpec((rows // sub, SUBLANES, n_exp), lambda i: (i, 0, 0)))
    scratch = [pltpu.VMEM((rows + 2 * batch, d), F32),
               pltpu.VMEM((rows, 2 * n_state // SSM_BLOCKS), F32),
               pltpu.VMEM((rows, d), F32)]
    return pl.pallas_call(
        functools.partial(_mixer_kernel, alpha, batch, steps, sub),
        grid=grid, in_specs=in_specs, out_specs=out_specs, out_shape=out_shape,
        scratch_shapes=scratch,
        compiler_params=pltpu.CompilerParams(
            dimension_semantics=("arbitrary",), vmem_limit_bytes=VMEM_LIMIT),
        name="mixer",
    )(x_rows, cprev, h0r, h0i, *consts)


def _swiglu(xb, wg, wu):
    g = jnp.dot(xb, wg.astype(BF16), preferred_element_type=F32)
    u = jnp.dot(xb, wu.astype(BF16), preferred_element_type=F32)
    return jax.nn.silu(g) * u


def _moe_kernel(alpha, x1_ref, comb_ref, wg_ref, wu_ref, wd_ref,
                wsg_ref, wsu_ref, wsd_ref, g_ref, b_ref, out_ref, xb_ref, acc_ref):
    e = pl.program_id(1)

    @pl.when(e == 0)
    def _shared():
        xb = x1_ref[...].astype(BF16)
        xb_ref[...] = xb
        hs = _swiglu(xb, wsg_ref[...], wsu_ref[...])
        acc_ref[...] = jnp.dot(hs.astype(BF16), wsd_ref[...].astype(BF16),
                               preferred_element_type=F32)

    comb = comb_ref[...]
    lane = lax.broadcasted_iota(jnp.int32, comb.shape, 1)
    c = jnp.sum(jnp.where(lane == e, comb, 0.0), axis=1, keepdims=True)
    h = _swiglu(xb_ref[...], wg_ref[...], wu_ref[...]) * c
    acc_ref[...] += jnp.dot(h.astype(BF16), wd_ref[...].astype(BF16),
                            preferred_element_type=F32)

    @pl.when(e == pl.num_programs(1) - 1)
    def _finish():
        out_ref[...] = _layer_norm(alpha * x1_ref[...] + acc_ref[...],
                                   g_ref[...], b_ref[...])


def _moe(x1, comb, p, *, alpha, tile):
    total, d = x1.shape
    n_exp, _, f = p['w_gate'].shape
    fs = p['ws_gate'].shape[1]
    grid = (total // tile, n_exp)
    in_specs = [pl.BlockSpec((tile, d), lambda i, e: (i, 0)),
                pl.BlockSpec((tile, n_exp), lambda i, e: (i, 0)),
                pl.BlockSpec((None, d, f), lambda i, e: (e, 0, 0)),
                pl.BlockSpec((None, d, f), lambda i, e: (e, 0, 0)),
                pl.BlockSpec((None, f, d), lambda i, e: (e, 0, 0)),
                _const_spec((d, fs)), _const_spec((d, fs)), _const_spec((fs, d)),
                _const_spec((1, d)), _const_spec((1, d))]
    return pl.pallas_call(
        functools.partial(_moe_kernel, alpha),
        grid=grid, in_specs=in_specs,
        out_specs=pl.BlockSpec((tile, d), lambda i, e: (i, 0)),
        out_shape=jax.ShapeDtypeStruct((total, d), F32),
        scratch_shapes=[pltpu.VMEM((tile, d), BF16), pltpu.VMEM((tile, d), F32)],
        compiler_params=pltpu.CompilerParams(
            dimension_semantics=("arbitrary", "arbitrary"), vmem_limit_bytes=VMEM_LIMIT),
        name="moe",
    )(x1, comb, p['w_gate'], p['w_up'], p['w_down'],
      p['ws_gate'], p['ws_up'], p['ws_down'], p['ln2_g'], p['ln2_b'])


SUB = 256
RUN_ALIGN = 16
FFN_CHUNK = 192
SEL_CHUNK = 512
DISPATCH_SUBTILES = 4
EXPERTS_PER_STEP = 4


def _expert_onehot(r, off, cnt):
    hit = (r >= off) & (r < off + cnt)
    return hit, jnp.where(hit, off + 1.0, 0.0)


def _moe_sparse_kernel(alpha, n_sub, cap, eg,
                       off_s, pc_s,
                       x1_ref, comb_ref, combt_ref, offr_ref, cntr_ref, offc_ref, cntc_ref,
                       wg_ref, wu_ref, wd_ref, wsg_ref, wsu_ref, wsd_ref, g_ref, b_ref,
                       out_ref, gbuf, xe):
    i = pl.program_id(0)
    g = pl.program_id(1)
    n_exp = comb_ref.shape[1]
    tri_r = lax.broadcasted_iota(jnp.int32, (SUB, SUB), 0)
    tri_c = lax.broadcasted_iota(jnp.int32, (SUB, SUB), 1)

    @pl.when((i == 0) & (g == 0))
    def _zero():
        xe[...] = jnp.zeros(xe.shape, xe.dtype)

    @pl.when(g == 0)
    def _dispatch():
        before = jnp.where(tri_r < tri_c, 1.0, 0.0).astype(BF16)
        for j in range(n_sub):
            tok = slice(j * SUB, (j + 1) * SUB)
            xj = x1_ref[tok, :].astype(BF16)
            sel = combt_ref[:, tok] != 0.0
            pos = jnp.dot(jnp.where(sel, 1.0, 0.0).astype(BF16), before,
                          preferred_element_type=F32)
            posm = jnp.where(sel, pos, -1.0).astype(BF16)
            off = offr_ref[j]
            cnt = cntr_ref[j]
            for rc in range(cap // SEL_CHUNK):
                r = (lax.broadcasted_iota(jnp.int32, (SEL_CHUNK, 1), 0)
                     + rc * SEL_CHUNK).astype(F32)
                hit, start1 = _expert_onehot(r, off, cnt)
                s = jnp.sum(start1, axis=1, keepdims=True)
                q = jnp.where(s > 0.0, r - (s - 1.0), -2.0)
                rank = jnp.dot(jnp.where(hit, 1.0, 0.0).astype(BF16), posm,
                               preferred_element_type=F32)
                pick = jnp.where(rank == q, 1.0, 0.0).astype(BF16)
                gbuf[j, rc * SEL_CHUNK:(rc + 1) * SEL_CHUNK, :] = jnp.dot(
                    pick, xj, preferred_element_type=F32).astype(BF16)

    def move_runs(e, to_packed):
        n = jnp.int32(0)
        for j in range(n_sub):
            idx = (i * n_sub + j) * n_exp + e
            o = off_s[idx]
            p = pc_s[idx]

            def body(c, carry, j=j, o=o, n=n):
                src = pl.ds(pl.multiple_of(o + c * RUN_ALIGN, RUN_ALIGN), RUN_ALIGN)
                dst = pl.ds(pl.multiple_of(n + c * RUN_ALIGN, RUN_ALIGN), RUN_ALIGN)
                if to_packed:
                    xe[dst, :] = gbuf[j, src, :]
                else:
                    gbuf[j, src, :] = xe[dst, :]
                return carry

            lax.fori_loop(0, lax.div(p, RUN_ALIGN), body, 0)
            n = n + p
        return n

    for ee in range(eg):
        e = g * eg + ee
        n = move_runs(e, True)

        def ffn(k, carry, ee=ee):
            rs = pl.ds(pl.multiple_of(k * FFN_CHUNK, RUN_ALIGN), FFN_CHUNK)
            xc = xe[rs, :]
            h = (jax.nn.silu(jnp.dot(xc, wg_ref[ee], preferred_element_type=F32))
                 * jnp.dot(xc, wu_ref[ee], preferred_element_type=F32))
            xe[rs, :] = jnp.dot(h.astype(BF16), wd_ref[ee],
                                preferred_element_type=F32).astype(BF16)
            return carry

        lax.fori_loop(0, lax.div(n + (FFN_CHUNK - 1), FFN_CHUNK), ffn, 0)
        move_runs(e, False)

    @pl.when(g == pl.num_programs(1) - 1)
    def _combine():
        earlier = jnp.where(tri_c < tri_r, 1.0, 0.0).astype(BF16)
        for j in range(n_sub):
            tok = slice(j * SUB, (j + 1) * SUB)
            x = x1_ref[tok, :]
            xb = x.astype(BF16)
            hs = (jax.nn.silu(jnp.dot(xb, wsg_ref[...], preferred_element_type=F32))
                  * jnp.dot(xb, wsu_ref[...], preferred_element_type=F32))
            y = jnp.dot(hs.astype(BF16), wsd_ref[...], preferred_element_type=F32)
            comb = comb_ref[tok, :]
            sel = comb != 0.0
            pos = jnp.dot(earlier, jnp.where(sel, 1.0, 0.0).astype(BF16),
                          preferred_element_type=F32)
            posm = jnp.where(sel, pos, -1.0).astype(BF16)
            c_hi = comb.astype(BF16)
            c_lo = (comb - c_hi.astype(F32)).astype(BF16)
            off = offc_ref[j]
            cnt = cntc_ref[j]
            for rc in range(cap // SEL_CHUNK):
                r = (lax.broadcasted_iota(jnp.int32, (1, SEL_CHUNK), 1)
                     + rc * SEL_CHUNK).astype(F32)
                hit, start1 = _expert_onehot(r, off, cnt)
                s = jnp.sum(start1, axis=0, keepdims=True)
                q = jnp.where(s > 0.0, r - (s - 1.0), -2.0)
                hb = jnp.where(hit, 1.0, 0.0).astype(BF16)
                pick = jnp.dot(posm, hb, preferred_element_type=F32) == q
                w_hi = jnp.where(pick, jnp.dot(c_hi, hb, preferred_element_type=F32), 0.0)
                w_lo = jnp.where(pick, jnp.dot(c_lo, hb, preferred_element_type=F32), 0.0)
                rows = gbuf[j, rc * SEL_CHUNK:(rc + 1) * SEL_CHUNK, :]
                y = y + jnp.dot(w_hi.astype(BF16), rows, preferred_element_type=F32)
                y = y + jnp.dot(w_lo.astype(BF16), rows, preferred_element_type=F32)
            out_ref[tok, :] = _layer_norm(alpha * x + y, g_ref[...], b_ref[...])


def _moe_sparse(x1, comb, combt, cnt, p, *, alpha, n_sub, eg=EXPERTS_PER_STEP):
    total, d = x1.shape
    n_exp, _, f = p['w_gate'].shape
    fs = p['ws_gate'].shape[1]
    tile = n_sub * SUB
    n_tiles = total // tile
    cap = -(-(SUB * TOP_K + n_exp * (RUN_ALIGN - 1)) // SEL_CHUNK) * SEL_CHUNK
    xe_rows = -(-(tile + n_sub * (RUN_ALIGN - 1)) // FFN_CHUNK) * FFN_CHUNK

    cnt = cnt[:, 0, :]
    pc = jnp.ceil(cnt / RUN_ALIGN) * RUN_ALIGN
    off = jnp.cumsum(pc, axis=1) - pc
    off_s = off.astype(jnp.int32).reshape(-1)
    pc_s = pc.astype(jnp.int32).reshape(-1)
    offr, cntr = off[:, None, :], cnt[:, None, :]
    offc, cntc = off[:, :, None], cnt[:, :, None]

    row_spec = pl.BlockSpec((n_sub, 1, n_exp), lambda i, g, *_: (i, 0, 0))
    col_spec = pl.BlockSpec((n_sub, n_exp, 1), lambda i, g, *_: (i, 0, 0))
    const = lambda shape: pl.BlockSpec(shape, lambda i, g, *_: (0,) * len(shape),
                                       pipeline_mode=pl.Buffered(1))
    grid_spec = pltpu.PrefetchScalarGridSpec(
        num_scalar_prefetch=2,
        grid=(n_tiles, n_exp // eg),
        in_specs=[pl.BlockSpec((tile, d), lambda i, g, *_: (i, 0), pipeline_mode=pl.Buffered(1)),
                  pl.BlockSpec((tile, n_exp), lambda i, g, *_: (i, 0)),
                  pl.BlockSpec((n_exp, tile), lambda i, g, *_: (0, i)),
                  row_spec, row_spec, col_spec, col_spec,
                  pl.BlockSpec((eg, d, f), lambda i, g, *_: (g, 0, 0)),
                  pl.BlockSpec((eg, d, f), lambda i, g, *_: (g, 0, 0)),
                  pl.BlockSpec((eg, f, d), lambda i, g, *_: (g, 0, 0)),
                  const((d, fs)), const((d, fs)), const((fs, d)),
                  const((1, d)), const((1, d))],
        out_specs=pl.BlockSpec((tile, d), lambda i, g, *_: (i, 0)),
        scratch_shapes=[pltpu.VMEM((n_sub, cap, d), BF16), pltpu.VMEM((xe_rows, d), BF16)])
    return pl.pallas_call(
        functools.partial(_moe_sparse_kernel, alpha, n_sub, cap, eg),
        grid_spec=grid_spec,
        out_shape=jax.ShapeDtypeStruct((total, d), F32),
        compiler_params=pltpu.CompilerParams(
            dimension_semantics=("arbitrary", "arbitrary"), vmem_limit_bytes=VMEM_LIMIT),
        name="moe_sparse",
    )(off_s, pc_s, x1, comb, combt, offr, cntr, offc, cntc,
      p['w_gate_bf'], p['w_up_bf'], p['w_down_bf'],
      p['ws_gate'].astype(BF16), p['ws_up'].astype(BF16), p['ws_down'].astype(BF16),
      p['ln2_g'], p['ln2_b'])


def _layer_params(l, w):
    g, n = w['a_re'].shape[1:]
    lr, li, bbr, bbi = _ssm_prep(w['a_re'][l], w['a_im'][l], w['log_dt'][l],
                                 w['ssm_b_re'][l], w['ssm_b_im'][l])
    bbd = jnp.concatenate([_block_diag(bbr, SSM_BLOCKS), _block_diag(bbi, SSM_BLOCKS)],
                          axis=-1).astype(BF16)
    c_re = w['ssm_c_re'][l].transpose(0, 2, 1)
    c_im = w['ssm_c_im'][l].transpose(0, 2, 1)
    cbd = jnp.concatenate([_block_diag(c_re, SSM_BLOCKS), _block_diag(-c_im, SSM_BLOCKS)],
                          axis=1).astype(BF16)
    row = lambda v: v.reshape(1, -1)
    return {
        'w_in': w['w_in'][l].astype(BF16), 'b_in': row(w['b_in'][l]),
        'conv_w': w['conv_w'][l], 'w_conv_out': w['w_conv_out'][l].astype(BF16),
        'lam_r': lr.reshape(1, g * n), 'lam_i': li.reshape(1, g * n),
        'bbd': bbd, 'cbd': cbd, 'ssm_d': row(w['ssm_d'][l]),
        'w_glu': w['w_glu'][l].astype(BF16), 'b_glu': row(w['b_glu'][l]),
        'w_ssm_out': w['w_ssm_out'][l].astype(BF16), 'w_o': w['w_o'][l].astype(BF16),
        'ln1_g': row(w['ln1_g'][l]), 'ln1_b': row(w['ln1_b'][l]),
        'w_router_t': w['w_router'][l].T, 'router_bias': w['router_bias'][l].reshape(-1, 1),
        'w_gate': w['w_gate'][l], 'w_up': w['w_up'][l], 'w_down': w['w_down'][l],
        'w_gate_bf': w['w_gate'][l].astype(BF16), 'w_up_bf': w['w_up'][l].astype(BF16),
        'w_down_bf': w['w_down'][l].astype(BF16),
        'ws_gate': w['ws_gate'][l], 'ws_up': w['ws_up'][l], 'ws_down': w['ws_down'][l],
        'ln2_g': row(w['ln2_g'][l]), 'ln2_b': row(w['ln2_b'][l]),
    }


def _pick_steps(batch, seq, max_rows):
    steps = max(1, min(seq, max_rows // batch))
    while seq % steps:
        steps -= 1
    return steps


def _pick_tile(total, max_tile):
    tile = min(total, max_tile)
    while total % tile or tile % 16:
        tile -= 16
    return tile


def _run_trunk(x, conv_st, re_st, im_st, params, alpha):
    bsz, seq, d = x.shape
    assert bsz % SUBLANES == 0
    steps = _pick_steps(bsz, seq, 512)
    tile = _pick_tile(bsz * seq, 1024)
    rows = x.transpose(1, 0, 2).reshape(seq * bsz, d)
    convs, res, ims = [], [], []
    for l, p in enumerate(params):
        kw = conv_st.shape[2]
        assert kw == 2
        cprev = conv_st[l].astype(F32).transpose(1, 0, 2).reshape(kw * bsz, d)
        h0r = re_st[l].astype(F32).reshape(bsz, -1)
        h0i = im_st[l].astype(F32).reshape(bsz, -1)
        if (bsz * steps) % SUB == 0 and (bsz * seq) % (DISPATCH_SUBTILES * SUB) == 0:
            x1, comb, cnew, hr, hi, combt, cnt = _mixer(
                rows, cprev, h0r, h0i, p, alpha=alpha, batch=bsz, steps=steps, sub=SUB)
            rows = _moe_sparse(x1, comb, combt, cnt, p, alpha=alpha, n_sub=DISPATCH_SUBTILES)
        else:
            x1, comb, cnew, hr, hi = _mixer(rows, cprev, h0r, h0i, p,
                                            alpha=alpha, batch=bsz, steps=steps)
            rows = _moe(x1, comb, p, alpha=alpha, tile=tile)
        convs.append(cnew.reshape(kw, bsz, d).transpose(1, 0, 2))
        res.append(hr.reshape(re_st.shape[1:]))
        ims.append(hi.reshape(im_st.shape[1:]))
    y = rows.reshape(seq, bsz, d).transpose(1, 0, 2)
    return y, jnp.stack(convs), jnp.stack(res), jnp.stack(ims)


def kernel(x_prompt, x_sample, state_conv, state_ssm_re, state_ssm_im,
           w_in, b_in, conv_w, w_conv_out, a_re, a_im, log_dt,
           ssm_b_re, ssm_b_im, ssm_c_re, ssm_c_im, ssm_d, w_glu, b_glu, w_ssm_out, w_o,
           ln1_g, ln1_b, w_router, router_bias, w_gate, w_up, w_down,
           ws_gate, ws_up, ws_down, ln2_g, ln2_b):
    w = dict(w_in=w_in, b_in=b_in, conv_w=conv_w, w_conv_out=w_conv_out,
             a_re=a_re, a_im=a_im, log_dt=log_dt,
             ssm_b_re=ssm_b_re, ssm_b_im=ssm_b_im, ssm_c_re=ssm_c_re, ssm_c_im=ssm_c_im,
             ssm_d=ssm_d, w_glu=w_glu, b_glu=b_glu, w_ssm_out=w_ssm_out, w_o=w_o,
             ln1_g=ln1_g, ln1_b=ln1_b, w_router=w_router, router_bias=router_bias,
             w_gate=w_gate, w_up=w_up, w_down=w_down,
             ws_gate=ws_gate, ws_up=ws_up, ws_down=ws_down, ln2_g=ln2_g, ln2_b=ln2_b)
    depth = w_in.shape[0]
    alpha = (2.0 * depth) ** 0.25
    params = [_layer_params(l, w) for l in range(depth)]
    bsz = x_prompt.shape[0]
    zero_conv = jnp.zeros((depth, bsz) + state_conv.shape[2:], x_prompt.dtype)
    zero_ssm = jnp.zeros((depth, bsz) + state_ssm_re.shape[2:], F32)
    y_p, conv_p, re_p, im_p = _run_trunk(x_prompt, zero_conv, zero_ssm, zero_ssm, params, alpha)
    y_s, conv_s, re_s, im_s = _run_trunk(x_sample, state_conv, state_ssm_re, state_ssm_im,
                                         params, alpha)
    return (y_p, y_s, conv_p, re_p, im_p, conv_s, re_s, im_s)
```

```python
import functools
import math

import jax
import jax.numpy as jnp
from jax import lax
from jax.experimental import pallas as pl
from jax.experimental.pallas import tpu as pltpu

F32 = jnp.float32
BF16 = jnp.bfloat16

LN_EPS = 1e-5
ROUTED_SCALE = 2.5
N_ROUTE_GROUPS = 8
TOPK_GROUPS = 4
TOP_K = 8

SUBLANES = 8
SSM_BLOCKS = 4
VMEM_LIMIT = 60 * 1024 * 1024


def _const_spec(shape):
    nd = len(shape)
    return pl.BlockSpec(shape, lambda *_: (0,) * nd, pipeline_mode=pl.Buffered(1))


def _ssm_prep_kernel(are_ref, aim_ref, ldt_ref, br_ref, bi_ref,
                     lr_ref, li_ref, bbr_ref, bbi_ref):
    dt = jnp.exp(ldt_ref[...])
    ar = are_ref[...]
    ai = aim_ref[...]
    mag = jnp.exp(ar * dt)
    lr = mag * jnp.cos(ai * dt)
    li = mag * jnp.sin(ai * dt)
    den = ar * ar + ai * ai
    fr = ((lr - 1.0) * ar + li * ai) / den
    fi = (li * ar - (lr - 1.0) * ai) / den
    lr_ref[...] = lr
    li_ref[...] = li
    br = br_ref[...]
    bi = bi_ref[...]
    bbr_ref[...] = fr * br - fi * bi
    bbi_ref[...] = fr * bi + fi * br


def _ssm_prep(a_re, a_im, log_dt, b_re, b_im):
    g, n = a_re.shape
    h = b_re.shape[-1]
    vec = jax.ShapeDtypeStruct((g, 1, n), F32)
    mat = jax.ShapeDtypeStruct((g, h, n), F32)
    return pl.pallas_call(
        _ssm_prep_kernel,
        out_shape=(vec, vec, mat, mat),
        name="ssm_prep",
    )(a_re.reshape(g, 1, n), a_im.reshape(g, 1, n), log_dt.reshape(g, 1, 1),
      b_re.transpose(0, 2, 1), b_im.transpose(0, 2, 1))


def _block_diag(m, nblk):
    g, p, q = m.shape
    gl = g // nblk
    eye = jnp.eye(gl, dtype=m.dtype)
    out = jnp.einsum('kapq,ab->kapbq', m.reshape(nblk, gl, p, q), eye)
    return out.reshape(nblk, gl * p, gl * q)


def _layer_norm(r, g, b):
    mu = jnp.mean(r, axis=-1, keepdims=True)
    d = r - mu
    var = jnp.mean(d * d, axis=-1, keepdims=True)
    return d * lax.rsqrt(var + LN_EPS) * g + b


def _route(scores, biased):
    n_exp, r = scores.shape
    gsz = n_exp // N_ROUTE_GROUPS
    neg = jnp.float32(-jnp.inf)
    rows = []
    for g in range(N_ROUTE_GROUPS):
        v = biased[g * gsz:(g + 1) * gsz, :]
        m1 = jnp.max(v, axis=0, keepdims=True)
        is_max = v == m1
        n_max = jnp.sum(is_max.astype(F32), axis=0, keepdims=True)
        rest = jnp.max(jnp.where(is_max, neg, v), axis=0, keepdims=True)
        rows.append(m1 + jnp.where(n_max >= 2.0, m1, rest))
    gscore = jnp.concatenate(rows, axis=0)
    gidx = lax.broadcasted_iota(jnp.int32, gscore.shape, 0)
    grank = jnp.zeros(gscore.shape, F32)
    for g in range(N_ROUTE_GROUPS):
        sg = gscore[g:g + 1, :]
        beats = (sg > gscore) | ((sg == gscore) & (gidx > g))
        grank = grank + beats.astype(F32)
    gkeep = grank < float(TOPK_GROUPS)
    masked = jnp.concatenate(
        [jnp.where(gkeep[g:g + 1, :], biased[g * gsz:(g + 1) * gsz, :], neg)
         for g in range(N_ROUTE_GROUPS)], axis=0)
    eidx = lax.broadcasted_iota(jnp.int32, masked.shape, 0)
    erank = jnp.zeros(masked.shape, F32)
    for e in range(n_exp):
        se = masked[e:e + 1, :]
        beats = (se > masked) | ((se == masked) & (eidx > e))
        erank = erank + beats.astype(F32)
    w = jnp.where(erank < float(TOP_K), scores, 0.0)
    return w / jnp.sum(w, axis=0, keepdims=True) * ROUTED_SCALE


def _mixer_kernel(alpha, batch, steps, sub,
                  x_ref, cprev_ref, h0r_ref, h0i_ref,
                  win_ref, bin_ref, convw_ref, wco_ref,
                  lamr_ref, lami_ref, bbd_ref, cbd_ref, dskip_ref,
                  wglu_ref, bglu_ref, wso_ref, wo_ref, ln1g_ref, ln1b_ref,
                  wrt_ref, rbias_ref,
                  x1_ref, comb_ref, cnew_ref, hr_ref, hi_ref, *rest):
    if sub:
        combt_ref, cnt_ref, ubuf, xk_ref, ys_ref = rest
    else:
        ubuf, xk_ref, ys_ref = rest
    rows = batch * steps
    d = x_ref.shape[1]
    d_blk = d // SSM_BLOCKS
    n_blk = lamr_ref.shape[1] // SSM_BLOCKS
    i = pl.program_id(0)

    @pl.when(i == 0)
    def _init():
        ubuf[0:2 * batch, :] = cprev_ref[...]
        hr_ref[...] = h0r_ref[...]
        hi_ref[...] = h0i_ref[...]

    x = x_ref[...]
    xb = x.astype(BF16)

    def proj(c):
        cols = slice(c * d, (c + 1) * d)
        return (jnp.dot(xb, win_ref[:, cols], preferred_element_type=F32)
                + bin_ref[:, cols])

    u = proj(1) * proj(2)
    ubuf[2 * batch:2 * batch + rows, :] = u
    conv = (convw_ref[0:1, :] * ubuf[0:rows, :]
            + convw_ref[1:2, :] * ubuf[batch:batch + rows, :]
            + convw_ref[2:3, :] * u)
    ya = jnp.dot((proj(0) * conv).astype(BF16), wco_ref[...],
                 preferred_element_type=F32)
    tail = ubuf[rows:rows + 2 * batch, :]
    ubuf[0:2 * batch, :] = tail
    cnew_ref[...] = tail

    us = proj(3)
    half = n_blk // 2
    for k in range(SSM_BLOCKS):
        usk = us[:, k * d_blk:(k + 1) * d_blk].astype(BF16)
        xk_ref[...] = jnp.dot(usk, bbd_ref[k], preferred_element_type=F32)
        for hf in range(2):
            st = slice(k * n_blk + hf * half, k * n_blk + (hf + 1) * half)
            re = slice(hf * half, (hf + 1) * half)
            im = slice(n_blk + hf * half, n_blk + (hf + 1) * half)
            if steps == 1:
                lr = lamr_ref[:, st]
                li = lami_ref[:, st]
                hr = hr_ref[:, st]
                hi = hi_ref[:, st]
                nhr = lr * hr - li * hi + xk_ref[:, re]
                nhi = lr * hi + li * hr + xk_ref[:, im]
                xk_ref[:, re] = nhr
                xk_ref[:, im] = nhi
                hr_ref[:, st] = nhr
                hi_ref[:, st] = nhi
            else:
                lr = jnp.broadcast_to(lamr_ref[:, st], (SUBLANES, half))
                li = jnp.broadcast_to(lami_ref[:, st], (SUBLANES, half))
                for s in range(batch // SUBLANES):
                    grp = slice(s * SUBLANES, (s + 1) * SUBLANES)

                    def step(t, carry, s=s, re=re, im=im, lr=lr, li=li):
                        hr, hi = carry
                        row = pl.multiple_of(t * batch + s * SUBLANES, SUBLANES)
                        rs = pl.ds(row, SUBLANES)
                        nhr = lr * hr - li * hi + xk_ref[rs, re]
                        nhi = lr * hi + li * hr + xk_ref[rs, im]
                        xk_ref[rs, re] = nhr
                        xk_ref[rs, im] = nhi
                        return nhr, nhi

                    hr, hi = lax.fori_loop(0, steps, step,
                                           (hr_ref[grp, st], hi_ref[grp, st]))
                    hr_ref[grp, st] = hr
                    hi_ref[grp, st] = hi
        ys_ref[:, k * d_blk:(k + 1) * d_blk] = jnp.dot(
            xk_ref[...].astype(BF16), cbd_ref[k], preferred_element_type=F32)
    ys = ys_ref[...] + dskip_ref[...] * us
    z = jax.nn.gelu(ys)
    gate = jnp.dot(z.astype(BF16), wglu_ref[...], preferred_element_type=F32) + bglu_ref[...]
    glu = z * jax.nn.sigmoid(gate)
    yb = jnp.dot(glu.astype(BF16), wso_ref[...], preferred_element_type=F32)

    m = jax.nn.sigmoid(proj(4)) * ya + jax.nn.sigmoid(proj(5)) * yb
    o = jnp.dot(m.astype(BF16), wo_ref[...], preferred_element_type=F32)
    x1 = _layer_norm(alpha * x + o, ln1g_ref[...], ln1b_ref[...])
    x1_ref[...] = x1

    logits = lax.dot_general(wrt_ref[...], x1, (((1,), (1,)), ((), ())),
                             precision=lax.Precision.HIGHEST,
                             preferred_element_type=F32)
    scores = jax.nn.sigmoid(logits)
    comb = _route(scores, scores + rbias_ref[...])
    comb_ref[...] = comb.T
    if sub:
        combt_ref[...] = comb
        ones = jnp.ones((SUBLANES, sub), BF16)
        for s in range(rows // sub):
            sel = jnp.where(comb[:, s * sub:(s + 1) * sub] != 0.0, 1.0, 0.0).astype(BF16)
            cnt_ref[s] = lax.dot_general(ones, sel, (((1,), (1,)), ((), ())),
                                         preferred_element_type=F32)


def _mixer(x_rows, cprev, h0r, h0i, p, *, alpha, batch, steps, sub=0):
    total, d = x_rows.shape
    rows = batch * steps
    assert not sub or rows % sub == 0
    n_state = h0r.shape[1]
    n_exp = p['w_router_t'].shape[0]
    grid = (total // rows,)
    consts = [p['w_in'], p['b_in'], p['conv_w'], p['w_conv_out'],
              p['lam_r'], p['lam_i'], p['bbd'], p['cbd'], p['ssm_d'],
              p['w_glu'], p['b_glu'], p['w_ssm_out'], p['w_o'], p['ln1_g'], p['ln1_b'],
              p['w_router_t'], p['router_bias']]
    in_specs = ([pl.BlockSpec((rows, d), lambda i: (i, 0)),
                 _const_spec(cprev.shape), _const_spec(h0r.shape), _const_spec(h0i.shape)]
                + [_const_spec(c.shape) for c in consts])
    out_shape = (jax.ShapeDtypeStruct((total, d), F32),
                 jax.ShapeDtypeStruct((total, n_exp), F32),
                 jax.ShapeDtypeStruct((2 * batch, d), F32),
                 jax.ShapeDtypeStruct((batch, n_state), F32),
                 jax.ShapeDtypeStruct((batch, n_state), F32))
    out_specs = (pl.BlockSpec((rows, d), lambda i: (i, 0)),
                 pl.BlockSpec((rows, n_exp), lambda i: (i, 0)),
                 pl.BlockSpec((2 * batch, d), lambda i: (0, 0)),
                 pl.BlockSpec((batch, n_state), lambda i: (0, 0)),
                 pl.BlockSpec((batch, n_state), lambda i: (0, 0)))
    if sub:
        out_shape += (jax.ShapeDtypeStruct((n_exp, total), F32),
                      jax.ShapeDtypeStruct((total // sub, SUBLANES, n_exp), F32))
        out_specs += (pl.BlockSpec((n_exp, rows), lambda i: (0, i)),
                      pl.BlockSpec((rows // sub, SUBLANES, n_exp), lambda i: (i, 0, 0)))
    scratch = [pltpu.VMEM((rows + 2 * batch, d), F32),
               pltpu.VMEM((rows, 2 * n_state // SSM_BLOCKS), F32),
               pltpu.VMEM((rows, d), F32)]
    return pl.pallas_call(
        functools.partial(_mixer_kernel, alpha, batch, steps, sub),
        grid=grid, in_specs=in_specs, out_specs=out_specs, out_shape=out_shape,
        scratch_shapes=scratch,
        compiler_params=pltpu.CompilerParams(
            dimension_semantics=("arbitrary",), vmem_limit_bytes=VMEM_LIMIT),
        name="mixer",
    )(x_rows, cprev, h0r, h0i, *consts)


def _swiglu(xb, wg, wu):
    g = jnp.dot(xb, wg.astype(BF16), preferred_element_type=F32)
    u = jnp.dot(xb, wu.astype(BF16), preferred_element_type=F32)
    return jax.nn.silu(g) * u


def _moe_kernel(alpha, x1_ref, comb_ref, wg_ref, wu_ref, wd_ref,
                wsg_ref, wsu_ref, wsd_ref, g_ref, b_ref, out_ref, xb_ref, acc_ref):
    e = pl.program_id(1)

    @pl.when(e == 0)
    def _shared():
        xb = x1_ref[...].astype(BF16)
        xb_ref[...] = xb
        hs = _swiglu(xb, wsg_ref[...], wsu_ref[...])
        acc_ref[...] = jnp.dot(hs.astype(BF16), wsd_ref[...].astype(BF16),
                               preferred_element_type=F32)

    comb = comb_ref[...]
    lane = lax.broadcasted_iota(jnp.int32, comb.shape, 1)
    c = jnp.sum(jnp.where(lane == e, comb, 0.0), axis=1, keepdims=True)
    h = _swiglu(xb_ref[...], wg_ref[...], wu_ref[...]) * c
    acc_ref[...] += jnp.dot(h.astype(BF16), wd_ref[...].astype(BF16),
                            preferred_element_type=F32)

    @pl.when(e == pl.num_programs(1) - 1)
    def _finish():
        out_ref[...] = _layer_norm(alpha * x1_ref[...] + acc_ref[...],
                                   g_ref[...], b_ref[...])


def _moe(x1, comb, p, *, alpha, tile):
    total, d = x1.shape
    n_exp, _, f = p['w_gate'].shape
    fs = p['ws_gate'].shape[1]
    grid = (total // tile, n_exp)
    in_specs = [pl.BlockSpec((tile, d), lambda i, e: (i, 0)),
                pl.BlockSpec((tile, n_exp), lambda i, e: (i, 0)),
                pl.BlockSpec((None, d, f), lambda i, e: (e, 0, 0)),
                pl.BlockSpec((None, d, f), lambda i, e: (e, 0, 0)),
                pl.BlockSpec((None, f, d), lambda i, e: (e, 0, 0)),
                _const_spec((d, fs)), _const_spec((d, fs)), _const_spec((fs, d)),
                _const_spec((1, d)), _const_spec((1, d))]
    return pl.pallas_call(
        functools.partial(_moe_kernel, alpha),
        grid=grid, in_specs=in_specs,
        out_specs=pl.BlockSpec((tile, d), lambda i, e: (i, 0)),
        out_shape=jax.ShapeDtypeStruct((total, d), F32),
        scratch_shapes=[pltpu.VMEM((tile, d), BF16), pltpu.VMEM((tile, d), F32)],
        compiler_params=pltpu.CompilerParams(
            dimension_semantics=("arbitrary", "arbitrary"), vmem_limit_bytes=VMEM_LIMIT),
        name="moe",
    )(x1, comb, p['w_gate'], p['w_up'], p['w_down'],
      p['ws_gate'], p['ws_up'], p['ws_down'], p['ln2_g'], p['ln2_b'])


SUB = 256
RUN_ALIGN = 16
FFN_CHUNK = 192
SEL_CHUNK = 512
PACK_STATIC = 4
PACK_FULL = 2
DISPATCH_SUBTILES = 4
EXPERTS_PER_STEP = 4


def _expert_onehot(r, off, cnt):
    hit = (r >= off) & (r < off + cnt)
    return hit, jnp.where(hit, off + 1.0, 0.0)


def _moe_sparse_kernel(alpha, n_sub, cap, eg,
                       off_s, pc_s,
                       x1_ref, comb_ref, combt_ref, offr_ref, cntr_ref, offc_ref, cntc_ref,
                       wg_ref, wu_ref, wd_ref, wsg_ref, wsu_ref, wsd_ref, g_ref, b_ref,
                       out_ref, gbuf, xe):
    i = pl.program_id(0)
    g = pl.program_id(1)
    n_exp = comb_ref.shape[1]
    tri_r = lax.broadcasted_iota(jnp.int32, (SUB, SUB), 0)
    tri_c = lax.broadcasted_iota(jnp.int32, (SUB, SUB), 1)

    def run(j, e):
        idx = (i * n_sub + j) * n_exp + e
        return off_s[idx], pc_s[idx]

    def used_rows(j):
        o, p = run(j, n_exp - 1)
        return o + p

    def rows16(start):
        return pl.ds(pl.multiple_of(start, RUN_ALIGN), RUN_ALIGN)

    @pl.when((i == 0) & (g == 0))
    def _zero():
        xe[...] = jnp.zeros(xe.shape, xe.dtype)

    @pl.when(g == 0)
    def _dispatch():
        before = jnp.where(tri_r < tri_c, 1.0, 0.0).astype(BF16)
        for j in range(n_sub):
            tok = slice(j * SUB, (j + 1) * SUB)
            xj = x1_ref[tok, :].astype(BF16)
            sel = combt_ref[:, tok] != 0.0
            pos = jnp.dot(jnp.where(sel, 1.0, 0.0).astype(BF16), before,
                          preferred_element_type=F32)
            posm = jnp.where(sel, pos, -1.0).astype(BF16)
            off = offr_ref[j]
            cnt = cntr_ref[j]
            used = used_rows(j)

            def sort_rows(rc, j=j, xj=xj, posm=posm, off=off, cnt=cnt):
                r = (lax.broadcasted_iota(jnp.int32, (SEL_CHUNK, 1), 0)
                     + rc * SEL_CHUNK).astype(F32)
                hit, start1 = _expert_onehot(r, off, cnt)
                s = jnp.sum(start1, axis=1, keepdims=True)
                q = jnp.where(s > 0.0, r - (s - 1.0), -2.0)
                rank = jnp.dot(jnp.where(hit, 1.0, 0.0).astype(BF16), posm,
                               preferred_element_type=F32)
                pick = jnp.where(rank == q, 1.0, 0.0).astype(BF16)
                gbuf[j, rc * SEL_CHUNK:(rc + 1) * SEL_CHUNK, :] = jnp.dot(
                    pick, xj, preferred_element_type=F32).astype(BF16)

            last = cap // SEL_CHUNK - 1
            for rc in range(last):
                sort_rows(rc)
            pl.when(last * SEL_CHUNK < used)(functools.partial(sort_rows, last))

            @pl.when(last * SEL_CHUNK >= used)
            def _blank(j=j):
                gbuf[j, last * SEL_CHUNK:, :] = jnp.zeros((SEL_CHUNK, gbuf.shape[2]), BF16)

    def ffn_rows(rs, ee):
        xc = xe[rs, :]
        h = (jax.nn.silu(jnp.dot(xc, wg_ref[ee], preferred_element_type=F32))
             * jnp.dot(xc, wu_ref[ee], preferred_element_type=F32))
        xe[rs, :] = jnp.dot(h.astype(BF16), wd_ref[ee], preferred_element_type=F32).astype(BF16)

    def pack(e, base, static):
        n = jnp.int32(base)
        for j in range(n_sub):
            o, p = run(j, e)
            if static:
                for c in range(PACK_STATIC):
                    xe[rows16(n + c * RUN_ALIGN), :] = gbuf[j, rows16(o + c * RUN_ALIGN), :]
            else:
                def chunk(c, carry, j=j, o=o, n=n):
                    xe[rows16(n + c * RUN_ALIGN), :] = gbuf[j, rows16(o + c * RUN_ALIGN), :]
                    return carry

                lax.fori_loop(0, lax.div(p, RUN_ALIGN), chunk, 0)
            n = n + p
        return n - base

    def unpack(e, base, static, keep_next=True):
        n = jnp.int32(base)
        for j in range(n_sub):
            o, p = run(j, e)
            if static:
                for c in range(PACK_STATIC):
                    dst = rows16(o + c * RUN_ALIGN)
                    rows = xe[rows16(n + c * RUN_ALIGN), :]
                    if keep_next and c >= PACK_FULL:
                        rows = jnp.where(c * RUN_ALIGN < p, rows, gbuf[j, dst, :])
                    gbuf[j, dst, :] = rows
            else:
                def chunk(c, carry, j=j, o=o, n=n):
                    gbuf[j, rows16(o + c * RUN_ALIGN), :] = xe[rows16(n + c * RUN_ALIGN), :]
                    return carry

                lax.fori_loop(0, lax.div(p, RUN_ALIGN), chunk, 0)
            n = n + p

    usual = jnp.bool_(True)
    for ee in range(eg):
        n = jnp.int32(0)
        for j in range(n_sub):
            p = run(j, g * eg + ee)[1]
            usual = usual & (p >= PACK_FULL * RUN_ALIGN) & (p <= PACK_STATIC * RUN_ALIGN)
            n = n + p
        usual = usual & (n <= FFN_CHUNK)

    @pl.when(usual)
    def _together():
        for ee in range(eg):
            pack(g * eg + ee, ee * FFN_CHUNK, True)
        for ee in range(eg):
            ffn_rows(slice(ee * FFN_CHUNK, (ee + 1) * FFN_CHUNK), ee)
        for ee in range(eg):
            unpack(g * eg + ee, ee * FFN_CHUNK, True, keep_next=ee == eg - 1)

    @pl.when(jnp.logical_not(usual))
    def _one_by_one():
        for ee in range(eg):
            e = g * eg + ee
            n = pack(e, 0, False)

            def ffn(k, carry, ee=ee):
                ffn_rows(pl.ds(pl.multiple_of(k * FFN_CHUNK, RUN_ALIGN), FFN_CHUNK), ee)
                return carry

            lax.fori_loop(0, lax.div(n + (FFN_CHUNK - 1), FFN_CHUNK), ffn, 0)
            unpack(e, 0, False)

    @pl.when(g == pl.num_programs(1) - 1)
    def _combine():
        earlier = jnp.where(tri_c < tri_r, 1.0, 0.0).astype(BF16)
        for j in range(n_sub):
            tok = slice(j * SUB, (j + 1) * SUB)
            x = x1_ref[tok, :]
            xb = x.astype(BF16)
            hs = (jax.nn.silu(jnp.dot(xb, wsg_ref[...], preferred_element_type=F32))
                  * jnp.dot(xb, wsu_ref[...], preferred_element_type=F32))
            y = jnp.dot(hs.astype(BF16), wsd_ref[...], preferred_element_type=F32)
            comb = comb_ref[tok, :]
            sel = comb != 0.0
            pos = jnp.dot(earlier, jnp.where(sel, 1.0, 0.0).astype(BF16),
                          preferred_element_type=F32)
            posm = jnp.where(sel, pos, -1.0).astype(BF16)
            c_hi = comb.astype(BF16)
            c_lo = (comb - c_hi.astype(F32)).astype(BF16)
            off = offc_ref[j]
            cnt = cntc_ref[j]
            used = used_rows(j)

            def gather_rows(rc, j=j, posm=posm, c_hi=c_hi, c_lo=c_lo, off=off, cnt=cnt):
                r = (lax.broadcasted_iota(jnp.int32, (1, SEL_CHUNK), 1)
                     + rc * SEL_CHUNK).astype(F32)
                hit, start1 = _expert_onehot(r, off, cnt)
                s = jnp.sum(start1, axis=0, keepdims=True)
                q = jnp.where(s > 0.0, r - (s - 1.0), -2.0)
                hb = jnp.where(hit, 1.0, 0.0).astype(BF16)
                pick = jnp.dot(posm, hb, preferred_element_type=F32) == q
                w_hi = jnp.where(pick, jnp.dot(c_hi, hb, preferred_element_type=F32), 0.0)
                w_lo = jnp.where(pick, jnp.dot(c_lo, hb, preferred_element_type=F32), 0.0)
                rows = gbuf[j, rc * SEL_CHUNK:(rc + 1) * SEL_CHUNK, :]
                return (jnp.dot(w_hi.astype(BF16), rows, preferred_element_type=F32)
                        + jnp.dot(w_lo.astype(BF16), rows, preferred_element_type=F32))

            last = cap // SEL_CHUNK - 1
            for rc in range(last):
                y = y + gather_rows(rc)
            out_ref[tok, :] = alpha * x + y

            @pl.when(last * SEL_CHUNK < used)
            def _tail(gather_rows=gather_rows, tok=tok):
                out_ref[tok, :] += gather_rows(last)

            out_ref[tok, :] = _layer_norm(out_ref[tok, :], g_ref[...], b_ref[...])


def _moe_sparse(x1, comb, combt, cnt, p, *, alpha, n_sub, eg=EXPERTS_PER_STEP):
    total, d = x1.shape
    n_exp, _, f = p['w_gate'].shape
    fs = p['ws_gate'].shape[1]
    tile = n_sub * SUB
    n_tiles = total // tile
    cap = -(-(SUB * TOP_K + n_exp * (RUN_ALIGN - 1)) // SEL_CHUNK) * SEL_CHUNK
    xe_rows = -(-(tile + n_sub * (RUN_ALIGN - 1)) // FFN_CHUNK) * FFN_CHUNK

    cnt = cnt[:, 0, :]
    pc = jnp.ceil(cnt / RUN_ALIGN) * RUN_ALIGN
    off = jnp.cumsum(pc, axis=1) - pc
    off_s = off.astype(jnp.int32).reshape(-1)
    pc_s = pc.astype(jnp.int32).reshape(-1)
    offr, cntr = off[:, None, :], cnt[:, None, :]
    offc, cntc = off[:, :, None], cnt[:, :, None]

    row_spec = pl.BlockSpec((n_sub, 1, n_exp), lambda i, g, *_: (i, 0, 0))
    col_spec = pl.BlockSpec((n_sub, n_exp, 1), lambda i, g, *_: (i, 0, 0))
    const = lambda shape: pl.BlockSpec(shape, lambda i, g, *_: (0,) * len(shape),
                                       pipeline_mode=pl.Buffered(1))
    grid_spec = pltpu.PrefetchScalarGridSpec(
        num_scalar_prefetch=2,
        grid=(n_tiles, n_exp // eg),
        in_specs=[pl.BlockSpec((tile, d), lambda i, g, *_: (i, 0), pipeline_mode=pl.Buffered(1)),
                  pl.BlockSpec((tile, n_exp), lambda i, g, *_: (i, 0)),
                  pl.BlockSpec((n_exp, tile), lambda i, g, *_: (0, i)),
                  row_spec, row_spec, col_spec, col_spec,
                  pl.BlockSpec((eg, d, f), lambda i, g, *_: (g, 0, 0)),
                  pl.BlockSpec((eg, d, f), lambda i, g, *_: (g, 0, 0)),
                  pl.BlockSpec((eg, f, d), lambda i, g, *_: (g, 0, 0)),
                  const((d, fs)), const((d, fs)), const((fs, d)),
                  const((1, d)), const((1, d))],
        out_specs=pl.BlockSpec((tile, d), lambda i, g, *_: (i, 0)),
        scratch_shapes=[pltpu.VMEM((n_sub, cap, d), BF16), pltpu.VMEM((xe_rows, d), BF16)])
    return pl.pallas_call(
        functools.partial(_moe_sparse_kernel, alpha, n_sub, cap, eg),
        grid_spec=grid_spec,
        out_shape=jax.ShapeDtypeStruct((total, d), F32),
        compiler_params=pltpu.CompilerParams(
            dimension_semantics=("arbitrary", "arbitrary"), vmem_limit_bytes=VMEM_LIMIT),
        name="moe_sparse",
    )(off_s, pc_s, x1, comb, combt, offr, cntr, offc, cntc,
      p['w_gate_bf'], p['w_up_bf'], p['w_down_bf'],
      p['ws_gate'].astype(BF16), p['ws_up'].astype(BF16), p['ws_down'].astype(BF16),
      p['ln2_g'], p['ln2_b'])


def _layer_params(l, w):
    g, n = w['a_re'].shape[1:]
    lr, li, bbr, bbi = _ssm_prep(w['a_re'][l], w['a_im'][l], w['log_dt'][l],
                                 w['ssm_b_re'][l], w['ssm_b_im'][l])
    bbd = jnp.concatenate([_block_diag(bbr, SSM_BLOCKS), _block_diag(bbi, SSM_BLOCKS)],
                          axis=-1).astype(BF16)
    c_re = w['ssm_c_re'][l].transpose(0, 2, 1)
    c_im = w['ssm_c_im'][l].transpose(0, 2, 1)
    cbd = jnp.concatenate([_block_diag(c_re, SSM_BLOCKS), _block_diag(-c_im, SSM_BLOCKS)],
                          axis=1).astype(BF16)
    row = lambda v: v.reshape(1, -1)
    return {
        'w_in': w['w_in'][l].astype(BF16), 'b_in': row(w['b_in'][l]),
        'conv_w': w['conv_w'][l], 'w_conv_out': w['w_conv_out'][l].astype(BF16),
        'lam_r': lr.reshape(1, g * n), 'lam_i': li.reshape(1, g * n),
        'bbd': bbd, 'cbd': cbd, 'ssm_d': row(w['ssm_d'][l]),
        'w_glu': w['w_glu'][l].astype(BF16), 'b_glu': row(w['b_glu'][l]),
        'w_ssm_out': w['w_ssm_out'][l].astype(BF16), 'w_o': w['w_o'][l].astype(BF16),
        'ln1_g': row(w['ln1_g'][l]), 'ln1_b': row(w['ln1_b'][l]),
        'w_router_t': w['w_router'][l].T, 'router_bias': w['router_bias'][l].reshape(-1, 1),
        'w_gate': w['w_gate'][l], 'w_up': w['w_up'][l], 'w_down': w['w_down'][l],
        'w_gate_bf': w['w_gate'][l].astype(BF16), 'w_up_bf': w['w_up'][l].astype(BF16),
        'w_down_bf': w['w_down'][l].astype(BF16),
        'ws_gate': w['ws_gate'][l], 'ws_up': w['ws_up'][l], 'ws_down': w['ws_down'][l],
        'ln2_g': row(w['ln2_g'][l]), 'ln2_b': row(w['ln2_b'][l]),
    }


def _pick_steps(batch, seq, max_rows):
    steps = max(1, min(seq, max_rows // batch))
    while seq % steps:
        steps -= 1
    return steps


def _pick_tile(total, max_tile):
    tile = min(total, max_tile)
    while total % tile or tile % 16:
        tile -= 16
    return tile


def _run_trunk(x, conv_st, re_st, im_st, params, alpha):
    bsz, seq, d = x.shape
    assert bsz % SUBLANES == 0
    steps = _pick_steps(bsz, seq, 512)
    tile = _pick_tile(bsz * seq, 1024)
    rows = x.transpose(1, 0, 2).reshape(seq * bsz, d)
    convs, res, ims = [], [], []
    for l, p in enumerate(params):
        kw = conv_st.shape[2]
        assert kw == 2
        cprev = conv_st[l].astype(F32).transpose(1, 0, 2).reshape(kw * bsz, d)
        h0r = re_st[l].astype(F32).reshape(bsz, -1)
        h0i = im_st[l].astype(F32).reshape(bsz, -1)
        if (bsz * steps) % SUB == 0 and (bsz * seq) % (DISPATCH_SUBTILES * SUB) == 0:
            x1, comb, cnew, hr, hi, combt, cnt = _mixer(
                rows, cprev, h0r, h0i, p, alpha=alpha, batch=bsz, steps=steps, sub=SUB)
            rows = _moe_sparse(x1, comb, combt, cnt, p, alpha=alpha, n_sub=DISPATCH_SUBTILES)
        else:
            x1, comb, cnew, hr, hi = _mixer(rows, cprev, h0r, h0i, p,
                                            alpha=alpha, batch=bsz, steps=steps)
            rows = _moe(x1, comb, p, alpha=alpha, tile=tile)
        convs.append(cnew.reshape(kw, bsz, d).transpose(1, 0, 2))
        res.append(hr.reshape(re_st.shape[1:]))
        ims.append(hi.reshape(im_st.shape[1:]))
    y = rows.reshape(seq, bsz, d).transpose(1, 0, 2)
    return y, jnp.stack(convs), jnp.stack(res), jnp.stack(ims)


def kernel(x_prompt, x_sample, state_conv, state_ssm_re, state_ssm_im,
           w_in, b_in, conv_w, w_conv_out, a_re, a_im, log_dt,
           ssm_b_re, ssm_b_im, ssm_c_re, ssm_c_im, ssm_d, w_glu, b_glu, w_ssm_out, w_o,
           ln1_g, ln1_b, w_router, router_bias, w_gate, w_up, w_down,
           ws_gate, ws_up, ws_down, ln2_g, ln2_b):
    w = dict(w_in=w_in, b_in=b_in, conv_w=conv_w, w_conv_out=w_conv_out,
             a_re=a_re, a_im=a_im, log_dt=log_dt,
             ssm_b_re=ssm_b_re, ssm_b_im=ssm_b_im, ssm_c_re=ssm_c_re, ssm_c_im=ssm_c_im,
             ssm_d=ssm_d, w_glu=w_glu, b_glu=b_glu, w_ssm_out=w_ssm_out, w_o=w_o,
             ln1_g=ln1_g, ln1_b=ln1_b, w_router=w_router, router_bias=router_bias,
             w_gate=w_gate, w_up=w_up, w_down=w_down,
             ws_gate=ws_gate, ws_up=ws_up, ws_down=ws_down, ln2_g=ln2_g, ln2_b=ln2_b)
    depth = w_in.shape[0]
    alpha = (2.0 * depth) ** 0.25
    params = [_layer_params(l, w) for l in range(depth)]
    bsz = x_prompt.shape[0]
    zero_conv = jnp.zeros((depth, bsz) + state_conv.shape[2:], x_prompt.dtype)
    zero_ssm = jnp.zeros((depth, bsz) + state_ssm_re.shape[2:], F32)
    y_p, conv_p, re_p, im_p = _run_trunk(x_prompt, zero_conv, zero_ssm, zero_ssm, params, alpha)
    y_s, conv_s, re_s, im_s = _run_trunk(x_sample, state_conv, state_ssm_re, state_ssm_im,
                                         params, alpha)
    return (y_p, y_s, conv_p, re_p, im_p, conv_s, re_s, im_s)
```

```python
import functools
import math

import jax
import jax.numpy as jnp
from jax import lax
from jax.experimental import pallas as pl
from jax.experimental.pallas import tpu as pltpu

F32 = jnp.float32
BF16 = jnp.bfloat16

LN_EPS = 1e-5
ROUTED_SCALE = 2.5
N_ROUTE_GROUPS = 8
TOPK_GROUPS = 4
TOP_K = 8

SUBLANES = 8
LANES = 128
SSM_BLOCKS = 4
VMEM_LIMIT = 60 * 1024 * 1024


def _const_spec(shape):
    nd = len(shape)
    return pl.BlockSpec(shape, lambda *_: (0,) * nd, pipeline_mode=pl.Buffered(1))


def _ssm_prep_kernel(are_ref, aim_ref, ldt_ref, br_ref, bi_ref,
                     lr_ref, li_ref, bbr_ref, bbi_ref):
    dt = jnp.exp(ldt_ref[...])
    ar = are_ref[...]
    ai = aim_ref[...]
    mag = jnp.exp(ar * dt)
    lr = mag * jnp.cos(ai * dt)
    li = mag * jnp.sin(ai * dt)
    den = ar * ar + ai * ai
    fr = ((lr - 1.0) * ar + li * ai) / den
    fi = (li * ar - (lr - 1.0) * ai) / den
    lr_ref[...] = lr
    li_ref[...] = li
    br = br_ref[...]
    bi = bi_ref[...]
    bbr_ref[...] = fr * br - fi * bi
    bbi_ref[...] = fr * bi + fi * br


def _ssm_prep(a_re, a_im, log_dt, b_re, b_im):
    g, n = a_re.shape
    h = b_re.shape[-1]
    vec = jax.ShapeDtypeStruct((g, 1, n), F32)
    mat = jax.ShapeDtypeStruct((g, h, n), F32)
    return pl.pallas_call(
        _ssm_prep_kernel,
        out_shape=(vec, vec, mat, mat),
        name="ssm_prep",
    )(a_re.reshape(g, 1, n), a_im.reshape(g, 1, n), log_dt.reshape(g, 1, 1),
      b_re.transpose(0, 2, 1), b_im.transpose(0, 2, 1))


def _block_diag(m, nblk):
    g, p, q = m.shape
    gl = g // nblk
    eye = jnp.eye(gl, dtype=m.dtype)
    out = jnp.einsum('kapq,ab->kapbq', m.reshape(nblk, gl, p, q), eye)
    return out.reshape(nblk, gl * p, gl * q)


def _layer_norm(r, g, b):
    mu = jnp.mean(r, axis=-1, keepdims=True)
    d = r - mu
    var = jnp.mean(d * d, axis=-1, keepdims=True)
    return d * lax.rsqrt(var + LN_EPS) * g + b


def _route(scores, biased):
    n_exp, r = scores.shape
    gsz = n_exp // N_ROUTE_GROUPS
    neg = jnp.float32(-jnp.inf)
    rows = []
    for g in range(N_ROUTE_GROUPS):
        v = biased[g * gsz:(g + 1) * gsz, :]
        m1 = jnp.max(v, axis=0, keepdims=True)
        is_max = v == m1
        n_max = jnp.sum(is_max.astype(F32), axis=0, keepdims=True)
        rest = jnp.max(jnp.where(is_max, neg, v), axis=0, keepdims=True)
        rows.append(m1 + jnp.where(n_max >= 2.0, m1, rest))
    gscore = jnp.concatenate(rows, axis=0)
    gidx = lax.broadcasted_iota(jnp.int32, gscore.shape, 0)
    grank = jnp.zeros(gscore.shape, F32)
    for g in range(N_ROUTE_GROUPS):
        sg = gscore[g:g + 1, :]
        beats = (sg > gscore) | ((sg == gscore) & (gidx > g))
        grank = grank + beats.astype(F32)
    gkeep = grank < float(TOPK_GROUPS)
    masked = jnp.concatenate(
        [jnp.where(gkeep[g:g + 1, :], biased[g * gsz:(g + 1) * gsz, :], neg)
         for g in range(N_ROUTE_GROUPS)], axis=0)
    eidx = lax.broadcasted_iota(jnp.int32, masked.shape, 0)
    erank = jnp.zeros(masked.shape, F32)
    for e in range(n_exp):
        se = masked[e:e + 1, :]
        beats = (se > masked) | ((se == masked) & (eidx > e))
        erank = erank + beats.astype(F32)
    w = jnp.where(erank < float(TOP_K), scores, 0.0)
    return w / jnp.sum(w, axis=0, keepdims=True) * ROUTED_SCALE


def _mixer_kernel(alpha, batch, steps, sub,
                  x_ref, cprev_ref, h0r_ref, h0i_ref,
                  win_ref, bin_ref, convw_ref, wco_ref,
                  lamr_ref, lami_ref, bbd_ref, cbd_ref, dskip_ref,
                  wglu_ref, bglu_ref, wso_ref, wo_ref, ln1g_ref, ln1b_ref,
                  wrt_ref, rbias_ref,
                  x1_ref, comb_ref, cnew_ref, hr_ref, hi_ref, *rest):
    if sub:
        combt_ref, cnt_ref, *rest = rest
    ubuf, xk_ref, ys_ref, *rest = rest
    rows = batch * steps
    d = x_ref.shape[-1]
    d_blk = d // SSM_BLOCKS
    n_blk = lamr_ref.shape[1] // SSM_BLOCKS
    i = pl.program_id(0)

    @pl.when(i == 0)
    def _init():
        ubuf[0:2 * batch, :] = cprev_ref[...]
        hr_ref[...] = h0r_ref[...]
        hi_ref[...] = h0i_ref[...]

    if len(x_ref.shape) == 3:
        xs_ref, = rest
        for b in range(batch):
            for c in range(d // LANES):
                xs_ref[c, pl.ds(b, steps, stride=batch), :] = x_ref[b, :, c * LANES:(c + 1) * LANES]
        x = jnp.concatenate([xs_ref[c] for c in range(d // LANES)], axis=1)
    else:
        x = x_ref[...]
    xb = x.astype(BF16)

    def proj(c):
        cols = slice(c * d, (c + 1) * d)
        return (jnp.dot(xb, win_ref[:, cols], preferred_element_type=F32)
                + bin_ref[:, cols])

    u = proj(1) * proj(2)
    ubuf[2 * batch:2 * batch + rows, :] = u
    conv = (convw_ref[0:1, :] * ubuf[0:rows, :]
            + convw_ref[1:2, :] * ubuf[batch:batch + rows, :]
            + convw_ref[2:3, :] * u)
    ya = jnp.dot((proj(0) * conv).astype(BF16), wco_ref[...],
                 preferred_element_type=F32)
    tail = ubuf[rows:rows + 2 * batch, :]
    ubuf[0:2 * batch, :] = tail
    cnew_ref[...] = tail

    us = proj(3)
    half = n_blk // 2
    for k in range(SSM_BLOCKS):
        usk = us[:, k * d_blk:(k + 1) * d_blk].astype(BF16)
        xk_ref[...] = jnp.dot(usk, bbd_ref[k], preferred_element_type=F32)
        for hf in range(2):
            st = slice(k * n_blk + hf * half, k * n_blk + (hf + 1) * half)
            re = slice(hf * half, (hf + 1) * half)
            im = slice(n_blk + hf * half, n_blk + (hf + 1) * half)
            if steps == 1:
                lr = lamr_ref[:, st]
                li = lami_ref[:, st]
                hr = hr_ref[:, st]
                hi = hi_ref[:, st]
                nhr = lr * hr - li * hi + xk_ref[:, re]
                nhi = lr * hi + li * hr + xk_ref[:, im]
                xk_ref[:, re] = nhr
                xk_ref[:, im] = nhi
                hr_ref[:, st] = nhr
                hi_ref[:, st] = nhi
            else:
                lr = jnp.broadcast_to(lamr_ref[:, st], (SUBLANES, half))
                li = jnp.broadcast_to(lami_ref[:, st], (SUBLANES, half))
                for s in range(batch // SUBLANES):
                    grp = slice(s * SUBLANES, (s + 1) * SUBLANES)

                    def step(t, carry, s=s, re=re, im=im, lr=lr, li=li):
                        hr, hi = carry
                        row = pl.multiple_of(t * batch + s * SUBLANES, SUBLANES)
                        rs = pl.ds(row, SUBLANES)
                        nhr = lr * hr - li * hi + xk_ref[rs, re]
                        nhi = lr * hi + li * hr + xk_ref[rs, im]
                        xk_ref[rs, re] = nhr
                        xk_ref[rs, im] = nhi
                        return nhr, nhi

                    hr, hi = lax.fori_loop(0, steps, step,
                                           (hr_ref[grp, st], hi_ref[grp, st]))
                    hr_ref[grp, st] = hr
                    hi_ref[grp, st] = hi
        ys_ref[:, k * d_blk:(k + 1) * d_blk] = jnp.dot(
            xk_ref[...].astype(BF16), cbd_ref[k], preferred_element_type=F32)
    ys = ys_ref[...] + dskip_ref[...] * us
    z = jax.nn.gelu(ys)
    gate = jnp.dot(z.astype(BF16), wglu_ref[...], preferred_element_type=F32) + bglu_ref[...]
    glu = z * jax.nn.sigmoid(gate)
    yb = jnp.dot(glu.astype(BF16), wso_ref[...], preferred_element_type=F32)

    m = jax.nn.sigmoid(proj(4)) * ya + jax.nn.sigmoid(proj(5)) * yb
    o = jnp.dot(m.astype(BF16), wo_ref[...], preferred_element_type=F32)
    x1 = _layer_norm(alpha * x + o, ln1g_ref[...], ln1b_ref[...])
    x1_ref[...] = x1

    logits = lax.dot_general(wrt_ref[...], x1, (((1,), (1,)), ((), ())),
                             precision=lax.Precision.HIGHEST,
                             preferred_element_type=F32)
    scores = jax.nn.sigmoid(logits)
    comb = _route(scores, scores + rbias_ref[...])
    comb_ref[...] = comb.T
    if sub:
        combt_ref[...] = comb
        ones = jnp.ones((SUBLANES, sub), BF16)
        for s in range(rows // sub):
            sel = jnp.where(comb[:, s * sub:(s + 1) * sub] != 0.0, 1.0, 0.0).astype(BF16)
            cnt_ref[s] = lax.dot_general(ones, sel, (((1,), (1,)), ((), ())),
                                         preferred_element_type=F32)


def _mixer(x, cprev, h0r, h0i, p, *, alpha, batch, steps, sub=0):
    d = x.shape[-1]
    total = x.size // d
    rows = batch * steps
    assert not sub or rows % sub == 0
    n_state = h0r.shape[1]
    n_exp = p['w_router_t'].shape[0]
    grid = (total // rows,)
    if x.ndim == 3:
        x_spec = pl.BlockSpec((batch, steps, d), lambda i: (0, i, 0))
    else:
        x_spec = pl.BlockSpec((rows, d), lambda i: (i, 0))
    consts = [p['w_in'], p['b_in'], p['conv_w'], p['w_conv_out'],
              p['lam_r'], p['lam_i'], p['bbd'], p['cbd'], p['ssm_d'],
              p['w_glu'], p['b_glu'], p['w_ssm_out'], p['w_o'], p['ln1_g'], p['ln1_b'],
              p['w_router_t'], p['router_bias']]
    in_specs = ([x_spec,
                 _const_spec(cprev.shape), _const_spec(h0r.shape), _const_spec(h0i.shape)]
                + [_const_spec(c.shape) for c in consts])
    out_shape = (jax.ShapeDtypeStruct((total, d), F32),
                 jax.ShapeDtypeStruct((total, n_exp), F32),
                 jax.ShapeDtypeStruct((2 * batch, d), F32),
                 jax.ShapeDtypeStruct((batch, n_state), F32),
                 jax.ShapeDtypeStruct((batch, n_state), F32))
    out_specs = (pl.BlockSpec((rows, d), lambda i: (i, 0)),
                 pl.BlockSpec((rows, n_exp), lambda i: (i, 0)),
                 pl.BlockSpec((2 * batch, d), lambda i: (0, 0)),
                 pl.BlockSpec((batch, n_state), lambda i: (0, 0)),
                 pl.BlockSpec((batch, n_state), lambda i: (0, 0)))
    if sub:
        out_shape += (jax.ShapeDtypeStruct((n_exp, total), F32),
                      jax.ShapeDtypeStruct((total // sub, SUBLANES, n_exp), F32))
        out_specs += (pl.BlockSpec((n_exp, rows), lambda i: (0, i)),
                      pl.BlockSpec((rows // sub, SUBLANES, n_exp), lambda i: (i, 0, 0)))
    scratch = [pltpu.VMEM((rows + 2 * batch, d), F32),
               pltpu.VMEM((rows, 2 * n_state // SSM_BLOCKS), F32),
               pltpu.VMEM((rows, d), F32)]
    if x.ndim == 3:
        scratch.append(pltpu.VMEM((d // LANES, rows, LANES), F32))
    return pl.pallas_call(
        functools.partial(_mixer_kernel, alpha, batch, steps, sub),
        grid=grid, in_specs=in_specs, out_specs=out_specs, out_shape=out_shape,
        scratch_shapes=scratch,
        compiler_params=pltpu.CompilerParams(
            dimension_semantics=("arbitrary",), vmem_limit_bytes=VMEM_LIMIT),
        name="mixer",
    )(x, cprev, h0r, h0i, *consts)


def _swiglu(xb, wg, wu):
    g = jnp.dot(xb, wg.astype(BF16), preferred_element_type=F32)
    u = jnp.dot(xb, wu.astype(BF16), preferred_element_type=F32)
    return jax.nn.silu(g) * u


def _moe_kernel(alpha, emit_bf16, x1_ref, comb_ref, wg_ref, wu_ref, wd_ref,
                wsg_ref, wsu_ref, wsd_ref, g_ref, b_ref, out_ref, *rest):
    if emit_bf16:
        wgb_ref, wub_ref, wdb_ref, xb_ref, acc_ref = rest
    else:
        xb_ref, acc_ref = rest
    e = pl.program_id(1)

    @pl.when(e == 0)
    def _shared():
        xb = x1_ref[...].astype(BF16)
        xb_ref[...] = xb
        hs = _swiglu(xb, wsg_ref[...], wsu_ref[...])
        acc_ref[...] = jnp.dot(hs.astype(BF16), wsd_ref[...].astype(BF16),
                               preferred_element_type=F32)

    wg = wg_ref[...].astype(BF16)
    wu = wu_ref[...].astype(BF16)
    wd = wd_ref[...].astype(BF16)
    if emit_bf16:
        wgb_ref[...] = wg
        wub_ref[...] = wu
        wdb_ref[...] = wd
    comb = comb_ref[...]
    lane = lax.broadcasted_iota(jnp.int32, comb.shape, 1)
    c = jnp.sum(jnp.where(lane == e, comb, 0.0), axis=1, keepdims=True)
    h = _swiglu(xb_ref[...], wg, wu) * c
    acc_ref[...] += jnp.dot(h.astype(BF16), wd, preferred_element_type=F32)

    @pl.when(e == pl.num_programs(1) - 1)
    def _finish():
        out_ref[...] = _layer_norm(alpha * x1_ref[...] + acc_ref[...],
                                   g_ref[...], b_ref[...])


def _moe(x1, comb, p, *, alpha, tile):
    total, d = x1.shape
    n_exp, _, f = p['w_gate'].shape
    fs = p['ws_gate'].shape[1]
    grid = (total // tile, n_exp)
    emit_bf16 = grid[0] == 1
    up_spec = pl.BlockSpec((None, d, f), lambda i, e: (e, 0, 0))
    down_spec = pl.BlockSpec((None, f, d), lambda i, e: (e, 0, 0))
    in_specs = [pl.BlockSpec((tile, d), lambda i, e: (i, 0)),
                pl.BlockSpec((tile, n_exp), lambda i, e: (i, 0)),
                up_spec, up_spec, down_spec,
                _const_spec((d, fs)), _const_spec((d, fs)), _const_spec((fs, d)),
                _const_spec((1, d)), _const_spec((1, d))]
    out_specs = [pl.BlockSpec((tile, d), lambda i, e: (i, 0))]
    out_shape = [jax.ShapeDtypeStruct((total, d), F32)]
    if emit_bf16:
        out_specs += [up_spec, up_spec, down_spec]
        out_shape += [jax.ShapeDtypeStruct(p[k].shape, BF16)
                      for k in ('w_gate', 'w_up', 'w_down')]
    res = pl.pallas_call(
        functools.partial(_moe_kernel, alpha, emit_bf16),
        grid=grid, in_specs=in_specs, out_specs=out_specs, out_shape=out_shape,
        scratch_shapes=[pltpu.VMEM((tile, d), BF16), pltpu.VMEM((tile, d), F32)],
        compiler_params=pltpu.CompilerParams(
            dimension_semantics=("arbitrary", "arbitrary"), vmem_limit_bytes=VMEM_LIMIT),
        name="moe",
    )(x1, comb, p['w_gate'], p['w_up'], p['w_down'],
      p['ws_gate'], p['ws_up'], p['ws_down'], p['ln2_g'], p['ln2_b'])
    if emit_bf16:
        return res[0], tuple(res[1:])
    return res[0], tuple(p[k].astype(BF16) for k in ('w_gate', 'w_up', 'w_down'))


SUB = 256
RUN_ALIGN = 16
FFN_CHUNK = 192
SEL_CHUNK = 512
PACK_STATIC = 4
PACK_FULL = 2
DISPATCH_SUBTILES = 4
EXPERTS_PER_STEP = 4


def _expert_onehot(r, off, cnt):
    hit = (r >= off) & (r < off + cnt)
    return hit, jnp.where(hit, off + 1.0, 0.0)


def _moe_sparse_kernel(alpha, batch, n_sub, cap, eg,
                       off_s, pc_s,
                       x1_ref, comb_ref, combt_ref, offr_ref, cntr_ref, offc_ref, cntc_ref,
                       wg_ref, wu_ref, wd_ref, wsg_ref, wsu_ref, wsd_ref, g_ref, b_ref,
                       out_ref, gbuf, xe, zbuf):
    i = pl.program_id(0)
    g = pl.program_id(1)
    n_exp = comb_ref.shape[1]
    tri_r = lax.broadcasted_iota(jnp.int32, (SUB, SUB), 0)
    tri_c = lax.broadcasted_iota(jnp.int32, (SUB, SUB), 1)

    def run(j, e):
        idx = (i * n_sub + j) * n_exp + e
        return off_s[idx], pc_s[idx]

    def used_rows(j):
        o, p = run(j, n_exp - 1)
        return o + p

    def rows16(start):
        return pl.ds(pl.multiple_of(start, RUN_ALIGN), RUN_ALIGN)

    @pl.when((i == 0) & (g == 0))
    def _zero():
        xe[...] = jnp.zeros(xe.shape, xe.dtype)

    @pl.when(g == 0)
    def _dispatch():
        before = jnp.where(tri_r < tri_c, 1.0, 0.0).astype(BF16)
        for j in range(n_sub):
            tok = slice(j * SUB, (j + 1) * SUB)
            xj = x1_ref[tok, :].astype(BF16)
            sel = combt_ref[:, tok] != 0.0
            pos = jnp.dot(jnp.where(sel, 1.0, 0.0).astype(BF16), before,
                          preferred_element_type=F32)
            posm = jnp.where(sel, pos, -1.0).astype(BF16)
            off = offr_ref[j]
            cnt = cntr_ref[j]
            used = used_rows(j)

            def sort_rows(rc, j=j, xj=xj, posm=posm, off=off, cnt=cnt):
                r = (lax.broadcasted_iota(jnp.int32, (SEL_CHUNK, 1), 0)
                     + rc * SEL_CHUNK).astype(F32)
                hit, start1 = _expert_onehot(r, off, cnt)
                s = jnp.sum(start1, axis=1, keepdims=True)
                q = jnp.where(s > 0.0, r - (s - 1.0), -2.0)
                rank = jnp.dot(jnp.where(hit, 1.0, 0.0).astype(BF16), posm,
                               preferred_element_type=F32)
                pick = jnp.where(rank == q, 1.0, 0.0).astype(BF16)
                gbuf[j, rc * SEL_CHUNK:(rc + 1) * SEL_CHUNK, :] = jnp.dot(
                    pick, xj, preferred_element_type=F32).astype(BF16)

            last = cap // SEL_CHUNK - 1
            for rc in range(last):
                sort_rows(rc)
            pl.when(last * SEL_CHUNK < used)(functools.partial(sort_rows, last))

            @pl.when(last * SEL_CHUNK >= used)
            def _blank(j=j):
                gbuf[j, last * SEL_CHUNK:, :] = jnp.zeros((SEL_CHUNK, gbuf.shape[2]), BF16)

    def ffn_rows(rs, ee):
        xc = xe[rs, :]
        h = (jax.nn.silu(jnp.dot(xc, wg_ref[ee], preferred_element_type=F32))
             * jnp.dot(xc, wu_ref[ee], preferred_element_type=F32))
        xe[rs, :] = jnp.dot(h.astype(BF16), wd_ref[ee], preferred_element_type=F32).astype(BF16)

    def pack(e, base, static):
        n = jnp.int32(base)
        for j in range(n_sub):
            o, p = run(j, e)
            if static:
                for c in range(PACK_STATIC):
                    xe[rows16(n + c * RUN_ALIGN), :] = gbuf[j, rows16(o + c * RUN_ALIGN), :]
            else:
                def chunk(c, carry, j=j, o=o, n=n):
                    xe[rows16(n + c * RUN_ALIGN), :] = gbuf[j, rows16(o + c * RUN_ALIGN), :]
                    return carry

                lax.fori_loop(0, lax.div(p, RUN_ALIGN), chunk, 0)
            n = n + p
        return n - base

    def unpack(e, base, static, keep_next=True):
        n = jnp.int32(base)
        for j in range(n_sub):
            o, p = run(j, e)
            if static:
                for c in range(PACK_STATIC):
                    dst = rows16(o + c * RUN_ALIGN)
                    rows = xe[rows16(n + c * RUN_ALIGN), :]
                    if keep_next and c >= PACK_FULL:
                        rows = jnp.where(c * RUN_ALIGN < p, rows, gbuf[j, dst, :])
                    gbuf[j, dst, :] = rows
            else:
                def chunk(c, carry, j=j, o=o, n=n):
                    gbuf[j, rows16(o + c * RUN_ALIGN), :] = xe[rows16(n + c * RUN_ALIGN), :]
                    return carry

                lax.fori_loop(0, lax.div(p, RUN_ALIGN), chunk, 0)
            n = n + p

    usual = jnp.bool_(True)
    for ee in range(eg):
        n = jnp.int32(0)
        for j in range(n_sub):
            p = run(j, g * eg + ee)[1]
            usual = usual & (p >= PACK_FULL * RUN_ALIGN) & (p <= PACK_STATIC * RUN_ALIGN)
            n = n + p
        usual = usual & (n <= FFN_CHUNK)

    @pl.when(usual)
    def _together():
        for ee in range(eg):
            pack(g * eg + ee, ee * FFN_CHUNK, True)
        for ee in range(eg):
            ffn_rows(slice(ee * FFN_CHUNK, (ee + 1) * FFN_CHUNK), ee)
        for ee in range(eg):
            unpack(g * eg + ee, ee * FFN_CHUNK, True, keep_next=ee == eg - 1)

    @pl.when(jnp.logical_not(usual))
    def _one_by_one():
        for ee in range(eg):
            e = g * eg + ee
            n = pack(e, 0, False)

            def ffn(k, carry, ee=ee):
                ffn_rows(pl.ds(pl.multiple_of(k * FFN_CHUNK, RUN_ALIGN), FFN_CHUNK), ee)
                return carry

            lax.fori_loop(0, lax.div(n + (FFN_CHUNK - 1), FFN_CHUNK), ffn, 0)
            unpack(e, 0, False)

    @pl.when(g == pl.num_programs(1) - 1)
    def _combine():
        earlier = jnp.where(tri_c < tri_r, 1.0, 0.0).astype(BF16)
        for j in range(n_sub):
            tok = slice(j * SUB, (j + 1) * SUB)
            x = x1_ref[tok, :]
            xb = x.astype(BF16)
            hs = (jax.nn.silu(jnp.dot(xb, wsg_ref[...], preferred_element_type=F32))
                  * jnp.dot(xb, wsu_ref[...], preferred_element_type=F32))
            y = jnp.dot(hs.astype(BF16), wsd_ref[...], preferred_element_type=F32)
            comb = comb_ref[tok, :]
            sel = comb != 0.0
            pos = jnp.dot(earlier, jnp.where(sel, 1.0, 0.0).astype(BF16),
                          preferred_element_type=F32)
            posm = jnp.where(sel, pos, -1.0).astype(BF16)
            c_hi = comb.astype(BF16)
            c_lo = (comb - c_hi.astype(F32)).astype(BF16)
            off = offc_ref[j]
            cnt = cntc_ref[j]
            used = used_rows(j)

            def gather_rows(rc, j=j, posm=posm, c_hi=c_hi, c_lo=c_lo, off=off, cnt=cnt):
                r = (lax.broadcasted_iota(jnp.int32, (1, SEL_CHUNK), 1)
                     + rc * SEL_CHUNK).astype(F32)
                hit, start1 = _expert_onehot(r, off, cnt)
                s = jnp.sum(start1, axis=0, keepdims=True)
                q = jnp.where(s > 0.0, r - (s - 1.0), -2.0)
                hb = jnp.where(hit, 1.0, 0.0).astype(BF16)
                pick = jnp.dot(posm, hb, preferred_element_type=F32) == q
                w_hi = jnp.where(pick, jnp.dot(c_hi, hb, preferred_element_type=F32), 0.0)
                w_lo = jnp.where(pick, jnp.dot(c_lo, hb, preferred_element_type=F32), 0.0)
                rows = gbuf[j, rc * SEL_CHUNK:(rc + 1) * SEL_CHUNK, :]
                return (jnp.dot(w_hi.astype(BF16), rows, preferred_element_type=F32)
                        + jnp.dot(w_lo.astype(BF16), rows, preferred_element_type=F32))

            last = cap // SEL_CHUNK - 1
            for rc in range(last):
                y = y + gather_rows(rc)
            n_lt = x.shape[1] // LANES
            lane_tiles = lambda v: [v[:, c * LANES:(c + 1) * LANES] for c in range(n_lt)]

            def put(v):
                for c, t in enumerate(lane_tiles(v)):
                    zbuf[c] = t

            get = lambda: jnp.concatenate([zbuf[c] for c in range(n_lt)], axis=1)
            put(alpha * x + y)

            @pl.when(last * SEL_CHUNK < used)
            def _tail(gather_rows=gather_rows, put=put, get=get):
                put(get() + gather_rows(last))

            put(_layer_norm(get(), g_ref[...], b_ref[...]))
            t_sub = SUB // batch
            for b in range(batch):
                for c in range(n_lt):
                    out_ref[b, j * t_sub:(j + 1) * t_sub, c * LANES:(c + 1) * LANES] = (
                        zbuf[c, pl.ds(b, t_sub, stride=batch), :])


def _moe_sparse(x1, comb, combt, cnt, p, w_bf16, *, alpha, batch, n_sub, eg=EXPERTS_PER_STEP):
    total, d = x1.shape
    w_gate, w_up, w_down = w_bf16
    n_exp, _, f = w_gate.shape
    fs = p['ws_gate'].shape[1]
    tile = n_sub * SUB
    n_tiles = total // tile
    assert SUB % batch == 0
    cap = -(-(SUB * TOP_K + n_exp * (RUN_ALIGN - 1)) // SEL_CHUNK) * SEL_CHUNK
    xe_rows = -(-(tile + n_sub * (RUN_ALIGN - 1)) // FFN_CHUNK) * FFN_CHUNK

    cnt = cnt[:, 0, :]
    pc = jnp.ceil(cnt / RUN_ALIGN) * RUN_ALIGN
    off = jnp.cumsum(pc, axis=1) - pc
    off_s = off.astype(jnp.int32).reshape(-1)
    pc_s = pc.astype(jnp.int32).reshape(-1)
    offr, cntr = off[:, None, :], cnt[:, None, :]
    offc, cntc = off[:, :, None], cnt[:, :, None]

    row_spec = pl.BlockSpec((n_sub, 1, n_exp), lambda i, g, *_: (i, 0, 0))
    col_spec = pl.BlockSpec((n_sub, n_exp, 1), lambda i, g, *_: (i, 0, 0))
    const = lambda shape: pl.BlockSpec(shape, lambda i, g, *_: (0,) * len(shape),
                                       pipeline_mode=pl.Buffered(1))
    grid_spec = pltpu.PrefetchScalarGridSpec(
        num_scalar_prefetch=2,
        grid=(n_tiles, n_exp // eg),
        in_specs=[pl.BlockSpec((tile, d), lambda i, g, *_: (i, 0), pipeline_mode=pl.Buffered(1)),
                  pl.BlockSpec((tile, n_exp), lambda i, g, *_: (i, 0)),
                  pl.BlockSpec((n_exp, tile), lambda i, g, *_: (0, i)),
                  row_spec, row_spec, col_spec, col_spec,
                  pl.BlockSpec((eg, d, f), lambda i, g, *_: (g, 0, 0)),
                  pl.BlockSpec((eg, d, f), lambda i, g, *_: (g, 0, 0)),
                  pl.BlockSpec((eg, f, d), lambda i, g, *_: (g, 0, 0)),
                  const((d, fs)), const((d, fs)), const((fs, d)),
                  const((1, d)), const((1, d))],
        out_specs=pl.BlockSpec((batch, tile // batch, d), lambda i, g, *_: (0, i, 0)),
        scratch_shapes=[pltpu.VMEM((n_sub, cap, d), BF16), pltpu.VMEM((xe_rows, d), BF16),
                        pltpu.VMEM((d // LANES, SUB, LANES), F32)])
    return pl.pallas_call(
        functools.partial(_moe_sparse_kernel, alpha, batch, n_sub, cap, eg),
        grid_spec=grid_spec,
        out_shape=jax.ShapeDtypeStruct((batch, total // batch, d), F32),
        compiler_params=pltpu.CompilerParams(
            dimension_semantics=("arbitrary", "arbitrary"), vmem_limit_bytes=VMEM_LIMIT),
        name="moe_sparse",
    )(off_s, pc_s, x1, comb, combt, offr, cntr, offc, cntc, w_gate, w_up, w_down,
      p['ws_gate'].astype(BF16), p['ws_up'].astype(BF16), p['ws_down'].astype(BF16),
      p['ln2_g'], p['ln2_b'])


def _layer_params(l, w):
    g, n = w['a_re'].shape[1:]
    lr, li, bbr, bbi = _ssm_prep(w['a_re'][l], w['a_im'][l], w['log_dt'][l],
                                 w['ssm_b_re'][l], w['ssm_b_im'][l])
    bbd = jnp.concatenate([_block_diag(bbr, SSM_BLOCKS), _block_diag(bbi, SSM_BLOCKS)],
                          axis=-1).astype(BF16)
    c_re = w['ssm_c_re'][l].transpose(0, 2, 1)
    c_im = w['ssm_c_im'][l].transpose(0, 2, 1)
    cbd = jnp.concatenate([_block_diag(c_re, SSM_BLOCKS), _block_diag(-c_im, SSM_BLOCKS)],
                          axis=1).astype(BF16)
    row = lambda v: v.reshape(1, -1)
    return {
        'w_in': w['w_in'][l].astype(BF16), 'b_in': row(w['b_in'][l]),
        'conv_w': w['conv_w'][l], 'w_conv_out': w['w_conv_out'][l].astype(BF16),
        'lam_r': lr.reshape(1, g * n), 'lam_i': li.reshape(1, g * n),
        'bbd': bbd, 'cbd': cbd, 'ssm_d': row(w['ssm_d'][l]),
        'w_glu': w['w_glu'][l].astype(BF16), 'b_glu': row(w['b_glu'][l]),
        'w_ssm_out': w['w_ssm_out'][l].astype(BF16), 'w_o': w['w_o'][l].astype(BF16),
        'ln1_g': row(w['ln1_g'][l]), 'ln1_b': row(w['ln1_b'][l]),
        'w_router_t': w['w_router'][l].T, 'router_bias': w['router_bias'][l].reshape(-1, 1),
        'w_gate': w['w_gate'][l], 'w_up': w['w_up'][l], 'w_down': w['w_down'][l],
        'ws_gate': w['ws_gate'][l], 'ws_up': w['ws_up'][l], 'ws_down': w['ws_down'][l],
        'ln2_g': row(w['ln2_g'][l]), 'ln2_b': row(w['ln2_b'][l]),
    }


def _pick_steps(batch, seq, max_rows):
    steps = max(1, min(seq, max_rows // batch))
    while seq % steps:
        steps -= 1
    return steps


def _pick_tile(total, max_tile):
    tile = min(total, max_tile)
    while total % tile or tile % 16:
        tile -= 16
    return tile


MIXER_ROWS = 512
MOE_TILE = 1024


def _trunk_layer(x, conv_l, re_l, im_l, p, alpha, w_bf16):
    bsz, seq, d = x.shape
    assert bsz % SUBLANES == 0
    steps = _pick_steps(bsz, seq, MIXER_ROWS)
    kw = conv_l.shape[1]
    assert kw == 2
    cprev = conv_l.astype(F32).transpose(1, 0, 2).reshape(kw * bsz, d)
    h0r = re_l.astype(F32).reshape(bsz, -1)
    h0i = im_l.astype(F32).reshape(bsz, -1)
    if ((bsz * steps) % SUB == 0 and (bsz * seq) % (DISPATCH_SUBTILES * SUB) == 0
            and SUB % bsz == 0):
        if w_bf16 is None:
            w_bf16 = tuple(p[k].astype(BF16) for k in ('w_gate', 'w_up', 'w_down'))
        x1, comb, cnew, hr, hi, combt, cnt = _mixer(
            x, cprev, h0r, h0i, p, alpha=alpha, batch=bsz, steps=steps, sub=SUB)
        y = _moe_sparse(x1, comb, combt, cnt, p, w_bf16, alpha=alpha, batch=bsz,
                        n_sub=DISPATCH_SUBTILES)
    else:
        rows = x.transpose(1, 0, 2).reshape(seq * bsz, d)
        x1, comb, cnew, hr, hi = _mixer(rows, cprev, h0r, h0i, p,
                                        alpha=alpha, batch=bsz, steps=steps)
        out, w_bf16 = _moe(x1, comb, p, alpha=alpha, tile=_pick_tile(bsz * seq, MOE_TILE))
        y = out.reshape(seq, bsz, d).transpose(1, 0, 2)
    states = (cnew.reshape(kw, bsz, d).transpose(1, 0, 2),
              hr.reshape(re_l.shape), hi.reshape(im_l.shape))
    return y, states, w_bf16


def kernel(x_prompt, x_sample, state_conv, state_ssm_re, state_ssm_im,
           w_in, b_in, conv_w, w_conv_out, a_re, a_im, log_dt,
           ssm_b_re, ssm_b_im, ssm_c_re, ssm_c_im, ssm_d, w_glu, b_glu, w_ssm_out, w_o,
           ln1_g, ln1_b, w_router, router_bias, w_gate, w_up, w_down,
           ws_gate, ws_up, ws_down, ln2_g, ln2_b):
    w = dict(w_in=w_in, b_in=b_in, conv_w=conv_w, w_conv_out=w_conv_out,
             a_re=a_re, a_im=a_im, log_dt=log_dt,
             ssm_b_re=ssm_b_re, ssm_b_im=ssm_b_im, ssm_c_re=ssm_c_re, ssm_c_im=ssm_c_im,
             ssm_d=ssm_d, w_glu=w_glu, b_glu=b_glu, w_ssm_out=w_ssm_out, w_o=w_o,
             ln1_g=ln1_g, ln1_b=ln1_b, w_router=w_router, router_bias=router_bias,
             w_gate=w_gate, w_up=w_up, w_down=w_down,
             ws_gate=ws_gate, ws_up=ws_up, ws_down=ws_down, ln2_g=ln2_g, ln2_b=ln2_b)
    depth = w_in.shape[0]
    alpha = (2.0 * depth) ** 0.25
    bsz = x_prompt.shape[0]
    zero_conv = jnp.zeros((bsz,) + state_conv.shape[2:], x_prompt.dtype)
    zero_ssm = jnp.zeros((bsz,) + state_ssm_re.shape[2:], F32)
    y_p, y_s = x_prompt, x_sample
    st_p, st_s = [], []
    for l in range(depth):
        p = _layer_params(l, w)
        y_s, st, w_bf16 = _trunk_layer(y_s, state_conv[l], state_ssm_re[l], state_ssm_im[l],
                                       p, alpha, None)
        st_s.append(st)
        y_p, st, _ = _trunk_layer(y_p, zero_conv, zero_ssm, zero_ssm, p, alpha, w_bf16)
        st_p.append(st)
    conv_p, re_p, im_p = (jnp.stack(v) for v in zip(*st_p))
    conv_s, re_s, im_s = (jnp.stack(v) for v in zip(*st_s))
    return (y_p, y_s, conv_p, re_p, im_p, conv_s, re_s, im_s)
```

```python
import functools
import math

import jax
import jax.numpy as jnp
from jax import lax
from jax.experimental import pallas as pl
from jax.experimental.pallas import tpu as pltpu

F32 = jnp.float32
BF16 = jnp.bfloat16

LN_EPS = 1e-5
ROUTED_SCALE = 2.5
N_ROUTE_GROUPS = 8
TOPK_GROUPS = 4
TOP_K = 8

SUBLANES = 8
LANES = 128
SSM_BLOCKS = 4
SCAN_UNROLL = 4
POST_ROWS = 512
VMEM_LIMIT = 60 * 1024 * 1024


def _const_spec(shape):
    nd = len(shape)
    return pl.BlockSpec(shape, lambda *_: (0,) * nd, pipeline_mode=pl.Buffered(1))


def _ssm_prep_kernel(are_ref, aim_ref, ldt_ref, br_ref, bi_ref,
                     lr_ref, li_ref, bbr_ref, bbi_ref):
    dt = jnp.exp(ldt_ref[...])
    ar = are_ref[...]
    ai = aim_ref[...]
    mag = jnp.exp(ar * dt)
    lr = mag * jnp.cos(ai * dt)
    li = mag * jnp.sin(ai * dt)
    den = ar * ar + ai * ai
    fr = ((lr - 1.0) * ar + li * ai) / den
    fi = (li * ar - (lr - 1.0) * ai) / den
    lr_ref[...] = lr
    li_ref[...] = li
    br = br_ref[...]
    bi = bi_ref[...]
    bbr_ref[...] = fr * br - fi * bi
    bbi_ref[...] = fr * bi + fi * br


def _ssm_prep(a_re, a_im, log_dt, b_re, b_im):
    g, n = a_re.shape
    h = b_re.shape[-1]
    vec = jax.ShapeDtypeStruct((g, 1, n), F32)
    mat = jax.ShapeDtypeStruct((g, h, n), F32)
    return pl.pallas_call(
        _ssm_prep_kernel,
        out_shape=(vec, vec, mat, mat),
        name="ssm_prep",
    )(a_re.reshape(g, 1, n), a_im.reshape(g, 1, n), log_dt.reshape(g, 1, 1),
      b_re.transpose(0, 2, 1), b_im.transpose(0, 2, 1))


def _block_diag(m, nblk):
    g, p, q = m.shape
    gl = g // nblk
    eye = jnp.eye(gl, dtype=m.dtype)
    out = jnp.einsum('kapq,ab->kapbq', m.reshape(nblk, gl, p, q), eye)
    return out.reshape(nblk, gl * p, gl * q)


def _layer_norm(r, g, b):
    mu = jnp.mean(r, axis=-1, keepdims=True)
    d = r - mu
    var = jnp.mean(d * d, axis=-1, keepdims=True)
    return d * lax.rsqrt(var + LN_EPS) * g + b


def _split_bf16(v):
    hi = v.astype(BF16)
    return hi, (v - hi.astype(F32)).astype(BF16)


def _route(scores, biased):
    n_exp, r = scores.shape
    gsz = n_exp // N_ROUTE_GROUPS
    neg = jnp.float32(-jnp.inf)
    rows = []
    for g in range(N_ROUTE_GROUPS):
        v = biased[g * gsz:(g + 1) * gsz, :]
        m1 = jnp.max(v, axis=0, keepdims=True)
        is_max = v == m1
        n_max = jnp.sum(is_max.astype(F32), axis=0, keepdims=True)
        rest = jnp.max(jnp.where(is_max, neg, v), axis=0, keepdims=True)
        rows.append(m1 + jnp.where(n_max >= 2.0, m1, rest))
    gscore = jnp.concatenate(rows, axis=0)
    gidx = lax.broadcasted_iota(jnp.int32, gscore.shape, 0)
    grank = jnp.zeros(gscore.shape, F32)
    for g in range(N_ROUTE_GROUPS):
        sg = gscore[g:g + 1, :]
        beats = (sg > gscore) | ((sg == gscore) & (gidx > g))
        grank = grank + beats.astype(F32)
    gkeep = grank < float(TOPK_GROUPS)
    masked = jnp.concatenate(
        [jnp.where(gkeep[g:g + 1, :], biased[g * gsz:(g + 1) * gsz, :], neg)
         for g in range(N_ROUTE_GROUPS)], axis=0)
    eidx = lax.broadcasted_iota(jnp.int32, masked.shape, 0).astype(F32)
    left = masked
    for _ in range(TOP_K):
        top = jnp.max(left, axis=0, keepdims=True)
        first = jnp.min(jnp.where(left == top, eidx, float(n_exp)), axis=0, keepdims=True)
        left = jnp.where(eidx == first, neg, left)
    w = jnp.where(left != masked, scores, 0.0)
    return w / jnp.sum(w, axis=0, keepdims=True) * ROUTED_SCALE


def _mixer_kernel(alpha, batch, steps, sub,
                  x_ref, cprev_ref, h0r_ref, h0i_ref,
                  win_ref, bin_ref, convw_ref, wco_ref,
                  lamr_ref, lami_ref, bbd_ref, cbd_ref, dskip_ref,
                  wglu_ref, bglu_ref, wso_ref, wo_ref, ln1g_ref, ln1b_ref,
                  wrt_ref, rbias_ref,
                  x1_ref, comb_ref, cnew_ref, hr_ref, hi_ref, *rest):
    if sub:
        combt_ref, cnt_ref, *rest = rest
    ubuf, xk_ref, ys_ref, *rest = rest
    rows = batch * steps
    d = x_ref.shape[-1]
    d_blk = d // SSM_BLOCKS
    n_blk = lamr_ref.shape[1] // SSM_BLOCKS
    i = pl.program_id(0)

    @pl.when(i == 0)
    def _init():
        ubuf[0:2 * batch, :] = cprev_ref[...]
        hr_ref[...] = h0r_ref[...]
        hi_ref[...] = h0i_ref[...]

    if len(x_ref.shape) == 3:
        xs_ref, = rest
        for b in range(batch):
            for c in range(d // LANES):
                xs_ref[c, pl.ds(b, steps, stride=batch), :] = x_ref[b, :, c * LANES:(c + 1) * LANES]
        x = jnp.concatenate([xs_ref[c] for c in range(d // LANES)], axis=1)
    else:
        x = x_ref[...]
    xb = x.astype(BF16)

    def proj(c):
        cols = slice(c * d, (c + 1) * d)
        return (jnp.dot(xb, win_ref[:, cols], preferred_element_type=F32)
                + bin_ref[:, cols])

    u = proj(1) * proj(2)
    ubuf[2 * batch:2 * batch + rows, :] = u
    conv = (convw_ref[0:1, :] * ubuf[0:rows, :]
            + convw_ref[1:2, :] * ubuf[batch:batch + rows, :]
            + convw_ref[2:3, :] * u)
    ya = jnp.dot((proj(0) * conv).astype(BF16), wco_ref[...],
                 preferred_element_type=F32)
    tail = ubuf[rows:rows + 2 * batch, :]
    ubuf[0:2 * batch, :] = tail
    cnew_ref[...] = tail
    ya_rows = slice(2 * batch, 2 * batch + rows)
    ubuf[ya_rows, :] = ya

    us = proj(3)
    half = n_blk // 2
    unroll = SCAN_UNROLL if steps % SCAN_UNROLL == 0 else 1
    for k in range(SSM_BLOCKS):
        usk = us[:, k * d_blk:(k + 1) * d_blk].astype(BF16)
        xk_ref[...] = jnp.dot(usk, bbd_ref[k], preferred_element_type=F32)
        for hf in range(2):
            st = slice(k * n_blk + hf * half, k * n_blk + (hf + 1) * half)
            re = slice(hf * half, (hf + 1) * half)
            im = slice(n_blk + hf * half, n_blk + (hf + 1) * half)
            if steps == 1:
                lr = lamr_ref[:, st]
                li = lami_ref[:, st]
                hr = hr_ref[:, st]
                hi = hi_ref[:, st]
                nhr = lr * hr - li * hi + xk_ref[:, re]
                nhi = lr * hi + li * hr + xk_ref[:, im]
                xk_ref[:, re] = nhr
                xk_ref[:, im] = nhi
                hr_ref[:, st] = nhr
                hi_ref[:, st] = nhi
            else:
                lr = jnp.broadcast_to(lamr_ref[:, st], (SUBLANES, half))
                li = jnp.broadcast_to(lami_ref[:, st], (SUBLANES, half))
                for s in range(batch // SUBLANES):
                    grp = slice(s * SUBLANES, (s + 1) * SUBLANES)

                    def step(tt, carry, s=s, re=re, im=im, lr=lr, li=li):
                        hr, hi = carry
                        for k_un in range(unroll):
                            row = pl.multiple_of((tt * unroll + k_un) * batch + s * SUBLANES,
                                                 SUBLANES)
                            rs = pl.ds(row, SUBLANES)
                            hr, hi = (lr * hr - li * hi + xk_ref[rs, re],
                                      lr * hi + li * hr + xk_ref[rs, im])
                            xk_ref[rs, re] = hr
                            xk_ref[rs, im] = hi
                        return hr, hi

                    hr, hi = lax.fori_loop(0, steps // unroll, step,
                                           (hr_ref[grp, st], hi_ref[grp, st]))
                    hr_ref[grp, st] = hr
                    hi_ref[grp, st] = hi
        ys_ref[:, k * d_blk:(k + 1) * d_blk] = jnp.dot(
            xk_ref[...].astype(BF16), cbd_ref[k], preferred_element_type=F32)
    ys_ref[...] = ys_ref[...] + dskip_ref[...] * us

    chunk = min(rows, POST_ROWS)
    for q in range(rows // chunk):
        rq = slice(q * chunk, (q + 1) * chunk)
        if len(x_ref.shape) == 3:
            xq = jnp.concatenate([xs_ref[c, rq, :] for c in range(d // LANES)], axis=1)
        else:
            xq = x_ref[rq, :]
        xbq = xq.astype(BF16)

        def projq(c, xbq=xbq):
            cols = slice(c * d, (c + 1) * d)
            return (jnp.dot(xbq, win_ref[:, cols], preferred_element_type=F32)
                    + bin_ref[:, cols])

        z = jax.nn.gelu(ys_ref[rq, :])
        gate = (jnp.dot(z.astype(BF16), wglu_ref[...], preferred_element_type=F32)
                + bglu_ref[...])
        glu = z * jax.nn.sigmoid(gate)
        yb = jnp.dot(glu.astype(BF16), wso_ref[...], preferred_element_type=F32)

        m = (jax.nn.sigmoid(projq(4)) * ubuf[2 * batch + q * chunk:2 * batch + (q + 1) * chunk, :]
             + jax.nn.sigmoid(projq(5)) * yb)
        o = jnp.dot(m.astype(BF16), wo_ref[...], preferred_element_type=F32)
        x1 = _layer_norm(alpha * xq + o, ln1g_ref[...], ln1b_ref[...])
        x1_ref[rq, :] = x1

        nt = (((1,), (1,)), ((), ()))
        w_hi, w_lo = _split_bf16(wrt_ref[...])
        x_hi, x_lo = _split_bf16(x1)
        logits = (lax.dot_general(w_hi, x_hi, nt, preferred_element_type=F32)
                  + lax.dot_general(w_hi, x_lo, nt, preferred_element_type=F32)
                  + lax.dot_general(w_lo, x_hi, nt, preferred_element_type=F32))
        scores = jax.nn.sigmoid(logits)
        comb = _route(scores, scores + rbias_ref[...])
        comb_ref[rq, :] = comb.T
        if sub:
            combt_ref[:, rq] = comb
    if sub:
        ones = jnp.ones((SUBLANES, sub), BF16)
        for s in range(rows // sub):
            sel = jnp.where(combt_ref[:, s * sub:(s + 1) * sub] != 0.0, 1.0, 0.0).astype(BF16)
            cnt_ref[s] = lax.dot_general(ones, sel, (((1,), (1,)), ((), ())),
                                         preferred_element_type=F32)


def _mixer(x, cprev, h0r, h0i, p, *, alpha, batch, steps, sub=0):
    d = x.shape[-1]
    total = x.size // d
    rows = batch * steps
    assert not sub or rows % sub == 0
    n_state = h0r.shape[1]
    n_exp = p['w_router_t'].shape[0]
    grid = (total // rows,)
    if x.ndim == 3:
        x_spec = pl.BlockSpec((batch, steps, d), lambda i: (0, i, 0))
    else:
        x_spec = pl.BlockSpec((rows, d), lambda i: (i, 0))
    consts = [p['w_in'], p['b_in'], p['conv_w'], p['w_conv_out'],
              p['lam_r'], p['lam_i'], p['bbd'], p['cbd'], p['ssm_d'],
              p['w_glu'], p['b_glu'], p['w_ssm_out'], p['w_o'], p['ln1_g'], p['ln1_b'],
              p['w_router_t'], p['router_bias']]
    in_specs = ([x_spec,
                 _const_spec(cprev.shape), _const_spec(h0r.shape), _const_spec(h0i.shape)]
                + [_const_spec(c.shape) for c in consts])
    out_shape = (jax.ShapeDtypeStruct((total, d), F32),
                 jax.ShapeDtypeStruct((total, n_exp), F32),
                 jax.ShapeDtypeStruct((2 * batch, d), F32),
                 jax.ShapeDtypeStruct((batch, n_state), F32),
                 jax.ShapeDtypeStruct((batch, n_state), F32))
    out_specs = (pl.BlockSpec((rows, d), lambda i: (i, 0)),
                 pl.BlockSpec((rows, n_exp), lambda i: (i, 0)),
                 pl.BlockSpec((2 * batch, d), lambda i: (0, 0)),
                 pl.BlockSpec((batch, n_state), lambda i: (0, 0)),
                 pl.BlockSpec((batch, n_state), lambda i: (0, 0)))
    if sub:
        out_shape += (jax.ShapeDtypeStruct((n_exp, total), F32),
                      jax.ShapeDtypeStruct((total // sub, SUBLANES, n_exp), F32))
        out_specs += (pl.BlockSpec((n_exp, rows), lambda i: (0, i)),
                      pl.BlockSpec((rows // sub, SUBLANES, n_exp), lambda i: (i, 0, 0)))
    scratch = [pltpu.VMEM((rows + 2 * batch, d), F32),
               pltpu.VMEM((rows, 2 * n_state // SSM_BLOCKS), F32),
               pltpu.VMEM((rows, d), F32)]
    if x.ndim == 3:
        scratch.append(pltpu.VMEM((d // LANES, rows, LANES), F32))
    return pl.pallas_call(
        functools.partial(_mixer_kernel, alpha, batch, steps, sub),
        grid=grid, in_specs=in_specs, out_specs=out_specs, out_shape=out_shape,
        scratch_shapes=scratch,
        compiler_params=pltpu.CompilerParams(
            dimension_semantics=("arbitrary",), vmem_limit_bytes=VMEM_LIMIT),
        name="mixer",
    )(x, cprev, h0r, h0i, *consts)


def _swiglu(xb, wg, wu):
    g = jnp.dot(xb, wg.astype(BF16), preferred_element_type=F32)
    u = jnp.dot(xb, wu.astype(BF16), preferred_element_type=F32)
    return jax.nn.silu(g) * u


def _moe_kernel(alpha, emit_bf16, x1_ref, comb_ref, wg_ref, wu_ref, wd_ref,
                wsg_ref, wsu_ref, wsd_ref, g_ref, b_ref, out_ref, *rest):
    if emit_bf16:
        wgb_ref, wub_ref, wdb_ref, xb_ref, acc_ref = rest
    else:
        xb_ref, acc_ref = rest
    e = pl.program_id(1)

    @pl.when(e == 0)
    def _shared():
        xb = x1_ref[...].astype(BF16)
        xb_ref[...] = xb
        hs = _swiglu(xb, wsg_ref[...], wsu_ref[...])
        acc_ref[...] = jnp.dot(hs.astype(BF16), wsd_ref[...].astype(BF16),
                               preferred_element_type=F32)

    wg = wg_ref[...].astype(BF16)
    wu = wu_ref[...].astype(BF16)
    wd = wd_ref[...].astype(BF16)
    if emit_bf16:
        wgb_ref[...] = wg
        wub_ref[...] = wu
        wdb_ref[...] = wd
    comb = comb_ref[...]
    lane = lax.broadcasted_iota(jnp.int32, comb.shape, 1)
    c = jnp.sum(jnp.where(lane == e, comb, 0.0), axis=1, keepdims=True)
    h = _swiglu(xb_ref[...], wg, wu) * c
    acc_ref[...] += jnp.dot(h.astype(BF16), wd, preferred_element_type=F32)

    @pl.when(e == pl.num_programs(1) - 1)
    def _finish():
        out_ref[...] = _layer_norm(alpha * x1_ref[...] + acc_ref[...],
                                   g_ref[...], b_ref[...])


def _moe(x1, comb, p, *, alpha, tile):
    total, d = x1.shape
    n_exp, _, f = p['w_gate'].shape
    fs = p['ws_gate'].shape[1]
    grid = (total // tile, n_exp)
    emit_bf16 = grid[0] == 1
    up_spec = pl.BlockSpec((None, d, f), lambda i, e: (e, 0, 0))
    down_spec = pl.BlockSpec((None, f, d), lambda i, e: (e, 0, 0))
    in_specs = [pl.BlockSpec((tile, d), lambda i, e: (i, 0)),
                pl.BlockSpec((tile, n_exp), lambda i, e: (i, 0)),
                up_spec, up_spec, down_spec,
                _const_spec((d, fs)), _const_spec((d, fs)), _const_spec((fs, d)),
                _const_spec((1, d)), _const_spec((1, d))]
    out_specs = [pl.BlockSpec((tile, d), lambda i, e: (i, 0))]
    out_shape = [jax.ShapeDtypeStruct((total, d), F32)]
    if emit_bf16:
        out_specs += [up_spec, up_spec, down_spec]
        out_shape += [jax.ShapeDtypeStruct(p[k].shape, BF16)
                      for k in ('w_gate', 'w_up', 'w_down')]
    res = pl.pallas_call(
        functools.partial(_moe_kernel, alpha, emit_bf16),
        grid=grid, in_specs=in_specs, out_specs=out_specs, out_shape=out_shape,
        scratch_shapes=[pltpu.VMEM((tile, d), BF16), pltpu.VMEM((tile, d), F32)],
        compiler_params=pltpu.CompilerParams(
            dimension_semantics=("arbitrary", "arbitrary"), vmem_limit_bytes=VMEM_LIMIT),
        name="moe",
    )(x1, comb, p['w_gate'], p['w_up'], p['w_down'],
      p['ws_gate'], p['ws_up'], p['ws_down'], p['ln2_g'], p['ln2_b'])
    if emit_bf16:
        return res[0], tuple(res[1:])
    return res[0], tuple(p[k].astype(BF16) for k in ('w_gate', 'w_up', 'w_down'))


SUB = 256
RUN_ALIGN = 16
FFN_CHUNK = 192
SEL_CHUNK = 512
PACK_STATIC = 4
PACK_FULL = 2
DISPATCH_SUBTILES = 4
EXPERTS_PER_STEP = 4


def _expert_onehot(r, off, cnt):
    hit = (r >= off) & (r < off + cnt)
    return hit, jnp.where(hit, off + 1.0, 0.0)


def _moe_sparse_kernel(alpha, batch, n_sub, cap, eg,
                       off_s, pc_s,
                       x1_ref, comb_ref, combt_ref, offr_ref, cntr_ref, offc_ref, cntc_ref,
                       wg_ref, wu_ref, wd_ref, wsg_ref, wsu_ref, wsd_ref, g_ref, b_ref,
                       out_ref, gbuf, xe, zbuf):
    i = pl.program_id(0)
    g = pl.program_id(1)
    n_exp = comb_ref.shape[1]
    tri_r = lax.broadcasted_iota(jnp.int32, (SUB, SUB), 0)
    tri_c = lax.broadcasted_iota(jnp.int32, (SUB, SUB), 1)

    def run(j, e):
        idx = (i * n_sub + j) * n_exp + e
        return off_s[idx], pc_s[idx]

    def used_rows(j):
        o, p = run(j, n_exp - 1)
        return o + p

    def rows16(start):
        return pl.ds(pl.multiple_of(start, RUN_ALIGN), RUN_ALIGN)

    @pl.when((i == 0) & (g == 0))
    def _zero():
        xe[...] = jnp.zeros(xe.shape, xe.dtype)

    @pl.when(g == 0)
    def _dispatch():
        before = jnp.where(tri_r < tri_c, 1.0, 0.0).astype(BF16)
        for j in range(n_sub):
            tok = slice(j * SUB, (j + 1) * SUB)
            xj = x1_ref[tok, :].astype(BF16)
            sel = combt_ref[:, tok] != 0.0
            pos = jnp.dot(jnp.where(sel, 1.0, 0.0).astype(BF16), before,
                          preferred_element_type=F32)
            posm = jnp.where(sel, pos, -1.0).astype(BF16)
            off = offr_ref[j]
            cnt = cntr_ref[j]
            used = used_rows(j)

            def sort_rows(rc, j=j, xj=xj, posm=posm, off=off, cnt=cnt):
                r = (lax.broadcasted_iota(jnp.int32, (SEL_CHUNK, 1), 0)
                     + rc * SEL_CHUNK).astype(F32)
                hit, start1 = _expert_onehot(r, off, cnt)
                s = jnp.sum(start1, axis=1, keepdims=True)
                q = jnp.where(s > 0.0, r - (s - 1.0), -2.0)
                rank = jnp.dot(jnp.where(hit, 1.0, 0.0).astype(BF16), posm,
                               preferred_element_type=F32)
                pick = jnp.where(rank == q, 1.0, 0.0).astype(BF16)
                gbuf[j, rc * SEL_CHUNK:(rc + 1) * SEL_CHUNK, :] = jnp.dot(
                    pick, xj, preferred_element_type=F32).astype(BF16)

            last = cap // SEL_CHUNK - 1
            for rc in range(last):
                sort_rows(rc)
            pl.when(last * SEL_CHUNK < used)(functools.partial(sort_rows, last))

            @pl.when(last * SEL_CHUNK >= used)
            def _blank(j=j):
                gbuf[j, last * SEL_CHUNK:, :] = jnp.zeros((SEL_CHUNK, gbuf.shape[2]), BF16)

    def ffn_rows(rs, ee):
        xc = xe[rs, :]
        h = (jax.nn.silu(jnp.dot(xc, wg_ref[ee], preferred_element_type=F32))
             * jnp.dot(xc, wu_ref[ee], preferred_element_type=F32))
        xe[rs, :] = jnp.dot(h.astype(BF16), wd_ref[ee], preferred_element_type=F32).astype(BF16)

    def pack(e, base, static):
        n = jnp.int32(base)
        for j in range(n_sub):
            o, p = run(j, e)
            if static:
                for c in range(PACK_STATIC):
                    xe[rows16(n + c * RUN_ALIGN), :] = gbuf[j, rows16(o + c * RUN_ALIGN), :]
            else:
                def chunk(c, carry, j=j, o=o, n=n):
                    xe[rows16(n + c * RUN_ALIGN), :] = gbuf[j, rows16(o + c * RUN_ALIGN), :]
                    return carry

                lax.fori_loop(0, lax.div(p, RUN_ALIGN), chunk, 0)
            n = n + p
        return n - base

    def unpack(e, base, static, keep_next=True):
        n = jnp.int32(base)
        for j in range(n_sub):
            o, p = run(j, e)
            if static:
                for c in range(PACK_STATIC):
                    dst = rows16(o + c * RUN_ALIGN)
                    rows = xe[rows16(n + c * RUN_ALIGN), :]
                    if keep_next and c >= PACK_FULL:
                        rows = jnp.where(c * RUN_ALIGN < p, rows, gbuf[j, dst, :])
                    gbuf[j, dst, :] = rows
            else:
                def chunk(c, carry, j=j, o=o, n=n):
                    gbuf[j, rows16(o + c * RUN_ALIGN), :] = xe[rows16(n + c * RUN_ALIGN), :]
                    return carry

                lax.fori_loop(0, lax.div(p, RUN_ALIGN), chunk, 0)
            n = n + p

    usual = jnp.bool_(True)
    for ee in range(eg):
        n = jnp.int32(0)
        for j in range(n_sub):
            p = run(j, g * eg + ee)[1]
            usual = usual & (p >= PACK_FULL * RUN_ALIGN) & (p <= PACK_STATIC * RUN_ALIGN)
            n = n + p
        usual = usual & (n <= FFN_CHUNK)

    @pl.when(usual)
    def _together():
        for ee in range(eg):
            pack(g * eg + ee, ee * FFN_CHUNK, True)
        for ee in range(eg):
            ffn_rows(slice(ee * FFN_CHUNK, (ee + 1) * FFN_CHUNK), ee)
        for ee in range(eg):
            unpack(g * eg + ee, ee * FFN_CHUNK, True, keep_next=ee == eg - 1)

    @pl.when(jnp.logical_not(usual))
    def _one_by_one():
        for ee in range(eg):
            e = g * eg + ee
            n = pack(e, 0, False)

            def ffn(k, carry, ee=ee):
                ffn_rows(pl.ds(pl.multiple_of(k * FFN_CHUNK, RUN_ALIGN), FFN_CHUNK), ee)
                return carry

            lax.fori_loop(0, lax.div(n + (FFN_CHUNK - 1), FFN_CHUNK), ffn, 0)
            unpack(e, 0, False)

    @pl.when(g == pl.num_programs(1) - 1)
    def _combine():
        earlier = jnp.where(tri_c < tri_r, 1.0, 0.0).astype(BF16)
        for j in range(n_sub):
            tok = slice(j * SUB, (j + 1) * SUB)
            x = x1_ref[tok, :]
            xb = x.astype(BF16)
            hs = (jax.nn.silu(jnp.dot(xb, wsg_ref[...], preferred_element_type=F32))
                  * jnp.dot(xb, wsu_ref[...], preferred_element_type=F32))
            y = jnp.dot(hs.astype(BF16), wsd_ref[...], preferred_element_type=F32)
            comb = comb_ref[tok, :]
            sel = comb != 0.0
            pos = jnp.dot(earlier, jnp.where(sel, 1.0, 0.0).astype(BF16),
                          preferred_element_type=F32)
            posm = jnp.where(sel, pos, -1.0).astype(BF16)
            c_hi = comb.astype(BF16)
            c_lo = (comb - c_hi.astype(F32)).astype(BF16)
            off = offc_ref[j]
            cnt = cntc_ref[j]
            used = used_rows(j)

            def gather_rows(rc, j=j, posm=posm, c_hi=c_hi, c_lo=c_lo, off=off, cnt=cnt):
                r = (lax.broadcasted_iota(jnp.int32, (1, SEL_CHUNK), 1)
                     + rc * SEL_CHUNK).astype(F32)
                hit, start1 = _expert_onehot(r, off, cnt)
                s = jnp.sum(start1, axis=0, keepdims=True)
                q = jnp.where(s > 0.0, r - (s - 1.0), -2.0)
                hb = jnp.where(hit, 1.0, 0.0).astype(BF16)
                pick = jnp.dot(posm, hb, preferred_element_type=F32) == q
                w_hi = jnp.where(pick, jnp.dot(c_hi, hb, preferred_element_type=F32), 0.0)
                w_lo = jnp.where(pick, jnp.dot(c_lo, hb, preferred_element_type=F32), 0.0)
                rows = gbuf[j, rc * SEL_CHUNK:(rc + 1) * SEL_CHUNK, :]
                return (jnp.dot(w_hi.astype(BF16), rows, preferred_element_type=F32)
                        + jnp.dot(w_lo.astype(BF16), rows, preferred_element_type=F32))

            last = cap // SEL_CHUNK - 1
            for rc in range(last):
                y = y + gather_rows(rc)
            n_lt = x.shape[1] // LANES
            lane_tiles = lambda v: [v[:, c * LANES:(c + 1) * LANES] for c in range(n_lt)]

            def put(v):
                for c, t in enumerate(lane_tiles(v)):
                    zbuf[c] = t

            get = lambda: jnp.concatenate([zbuf[c] for c in range(n_lt)], axis=1)
            put(alpha * x + y)

            @pl.when(last * SEL_CHUNK < used)
            def _tail(gather_rows=gather_rows, put=put, get=get):
                put(get() + gather_rows(last))

            put(_layer_norm(get(), g_ref[...], b_ref[...]))
            t_sub = SUB // batch
            for b in range(batch):
                for c in range(n_lt):
                    out_ref[b, j * t_sub:(j + 1) * t_sub, c * LANES:(c + 1) * LANES] = (
                        zbuf[c, pl.ds(b, t_sub, stride=batch), :])


def _moe_sparse(x1, comb, combt, cnt, p, w_bf16, *, alpha, batch, n_sub, eg=EXPERTS_PER_STEP):
    total, d = x1.shape
    w_gate, w_up, w_down = w_bf16
    n_exp, _, f = w_gate.shape
    fs = p['ws_gate'].shape[1]
    tile = n_sub * SUB
    n_tiles = total // tile
    assert SUB % batch == 0
    cap = -(-(SUB * TOP_K + n_exp * (RUN_ALIGN - 1)) // SEL_CHUNK) * SEL_CHUNK
    xe_rows = -(-(tile + n_sub * (RUN_ALIGN - 1)) // FFN_CHUNK) * FFN_CHUNK

    cnt = cnt[:, 0, :]
    pc = jnp.ceil(cnt / RUN_ALIGN) * RUN_ALIGN
    off = jnp.cumsum(pc, axis=1) - pc
    off_s = off.astype(jnp.int32).reshape(-1)
    pc_s = pc.astype(jnp.int32).reshape(-1)
    offr, cntr = off[:, None, :], cnt[:, None, :]
    offc, cntc = off[:, :, None], cnt[:, :, None]

    row_spec = pl.BlockSpec((n_sub, 1, n_exp), lambda i, g, *_: (i, 0, 0))
    col_spec = pl.BlockSpec((n_sub, n_exp, 1), lambda i, g, *_: (i, 0, 0))
    const = lambda shape: pl.BlockSpec(shape, lambda i, g, *_: (0,) * len(shape),
                                       pipeline_mode=pl.Buffered(1))
    grid_spec = pltpu.PrefetchScalarGridSpec(
        num_scalar_prefetch=2,
        grid=(n_tiles, n_exp // eg),
        in_specs=[pl.BlockSpec((tile, d), lambda i, g, *_: (i, 0), pipeline_mode=pl.Buffered(1)),
                  pl.BlockSpec((tile, n_exp), lambda i, g, *_: (i, 0)),
                  pl.BlockSpec((n_exp, tile), lambda i, g, *_: (0, i)),
                  row_spec, row_spec, col_spec, col_spec,
                  pl.BlockSpec((eg, d, f), lambda i, g, *_: (g, 0, 0)),
                  pl.BlockSpec((eg, d, f), lambda i, g, *_: (g, 0, 0)),
                  pl.BlockSpec((eg, f, d), lambda i, g, *_: (g, 0, 0)),
                  const((d, fs)), const((d, fs)), const((fs, d)),
                  const((1, d)), const((1, d))],
        out_specs=pl.BlockSpec((batch, tile // batch, d), lambda i, g, *_: (0, i, 0)),
        scratch_shapes=[pltpu.VMEM((n_sub, cap, d), BF16), pltpu.VMEM((xe_rows, d), BF16),
                        pltpu.VMEM((d // LANES, SUB, LANES), F32)])
    return pl.pallas_call(
        functools.partial(_moe_sparse_kernel, alpha, batch, n_sub, cap, eg),
        grid_spec=grid_spec,
        out_shape=jax.ShapeDtypeStruct((batch, total // batch, d), F32),
        compiler_params=pltpu.CompilerParams(
            dimension_semantics=("arbitrary", "arbitrary"), vmem_limit_bytes=VMEM_LIMIT),
        name="moe_sparse",
    )(off_s, pc_s, x1, comb, combt, offr, cntr, offc, cntc, w_gate, w_up, w_down,
      p['ws_gate'].astype(BF16), p['ws_up'].astype(BF16), p['ws_down'].astype(BF16),
      p['ln2_g'], p['ln2_b'])


def _layer_params(l, w):
    g, n = w['a_re'].shape[1:]
    lr, li, bbr, bbi = _ssm_prep(w['a_re'][l], w['a_im'][l], w['log_dt'][l],
                                 w['ssm_b_re'][l], w['ssm_b_im'][l])
    bbd = jnp.concatenate([_block_diag(bbr, SSM_BLOCKS), _block_diag(bbi, SSM_BLOCKS)],
                          axis=-1).astype(BF16)
    c_re = w['ssm_c_re'][l].transpose(0, 2, 1)
    c_im = w['ssm_c_im'][l].transpose(0, 2, 1)
    cbd = jnp.concatenate([_block_diag(c_re, SSM_BLOCKS), _block_diag(-c_im, SSM_BLOCKS)],
                          axis=1).astype(BF16)
    row = lambda v: v.reshape(1, -1)
    return {
        'w_in': w['w_in'][l].astype(BF16), 'b_in': row(w['b_in'][l]),
        'conv_w': w['conv_w'][l], 'w_conv_out': w['w_conv_out'][l].astype(BF16),
        'lam_r': lr.reshape(1, g * n), 'lam_i': li.reshape(1, g * n),
        'bbd': bbd, 'cbd': cbd, 'ssm_d': row(w['ssm_d'][l]),
        'w_glu': w['w_glu'][l].astype(BF16), 'b_glu': row(w['b_glu'][l]),
        'w_ssm_out': w['w_ssm_out'][l].astype(BF16), 'w_o': w['w_o'][l].astype(BF16),
        'ln1_g': row(w['ln1_g'][l]), 'ln1_b': row(w['ln1_b'][l]),
        'w_router_t': w['w_router'][l].T, 'router_bias': w['router_bias'][l].reshape(-1, 1),
        'w_gate': w['w_gate'][l], 'w_up': w['w_up'][l], 'w_down': w['w_down'][l],
        'ws_gate': w['ws_gate'][l], 'ws_up': w['ws_up'][l], 'ws_down': w['ws_down'][l],
        'ln2_g': row(w['ln2_g'][l]), 'ln2_b': row(w['ln2_b'][l]),
    }


def _pick_steps(batch, seq, max_rows):
    steps = max(1, min(seq, max_rows // batch))
    while seq % steps:
        steps -= 1
    return steps


def _pick_tile(total, max_tile):
    tile = min(total, max_tile)
    while total % tile or tile % 16:
        tile -= 16
    return tile


MIXER_ROWS = 512
MOE_TILE = 1024


def _trunk_layer(x, conv_l, re_l, im_l, p, alpha, w_bf16):
    bsz, seq, d = x.shape
    assert bsz % SUBLANES == 0
    steps = _pick_steps(bsz, seq, MIXER_ROWS)
    kw = conv_l.shape[1]
    assert kw == 2
    cprev = conv_l.astype(F32).transpose(1, 0, 2).reshape(kw * bsz, d)
    h0r = re_l.astype(F32).reshape(bsz, -1)
    h0i = im_l.astype(F32).reshape(bsz, -1)
    if ((bsz * steps) % SUB == 0 and (bsz * seq) % (DISPATCH_SUBTILES * SUB) == 0
            and SUB % bsz == 0):
        if w_bf16 is None:
            w_bf16 = tuple(p[k].astype(BF16) for k in ('w_gate', 'w_up', 'w_down'))
        x1, comb, cnew, hr, hi, combt, cnt = _mixer(
            x, cprev, h0r, h0i, p, alpha=alpha, batch=bsz, steps=steps, sub=SUB)
        y = _moe_sparse(x1, comb, combt, cnt, p, w_bf16, alpha=alpha, batch=bsz,
                        n_sub=DISPATCH_SUBTILES)
    else:
        rows = x.transpose(1, 0, 2).reshape(seq * bsz, d)
        x1, comb, cnew, hr, hi = _mixer(rows, cprev, h0r, h0i, p,
                                        alpha=alpha, batch=bsz, steps=steps)
        out, w_bf16 = _moe(x1, comb, p, alpha=alpha, tile=_pick_tile(bsz * seq, MOE_TILE))
        y = out.reshape(seq, bsz, d).transpose(1, 0, 2)
    states = (cnew.reshape(kw, bsz, d).transpose(1, 0, 2),
              hr.reshape(re_l.shape), hi.reshape(im_l.shape))
    return y, states, w_bf16


def kernel(x_prompt, x_sample, state_conv, state_ssm_re, state_ssm_im,
           w_in, b_in, conv_w, w_conv_out, a_re, a_im, log_dt,
           ssm_b_re, ssm_b_im, ssm_c_re, ssm_c_im, ssm_d, w_glu, b_glu, w_ssm_out, w_o,
           ln1_g, ln1_b, w_router, router_bias, w_gate, w_up, w_down,
           ws_gate, ws_up, ws_down, ln2_g, ln2_b):
    w = dict(w_in=w_in, b_in=b_in, conv_w=conv_w, w_conv_out=w_conv_out,
             a_re=a_re, a_im=a_im, log_dt=log_dt,
             ssm_b_re=ssm_b_re, ssm_b_im=ssm_b_im, ssm_c_re=ssm_c_re, ssm_c_im=ssm_c_im,
             ssm_d=ssm_d, w_glu=w_glu, b_glu=b_glu, w_ssm_out=w_ssm_out, w_o=w_o,
             ln1_g=ln1_g, ln1_b=ln1_b, w_router=w_router, router_bias=router_bias,
             w_gate=w_gate, w_up=w_up, w_down=w_down,
             ws_gate=ws_gate, ws_up=ws_up, ws_down=ws_down, ln2_g=ln2_g, ln2_b=ln2_b)
    depth = w_in.shape[0]
    alpha = (2.0 * depth) ** 0.25
    bsz = x_prompt.shape[0]
    zero_conv = jnp.zeros((bsz,) + state_conv.shape[2:], x_prompt.dtype)
    zero_ssm = jnp.zeros((bsz,) + state_ssm_re.shape[2:], F32)
    y_p, y_s = x_prompt, x_sample
    st_p, st_s = [], []
    for l in range(depth):
        p = _layer_params(l, w)
        y_s, st, w_bf16 = _trunk_layer(y_s, state_conv[l], state_ssm_re[l], state_ssm_im[l],
                                       p, alpha, None)
        st_s.append(st)
        y_p, st, _ = _trunk_layer(y_p, zero_conv, zero_ssm, zero_ssm, p, alpha, w_bf16)
        st_p.append(st)
    conv_p, re_p, im_p = (jnp.stack(v) for v in zip(*st_p))
    conv_s, re_s, im_s = (jnp.stack(v) for v in zip(*st_s))
    return (y_p, y_s, conv_p, re_p, im_p, conv_s, re_s, im_s)
```

```python
import functools
import math

import jax
import jax.numpy as jnp
from jax import lax
from jax.experimental import pallas as pl
from jax.experimental.pallas import tpu as pltpu

F32 = jnp.float32
BF16 = jnp.bfloat16

LN_EPS = 1e-5
ROUTED_SCALE = 2.5
N_ROUTE_GROUPS = 8
TOPK_GROUPS = 4
TOP_K = 8

SUBLANES = 8
LANES = 128
SSM_BLOCKS = 4
SCAN_UNROLL = 4
POST_ROWS = 512
VMEM_LIMIT = 60 * 1024 * 1024


def _const_spec(shape):
    nd = len(shape)
    return pl.BlockSpec(shape, lambda *_: (0,) * nd, pipeline_mode=pl.Buffered(1))


def _ssm_prep_kernel(are_ref, aim_ref, ldt_ref, br_ref, bi_ref,
                     lr_ref, li_ref, bbr_ref, bbi_ref):
    dt = jnp.exp(ldt_ref[...])
    ar = are_ref[...]
    ai = aim_ref[...]
    mag = jnp.exp(ar * dt)
    lr = mag * jnp.cos(ai * dt)
    li = mag * jnp.sin(ai * dt)
    den = ar * ar + ai * ai
    fr = ((lr - 1.0) * ar + li * ai) / den
    fi = (li * ar - (lr - 1.0) * ai) / den
    lr_ref[...] = lr
    li_ref[...] = li
    br = br_ref[...]
    bi = bi_ref[...]
    bbr_ref[...] = fr * br - fi * bi
    bbi_ref[...] = fr * bi + fi * br


def _ssm_prep(a_re, a_im, log_dt, b_re, b_im):
    g, n = a_re.shape
    h = b_re.shape[-1]
    vec = jax.ShapeDtypeStruct((g, 1, n), F32)
    mat = jax.ShapeDtypeStruct((g, h, n), F32)
    return pl.pallas_call(
        _ssm_prep_kernel,
        out_shape=(vec, vec, mat, mat),
        name="ssm_prep",
    )(a_re.reshape(g, 1, n), a_im.reshape(g, 1, n), log_dt.reshape(g, 1, 1),
      b_re.transpose(0, 2, 1), b_im.transpose(0, 2, 1))


def _block_diag(m, nblk):
    g, p, q = m.shape
    gl = g // nblk
    eye = jnp.eye(gl, dtype=m.dtype)
    out = jnp.einsum('kapq,ab->kapbq', m.reshape(nblk, gl, p, q), eye)
    return out.reshape(nblk, gl * p, gl * q)


def _layer_norm(r, g, b):
    mu = jnp.mean(r, axis=-1, keepdims=True)
    d = r - mu
    var = jnp.mean(d * d, axis=-1, keepdims=True)
    return d * lax.rsqrt(var + LN_EPS) * g + b


def _split_bf16(v):
    hi = v.astype(BF16)
    return hi, (v - hi.astype(F32)).astype(BF16)


def _route(scores, biased):
    n_exp, r = scores.shape
    gsz = n_exp // N_ROUTE_GROUPS
    neg = jnp.float32(-jnp.inf)
    rows = []
    for g in range(N_ROUTE_GROUPS):
        v = biased[g * gsz:(g + 1) * gsz, :]
        m1 = jnp.max(v, axis=0, keepdims=True)
        is_max = v == m1
        n_max = jnp.sum(is_max.astype(F32), axis=0, keepdims=True)
        rest = jnp.max(jnp.where(is_max, neg, v), axis=0, keepdims=True)
        rows.append(m1 + jnp.where(n_max >= 2.0, m1, rest))
    gscore = jnp.concatenate(rows, axis=0)
    gidx = lax.broadcasted_iota(jnp.int32, gscore.shape, 0)
    grank = jnp.zeros(gscore.shape, F32)
    for g in range(N_ROUTE_GROUPS):
        sg = gscore[g:g + 1, :]
        beats = (sg > gscore) | ((sg == gscore) & (gidx > g))
        grank = grank + beats.astype(F32)
    gkeep = grank < float(TOPK_GROUPS)
    masked = jnp.concatenate(
        [jnp.where(gkeep[g:g + 1, :], biased[g * gsz:(g + 1) * gsz, :], neg)
         for g in range(N_ROUTE_GROUPS)], axis=0)
    eidx = lax.broadcasted_iota(jnp.int32, masked.shape, 0).astype(F32)
    left = masked
    for _ in range(TOP_K):
        top = jnp.max(left, axis=0, keepdims=True)
        first = jnp.min(jnp.where(left == top, eidx, float(n_exp)), axis=0, keepdims=True)
        left = jnp.where(eidx == first, neg, left)
    w = jnp.where(left != masked, scores, 0.0)
    return w / jnp.sum(w, axis=0, keepdims=True) * ROUTED_SCALE


def _mixer_kernel(alpha, batch, steps, sub,
                  x_ref, cprev_ref, h0r_ref, h0i_ref,
                  win_ref, bin_ref, convw_ref, wco_ref,
                  lamr_ref, lami_ref, bbd_ref, cbd_ref, dskip_ref,
                  wglu_ref, bglu_ref, wso_ref, wo_ref, ln1g_ref, ln1b_ref,
                  wrt_ref, rbias_ref,
                  x1_ref, comb_ref, cnew_ref, hr_ref, hi_ref, *rest):
    if sub:
        combt_ref, cnt_ref, *rest = rest
    ubuf, xk_ref, ys_ref, *rest = rest
    rows = batch * steps
    d = x_ref.shape[-1]
    d_blk = d // SSM_BLOCKS
    n_blk = lamr_ref.shape[1] // SSM_BLOCKS
    i = pl.program_id(0)

    @pl.when(i == 0)
    def _init():
        ubuf[0:2 * batch, :] = cprev_ref[...]
        hr_ref[...] = h0r_ref[...]
        hi_ref[...] = h0i_ref[...]

    if len(x_ref.shape) == 3:
        xs_ref, = rest
        for b in range(batch):
            for c in range(d // LANES):
                xs_ref[c, pl.ds(b, steps, stride=batch), :] = x_ref[b, :, c * LANES:(c + 1) * LANES]
        x = jnp.concatenate([xs_ref[c] for c in range(d // LANES)], axis=1)
    else:
        x = x_ref[...]
    xb = x.astype(BF16)

    def proj(c):
        cols = slice(c * d, (c + 1) * d)
        return (jnp.dot(xb, win_ref[:, cols], preferred_element_type=F32)
                + bin_ref[:, cols])

    u = proj(1) * proj(2)
    ubuf[2 * batch:2 * batch + rows, :] = u
    conv = (convw_ref[0:1, :] * ubuf[0:rows, :]
            + convw_ref[1:2, :] * ubuf[batch:batch + rows, :]
            + convw_ref[2:3, :] * u)
    ya = jnp.dot((proj(0) * conv).astype(BF16), wco_ref[...],
                 preferred_element_type=F32)
    tail = ubuf[rows:rows + 2 * batch, :]
    ubuf[0:2 * batch, :] = tail
    cnew_ref[...] = tail
    ya_rows = slice(2 * batch, 2 * batch + rows)
    ubuf[ya_rows, :] = ya

    us = proj(3)
    half = n_blk // 2
    unroll = SCAN_UNROLL if steps % SCAN_UNROLL == 0 else 1
    for k in range(SSM_BLOCKS):
        usk = us[:, k * d_blk:(k + 1) * d_blk].astype(BF16)
        xk_ref[...] = jnp.dot(usk, bbd_ref[k], preferred_element_type=F32)
        for hf in range(2):
            st = slice(k * n_blk + hf * half, k * n_blk + (hf + 1) * half)
            re = slice(hf * half, (hf + 1) * half)
            im = slice(n_blk + hf * half, n_blk + (hf + 1) * half)
            if steps == 1:
                lr = lamr_ref[:, st]
                li = lami_ref[:, st]
                hr = hr_ref[:, st]
                hi = hi_ref[:, st]
                nhr = lr * hr - li * hi + xk_ref[:, re]
                nhi = lr * hi + li * hr + xk_ref[:, im]
                xk_ref[:, re] = nhr
                xk_ref[:, im] = nhi
                hr_ref[:, st] = nhr
                hi_ref[:, st] = nhi
            else:
                lr = jnp.broadcast_to(lamr_ref[:, st], (SUBLANES, half))
                li = jnp.broadcast_to(lami_ref[:, st], (SUBLANES, half))
                for s in range(batch // SUBLANES):
                    grp = slice(s * SUBLANES, (s + 1) * SUBLANES)

                    def step(tt, carry, s=s, re=re, im=im, lr=lr, li=li):
                        hr, hi = carry
                        for k_un in range(unroll):
                            row = pl.multiple_of((tt * unroll + k_un) * batch + s * SUBLANES,
                                                 SUBLANES)
                            rs = pl.ds(row, SUBLANES)
                            hr, hi = (lr * hr - li * hi + xk_ref[rs, re],
                                      lr * hi + li * hr + xk_ref[rs, im])
                            xk_ref[rs, re] = hr
                            xk_ref[rs, im] = hi
                        return hr, hi

                    hr, hi = lax.fori_loop(0, steps // unroll, step,
                                           (hr_ref[grp, st], hi_ref[grp, st]))
                    hr_ref[grp, st] = hr
                    hi_ref[grp, st] = hi
        ys_ref[:, k * d_blk:(k + 1) * d_blk] = jnp.dot(
            xk_ref[...].astype(BF16), cbd_ref[k], preferred_element_type=F32)
    ys_ref[...] = ys_ref[...] + dskip_ref[...] * us

    chunk = min(rows, POST_ROWS)
    for q in range(rows // chunk):
        rq = slice(q * chunk, (q + 1) * chunk)
        if len(x_ref.shape) == 3:
            xq = jnp.concatenate([xs_ref[c, rq, :] for c in range(d // LANES)], axis=1)
        else:
            xq = x_ref[rq, :]
        xbq = xq.astype(BF16)

        def projq(c, xbq=xbq):
            cols = slice(c * d, (c + 1) * d)
            return (jnp.dot(xbq, win_ref[:, cols], preferred_element_type=F32)
                    + bin_ref[:, cols])

        z = jax.nn.gelu(ys_ref[rq, :])
        gate = (jnp.dot(z.astype(BF16), wglu_ref[...], preferred_element_type=F32)
                + bglu_ref[...])
        glu = z * jax.nn.sigmoid(gate)
        yb = jnp.dot(glu.astype(BF16), wso_ref[...], preferred_element_type=F32)

        m = (jax.nn.sigmoid(projq(4)) * ubuf[2 * batch + q * chunk:2 * batch + (q + 1) * chunk, :]
             + jax.nn.sigmoid(projq(5)) * yb)
        o = jnp.dot(m.astype(BF16), wo_ref[...], preferred_element_type=F32)
        x1 = _layer_norm(alpha * xq + o, ln1g_ref[...], ln1b_ref[...])
        x1_ref[rq, :] = x1

        nt = (((1,), (1,)), ((), ()))
        w_hi, w_lo = _split_bf16(wrt_ref[...])
        x_hi, x_lo = _split_bf16(x1)
        logits = (lax.dot_general(w_hi, x_hi, nt, preferred_element_type=F32)
                  + lax.dot_general(w_hi, x_lo, nt, preferred_element_type=F32)
                  + lax.dot_general(w_lo, x_hi, nt, preferred_element_type=F32))
        scores = jax.nn.sigmoid(logits)
        comb = _route(scores, scores + rbias_ref[...])
        comb_ref[rq, :] = comb.T
        if sub:
            combt_ref[:, rq] = comb
    if sub:
        ones = jnp.ones((SUBLANES, sub), BF16)
        for s in range(rows // sub):
            sel = jnp.where(combt_ref[:, s * sub:(s + 1) * sub] != 0.0, 1.0, 0.0).astype(BF16)
            cnt_ref[s] = lax.dot_general(ones, sel, (((1,), (1,)), ((), ())),
                                         preferred_element_type=F32)


def _mixer(x, cprev, h0r, h0i, p, *, alpha, batch, steps, sub=0):
    d = x.shape[-1]
    total = x.size // d
    rows = batch * steps
    assert not sub or rows % sub == 0
    n_state = h0r.shape[1]
    n_exp = p['w_router_t'].shape[0]
    grid = (total // rows,)
    if x.ndim == 3:
        x_spec = pl.BlockSpec((batch, steps, d), lambda i: (0, i, 0))
    else:
        x_spec = pl.BlockSpec((rows, d), lambda i: (i, 0))
    consts = [p['w_in'], p['b_in'], p['conv_w'], p['w_conv_out'],
              p['lam_r'], p['lam_i'], p['bbd'], p['cbd'], p['ssm_d'],
              p['w_glu'], p['b_glu'], p['w_ssm_out'], p['w_o'], p['ln1_g'], p['ln1_b'],
              p['w_router_t'], p['router_bias']]
    in_specs = ([x_spec,
                 _const_spec(cprev.shape), _const_spec(h0r.shape), _const_spec(h0i.shape)]
                + [_const_spec(c.shape) for c in consts])
    out_shape = (jax.ShapeDtypeStruct((total, d), F32),
                 jax.ShapeDtypeStruct((total, n_exp), F32),
                 jax.ShapeDtypeStruct((2 * batch, d), F32),
                 jax.ShapeDtypeStruct((batch, n_state), F32),
                 jax.ShapeDtypeStruct((batch, n_state), F32))
    out_specs = (pl.BlockSpec((rows, d), lambda i: (i, 0)),
                 pl.BlockSpec((rows, n_exp), lambda i: (i, 0)),
                 pl.BlockSpec((2 * batch, d), lambda i: (0, 0)),
                 pl.BlockSpec((batch, n_state), lambda i: (0, 0)),
                 pl.BlockSpec((batch, n_state), lambda i: (0, 0)))
    if sub:
        out_shape += (jax.ShapeDtypeStruct((n_exp, total), F32),
                      jax.ShapeDtypeStruct((total // sub, SUBLANES, n_exp), F32))
        out_specs += (pl.BlockSpec((n_exp, rows), lambda i: (0, i)),
                      pl.BlockSpec((rows // sub, SUBLANES, n_exp), lambda i: (i, 0, 0)))
    scratch = [pltpu.VMEM((rows + 2 * batch, d), F32),
               pltpu.VMEM((rows, 2 * n_state // SSM_BLOCKS), F32),
               pltpu.VMEM((rows, d), F32)]
    if x.ndim == 3:
        scratch.append(pltpu.VMEM((d // LANES, rows, LANES), F32))
    return pl.pallas_call(
        functools.partial(_mixer_kernel, alpha, batch, steps, sub),
        grid=grid, in_specs=in_specs, out_specs=out_specs, out_shape=out_shape,
        scratch_shapes=scratch,
        compiler_params=pltpu.CompilerParams(
            dimension_semantics=("arbitrary",), vmem_limit_bytes=VMEM_LIMIT),
        name="mixer",
    )(x, cprev, h0r, h0i, *consts)


def _swiglu(xb, wg, wu):
    g = jnp.dot(xb, wg.astype(BF16), preferred_element_type=F32)
    u = jnp.dot(xb, wu.astype(BF16), preferred_element_type=F32)
    return jax.nn.silu(g) * u


def _moe_kernel(alpha, emit_bf16, x1_ref, comb_ref, wg_ref, wu_ref, wd_ref,
                wsg_ref, wsu_ref, wsd_ref, g_ref, b_ref, out_ref, *rest):
    if emit_bf16:
        wgb_ref, wub_ref, wdb_ref, xb_ref, acc_ref = rest
    else:
        xb_ref, acc_ref = rest
    e = pl.program_id(1)

    @pl.when(e == 0)
    def _shared():
        xb = x1_ref[...].astype(BF16)
        xb_ref[...] = xb
        hs = _swiglu(xb, wsg_ref[...], wsu_ref[...])
        acc_ref[...] = jnp.dot(hs.astype(BF16), wsd_ref[...].astype(BF16),
                               preferred_element_type=F32)

    wg = wg_ref[...].astype(BF16)
    wu = wu_ref[...].astype(BF16)
    wd = wd_ref[...].astype(BF16)
    if emit_bf16:
        wgb_ref[...] = wg
        wub_ref[...] = wu
        wdb_ref[...] = wd
    comb = comb_ref[...]
    lane = lax.broadcasted_iota(jnp.int32, comb.shape, 1)
    c = jnp.sum(jnp.where(lane == e, comb, 0.0), axis=1, keepdims=True)
    h = _swiglu(xb_ref[...], wg, wu) * c
    acc_ref[...] += jnp.dot(h.astype(BF16), wd, preferred_element_type=F32)

    @pl.when(e == pl.num_programs(1) - 1)
    def _finish():
        out_ref[...] = _layer_norm(alpha * x1_ref[...] + acc_ref[...],
                                   g_ref[...], b_ref[...])


def _moe(x1, comb, p, *, alpha, tile):
    total, d = x1.shape
    n_exp, _, f = p['w_gate'].shape
    fs = p['ws_gate'].shape[1]
    grid = (total // tile, n_exp)
    emit_bf16 = grid[0] == 1
    up_spec = pl.BlockSpec((None, d, f), lambda i, e: (e, 0, 0))
    down_spec = pl.BlockSpec((None, f, d), lambda i, e: (e, 0, 0))
    in_specs = [pl.BlockSpec((tile, d), lambda i, e: (i, 0)),
                pl.BlockSpec((tile, n_exp), lambda i, e: (i, 0)),
                up_spec, up_spec, down_spec,
                _const_spec((d, fs)), _const_spec((d, fs)), _const_spec((fs, d)),
                _const_spec((1, d)), _const_spec((1, d))]
    out_specs = [pl.BlockSpec((tile, d), lambda i, e: (i, 0))]
    out_shape = [jax.ShapeDtypeStruct((total, d), F32)]
    if emit_bf16:
        out_specs += [up_spec, up_spec, down_spec]
        out_shape += [jax.ShapeDtypeStruct(p[k].shape, BF16)
                      for k in ('w_gate', 'w_up', 'w_down')]
    res = pl.pallas_call(
        functools.partial(_moe_kernel, alpha, emit_bf16),
        grid=grid, in_specs=in_specs, out_specs=out_specs, out_shape=out_shape,
        scratch_shapes=[pltpu.VMEM((tile, d), BF16), pltpu.VMEM((tile, d), F32)],
        compiler_params=pltpu.CompilerParams(
            dimension_semantics=("arbitrary", "arbitrary"), vmem_limit_bytes=VMEM_LIMIT),
        name="moe",
    )(x1, comb, p['w_gate'], p['w_up'], p['w_down'],
      p['ws_gate'], p['ws_up'], p['ws_down'], p['ln2_g'], p['ln2_b'])
    if emit_bf16:
        return res[0], tuple(res[1:])
    return res[0], tuple(p[k].astype(BF16) for k in ('w_gate', 'w_up', 'w_down'))


SUB = 256
RUN_ALIGN = 16
FFN_CHUNK = 192
SEL_CHUNK = 512
PACK_STATIC = 4
PACK_FULL = 2
DISPATCH_SUBTILES = 4
EXPERTS_PER_STEP = 4


def _expert_onehot(r, off, cnt):
    hit = (r >= off) & (r < off + cnt)
    return hit, jnp.where(hit, off + 1.0, 0.0)


def _moe_sparse_kernel(alpha, batch, n_sub, cap, eg,
                       off_s, pc_s,
                       x1_ref, comb_ref, combt_ref, offr_ref, cntr_ref, offc_ref, cntc_ref,
                       wg_ref, wu_ref, wd_ref, wsg_ref, wsu_ref, wsd_ref, g_ref, b_ref,
                       out_ref, gbuf, xe, zbuf, *xq):
    i = pl.program_id(0)
    g = pl.program_id(1)
    n_exp = comb_ref.shape[1]
    tri_r = lax.broadcasted_iota(jnp.int32, (SUB, SUB), 0)
    tri_c = lax.broadcasted_iota(jnp.int32, (SUB, SUB), 1)

    def run(j, e):
        idx = (i * n_sub + j) * n_exp + e
        return off_s[idx], pc_s[idx]

    def used_rows(j):
        o, p = run(j, n_exp - 1)
        return o + p

    def rows_at(start, chunks):
        return pl.ds(pl.multiple_of(start, RUN_ALIGN), chunks * RUN_ALIGN)

    rows16 = functools.partial(rows_at, chunks=1)

    @pl.when((i == 0) & (g == 0))
    def _zero():
        xe[...] = jnp.zeros(xe.shape, xe.dtype)
        for buf in xq:
            buf[...] = jnp.zeros(buf.shape, buf.dtype)

    @pl.when(g == 0)
    def _dispatch():
        before = jnp.where(tri_r < tri_c, 1.0, 0.0).astype(BF16)
        for j in range(n_sub):
            tok = slice(j * SUB, (j + 1) * SUB)
            xj = x1_ref[tok, :].astype(BF16)
            sel = combt_ref[:, tok] != 0.0
            pos = jnp.dot(jnp.where(sel, 1.0, 0.0).astype(BF16), before,
                          preferred_element_type=F32)
            posm = jnp.where(sel, pos, -1.0).astype(BF16)
            off = offr_ref[j]
            cnt = cntr_ref[j]
            used = used_rows(j)

            def sort_rows(rc, j=j, xj=xj, posm=posm, off=off, cnt=cnt):
                r = (lax.broadcasted_iota(jnp.int32, (SEL_CHUNK, 1), 0)
                     + rc * SEL_CHUNK).astype(F32)
                hit, start1 = _expert_onehot(r, off, cnt)
                s = jnp.sum(start1, axis=1, keepdims=True)
                q = jnp.where(s > 0.0, r - (s - 1.0), -2.0)
                rank = jnp.dot(jnp.where(hit, 1.0, 0.0).astype(BF16), posm,
                               preferred_element_type=F32)
                pick = jnp.where(rank == q, 1.0, 0.0).astype(BF16)
                gbuf[j, rc * SEL_CHUNK:(rc + 1) * SEL_CHUNK, :] = jnp.dot(
                    pick, xj, preferred_element_type=F32).astype(BF16)

            last = cap // SEL_CHUNK - 1
            for rc in range(last):
                sort_rows(rc)
            pl.when(last * SEL_CHUNK < used)(functools.partial(sort_rows, last))

            @pl.when(last * SEL_CHUNK >= used)
            def _blank(j=j):
                gbuf[j, last * SEL_CHUNK:, :] = jnp.zeros((SEL_CHUNK, gbuf.shape[2]), BF16)

    def ffn_rows(rs, ee):
        xc = xe[rs, :]
        h = (jax.nn.silu(jnp.dot(xc, wg_ref[ee], preferred_element_type=F32))
             * jnp.dot(xc, wu_ref[ee], preferred_element_type=F32))
        xe[rs, :] = jnp.dot(h.astype(BF16), wd_ref[ee], preferred_element_type=F32).astype(BF16)

    def pack(e, buf, static):
        n = jnp.int32(0)
        for j in range(n_sub):
            o, p = run(j, e)
            if static:
                buf[rows_at(n, PACK_STATIC), :] = gbuf[j, rows_at(o, PACK_STATIC), :]
            else:
                def chunk(c, carry, j=j, o=o, n=n):
                    buf[rows16(n + c * RUN_ALIGN), :] = gbuf[j, rows16(o + c * RUN_ALIGN), :]
                    return carry

                lax.fori_loop(0, lax.div(p, RUN_ALIGN), chunk, 0)
            n = n + p
        return n

    def unpack(e, buf, static, keep_next=True):
        n = jnp.int32(0)
        for j in range(n_sub):
            o, p = run(j, e)
            if static and not keep_next:
                gbuf[j, rows_at(o, PACK_STATIC), :] = buf[rows_at(n, PACK_STATIC), :]
            elif static:
                full = PACK_FULL * RUN_ALIGN
                gbuf[j, rows_at(o, PACK_FULL), :] = buf[rows_at(n, PACK_FULL), :]
                rest = PACK_STATIC - PACK_FULL
                dst = rows_at(o + full, rest)
                row = lax.broadcasted_iota(jnp.int32, (rest * RUN_ALIGN, gbuf.shape[2]), 0)
                gbuf[j, dst, :] = jnp.where(row < p - full, buf[rows_at(n + full, rest), :],
                                            gbuf[j, dst, :])
            else:
                def chunk(c, carry, j=j, o=o, n=n):
                    gbuf[j, rows16(o + c * RUN_ALIGN), :] = buf[rows16(n + c * RUN_ALIGN), :]
                    return carry

                lax.fori_loop(0, lax.div(p, RUN_ALIGN), chunk, 0)
            n = n + p

    usual = jnp.bool_(True)
    for ee in range(eg):
        n = jnp.int32(0)
        for j in range(n_sub):
            p = run(j, g * eg + ee)[1]
            usual = usual & (p >= PACK_FULL * RUN_ALIGN) & (p <= PACK_STATIC * RUN_ALIGN)
            n = n + p
        usual = usual & (n <= FFN_CHUNK)

    @pl.when(usual)
    def _together():
        for ee in range(eg):
            pack(g * eg + ee, xq[ee], True)
        xcs = [xq[ee][0:FFN_CHUNK, :] for ee in range(eg)]
        hs = [(jax.nn.silu(jnp.dot(xc, wg_ref[ee], preferred_element_type=F32))
               * jnp.dot(xc, wu_ref[ee], preferred_element_type=F32)).astype(BF16)
              for ee, xc in enumerate(xcs)]
        outs = [jnp.dot(h, wd_ref[ee], preferred_element_type=F32).astype(BF16)
                for ee, h in enumerate(hs)]
        for ee, o in enumerate(outs):
            xq[ee][0:FFN_CHUNK, :] = o
        for ee in range(eg):
            unpack(g * eg + ee, xq[ee], True, keep_next=ee == eg - 1)

    @pl.when(jnp.logical_not(usual))
    def _one_by_one():
        for ee in range(eg):
            e = g * eg + ee
            n = pack(e, xe, False)

            def ffn(k, carry, ee=ee):
                ffn_rows(pl.ds(pl.multiple_of(k * FFN_CHUNK, RUN_ALIGN), FFN_CHUNK), ee)
                return carry

            lax.fori_loop(0, lax.div(n + (FFN_CHUNK - 1), FFN_CHUNK), ffn, 0)
            unpack(e, xe, False)

    @pl.when(g == pl.num_programs(1) - 1)
    def _combine():
        earlier = jnp.where(tri_c < tri_r, 1.0, 0.0).astype(BF16)
        for j in range(n_sub):
            tok = slice(j * SUB, (j + 1) * SUB)
            x = x1_ref[tok, :]
            xb = x.astype(BF16)
            hs = (jax.nn.silu(jnp.dot(xb, wsg_ref[...], preferred_element_type=F32))
                  * jnp.dot(xb, wsu_ref[...], preferred_element_type=F32))
            y = jnp.dot(hs.astype(BF16), wsd_ref[...], preferred_element_type=F32)
            comb = comb_ref[tok, :]
            sel = comb != 0.0
            pos = jnp.dot(earlier, jnp.where(sel, 1.0, 0.0).astype(BF16),
                          preferred_element_type=F32)
            posm = jnp.where(sel, pos, -1.0).astype(BF16)
            c_hi = comb.astype(BF16)
            c_lo = (comb - c_hi.astype(F32)).astype(BF16)
            off = offc_ref[j]
            cnt = cntc_ref[j]
            used = used_rows(j)

            def gather_rows(rc, j=j, posm=posm, c_hi=c_hi, c_lo=c_lo, off=off, cnt=cnt):
                r = (lax.broadcasted_iota(jnp.int32, (1, SEL_CHUNK), 1)
                     + rc * SEL_CHUNK).astype(F32)
                hit, start1 = _expert_onehot(r, off, cnt)
                s = jnp.sum(start1, axis=0, keepdims=True)
                q = jnp.where(s > 0.0, r - (s - 1.0), -2.0)
                hb = jnp.where(hit, 1.0, 0.0).astype(BF16)
                pick = jnp.dot(posm, hb, preferred_element_type=F32) == q
                w_hi = jnp.where(pick, jnp.dot(c_hi, hb, preferred_element_type=F32), 0.0)
                w_lo = jnp.where(pick, jnp.dot(c_lo, hb, preferred_element_type=F32), 0.0)
                rows = gbuf[j, rc * SEL_CHUNK:(rc + 1) * SEL_CHUNK, :]
                return (jnp.dot(w_hi.astype(BF16), rows, preferred_element_type=F32)
                        + jnp.dot(w_lo.astype(BF16), rows, preferred_element_type=F32))

            last = cap // SEL_CHUNK - 1
            for rc in range(last):
                y = y + gather_rows(rc)
            n_lt = x.shape[1] // LANES
            lane_tiles = lambda v: [v[:, c * LANES:(c + 1) * LANES] for c in range(n_lt)]

            def put(v):
                for c, t in enumerate(lane_tiles(v)):
                    zbuf[c] = t

            get = lambda: jnp.concatenate([zbuf[c] for c in range(n_lt)], axis=1)
            put(alpha * x + y)

            @pl.when(last * SEL_CHUNK < used)
            def _tail(gather_rows=gather_rows, put=put, get=get):
                put(get() + gather_rows(last))

            put(_layer_norm(get(), g_ref[...], b_ref[...]))
            t_sub = SUB // batch
            for b in range(batch):
                for c in range(n_lt):
                    out_ref[b, j * t_sub:(j + 1) * t_sub, c * LANES:(c + 1) * LANES] = (
                        zbuf[c, pl.ds(b, t_sub, stride=batch), :])


def _moe_sparse(x1, comb, combt, cnt, p, w_bf16, *, alpha, batch, n_sub, eg=EXPERTS_PER_STEP):
    total, d = x1.shape
    w_gate, w_up, w_down = w_bf16
    n_exp, _, f = w_gate.shape
    fs = p['ws_gate'].shape[1]
    tile = n_sub * SUB
    n_tiles = total // tile
    assert SUB % batch == 0
    cap = -(-(SUB * TOP_K + n_exp * (RUN_ALIGN - 1)) // SEL_CHUNK) * SEL_CHUNK
    xe_rows = -(-(tile + n_sub * (RUN_ALIGN - 1)) // FFN_CHUNK) * FFN_CHUNK

    cnt = cnt[:, 0, :]
    pc = jnp.ceil(cnt / RUN_ALIGN) * RUN_ALIGN
    off = jnp.cumsum(pc, axis=1) - pc
    off_s = off.astype(jnp.int32).reshape(-1)
    pc_s = pc.astype(jnp.int32).reshape(-1)
    offr, cntr = off[:, None, :], cnt[:, None, :]
    offc, cntc = off[:, :, None], cnt[:, :, None]

    row_spec = pl.BlockSpec((n_sub, 1, n_exp), lambda i, g, *_: (i, 0, 0))
    col_spec = pl.BlockSpec((n_sub, n_exp, 1), lambda i, g, *_: (i, 0, 0))
    const = lambda shape: pl.BlockSpec(shape, lambda i, g, *_: (0,) * len(shape),
                                       pipeline_mode=pl.Buffered(1))
    grid_spec = pltpu.PrefetchScalarGridSpec(
        num_scalar_prefetch=2,
        grid=(n_tiles, n_exp // eg),
        in_specs=[pl.BlockSpec((tile, d), lambda i, g, *_: (i, 0), pipeline_mode=pl.Buffered(1)),
                  pl.BlockSpec((tile, n_exp), lambda i, g, *_: (i, 0)),
                  pl.BlockSpec((n_exp, tile), lambda i, g, *_: (0, i)),
                  row_spec, row_spec, col_spec, col_spec,
                  pl.BlockSpec((eg, d, f), lambda i, g, *_: (g, 0, 0)),
                  pl.BlockSpec((eg, d, f), lambda i, g, *_: (g, 0, 0)),
                  pl.BlockSpec((eg, f, d), lambda i, g, *_: (g, 0, 0)),
                  const((d, fs)), const((d, fs)), const((fs, d)),
                  const((1, d)), const((1, d))],
        out_specs=pl.BlockSpec((batch, tile // batch, d), lambda i, g, *_: (0, i, 0),
                               pipeline_mode=pl.Buffered(1)),
        scratch_shapes=[pltpu.VMEM((n_sub, cap, d), BF16), pltpu.VMEM((xe_rows, d), BF16),
                        pltpu.VMEM((d // LANES, SUB, LANES), F32)]
        + [pltpu.VMEM((FFN_CHUNK + PACK_STATIC * RUN_ALIGN, d), BF16)] * eg)
    return pl.pallas_call(
        functools.partial(_moe_sparse_kernel, alpha, batch, n_sub, cap, eg),
        grid_spec=grid_spec,
        out_shape=jax.ShapeDtypeStruct((batch, total // batch, d), F32),
        compiler_params=pltpu.CompilerParams(
            dimension_semantics=("arbitrary", "arbitrary"), vmem_limit_bytes=VMEM_LIMIT),
        name="moe_sparse",
    )(off_s, pc_s, x1, comb, combt, offr, cntr, offc, cntc, w_gate, w_up, w_down,
      p['ws_gate'].astype(BF16), p['ws_up'].astype(BF16), p['ws_down'].astype(BF16),
      p['ln2_g'], p['ln2_b'])


def _layer_params(l, w):
    g, n = w['a_re'].shape[1:]
    lr, li, bbr, bbi = _ssm_prep(w['a_re'][l], w['a_im'][l], w['log_dt'][l],
                                 w['ssm_b_re'][l], w['ssm_b_im'][l])
    bbd = jnp.concatenate([_block_diag(bbr, SSM_BLOCKS), _block_diag(bbi, SSM_BLOCKS)],
                          axis=-1).astype(BF16)
    c_re = w['ssm_c_re'][l].transpose(0, 2, 1)
    c_im = w['ssm_c_im'][l].transpose(0, 2, 1)
    cbd = jnp.concatenate([_block_diag(c_re, SSM_BLOCKS), _block_diag(-c_im, SSM_BLOCKS)],
                          axis=1).astype(BF16)
    row = lambda v: v.reshape(1, -1)
    return {
        'w_in': w['w_in'][l].astype(BF16), 'b_in': row(w['b_in'][l]),
        'conv_w': w['conv_w'][l], 'w_conv_out': w['w_conv_out'][l].astype(BF16),
        'lam_r': lr.reshape(1, g * n), 'lam_i': li.reshape(1, g * n),
        'bbd': bbd, 'cbd': cbd, 'ssm_d': row(w['ssm_d'][l]),
        'w_glu': w['w_glu'][l].astype(BF16), 'b_glu': row(w['b_glu'][l]),
        'w_ssm_out': w['w_ssm_out'][l].astype(BF16), 'w_o': w['w_o'][l].astype(BF16),
        'ln1_g': row(w['ln1_g'][l]), 'ln1_b': row(w['ln1_b'][l]),
        'w_router_t': w['w_router'][l].T, 'router_bias': w['router_bias'][l].reshape(-1, 1),
        'w_gate': w['w_gate'][l], 'w_up': w['w_up'][l], 'w_down': w['w_down'][l],
        'ws_gate': w['ws_gate'][l], 'ws_up': w['ws_up'][l], 'ws_down': w['ws_down'][l],
        'ln2_g': row(w['ln2_g'][l]), 'ln2_b': row(w['ln2_b'][l]),
    }


def _pick_steps(batch, seq, max_rows):
    steps = max(1, min(seq, max_rows // batch))
    while seq % steps:
        steps -= 1
    return steps


def _pick_tile(total, max_tile):
    tile = min(total, max_tile)
    while total % tile or tile % 16:
        tile -= 16
    return tile


MIXER_ROWS = 512
MOE_TILE = 1024


def _trunk_layer(x, conv_l, re_l, im_l, p, alpha, w_bf16):
    bsz, seq, d = x.shape
    assert bsz % SUBLANES == 0
    steps = _pick_steps(bsz, seq, MIXER_ROWS)
    kw = conv_l.shape[1]
    assert kw == 2
    cprev = conv_l.astype(F32).transpose(1, 0, 2).reshape(kw * bsz, d)
    h0r = re_l.astype(F32).reshape(bsz, -1)
    h0i = im_l.astype(F32).reshape(bsz, -1)
    if ((bsz * steps) % SUB == 0 and (bsz * seq) % (DISPATCH_SUBTILES * SUB) == 0
            and SUB % bsz == 0):
        if w_bf16 is None:
            w_bf16 = tuple(p[k].astype(BF16) for k in ('w_gate', 'w_up', 'w_down'))
        x1, comb, cnew, hr, hi, combt, cnt = _mixer(
            x, cprev, h0r, h0i, p, alpha=alpha, batch=bsz, steps=steps, sub=SUB)
        y = _moe_sparse(x1, comb, combt, cnt, p, w_bf16, alpha=alpha, batch=bsz,
                        n_sub=DISPATCH_SUBTILES)
    else:
        rows = x.transpose(1, 0, 2).reshape(seq * bsz, d)
        x1, comb, cnew, hr, hi = _mixer(rows, cprev, h0r, h0i, p,
                                        alpha=alpha, batch=bsz, steps=steps)
        out, w_bf16 = _moe(x1, comb, p, alpha=alpha, tile=_pick_tile(bsz * seq, MOE_TILE))
        y = out.reshape(seq, bsz, d).transpose(1, 0, 2)
    states = (cnew.reshape(kw, bsz, d).transpose(1, 0, 2),
              hr.reshape(re_l.shape), hi.reshape(im_l.shape))
    return y, states, w_bf16


def kernel(x_prompt, x_sample, state_conv, state_ssm_re, state_ssm_im,
           w_in, b_in, conv_w, w_conv_out, a_re, a_im, log_dt,
           ssm_b_re, ssm_b_im, ssm_c_re, ssm_c_im, ssm_d, w_glu, b_glu, w_ssm_out, w_o,
           ln1_g, ln1_b, w_router, router_bias, w_gate, w_up, w_down,
           ws_gate, ws_up, ws_down, ln2_g, ln2_b):
    w = dict(w_in=w_in, b_in=b_in, conv_w=conv_w, w_conv_out=w_conv_out,
             a_re=a_re, a_im=a_im, log_dt=log_dt,
             ssm_b_re=ssm_b_re, ssm_b_im=ssm_b_im, ssm_c_re=ssm_c_re, ssm_c_im=ssm_c_im,
             ssm_d=ssm_d, w_glu=w_glu, b_glu=b_glu, w_ssm_out=w_ssm_out, w_o=w_o,
             ln1_g=ln1_g, ln1_b=ln1_b, w_router=w_router, router_bias=router_bias,
             w_gate=w_gate, w_up=w_up, w_down=w_down,
             ws_gate=ws_gate, ws_up=ws_up, ws_down=ws_down, ln2_g=ln2_g, ln2_b=ln2_b)
    depth = w_in.shape[0]
    alpha = (2.0 * depth) ** 0.25
    bsz = x_prompt.shape[0]
    zero_conv = jnp.zeros((bsz,) + state_conv.shape[2:], x_prompt.dtype)
    zero_ssm = jnp.zeros((bsz,) + state_ssm_re.shape[2:], F32)
    y_p, y_s = x_prompt, x_sample
    st_p, st_s = [], []
    for l in range(depth):
        p = _layer_params(l, w)
        y_s, st, w_bf16 = _trunk_layer(y_s, state_conv[l], state_ssm_re[l], state_ssm_im[l],
                                       p, alpha, None)
        st_s.append(st)
        y_p, st, _ = _trunk_layer(y_p, zero_conv, zero_ssm, zero_ssm, p, alpha, w_bf16)
        st_p.append(st)
    conv_p, re_p, im_p = (jnp.stack(v) for v in zip(*st_p))
    conv_s, re_s, im_s = (jnp.stack(v) for v in zip(*st_s))
    return (y_p, y_s, conv_p, re_p, im_p, conv_s, re_s, im_s)
```

```python
import functools
import math

import jax
import jax.numpy as jnp
from jax import lax
from jax.experimental import pallas as pl
from jax.experimental.pallas import tpu as pltpu

F32 = jnp.float32
BF16 = jnp.bfloat16

LN_EPS = 1e-5
ROUTED_SCALE = 2.5
N_ROUTE_GROUPS = 8
TOPK_GROUPS = 4
TOP_K = 8

SUBLANES = 8
LANES = 128
SSM_BLOCKS = 4
SCAN_UNROLL = 4
POST_ROWS = 512
DENSE_EXPERTS_PER_STEP = 2
VMEM_LIMIT = 60 * 1024 * 1024


def _const_spec(shape):
    nd = len(shape)
    return pl.BlockSpec(shape, lambda *_: (0,) * nd, pipeline_mode=pl.Buffered(1))


def _ssm_prep_kernel(are_ref, aim_ref, ldt_ref, br_ref, bi_ref,
                     lr_ref, li_ref, bbr_ref, bbi_ref):
    dt = jnp.exp(ldt_ref[...])
    ar = are_ref[...]
    ai = aim_ref[...]
    mag = jnp.exp(ar * dt)
    lr = mag * jnp.cos(ai * dt)
    li = mag * jnp.sin(ai * dt)
    den = ar * ar + ai * ai
    fr = ((lr - 1.0) * ar + li * ai) / den
    fi = (li * ar - (lr - 1.0) * ai) / den
    lr_ref[...] = lr
    li_ref[...] = li
    br = br_ref[...]
    bi = bi_ref[...]
    bbr_ref[...] = fr * br - fi * bi
    bbi_ref[...] = fr * bi + fi * br


def _ssm_prep(a_re, a_im, log_dt, b_re, b_im):
    g, n = a_re.shape
    h = b_re.shape[-1]
    vec = jax.ShapeDtypeStruct((g, 1, n), F32)
    mat = jax.ShapeDtypeStruct((g, h, n), F32)
    return pl.pallas_call(
        _ssm_prep_kernel,
        out_shape=(vec, vec, mat, mat),
        name="ssm_prep",
    )(a_re.reshape(g, 1, n), a_im.reshape(g, 1, n), log_dt.reshape(g, 1, 1),
      b_re.transpose(0, 2, 1), b_im.transpose(0, 2, 1))


def _block_diag(m, nblk):
    g, p, q = m.shape
    gl = g // nblk
    eye = jnp.eye(gl, dtype=m.dtype)
    out = jnp.einsum('kapq,ab->kapbq', m.reshape(nblk, gl, p, q), eye)
    return out.reshape(nblk, gl * p, gl * q)


def _layer_norm(r, g, b):
    mu = jnp.mean(r, axis=-1, keepdims=True)
    d = r - mu
    var = jnp.mean(d * d, axis=-1, keepdims=True)
    return d * lax.rsqrt(var + LN_EPS) * g + b


def _split_bf16(v):
    hi = v.astype(BF16)
    return hi, (v - hi.astype(F32)).astype(BF16)


def _route(scores, biased):
    n_exp, r = scores.shape
    gsz = n_exp // N_ROUTE_GROUPS
    neg = jnp.float32(-jnp.inf)
    rows = []
    for g in range(N_ROUTE_GROUPS):
        v = biased[g * gsz:(g + 1) * gsz, :]
        m1 = jnp.max(v, axis=0, keepdims=True)
        is_max = v == m1
        n_max = jnp.sum(is_max.astype(F32), axis=0, keepdims=True)
        rest = jnp.max(jnp.where(is_max, neg, v), axis=0, keepdims=True)
        rows.append(m1 + jnp.where(n_max >= 2.0, m1, rest))
    gscore = jnp.concatenate(rows, axis=0)
    gidx = lax.broadcasted_iota(jnp.int32, gscore.shape, 0)
    grank = jnp.zeros(gscore.shape, F32)
    for g in range(N_ROUTE_GROUPS):
        sg = gscore[g:g + 1, :]
        beats = (sg > gscore) | ((sg == gscore) & (gidx > g))
        grank = grank + beats.astype(F32)
    gkeep = grank < float(TOPK_GROUPS)
    masked = jnp.concatenate(
        [jnp.where(gkeep[g:g + 1, :], biased[g * gsz:(g + 1) * gsz, :], neg)
         for g in range(N_ROUTE_GROUPS)], axis=0)
    eidx = lax.broadcasted_iota(jnp.int32, masked.shape, 0).astype(F32)
    left = masked
    for _ in range(TOP_K):
        top = jnp.max(left, axis=0, keepdims=True)
        first = jnp.min(jnp.where(left == top, eidx, float(n_exp)), axis=0, keepdims=True)
        left = jnp.where(eidx == first, neg, left)
    w = jnp.where(left != masked, scores, 0.0)
    return w / jnp.sum(w, axis=0, keepdims=True) * ROUTED_SCALE


def _mixer_kernel(alpha, batch, steps, sub,
                  x_ref, cprev_ref, h0r_ref, h0i_ref,
                  win_ref, bin_ref, convw_ref, wco_ref,
                  lamr_ref, lami_ref, bbd_ref, cbd_ref, dskip_ref,
                  wglu_ref, bglu_ref, wso_ref, wo_ref, ln1g_ref, ln1b_ref,
                  wrt_ref, rbias_ref,
                  x1_ref, comb_ref, cnew_ref, hr_ref, hi_ref, *rest):
    if sub:
        combt_ref, cnt_ref, *rest = rest
    ubuf, xk_ref, ys_ref, *rest = rest
    rows = batch * steps
    d = x_ref.shape[-1]
    d_blk = d // SSM_BLOCKS
    n_blk = lamr_ref.shape[1] // SSM_BLOCKS
    i = pl.program_id(0)

    @pl.when(i == 0)
    def _init():
        ubuf[0:2 * batch, :] = cprev_ref[...]
        hr_ref[...] = h0r_ref[...]
        hi_ref[...] = h0i_ref[...]

    if len(x_ref.shape) == 3:
        xs_ref, = rest
        for b in range(batch):
            for c in range(d // LANES):
                xs_ref[c, pl.ds(b, steps, stride=batch), :] = x_ref[b, :, c * LANES:(c + 1) * LANES]
        x = jnp.concatenate([xs_ref[c] for c in range(d // LANES)], axis=1)
    else:
        x = x_ref[...]
    xb = x.astype(BF16)

    def proj(c):
        cols = slice(c * d, (c + 1) * d)
        return (jnp.dot(xb, win_ref[:, cols], preferred_element_type=F32)
                + bin_ref[:, cols])

    u = proj(1) * proj(2)
    ubuf[2 * batch:2 * batch + rows, :] = u
    conv = (convw_ref[0:1, :] * ubuf[0:rows, :]
            + convw_ref[1:2, :] * ubuf[batch:batch + rows, :]
            + convw_ref[2:3, :] * u)
    ya = jnp.dot((proj(0) * conv).astype(BF16), wco_ref[...],
                 preferred_element_type=F32)
    tail = ubuf[rows:rows + 2 * batch, :]
    ubuf[0:2 * batch, :] = tail
    cnew_ref[...] = tail
    ya_rows = slice(2 * batch, 2 * batch + rows)
    ubuf[ya_rows, :] = ya

    us = proj(3)
    half = n_blk // 2
    unroll = SCAN_UNROLL if steps % SCAN_UNROLL == 0 else 1
    for k in range(SSM_BLOCKS):
        usk = us[:, k * d_blk:(k + 1) * d_blk].astype(BF16)
        xk_ref[...] = jnp.dot(usk, bbd_ref[k], preferred_element_type=F32)
        for hf in range(2):
            st = slice(k * n_blk + hf * half, k * n_blk + (hf + 1) * half)
            re = slice(hf * half, (hf + 1) * half)
            im = slice(n_blk + hf * half, n_blk + (hf + 1) * half)
            if steps == 1:
                lr = lamr_ref[:, st]
                li = lami_ref[:, st]
                hr = hr_ref[:, st]
                hi = hi_ref[:, st]
                nhr = lr * hr - li * hi + xk_ref[:, re]
                nhi = lr * hi + li * hr + xk_ref[:, im]
                xk_ref[:, re] = nhr
                xk_ref[:, im] = nhi
                hr_ref[:, st] = nhr
                hi_ref[:, st] = nhi
            else:
                lr = jnp.broadcast_to(lamr_ref[:, st], (SUBLANES, half))
                li = jnp.broadcast_to(lami_ref[:, st], (SUBLANES, half))
                for s in range(batch // SUBLANES):
                    grp = slice(s * SUBLANES, (s + 1) * SUBLANES)

                    def step(tt, carry, s=s, re=re, im=im, lr=lr, li=li):
                        hr, hi = carry
                        for k_un in range(unroll):
                            row = pl.multiple_of((tt * unroll + k_un) * batch + s * SUBLANES,
                                                 SUBLANES)
                            rs = pl.ds(row, SUBLANES)
                            hr, hi = (lr * hr - li * hi + xk_ref[rs, re],
                                      lr * hi + li * hr + xk_ref[rs, im])
                            xk_ref[rs, re] = hr
                            xk_ref[rs, im] = hi
                        return hr, hi

                    hr, hi = lax.fori_loop(0, steps // unroll, step,
                                           (hr_ref[grp, st], hi_ref[grp, st]))
                    hr_ref[grp, st] = hr
                    hi_ref[grp, st] = hi
        ys_ref[:, k * d_blk:(k + 1) * d_blk] = jnp.dot(
            xk_ref[...].astype(BF16), cbd_ref[k], preferred_element_type=F32)
    ys_ref[...] = ys_ref[...] + dskip_ref[...] * us

    chunk = min(rows, POST_ROWS)
    for q in range(rows // chunk):
        rq = slice(q * chunk, (q + 1) * chunk)
        if len(x_ref.shape) == 3:
            xq = jnp.concatenate([xs_ref[c, rq, :] for c in range(d // LANES)], axis=1)
        else:
            xq = x_ref[rq, :]
        xbq = xq.astype(BF16)

        def projq(c, xbq=xbq):
            cols = slice(c * d, (c + 1) * d)
            return (jnp.dot(xbq, win_ref[:, cols], preferred_element_type=F32)
                    + bin_ref[:, cols])

        z = jax.nn.gelu(ys_ref[rq, :])
        gate = (jnp.dot(z.astype(BF16), wglu_ref[...], preferred_element_type=F32)
                + bglu_ref[...])
        glu = z * jax.nn.sigmoid(gate)
        yb = jnp.dot(glu.astype(BF16), wso_ref[...], preferred_element_type=F32)

        m = (jax.nn.sigmoid(projq(4)) * ubuf[2 * batch + q * chunk:2 * batch + (q + 1) * chunk, :]
             + jax.nn.sigmoid(projq(5)) * yb)
        o = jnp.dot(m.astype(BF16), wo_ref[...], preferred_element_type=F32)
        x1 = _layer_norm(alpha * xq + o, ln1g_ref[...], ln1b_ref[...])
        x1_ref[rq, :] = x1

        nt = (((1,), (1,)), ((), ()))
        w_hi, w_lo = _split_bf16(wrt_ref[...])
        x_hi, x_lo = _split_bf16(x1)
        logits = (lax.dot_general(w_hi, x_hi, nt, preferred_element_type=F32)
                  + lax.dot_general(w_hi, x_lo, nt, preferred_element_type=F32)
                  + lax.dot_general(w_lo, x_hi, nt, preferred_element_type=F32))
        scores = jax.nn.sigmoid(logits)
        comb = _route(scores, scores + rbias_ref[...])
        comb_ref[rq, :] = comb.T
        if sub:
            combt_ref[:, rq] = comb
    if sub:
        ones = jnp.ones((SUBLANES, sub), BF16)
        for s in range(rows // sub):
            sel = jnp.where(combt_ref[:, s * sub:(s + 1) * sub] != 0.0, 1.0, 0.0).astype(BF16)
            cnt_ref[s] = lax.dot_general(ones, sel, (((1,), (1,)), ((), ())),
                                         preferred_element_type=F32)


def _mixer(x, cprev, h0r, h0i, p, *, alpha, batch, steps, sub=0):
    d = x.shape[-1]
    total = x.size // d
    rows = batch * steps
    assert not sub or rows % sub == 0
    n_state = h0r.shape[1]
    n_exp = p['w_router_t'].shape[0]
    grid = (total // rows,)
    if x.ndim == 3:
        x_spec = pl.BlockSpec((batch, steps, d), lambda i: (0, i, 0))
    else:
        x_spec = pl.BlockSpec((rows, d), lambda i: (i, 0))
    consts = [p['w_in'], p['b_in'], p['conv_w'], p['w_conv_out'],
              p['lam_r'], p['lam_i'], p['bbd'], p['cbd'], p['ssm_d'],
              p['w_glu'], p['b_glu'], p['w_ssm_out'], p['w_o'], p['ln1_g'], p['ln1_b'],
              p['w_router_t'], p['router_bias']]
    in_specs = ([x_spec,
                 _const_spec(cprev.shape), _const_spec(h0r.shape), _const_spec(h0i.shape)]
                + [_const_spec(c.shape) for c in consts])
    out_shape = (jax.ShapeDtypeStruct((total, d), F32),
                 jax.ShapeDtypeStruct((total, n_exp), F32),
                 jax.ShapeDtypeStruct((2 * batch, d), F32),
                 jax.ShapeDtypeStruct((batch, n_state), F32),
                 jax.ShapeDtypeStruct((batch, n_state), F32))
    out_specs = (pl.BlockSpec((rows, d), lambda i: (i, 0)),
                 pl.BlockSpec((rows, n_exp), lambda i: (i, 0)),
                 pl.BlockSpec((2 * batch, d), lambda i: (0, 0)),
                 pl.BlockSpec((batch, n_state), lambda i: (0, 0)),
                 pl.BlockSpec((batch, n_state), lambda i: (0, 0)))
    if sub:
        out_shape += (jax.ShapeDtypeStruct((n_exp, total), F32),
                      jax.ShapeDtypeStruct((total // sub, SUBLANES, n_exp), F32))
        out_specs += (pl.BlockSpec((n_exp, rows), lambda i: (0, i)),
                      pl.BlockSpec((rows // sub, SUBLANES, n_exp), lambda i: (i, 0, 0)))
    scratch = [pltpu.VMEM((rows + 2 * batch, d), F32),
               pltpu.VMEM((rows, 2 * n_state // SSM_BLOCKS), F32),
               pltpu.VMEM((rows, d), F32)]
    if x.ndim == 3:
        scratch.append(pltpu.VMEM((d // LANES, rows, LANES), F32))
    return pl.pallas_call(
        functools.partial(_mixer_kernel, alpha, batch, steps, sub),
        grid=grid, in_specs=in_specs, out_specs=out_specs, out_shape=out_shape,
        scratch_shapes=scratch,
        compiler_params=pltpu.CompilerParams(
            dimension_semantics=("arbitrary",), vmem_limit_bytes=VMEM_LIMIT),
        name="mixer",
    )(x, cprev, h0r, h0i, *consts)


def _swiglu(xb, wg, wu):
    g = jnp.dot(xb, wg.astype(BF16), preferred_element_type=F32)
    u = jnp.dot(xb, wu.astype(BF16), preferred_element_type=F32)
    return jax.nn.silu(g) * u


def _moe_kernel(alpha, emit_bf16, x1_ref, comb_ref, wg_ref, wu_ref, wd_ref,
                wsg_ref, wsu_ref, wsd_ref, g_ref, b_ref, out_ref, *rest):
    if emit_bf16:
        wgb_ref, wub_ref, wdb_ref, xb_ref, acc_ref = rest
    else:
        xb_ref, acc_ref = rest
    e = pl.program_id(1)

    @pl.when(e == 0)
    def _shared():
        xb = x1_ref[...].astype(BF16)
        xb_ref[...] = xb
        hs = _swiglu(xb, wsg_ref[...], wsu_ref[...])
        acc_ref[...] = jnp.dot(hs.astype(BF16), wsd_ref[...].astype(BF16),
                               preferred_element_type=F32)

    comb = comb_ref[...]
    lane = lax.broadcasted_iota(jnp.int32, comb.shape, 1)
    xb = xb_ref[...]
    acc = acc_ref[...]
    for k in range(wg_ref.shape[0]):
        wg = wg_ref[k].astype(BF16)
        wu = wu_ref[k].astype(BF16)
        wd = wd_ref[k].astype(BF16)
        if emit_bf16:
            wgb_ref[k] = wg
            wub_ref[k] = wu
            wdb_ref[k] = wd
        c = jnp.sum(jnp.where(lane == e * wg_ref.shape[0] + k, comb, 0.0), axis=1, keepdims=True)
        h = _swiglu(xb, wg, wu) * c
        acc = acc + jnp.dot(h.astype(BF16), wd, preferred_element_type=F32)
    acc_ref[...] = acc

    @pl.when(e == pl.num_programs(1) - 1)
    def _finish():
        out_ref[...] = _layer_norm(alpha * x1_ref[...] + acc_ref[...],
                                   g_ref[...], b_ref[...])


def _moe(x1, comb, p, *, alpha, tile):
    total, d = x1.shape
    n_exp, _, f = p['w_gate'].shape
    fs = p['ws_gate'].shape[1]
    eg = DENSE_EXPERTS_PER_STEP
    grid = (total // tile, n_exp // eg)
    emit_bf16 = grid[0] == 1
    up_spec = pl.BlockSpec((eg, d, f), lambda i, e: (e, 0, 0))
    down_spec = pl.BlockSpec((eg, f, d), lambda i, e: (e, 0, 0))
    in_specs = [pl.BlockSpec((tile, d), lambda i, e: (i, 0)),
                pl.BlockSpec((tile, n_exp), lambda i, e: (i, 0)),
                up_spec, up_spec, down_spec,
                _const_spec((d, fs)), _const_spec((d, fs)), _const_spec((fs, d)),
                _const_spec((1, d)), _const_spec((1, d))]
    out_specs = [pl.BlockSpec((tile, d), lambda i, e: (i, 0))]
    out_shape = [jax.ShapeDtypeStruct((total, d), F32)]
    if emit_bf16:
        out_specs += [up_spec, up_spec, down_spec]
        out_shape += [jax.ShapeDtypeStruct(p[k].shape, BF16)
                      for k in ('w_gate', 'w_up', 'w_down')]
    res = pl.pallas_call(
        functools.partial(_moe_kernel, alpha, emit_bf16),
        grid=grid, in_specs=in_specs, out_specs=out_specs, out_shape=out_shape,
        scratch_shapes=[pltpu.VMEM((tile, d), BF16), pltpu.VMEM((tile, d), F32)],
        compiler_params=pltpu.CompilerParams(
            dimension_semantics=("arbitrary", "arbitrary"), vmem_limit_bytes=VMEM_LIMIT),
        name="moe",
    )(x1, comb, p['w_gate'], p['w_up'], p['w_down'],
      p['ws_gate'], p['ws_up'], p['ws_down'], p['ln2_g'], p['ln2_b'])
    if emit_bf16:
        return res[0], tuple(res[1:])
    return res[0], tuple(p[k].astype(BF16) for k in ('w_gate', 'w_up', 'w_down'))


SUB = 256
RUN_ALIGN = 16
FFN_CHUNK = 192
SEL_CHUNK = 512
PACK_STATIC = 4
PACK_FULL = 2
DISPATCH_SUBTILES = 4
EXPERTS_PER_STEP = 4


def _expert_onehot(r, off, cnt):
    hit = (r >= off) & (r < off + cnt)
    return hit, jnp.where(hit, off + 1.0, 0.0)


def _moe_sparse_kernel(alpha, batch, n_sub, cap, eg,
                       off_s, pc_s,
                       x1_ref, comb_ref, combt_ref, offr_ref, cntr_ref, offc_ref, cntc_ref,
                       wg_ref, wu_ref, wd_ref, wsg_ref, wsu_ref, wsd_ref, g_ref, b_ref,
                       out_ref, gbuf, xe, zbuf, *xq):
    i = pl.program_id(0)
    g = pl.program_id(1)
    n_exp = comb_ref.shape[1]
    tri_r = lax.broadcasted_iota(jnp.int32, (SUB, SUB), 0)
    tri_c = lax.broadcasted_iota(jnp.int32, (SUB, SUB), 1)

    def run(j, e):
        idx = (i * n_sub + j) * n_exp + e
        return off_s[idx], pc_s[idx]

    def used_rows(j):
        o, p = run(j, n_exp - 1)
        return o + p

    def rows_at(start, chunks):
        return pl.ds(pl.multiple_of(start, RUN_ALIGN), chunks * RUN_ALIGN)

    rows16 = functools.partial(rows_at, chunks=1)

    @pl.when((i == 0) & (g == 0))
    def _zero():
        xe[...] = jnp.zeros(xe.shape, xe.dtype)
        for buf in xq:
            buf[...] = jnp.zeros(buf.shape, buf.dtype)

    @pl.when(g == 0)
    def _dispatch():
        before = jnp.where(tri_r < tri_c, 1.0, 0.0).astype(BF16)
        for j in range(n_sub):
            tok = slice(j * SUB, (j + 1) * SUB)
            xj = x1_ref[tok, :].astype(BF16)
            sel = combt_ref[:, tok] != 0.0
            pos = jnp.dot(jnp.where(sel, 1.0, 0.0).astype(BF16), before,
                          preferred_element_type=F32)
            posm = jnp.where(sel, pos, -1.0).astype(BF16)
            off = offr_ref[j]
            cnt = cntr_ref[j]
            used = used_rows(j)

            def sort_rows(rc, j=j, xj=xj, posm=posm, off=off, cnt=cnt):
                r = (lax.broadcasted_iota(jnp.int32, (SEL_CHUNK, 1), 0)
                     + rc * SEL_CHUNK).astype(F32)
                hit, start1 = _expert_onehot(r, off, cnt)
                s = jnp.sum(start1, axis=1, keepdims=True)
                q = jnp.where(s > 0.0, r - (s - 1.0), -2.0)
                rank = jnp.dot(jnp.where(hit, 1.0, 0.0).astype(BF16), posm,
                               preferred_element_type=F32)
                pick = jnp.where(rank == q, 1.0, 0.0).astype(BF16)
                gbuf[j, rc * SEL_CHUNK:(rc + 1) * SEL_CHUNK, :] = jnp.dot(
                    pick, xj, preferred_element_type=F32).astype(BF16)

            last = cap // SEL_CHUNK - 1
            for rc in range(last):
                sort_rows(rc)
            pl.when(last * SEL_CHUNK < used)(functools.partial(sort_rows, last))

            @pl.when(last * SEL_CHUNK >= used)
            def _blank(j=j):
                gbuf[j, last * SEL_CHUNK:, :] = jnp.zeros((SEL_CHUNK, gbuf.shape[2]), BF16)

    def ffn_rows(rs, ee):
        xc = xe[rs, :]
        h = (jax.nn.silu(jnp.dot(xc, wg_ref[ee], preferred_element_type=F32))
             * jnp.dot(xc, wu_ref[ee], preferred_element_type=F32))
        xe[rs, :] = jnp.dot(h.astype(BF16), wd_ref[ee], preferred_element_type=F32).astype(BF16)

    def pack(e, buf, static):
        n = jnp.int32(0)
        for j in range(n_sub):
            o, p = run(j, e)
            if static:
                buf[rows_at(n, PACK_STATIC), :] = gbuf[j, rows_at(o, PACK_STATIC), :]
            else:
                def chunk(c, carry, j=j, o=o, n=n):
                    buf[rows16(n + c * RUN_ALIGN), :] = gbuf[j, rows16(o + c * RUN_ALIGN), :]
                    return carry

                lax.fori_loop(0, lax.div(p, RUN_ALIGN), chunk, 0)
            n = n + p
        return n

    def unpack(e, buf, static, keep_next=True):
        n = jnp.int32(0)
        for j in range(n_sub):
            o, p = run(j, e)
            if static and not keep_next:
                gbuf[j, rows_at(o, PACK_STATIC), :] = buf[rows_at(n, PACK_STATIC), :]
            elif static:
                full = PACK_FULL * RUN_ALIGN
                gbuf[j, rows_at(o, PACK_FULL), :] = buf[rows_at(n, PACK_FULL), :]
                rest = PACK_STATIC - PACK_FULL
                dst = rows_at(o + full, rest)
                row = lax.broadcasted_iota(jnp.int32, (rest * RUN_ALIGN, gbuf.shape[2]), 0)
                gbuf[j, dst, :] = jnp.where(row < p - full, buf[rows_at(n + full, rest), :],
                                            gbuf[j, dst, :])
            else:
                def chunk(c, carry, j=j, o=o, n=n):
                    gbuf[j, rows16(o + c * RUN_ALIGN), :] = buf[rows16(n + c * RUN_ALIGN), :]
                    return carry

                lax.fori_loop(0, lax.div(p, RUN_ALIGN), chunk, 0)
            n = n + p

    usual = jnp.bool_(True)
    for ee in range(eg):
        n = jnp.int32(0)
        for j in range(n_sub):
            p = run(j, g * eg + ee)[1]
            usual = usual & (p >= PACK_FULL * RUN_ALIGN) & (p <= PACK_STATIC * RUN_ALIGN)
            n = n + p
        usual = usual & (n <= FFN_CHUNK)

    @pl.when(usual)
    def _together():
        for ee in range(eg):
            pack(g * eg + ee, xq[ee], True)
        xcs = [xq[ee][0:FFN_CHUNK, :] for ee in range(eg)]
        hs = [(jax.nn.silu(jnp.dot(xc, wg_ref[ee], preferred_element_type=F32))
               * jnp.dot(xc, wu_ref[ee], preferred_element_type=F32)).astype(BF16)
              for ee, xc in enumerate(xcs)]
        outs = [jnp.dot(h, wd_ref[ee], preferred_element_type=F32).astype(BF16)
                for ee, h in enumerate(hs)]
        for ee, o in enumerate(outs):
            xq[ee][0:FFN_CHUNK, :] = o
        for ee in range(eg):
            unpack(g * eg + ee, xq[ee], True, keep_next=ee == eg - 1)

    @pl.when(jnp.logical_not(usual))
    def _one_by_one():
        for ee in range(eg):
            e = g * eg + ee
            n = pack(e, xe, False)

            def ffn(k, carry, ee=ee):
                ffn_rows(pl.ds(pl.multiple_of(k * FFN_CHUNK, RUN_ALIGN), FFN_CHUNK), ee)
                return carry

            lax.fori_loop(0, lax.div(n + (FFN_CHUNK - 1), FFN_CHUNK), ffn, 0)
            unpack(e, xe, False)

    @pl.when(g == pl.num_programs(1) - 1)
    def _combine():
        earlier = jnp.where(tri_c < tri_r, 1.0, 0.0).astype(BF16)
        for j in range(n_sub):
            tok = slice(j * SUB, (j + 1) * SUB)
            x = x1_ref[tok, :]
            xb = x.astype(BF16)
            hs = (jax.nn.silu(jnp.dot(xb, wsg_ref[...], preferred_element_type=F32))
                  * jnp.dot(xb, wsu_ref[...], preferred_element_type=F32))
            y = jnp.dot(hs.astype(BF16), wsd_ref[...], preferred_element_type=F32)
            comb = comb_ref[tok, :]
            sel = comb != 0.0
            pos = jnp.dot(earlier, jnp.where(sel, 1.0, 0.0).astype(BF16),
                          preferred_element_type=F32)
            posm = jnp.where(sel, pos, -1.0).astype(BF16)
            c_parts = jnp.concatenate(_split_bf16(comb), axis=1)
            off = offc_ref[j]
            cnt = cntc_ref[j]
            used = used_rows(j)

            def gather_rows(rc, j=j, posm=posm, c_parts=c_parts, off=off, cnt=cnt):
                r = (lax.broadcasted_iota(jnp.int32, (1, SEL_CHUNK), 1)
                     + rc * SEL_CHUNK).astype(F32)
                hit, start1 = _expert_onehot(r, off, cnt)
                s = jnp.sum(start1, axis=0, keepdims=True)
                q = jnp.where(s > 0.0, r - (s - 1.0), -2.0)
                hb = jnp.where(hit, 1.0, 0.0).astype(BF16)
                pick = jnp.dot(posm, hb, preferred_element_type=F32) == q
                weight = jnp.dot(c_parts, jnp.concatenate([hb, hb], axis=0),
                                 preferred_element_type=F32)
                w_hi, w_lo = _split_bf16(jnp.where(pick, weight, 0.0))
                rows = gbuf[j, rc * SEL_CHUNK:(rc + 1) * SEL_CHUNK, :]
                return (jnp.dot(w_hi, rows, preferred_element_type=F32)
                        + jnp.dot(w_lo, rows, preferred_element_type=F32))

            last = cap // SEL_CHUNK - 1
            for rc in range(last):
                y = y + gather_rows(rc)
            n_lt = x.shape[1] // LANES
            lane_tiles = lambda v: [v[:, c * LANES:(c + 1) * LANES] for c in range(n_lt)]

            def put(v):
                for c, t in enumerate(lane_tiles(v)):
                    zbuf[c] = t

            get = lambda: jnp.concatenate([zbuf[c] for c in range(n_lt)], axis=1)
            put(alpha * x + y)

            @pl.when(last * SEL_CHUNK < used)
            def _tail(gather_rows=gather_rows, put=put, get=get):
                put(get() + gather_rows(last))

            put(_layer_norm(get(), g_ref[...], b_ref[...]))
            t_sub = SUB // batch
            for b in range(batch):
                for c in range(n_lt):
                    out_ref[b, j * t_sub:(j + 1) * t_sub, c * LANES:(c + 1) * LANES] = (
                        zbuf[c, pl.ds(b, t_sub, stride=batch), :])


def _moe_sparse(x1, comb, combt, cnt, p, w_bf16, *, alpha, batch, n_sub, eg=EXPERTS_PER_STEP):
    total, d = x1.shape
    w_gate, w_up, w_down = w_bf16
    n_exp, _, f = w_gate.shape
    fs = p['ws_gate'].shape[1]
    tile = n_sub * SUB
    n_tiles = total // tile
    assert SUB % batch == 0
    cap = -(-(SUB * TOP_K + n_exp * (RUN_ALIGN - 1)) // SEL_CHUNK) * SEL_CHUNK
    xe_rows = -(-(tile + n_sub * (RUN_ALIGN - 1)) // FFN_CHUNK) * FFN_CHUNK

    cnt = cnt[:, 0, :]
    pc = jnp.ceil(cnt / RUN_ALIGN) * RUN_ALIGN
    off = jnp.cumsum(pc, axis=1) - pc
    off_s = off.astype(jnp.int32).reshape(-1)
    pc_s = pc.astype(jnp.int32).reshape(-1)
    offr, cntr = off[:, None, :], cnt[:, None, :]
    offc, cntc = off[:, :, None], cnt[:, :, None]

    row_spec = pl.BlockSpec((n_sub, 1, n_exp), lambda i, g, *_: (i, 0, 0))
    col_spec = pl.BlockSpec((n_sub, n_exp, 1), lambda i, g, *_: (i, 0, 0))
    const = lambda shape: pl.BlockSpec(shape, lambda i, g, *_: (0,) * len(shape),
                                       pipeline_mode=pl.Buffered(1))
    grid_spec = pltpu.PrefetchScalarGridSpec(
        num_scalar_prefetch=2,
        grid=(n_tiles, n_exp // eg),
        in_specs=[pl.BlockSpec((tile, d), lambda i, g, *_: (i, 0), pipeline_mode=pl.Buffered(1)),
                  pl.BlockSpec((tile, n_exp), lambda i, g, *_: (i, 0)),
                  pl.BlockSpec((n_exp, tile), lambda i, g, *_: (0, i)),
                  row_spec, row_spec, col_spec, col_spec,
                  pl.BlockSpec((eg, d, f), lambda i, g, *_: (g, 0, 0)),
                  pl.BlockSpec((eg, d, f), lambda i, g, *_: (g, 0, 0)),
                  pl.BlockSpec((eg, f, d), lambda i, g, *_: (g, 0, 0)),
                  const((d, fs)), const((d, fs)), const((fs, d)),
                  const((1, d)), const((1, d))],
        out_specs=pl.BlockSpec((batch, tile // batch, d), lambda i, g, *_: (0, i, 0),
                               pipeline_mode=pl.Buffered(1)),
        scratch_shapes=[pltpu.VMEM((n_sub, cap, d), BF16), pltpu.VMEM((xe_rows, d), BF16),
                        pltpu.VMEM((d // LANES, SUB, LANES), F32)]
        + [pltpu.VMEM((FFN_CHUNK + PACK_STATIC * RUN_ALIGN, d), BF16)] * eg)
    return pl.pallas_call(
        functools.partial(_moe_sparse_kernel, alpha, batch, n_sub, cap, eg),
        grid_spec=grid_spec,
        out_shape=jax.ShapeDtypeStruct((batch, total // batch, d), F32),
        compiler_params=pltpu.CompilerParams(
            dimension_semantics=("arbitrary", "arbitrary"), vmem_limit_bytes=VMEM_LIMIT),
        name="moe_sparse",
    )(off_s, pc_s, x1, comb, combt, offr, cntr, offc, cntc, w_gate, w_up, w_down,
      p['ws_gate'].astype(BF16), p['ws_up'].astype(BF16), p['ws_down'].astype(BF16),
      p['ln2_g'], p['ln2_b'])


def _layer_params(l, w):
    g, n = w['a_re'].shape[1:]
    lr, li, bbr, bbi = _ssm_prep(w['a_re'][l], w['a_im'][l], w['log_dt'][l],
                                 w['ssm_b_re'][l], w['ssm_b_im'][l])
    bbd = jnp.concatenate([_block_diag(bbr, SSM_BLOCKS), _block_diag(bbi, SSM_BLOCKS)],
                          axis=-1).astype(BF16)
    c_re = w['ssm_c_re'][l].transpose(0, 2, 1)
    c_im = w['ssm_c_im'][l].transpose(0, 2, 1)
    cbd = jnp.concatenate([_block_diag(c_re, SSM_BLOCKS), _block_diag(-c_im, SSM_BLOCKS)],
                          axis=1).astype(BF16)
    row = lambda v: v.reshape(1, -1)
    return {
        'w_in': w['w_in'][l].astype(BF16), 'b_in': row(w['b_in'][l]),
        'conv_w': w['conv_w'][l], 'w_conv_out': w['w_conv_out'][l].astype(BF16),
        'lam_r': lr.reshape(1, g * n), 'lam_i': li.reshape(1, g * n),
        'bbd': bbd, 'cbd': cbd, 'ssm_d': row(w['ssm_d'][l]),
        'w_glu': w['w_glu'][l].astype(BF16), 'b_glu': row(w['b_glu'][l]),
        'w_ssm_out': w['w_ssm_out'][l].astype(BF16), 'w_o': w['w_o'][l].astype(BF16),
        'ln1_g': row(w['ln1_g'][l]), 'ln1_b': row(w['ln1_b'][l]),
        'w_router_t': w['w_router'][l].T, 'router_bias': w['router_bias'][l].reshape(-1, 1),
        'w_gate': w['w_gate'][l], 'w_up': w['w_up'][l], 'w_down': w['w_down'][l],
        'ws_gate': w['ws_gate'][l], 'ws_up': w['ws_up'][l], 'ws_down': w['ws_down'][l],
        'ln2_g': row(w['ln2_g'][l]), 'ln2_b': row(w['ln2_b'][l]),
    }


def _pick_steps(batch, seq, max_rows):
    steps = max(1, min(seq, max_rows // batch))
    while seq % steps:
        steps -= 1
    return steps


def _pick_tile(total, max_tile):
    tile = min(total, max_tile)
    while total % tile or tile % 16:
        tile -= 16
    return tile


MIXER_ROWS = 512
MOE_TILE = 1024


def _trunk_layer(x, conv_l, re_l, im_l, p, alpha, w_bf16):
    bsz, seq, d = x.shape
    assert bsz % SUBLANES == 0
    steps = _pick_steps(bsz, seq, MIXER_ROWS)
    kw = conv_l.shape[1]
    assert kw == 2
    cprev = conv_l.astype(F32).transpose(1, 0, 2).reshape(kw * bsz, d)
    h0r = re_l.astype(F32).reshape(bsz, -1)
    h0i = im_l.astype(F32).reshape(bsz, -1)
    if ((bsz * steps) % SUB == 0 and (bsz * seq) % (DISPATCH_SUBTILES * SUB) == 0
            and SUB % bsz == 0):
        if w_bf16 is None:
            w_bf16 = tuple(p[k].astype(BF16) for k in ('w_gate', 'w_up', 'w_down'))
        x1, comb, cnew, hr, hi, combt, cnt = _mixer(
            x, cprev, h0r, h0i, p, alpha=alpha, batch=bsz, steps=steps, sub=SUB)
        y = _moe_sparse(x1, comb, combt, cnt, p, w_bf16, alpha=alpha, batch=bsz,
                        n_sub=DISPATCH_SUBTILES)
    else:
        rows = x.transpose(1, 0, 2).reshape(seq * bsz, d)
        x1, comb, cnew, hr, hi = _mixer(rows, cprev, h0r, h0i, p,
                                        alpha=alpha, batch=bsz, steps=steps)
        out, w_bf16 = _moe(x1, comb, p, alpha=alpha, tile=_pick_tile(bsz * seq, MOE_TILE))
        y = out.reshape(seq, bsz, d).transpose(1, 0, 2)
    states = (cnew.reshape(kw, bsz, d).transpose(1, 0, 2),
              hr.reshape(re_l.shape), hi.reshape(im_l.shape))
    return y, states, w_bf16


def kernel(x_prompt, x_sample, state_conv, state_ssm_re, state_ssm_im,
           w_in, b_in, conv_w, w_conv_out, a_re, a_im, log_dt,
           ssm_b_re, ssm_b_im, ssm_c_re, ssm_c_im, ssm_d, w_glu, b_glu, w_ssm_out, w_o,
           ln1_g, ln1_b, w_router, router_bias, w_gate, w_up, w_down,
           ws_gate, ws_up, ws_down, ln2_g, ln2_b):
    w = dict(w_in=w_in, b_in=b_in, conv_w=conv_w, w_conv_out=w_conv_out,
             a_re=a_re, a_im=a_im, log_dt=log_dt,
             ssm_b_re=ssm_b_re, ssm_b_im=ssm_b_im, ssm_c_re=ssm_c_re, ssm_c_im=ssm_c_im,
             ssm_d=ssm_d, w_glu=w_glu, b_glu=b_glu, w_ssm_out=w_ssm_out, w_o=w_o,
             ln1_g=ln1_g, ln1_b=ln1_b, w_router=w_router, router_bias=router_bias,
             w_gate=w_gate, w_up=w_up, w_down=w_down,
             ws_gate=ws_gate, ws_up=ws_up, ws_down=ws_down, ln2_g=ln2_g, ln2_b=ln2_b)
    depth = w_in.shape[0]
    alpha = (2.0 * depth) ** 0.25
    bsz = x_prompt.shape[0]
    zero_conv = jnp.zeros((bsz,) + state_conv.shape[2:], x_prompt.dtype)
    zero_ssm = jnp.zeros((bsz,) + state_ssm_re.shape[2:], F32)
    y_p, y_s = x_prompt, x_sample
    st_p, st_s = [], []
    for l in range(depth):
        p = _layer_params(l, w)
        y_s, st, w_bf16 = _trunk_layer(y_s, state_conv[l], state_ssm_re[l], state_ssm_im[l],
                                       p, alpha, None)
        st_s.append(st)
        y_p, st, _ = _trunk_layer(y_p, zero_conv, zero_ssm, zero_ssm, p, alpha, w_bf16)
        st_p.append(st)
    conv_p, re_p, im_p = (jnp.stack(v) for v in zip(*st_p))
    conv_s, re_s, im_s = (jnp.stack(v) for v in zip(*st_s))
    return (y_p, y_s, conv_p, re_p, im_p, conv_s, re_s, im_s)
```

```python
import functools
import math

import jax
import jax.numpy as jnp
from jax import lax
from jax.experimental import pallas as pl
from jax.experimental.pallas import tpu as pltpu

F32 = jnp.float32
BF16 = jnp.bfloat16

LN_EPS = 1e-5
ROUTED_SCALE = 2.5
N_ROUTE_GROUPS = 8
TOPK_GROUPS = 4
TOP_K = 8

SUBLANES = 8
LANES = 128
SSM_BLOCKS = 4
SCAN_UNROLL = 4
POST_ROWS = 512
DENSE_EXPERTS_PER_STEP = 2
VMEM_LIMIT = 60 * 1024 * 1024


def _const_spec(shape):
    nd = len(shape)
    return pl.BlockSpec(shape, lambda *_: (0,) * nd, pipeline_mode=pl.Buffered(1))


def _ssm_prep_kernel(are_ref, aim_ref, ldt_ref, br_ref, bi_ref,
                     lr_ref, li_ref, bbr_ref, bbi_ref):
    dt = jnp.exp(ldt_ref[...])
    ar = are_ref[...]
    ai = aim_ref[...]
    mag = jnp.exp(ar * dt)
    lr = mag * jnp.cos(ai * dt)
    li = mag * jnp.sin(ai * dt)
    den = ar * ar + ai * ai
    fr = ((lr - 1.0) * ar + li * ai) / den
    fi = (li * ar - (lr - 1.0) * ai) / den
    lr_ref[...] = lr
    li_ref[...] = li
    br = br_ref[...]
    bi = bi_ref[...]
    bbr_ref[...] = fr * br - fi * bi
    bbi_ref[...] = fr * bi + fi * br


def _ssm_prep(a_re, a_im, log_dt, b_re, b_im):
    g, n = a_re.shape
    h = b_re.shape[-1]
    vec = jax.ShapeDtypeStruct((g, 1, n), F32)
    mat = jax.ShapeDtypeStruct((g, h, n), F32)
    return pl.pallas_call(
        _ssm_prep_kernel,
        out_shape=(vec, vec, mat, mat),
        name="ssm_prep",
    )(a_re.reshape(g, 1, n), a_im.reshape(g, 1, n), log_dt.reshape(g, 1, 1),
      b_re.transpose(0, 2, 1), b_im.transpose(0, 2, 1))


def _block_diag(m, nblk):
    g, p, q = m.shape
    gl = g // nblk
    eye = jnp.eye(gl, dtype=m.dtype)
    out = jnp.einsum('kapq,ab->kapbq', m.reshape(nblk, gl, p, q), eye)
    return out.reshape(nblk, gl * p, gl * q)


def _layer_norm(r, g, b):
    mu = jnp.mean(r, axis=-1, keepdims=True)
    d = r - mu
    var = jnp.mean(d * d, axis=-1, keepdims=True)
    return d * lax.rsqrt(var + LN_EPS) * g + b


def _split_bf16(v):
    hi = v.astype(BF16)
    return hi, (v - hi.astype(F32)).astype(BF16)


def _route(scores, biased):
    n_exp, r = scores.shape
    gsz = n_exp // N_ROUTE_GROUPS
    neg = jnp.float32(-jnp.inf)
    rows = []
    for g in range(N_ROUTE_GROUPS):
        v = biased[g * gsz:(g + 1) * gsz, :]
        m1 = jnp.max(v, axis=0, keepdims=True)
        is_max = v == m1
        n_max = jnp.sum(is_max.astype(F32), axis=0, keepdims=True)
        rest = jnp.max(jnp.where(is_max, neg, v), axis=0, keepdims=True)
        rows.append(m1 + jnp.where(n_max >= 2.0, m1, rest))
    gscore = jnp.concatenate(rows, axis=0)
    gidx = lax.broadcasted_iota(jnp.int32, gscore.shape, 0)
    grank = jnp.zeros(gscore.shape, F32)
    for g in range(N_ROUTE_GROUPS):
        sg = gscore[g:g + 1, :]
        beats = (sg > gscore) | ((sg == gscore) & (gidx > g))
        grank = grank + beats.astype(F32)
    gkeep = grank < float(TOPK_GROUPS)
    masked = jnp.concatenate(
        [jnp.where(gkeep[g:g + 1, :], biased[g * gsz:(g + 1) * gsz, :], neg)
         for g in range(N_ROUTE_GROUPS)], axis=0)
    eidx = lax.broadcasted_iota(jnp.int32, masked.shape, 0).astype(F32)
    left = masked
    for _ in range(TOP_K):
        top = jnp.max(left, axis=0, keepdims=True)
        first = jnp.min(jnp.where(left == top, eidx, float(n_exp)), axis=0, keepdims=True)
        left = jnp.where(eidx == first, neg, left)
    w = jnp.where(left != masked, scores, 0.0)
    return w / jnp.sum(w, axis=0, keepdims=True) * ROUTED_SCALE


def _mixer_kernel(alpha, batch, steps, sub,
                  x_ref, cprev_ref, h0r_ref, h0i_ref,
                  win_ref, bin_ref, convw_ref, wco_ref,
                  lamr_ref, lami_ref, bbd_ref, cbd_ref, dskip_ref,
                  wglu_ref, bglu_ref, wso_ref, wo_ref, ln1g_ref, ln1b_ref,
                  wrt_ref, rbias_ref,
                  x1_ref, comb_ref, cnew_ref, hr_ref, hi_ref, *rest):
    if sub:
        combt_ref, cnt_ref, *rest = rest
    ubuf, xk_ref, ys_ref, *rest = rest
    rows = batch * steps
    d = x_ref.shape[-1]
    d_blk = d // SSM_BLOCKS
    n_blk = lamr_ref.shape[1] // SSM_BLOCKS
    i = pl.program_id(0)

    @pl.when(i == 0)
    def _init():
        ubuf[0:2 * batch, :] = cprev_ref[...]
        hr_ref[...] = h0r_ref[...]
        hi_ref[...] = h0i_ref[...]

    if len(x_ref.shape) == 3:
        xs_ref, = rest
        for b in range(batch):
            for c in range(d // LANES):
                xs_ref[c, pl.ds(b, steps, stride=batch), :] = x_ref[b, :, c * LANES:(c + 1) * LANES]
        x = jnp.concatenate([xs_ref[c] for c in range(d // LANES)], axis=1)
    else:
        x = x_ref[...]
    xb = x.astype(BF16)

    def proj(c):
        cols = slice(c * d, (c + 1) * d)
        return (jnp.dot(xb, win_ref[:, cols], preferred_element_type=F32)
                + bin_ref[:, cols])

    u = proj(1) * proj(2)
    ubuf[2 * batch:2 * batch + rows, :] = u
    conv = (convw_ref[0:1, :] * ubuf[0:rows, :]
            + convw_ref[1:2, :] * ubuf[batch:batch + rows, :]
            + convw_ref[2:3, :] * u)
    ya = jnp.dot((proj(0) * conv).astype(BF16), wco_ref[...],
                 preferred_element_type=F32)
    tail = ubuf[rows:rows + 2 * batch, :]
    ubuf[0:2 * batch, :] = tail
    cnew_ref[...] = tail
    ya_rows = slice(2 * batch, 2 * batch + rows)
    ubuf[ya_rows, :] = ya

    us = proj(3)
    half = n_blk // 2
    unroll = SCAN_UNROLL if steps % SCAN_UNROLL == 0 else 1
    for k in range(SSM_BLOCKS):
        usk = us[:, k * d_blk:(k + 1) * d_blk].astype(BF16)
        xk_ref[...] = jnp.dot(usk, bbd_ref[k], preferred_element_type=F32)
        for hf in range(2):
            st = slice(k * n_blk + hf * half, k * n_blk + (hf + 1) * half)
            re = slice(hf * half, (hf + 1) * half)
            im = slice(n_blk + hf * half, n_blk + (hf + 1) * half)
            if steps == 1:
                lr = lamr_ref[:, st]
                li = lami_ref[:, st]
                hr = hr_ref[:, st]
                hi = hi_ref[:, st]
                nhr = lr * hr - li * hi + xk_ref[:, re]
                nhi = lr * hi + li * hr + xk_ref[:, im]
                xk_ref[:, re] = nhr
                xk_ref[:, im] = nhi
                hr_ref[:, st] = nhr
                hi_ref[:, st] = nhi
            else:
                lr = jnp.broadcast_to(lamr_ref[:, st], (SUBLANES, half))
                li = jnp.broadcast_to(lami_ref[:, st], (SUBLANES, half))
                for s in range(batch // SUBLANES):
                    grp = slice(s * SUBLANES, (s + 1) * SUBLANES)

                    def step(tt, carry, s=s, re=re, im=im, lr=lr, li=li):
                        hr, hi = carry
                        for k_un in range(unroll):
                            row = pl.multiple_of((tt * unroll + k_un) * batch + s * SUBLANES,
                                                 SUBLANES)
                            rs = pl.ds(row, SUBLANES)
                            hr, hi = (lr * hr - li * hi + xk_ref[rs, re],
                                      lr * hi + li * hr + xk_ref[rs, im])
                            xk_ref[rs, re] = hr
                            xk_ref[rs, im] = hi
                        return hr, hi

                    hr, hi = lax.fori_loop(0, steps // unroll, step,
                                           (hr_ref[grp, st], hi_ref[grp, st]))
                    hr_ref[grp, st] = hr
                    hi_ref[grp, st] = hi
        ys_ref[:, k * d_blk:(k + 1) * d_blk] = jnp.dot(
            xk_ref[...].astype(BF16), cbd_ref[k], preferred_element_type=F32)
    ys_ref[...] = ys_ref[...] + dskip_ref[...] * us

    chunk = min(rows, POST_ROWS)
    for q in range(rows // chunk):
        rq = slice(q * chunk, (q + 1) * chunk)
        if len(x_ref.shape) == 3:
            xq = jnp.concatenate([xs_ref[c, rq, :] for c in range(d // LANES)], axis=1)
        else:
            xq = x_ref[rq, :]
        xbq = xq.astype(BF16)

        def projq(c, xbq=xbq):
            cols = slice(c * d, (c + 1) * d)
            return (jnp.dot(xbq, win_ref[:, cols], preferred_element_type=F32)
                    + bin_ref[:, cols])

        z = jax.nn.gelu(ys_ref[rq, :])
        gate = (jnp.dot(z.astype(BF16), wglu_ref[...], preferred_element_type=F32)
                + bglu_ref[...])
        glu = z * jax.nn.sigmoid(gate)
        yb = jnp.dot(glu.astype(BF16), wso_ref[...], preferred_element_type=F32)

        m = (jax.nn.sigmoid(projq(4)) * ubuf[2 * batch + q * chunk:2 * batch + (q + 1) * chunk, :]
             + jax.nn.sigmoid(projq(5)) * yb)
        o = jnp.dot(m.astype(BF16), wo_ref[...], preferred_element_type=F32)
        x1 = _layer_norm(alpha * xq + o, ln1g_ref[...], ln1b_ref[...])
        x1_ref[rq, :] = x1

        nt = (((1,), (1,)), ((), ()))
        w_hi, w_lo = _split_bf16(wrt_ref[...])
        x_hi, x_lo = _split_bf16(x1)
        logits = (lax.dot_general(w_hi, x_hi, nt, preferred_element_type=F32)
                  + lax.dot_general(w_hi, x_lo, nt, preferred_element_type=F32)
                  + lax.dot_general(w_lo, x_hi, nt, preferred_element_type=F32))
        scores = jax.nn.sigmoid(logits)
        comb = _route(scores, scores + rbias_ref[...])
        comb_ref[rq, :] = comb.T
        if sub:
            combt_ref[:, rq] = comb
    if sub:
        ones = jnp.ones((SUBLANES, sub), BF16)
        for s in range(rows // sub):
            sel = jnp.where(combt_ref[:, s * sub:(s + 1) * sub] != 0.0, 1.0, 0.0).astype(BF16)
            cnt_ref[s] = lax.dot_general(ones, sel, (((1,), (1,)), ((), ())),
                                         preferred_element_type=F32)


def _mixer(x, cprev, h0r, h0i, p, *, alpha, batch, steps, sub=0):
    d = x.shape[-1]
    total = x.size // d
    rows = batch * steps
    assert not sub or rows % sub == 0
    n_state = h0r.shape[1]
    n_exp = p['w_router_t'].shape[0]
    grid = (total // rows,)
    if x.ndim == 3:
        x_spec = pl.BlockSpec((batch, steps, d), lambda i: (0, i, 0))
    else:
        x_spec = pl.BlockSpec((rows, d), lambda i: (i, 0))
    consts = [p['w_in'], p['b_in'], p['conv_w'], p['w_conv_out'],
              p['lam_r'], p['lam_i'], p['bbd'], p['cbd'], p['ssm_d'],
              p['w_glu'], p['b_glu'], p['w_ssm_out'], p['w_o'], p['ln1_g'], p['ln1_b'],
              p['w_router_t'], p['router_bias']]
    in_specs = ([x_spec,
                 _const_spec(cprev.shape), _const_spec(h0r.shape), _const_spec(h0i.shape)]
                + [_const_spec(c.shape) for c in consts])
    out_shape = (jax.ShapeDtypeStruct((total, d), F32),
                 jax.ShapeDtypeStruct((total, n_exp), F32),
                 jax.ShapeDtypeStruct((2 * batch, d), F32),
                 jax.ShapeDtypeStruct((batch, n_state), F32),
                 jax.ShapeDtypeStruct((batch, n_state), F32))
    out_specs = (pl.BlockSpec((rows, d), lambda i: (i, 0)),
                 pl.BlockSpec((rows, n_exp), lambda i: (i, 0)),
                 pl.BlockSpec((2 * batch, d), lambda i: (0, 0)),
                 pl.BlockSpec((batch, n_state), lambda i: (0, 0)),
                 pl.BlockSpec((batch, n_state), lambda i: (0, 0)))
    if sub:
        out_shape += (jax.ShapeDtypeStruct((n_exp, total), F32),
                      jax.ShapeDtypeStruct((total // sub, SUBLANES, n_exp), F32))
        out_specs += (pl.BlockSpec((n_exp, rows), lambda i: (0, i)),
                      pl.BlockSpec((rows // sub, SUBLANES, n_exp), lambda i: (i, 0, 0)))
    scratch = [pltpu.VMEM((rows + 2 * batch, d), F32),
               pltpu.VMEM((rows, 2 * n_state // SSM_BLOCKS), F32),
               pltpu.VMEM((rows, d), F32)]
    if x.ndim == 3:
        scratch.append(pltpu.VMEM((d // LANES, rows, LANES), F32))
    return pl.pallas_call(
        functools.partial(_mixer_kernel, alpha, batch, steps, sub),
        grid=grid, in_specs=in_specs, out_specs=out_specs, out_shape=out_shape,
        scratch_shapes=scratch,
        compiler_params=pltpu.CompilerParams(
            dimension_semantics=("arbitrary",), vmem_limit_bytes=VMEM_LIMIT),
        name="mixer",
    )(x, cprev, h0r, h0i, *consts)


def _swiglu(xb, wg, wu):
    g = jnp.dot(xb, wg.astype(BF16), preferred_element_type=F32)
    u = jnp.dot(xb, wu.astype(BF16), preferred_element_type=F32)
    return jax.nn.silu(g) * u


def _moe_kernel(alpha, emit_bf16, x1_ref, comb_ref, wg_ref, wu_ref, wd_ref,
                wsg_ref, wsu_ref, wsd_ref, g_ref, b_ref, out_ref, *rest):
    if emit_bf16:
        wgb_ref, wub_ref, wdb_ref, xb_ref, acc_ref = rest
    else:
        xb_ref, acc_ref = rest
    e = pl.program_id(1)

    @pl.when(e == 0)
    def _shared():
        xb = x1_ref[...].astype(BF16)
        xb_ref[...] = xb
        hs = _swiglu(xb, wsg_ref[...], wsu_ref[...])
        acc_ref[...] = jnp.dot(hs.astype(BF16), wsd_ref[...].astype(BF16),
                               preferred_element_type=F32)

    comb = comb_ref[...]
    lane = lax.broadcasted_iota(jnp.int32, comb.shape, 1)
    xb = xb_ref[...]
    acc = acc_ref[...]
    for k in range(wg_ref.shape[0]):
        wg = wg_ref[k].astype(BF16)
        wu = wu_ref[k].astype(BF16)
        wd = wd_ref[k].astype(BF16)
        if emit_bf16:
            wgb_ref[k] = wg
            wub_ref[k] = wu
            wdb_ref[k] = wd
        c = jnp.sum(jnp.where(lane == e * wg_ref.shape[0] + k, comb, 0.0), axis=1, keepdims=True)
        h = _swiglu(xb, wg, wu) * c
        acc = acc + jnp.dot(h.astype(BF16), wd, preferred_element_type=F32)
    acc_ref[...] = acc

    @pl.when(e == pl.num_programs(1) - 1)
    def _finish():
        out_ref[...] = _layer_norm(alpha * x1_ref[...] + acc_ref[...],
                                   g_ref[...], b_ref[...])


def _moe(x1, comb, p, *, alpha, tile):
    total, d = x1.shape
    n_exp, _, f = p['w_gate'].shape
    fs = p['ws_gate'].shape[1]
    eg = DENSE_EXPERTS_PER_STEP
    grid = (total // tile, n_exp // eg)
    emit_bf16 = grid[0] == 1
    up_spec = pl.BlockSpec((eg, d, f), lambda i, e: (e, 0, 0))
    down_spec = pl.BlockSpec((eg, f, d), lambda i, e: (e, 0, 0))
    in_specs = [pl.BlockSpec((tile, d), lambda i, e: (i, 0)),
                pl.BlockSpec((tile, n_exp), lambda i, e: (i, 0)),
                up_spec, up_spec, down_spec,
                _const_spec((d, fs)), _const_spec((d, fs)), _const_spec((fs, d)),
                _const_spec((1, d)), _const_spec((1, d))]
    out_specs = [pl.BlockSpec((tile, d), lambda i, e: (i, 0))]
    out_shape = [jax.ShapeDtypeStruct((total, d), F32)]
    if emit_bf16:
        out_specs += [up_spec, up_spec, down_spec]
        out_shape += [jax.ShapeDtypeStruct(p[k].shape, BF16)
                      for k in ('w_gate', 'w_up', 'w_down')]
    res = pl.pallas_call(
        functools.partial(_moe_kernel, alpha, emit_bf16),
        grid=grid, in_specs=in_specs, out_specs=out_specs, out_shape=out_shape,
        scratch_shapes=[pltpu.VMEM((tile, d), BF16), pltpu.VMEM((tile, d), F32)],
        compiler_params=pltpu.CompilerParams(
            dimension_semantics=("arbitrary", "arbitrary"), vmem_limit_bytes=VMEM_LIMIT),
        name="moe",
    )(x1, comb, p['w_gate'], p['w_up'], p['w_down'],
      p['ws_gate'], p['ws_up'], p['ws_down'], p['ln2_g'], p['ln2_b'])
    if emit_bf16:
        return res[0], tuple(res[1:])
    return res[0], tuple(p[k].astype(BF16) for k in ('w_gate', 'w_up', 'w_down'))


SUB = 256
RUN_ALIGN = 16
FFN_CHUNK = 192
SEL_CHUNK = 512
PACK_STATIC = 4
PACK_FULL = 2
DISPATCH_SUBTILES = 4
EXPERTS_PER_STEP = 4


def _expert_onehot(r, off, cnt):
    hit = (r >= off) & (r < off + cnt)
    return hit, jnp.where(hit, off + 1.0, 0.0)


def _moe_sparse_kernel(alpha, batch, n_sub, cap, eg,
                       off_s, pc_s,
                       x1_ref, comb_ref, combt_ref, offr_ref, cntr_ref, offc_ref, cntc_ref,
                       wg_ref, wu_ref, wd_ref, wsg_ref, wsu_ref, wsd_ref, g_ref, b_ref,
                       out_ref, gbuf, xe, zbuf, *xq):
    i = pl.program_id(0)
    g = pl.program_id(1)
    n_exp = comb_ref.shape[1]
    tri_r = lax.broadcasted_iota(jnp.int32, (SUB, SUB), 0)
    tri_c = lax.broadcasted_iota(jnp.int32, (SUB, SUB), 1)

    def run(j, e):
        idx = (i * n_sub + j) * n_exp + e
        return off_s[idx], pc_s[idx]

    def used_rows(j):
        o, p = run(j, n_exp - 1)
        return o + p

    def rows_at(start, chunks):
        return pl.ds(pl.multiple_of(start, RUN_ALIGN), chunks * RUN_ALIGN)

    rows16 = functools.partial(rows_at, chunks=1)

    @pl.when((i == 0) & (g == 0))
    def _zero():
        xe[...] = jnp.zeros(xe.shape, xe.dtype)
        for buf in xq:
            buf[...] = jnp.zeros(buf.shape, buf.dtype)

    @pl.when(g == 0)
    def _dispatch():
        before = jnp.where(tri_r < tri_c, 1.0, 0.0).astype(BF16)
        for j in range(n_sub):
            tok = slice(j * SUB, (j + 1) * SUB)
            xj = x1_ref[tok, :].astype(BF16)
            sel = combt_ref[:, tok] != 0.0
            pos = jnp.dot(jnp.where(sel, 1.0, 0.0).astype(BF16), before,
                          preferred_element_type=F32)
            posm = jnp.where(sel, pos, -1.0).astype(BF16)
            off = offr_ref[j]
            cnt = cntr_ref[j]
            used = used_rows(j)

            def sort_rows(rc, j=j, xj=xj, posm=posm, off=off, cnt=cnt):
                r = (lax.broadcasted_iota(jnp.int32, (SEL_CHUNK, 1), 0)
                     + rc * SEL_CHUNK).astype(F32)
                hit, start1 = _expert_onehot(r, off, cnt)
                s = jnp.sum(start1, axis=1, keepdims=True)
                q = jnp.where(s > 0.0, r - (s - 1.0), -2.0)
                rank = jnp.dot(jnp.where(hit, 1.0, 0.0).astype(BF16), posm,
                               preferred_element_type=F32)
                pick = jnp.where(rank == q, 1.0, 0.0).astype(BF16)
                gbuf[j, rc * SEL_CHUNK:(rc + 1) * SEL_CHUNK, :] = jnp.dot(
                    pick, xj, preferred_element_type=F32).astype(BF16)

            last = cap // SEL_CHUNK - 1
            for rc in range(last):
                sort_rows(rc)
            pl.when(last * SEL_CHUNK < used)(functools.partial(sort_rows, last))

            @pl.when(last * SEL_CHUNK >= used)
            def _blank(j=j):
                gbuf[j, last * SEL_CHUNK:, :] = jnp.zeros((SEL_CHUNK, gbuf.shape[2]), BF16)

    def ffn_rows(rs, ee):
        xc = xe[rs, :]
        h = (jax.nn.silu(jnp.dot(xc, wg_ref[ee], preferred_element_type=F32))
             * jnp.dot(xc, wu_ref[ee], preferred_element_type=F32))
        xe[rs, :] = jnp.dot(h.astype(BF16), wd_ref[ee], preferred_element_type=F32).astype(BF16)

    def pack(e, buf, static):
        n = jnp.int32(0)
        for j in range(n_sub):
            o, p = run(j, e)
            if static:
                buf[rows_at(n, PACK_STATIC), :] = gbuf[j, rows_at(o, PACK_STATIC), :]
            else:
                def chunk(c, carry, j=j, o=o, n=n):
                    buf[rows16(n + c * RUN_ALIGN), :] = gbuf[j, rows16(o + c * RUN_ALIGN), :]
                    return carry

                lax.fori_loop(0, lax.div(p, RUN_ALIGN), chunk, 0)
            n = n + p
        return n

    def unpack(e, buf, static, keep_next=True):
        n = jnp.int32(0)
        for j in range(n_sub):
            o, p = run(j, e)
            if static and not keep_next:
                gbuf[j, rows_at(o, PACK_STATIC), :] = buf[rows_at(n, PACK_STATIC), :]
            elif static:
                full = PACK_FULL * RUN_ALIGN
                gbuf[j, rows_at(o, PACK_FULL), :] = buf[rows_at(n, PACK_FULL), :]
                rest = PACK_STATIC - PACK_FULL
                dst = rows_at(o + full, rest)
                row = lax.broadcasted_iota(jnp.int32, (rest * RUN_ALIGN, gbuf.shape[2]), 0)
                gbuf[j, dst, :] = jnp.where(row < p - full, buf[rows_at(n + full, rest), :],
                                            gbuf[j, dst, :])
            else:
                def chunk(c, carry, j=j, o=o, n=n):
                    gbuf[j, rows16(o + c * RUN_ALIGN), :] = buf[rows16(n + c * RUN_ALIGN), :]
                    return carry

                lax.fori_loop(0, lax.div(p, RUN_ALIGN), chunk, 0)
            n = n + p

    usual = jnp.bool_(True)
    for ee in range(eg):
        n = jnp.int32(0)
        for j in range(n_sub):
            p = run(j, g * eg + ee)[1]
            usual = usual & (p >= PACK_FULL * RUN_ALIGN) & (p <= PACK_STATIC * RUN_ALIGN)
            n = n + p
        usual = usual & (n <= FFN_CHUNK)

    @pl.when(usual)
    def _together():
        for ee in range(eg):
            pack(g * eg + ee, xq[ee], True)
        xcs = [xq[ee][0:FFN_CHUNK, :] for ee in range(eg)]
        hs = [(jax.nn.silu(jnp.dot(xc, wg_ref[ee], preferred_element_type=F32))
               * jnp.dot(xc, wu_ref[ee], preferred_element_type=F32)).astype(BF16)
              for ee, xc in enumerate(xcs)]
        outs = [jnp.dot(h, wd_ref[ee], preferred_element_type=F32).astype(BF16)
                for ee, h in enumerate(hs)]
        for ee, o in enumerate(outs):
            xq[ee][0:FFN_CHUNK, :] = o
        for ee in range(eg):
            unpack(g * eg + ee, xq[ee], True, keep_next=ee == eg - 1)

    @pl.when(jnp.logical_not(usual))
    def _one_by_one():
        for ee in range(eg):
            e = g * eg + ee
            n = pack(e, xe, False)

            def ffn(k, carry, ee=ee):
                ffn_rows(pl.ds(pl.multiple_of(k * FFN_CHUNK, RUN_ALIGN), FFN_CHUNK), ee)
                return carry

            lax.fori_loop(0, lax.div(n + (FFN_CHUNK - 1), FFN_CHUNK), ffn, 0)
            unpack(e, xe, False)

    @pl.when(g == pl.num_programs(1) - 1)
    def _combine():
        earlier = jnp.where(tri_c < tri_r, 1.0, 0.0).astype(BF16)
        for j in range(n_sub):
            tok = slice(j * SUB, (j + 1) * SUB)
            x = x1_ref[tok, :]
            xb = x.astype(BF16)
            hs = (jax.nn.silu(jnp.dot(xb, wsg_ref[...], preferred_element_type=F32))
                  * jnp.dot(xb, wsu_ref[...], preferred_element_type=F32))
            y = jnp.dot(hs.astype(BF16), wsd_ref[...], preferred_element_type=F32)
            comb = comb_ref[tok, :]
            sel = comb != 0.0
            pos = jnp.dot(earlier, jnp.where(sel, 1.0, 0.0).astype(BF16),
                          preferred_element_type=F32)
            posm = jnp.where(sel, pos, -1.0).astype(BF16)
            c_bf = comb.astype(BF16)
            off = offc_ref[j]
            cnt = cntc_ref[j]
            used = used_rows(j)

            def gather_rows(rc, j=j, posm=posm, c_bf=c_bf, off=off, cnt=cnt):
                r = (lax.broadcasted_iota(jnp.int32, (1, SEL_CHUNK), 1)
                     + rc * SEL_CHUNK).astype(F32)
                hit, start1 = _expert_onehot(r, off, cnt)
                s = jnp.sum(start1, axis=0, keepdims=True)
                q = jnp.where(s > 0.0, r - (s - 1.0), -2.0)
                hb = jnp.where(hit, 1.0, 0.0).astype(BF16)
                pick = jnp.dot(posm, hb, preferred_element_type=F32) == q
                weight = jnp.dot(c_bf, hb, preferred_element_type=F32)
                w = jnp.where(pick, weight, 0.0).astype(BF16)
                rows = gbuf[j, rc * SEL_CHUNK:(rc + 1) * SEL_CHUNK, :]
                return jnp.dot(w, rows, preferred_element_type=F32)

            last = cap // SEL_CHUNK - 1
            for rc in range(last):
                y = y + gather_rows(rc)
            n_lt = x.shape[1] // LANES
            lane_tiles = lambda v: [v[:, c * LANES:(c + 1) * LANES] for c in range(n_lt)]

            def put(v):
                for c, t in enumerate(lane_tiles(v)):
                    zbuf[c] = t

            get = lambda: jnp.concatenate([zbuf[c] for c in range(n_lt)], axis=1)
            put(alpha * x + y)

            @pl.when(last * SEL_CHUNK < used)
            def _tail(gather_rows=gather_rows, put=put, get=get):
                put(get() + gather_rows(last))

            put(_layer_norm(get(), g_ref[...], b_ref[...]))
            t_sub = SUB // batch
            for b in range(batch):
                for c in range(n_lt):
                    out_ref[b, j * t_sub:(j + 1) * t_sub, c * LANES:(c + 1) * LANES] = (
                        zbuf[c, pl.ds(b, t_sub, stride=batch), :])


def _moe_sparse(x1, comb, combt, cnt, p, w_bf16, *, alpha, batch, n_sub, eg=EXPERTS_PER_STEP):
    total, d = x1.shape
    w_gate, w_up, w_down = w_bf16
    n_exp, _, f = w_gate.shape
    fs = p['ws_gate'].shape[1]
    tile = n_sub * SUB
    n_tiles = total // tile
    assert SUB % batch == 0
    cap = -(-(SUB * TOP_K + n_exp * (RUN_ALIGN - 1)) // SEL_CHUNK) * SEL_CHUNK
    xe_rows = -(-(tile + n_sub * (RUN_ALIGN - 1)) // FFN_CHUNK) * FFN_CHUNK

    cnt = cnt[:, 0, :]
    pc = jnp.ceil(cnt / RUN_ALIGN) * RUN_ALIGN
    off = jnp.cumsum(pc, axis=1) - pc
    off_s = off.astype(jnp.int32).reshape(-1)
    pc_s = pc.astype(jnp.int32).reshape(-1)
    offr, cntr = off[:, None, :], cnt[:, None, :]
    offc, cntc = off[:, :, None], cnt[:, :, None]

    row_spec = pl.BlockSpec((n_sub, 1, n_exp), lambda i, g, *_: (i, 0, 0))
    col_spec = pl.BlockSpec((n_sub, n_exp, 1), lambda i, g, *_: (i, 0, 0))
    const = lambda shape: pl.BlockSpec(shape, lambda i, g, *_: (0,) * len(shape),
                                       pipeline_mode=pl.Buffered(1))
    grid_spec = pltpu.PrefetchScalarGridSpec(
        num_scalar_prefetch=2,
        grid=(n_tiles, n_exp // eg),
        in_specs=[pl.BlockSpec((tile, d), lambda i, g, *_: (i, 0), pipeline_mode=pl.Buffered(1)),
                  pl.BlockSpec((tile, n_exp), lambda i, g, *_: (i, 0)),
                  pl.BlockSpec((n_exp, tile), lambda i, g, *_: (0, i)),
                  row_spec, row_spec, col_spec, col_spec,
                  pl.BlockSpec((eg, d, f), lambda i, g, *_: (g, 0, 0)),
                  pl.BlockSpec((eg, d, f), lambda i, g, *_: (g, 0, 0)),
                  pl.BlockSpec((eg, f, d), lambda i, g, *_: (g, 0, 0)),
                  const((d, fs)), const((d, fs)), const((fs, d)),
                  const((1, d)), const((1, d))],
        out_specs=pl.BlockSpec((batch, tile // batch, d), lambda i, g, *_: (0, i, 0),
                               pipeline_mode=pl.Buffered(1)),
        scratch_shapes=[pltpu.VMEM((n_sub, cap, d), BF16), pltpu.VMEM((xe_rows, d), BF16),
                        pltpu.VMEM((d // LANES, SUB, LANES), F32)]
        + [pltpu.VMEM((FFN_CHUNK + PACK_STATIC * RUN_ALIGN, d), BF16)] * eg)
    return pl.pallas_call(
        functools.partial(_moe_sparse_kernel, alpha, batch, n_sub, cap, eg),
        grid_spec=grid_spec,
        out_shape=jax.ShapeDtypeStruct((batch, total // batch, d), F32),
        compiler_params=pltpu.CompilerParams(
            dimension_semantics=("arbitrary", "arbitrary"), vmem_limit_bytes=VMEM_LIMIT),
        name="moe_sparse",
    )(off_s, pc_s, x1, comb, combt, offr, cntr, offc, cntc, w_gate, w_up, w_down,
      p['ws_gate'].astype(BF16), p['ws_up'].astype(BF16), p['ws_down'].astype(BF16),
      p['ln2_g'], p['ln2_b'])


def _layer_params(l, w):
    g, n = w['a_re'].shape[1:]
    lr, li, bbr, bbi = _ssm_prep(w['a_re'][l], w['a_im'][l], w['log_dt'][l],
                                 w['ssm_b_re'][l], w['ssm_b_im'][l])
    bbd = jnp.concatenate([_block_diag(bbr, SSM_BLOCKS), _block_diag(bbi, SSM_BLOCKS)],
                          axis=-1).astype(BF16)
    c_re = w['ssm_c_re'][l].transpose(0, 2, 1)
    c_im = w['ssm_c_im'][l].transpose(0, 2, 1)
    cbd = jnp.concatenate([_block_diag(c_re, SSM_BLOCKS), _block_diag(-c_im, SSM_BLOCKS)],
                          axis=1).astype(BF16)
    row = lambda v: v.reshape(1, -1)
    return {
        'w_in': w['w_in'][l].astype(BF16), 'b_in': row(w['b_in'][l]),
        'conv_w': w['conv_w'][l], 'w_conv_out': w['w_conv_out'][l].astype(BF16),
        'lam_r': lr.reshape(1, g * n), 'lam_i': li.reshape(1, g * n),
        'bbd': bbd, 'cbd': cbd, 'ssm_d': row(w['ssm_d'][l]),
        'w_glu': w['w_glu'][l].astype(BF16), 'b_glu': row(w['b_glu'][l]),
        'w_ssm_out': w['w_ssm_out'][l].astype(BF16), 'w_o': w['w_o'][l].astype(BF16),
        'ln1_g': row(w['ln1_g'][l]), 'ln1_b': row(w['ln1_b'][l]),
        'w_router_t': w['w_router'][l].T, 'router_bias': w['router_bias'][l].reshape(-1, 1),
        'w_gate': w['w_gate'][l], 'w_up': w['w_up'][l], 'w_down': w['w_down'][l],
        'ws_gate': w['ws_gate'][l], 'ws_up': w['ws_up'][l], 'ws_down': w['ws_down'][l],
        'ln2_g': row(w['ln2_g'][l]), 'ln2_b': row(w['ln2_b'][l]),
    }


def _pick_steps(batch, seq, max_rows):
    steps = max(1, min(seq, max_rows // batch))
    while seq % steps:
        steps -= 1
    return steps


def _pick_tile(total, max_tile):
    tile = min(total, max_tile)
    while total % tile or tile % 16:
        tile -= 16
    return tile


MIXER_ROWS = 512
MOE_TILE = 1024


def _trunk_layer(x, conv_l, re_l, im_l, p, alpha, w_bf16):
    bsz, seq, d = x.shape
    assert bsz % SUBLANES == 0
    steps = _pick_steps(bsz, seq, MIXER_ROWS)
    kw = conv_l.shape[1]
    assert kw == 2
    cprev = conv_l.astype(F32).transpose(1, 0, 2).reshape(kw * bsz, d)
    h0r = re_l.astype(F32).reshape(bsz, -1)
    h0i = im_l.astype(F32).reshape(bsz, -1)
    if ((bsz * steps) % SUB == 0 and (bsz * seq) % (DISPATCH_SUBTILES * SUB) == 0
            and SUB % bsz == 0):
        if w_bf16 is None:
            w_bf16 = tuple(p[k].astype(BF16) for k in ('w_gate', 'w_up', 'w_down'))
        x1, comb, cnew, hr, hi, combt, cnt = _mixer(
            x, cprev, h0r, h0i, p, alpha=alpha, batch=bsz, steps=steps, sub=SUB)
        y = _moe_sparse(x1, comb, combt, cnt, p, w_bf16, alpha=alpha, batch=bsz,
                        n_sub=DISPATCH_SUBTILES)
    else:
        rows = x.transpose(1, 0, 2).reshape(seq * bsz, d)
        x1, comb, cnew, hr, hi = _mixer(rows, cprev, h0r, h0i, p,
                                        alpha=alpha, batch=bsz, steps=steps)
        out, w_bf16 = _moe(x1, comb, p, alpha=alpha, tile=_pick_tile(bsz * seq, MOE_TILE))
        y = out.reshape(seq, bsz, d).transpose(1, 0, 2)
    states = (cnew.reshape(kw, bsz, d).transpose(1, 0, 2),
              hr.reshape(re_l.shape), hi.reshape(im_l.shape))
    return y, states, w_bf16


def kernel(x_prompt, x_sample, state_conv, state_ssm_re, state_ssm_im,
           w_in, b_in, conv_w, w_conv_out, a_re, a_im, log_dt,
           ssm_b_re, ssm_b_im, ssm_c_re, ssm_c_im, ssm_d, w_glu, b_glu, w_ssm_out, w_o,
           ln1_g, ln1_b, w_router, router_bias, w_gate, w_up, w_down,
           ws_gate, ws_up, ws_down, ln2_g, ln2_b):
    w = dict(w_in=w_in, b_in=b_in, conv_w=conv_w, w_conv_out=w_conv_out,
             a_re=a_re, a_im=a_im, log_dt=log_dt,
             ssm_b_re=ssm_b_re, ssm_b_im=ssm_b_im, ssm_c_re=ssm_c_re, ssm_c_im=ssm_c_im,
             ssm_d=ssm_d, w_glu=w_glu, b_glu=b_glu, w_ssm_out=w_ssm_out, w_o=w_o,
             ln1_g=ln1_g, ln1_b=ln1_b, w_router=w_router, router_bias=router_bias,
             w_gate=w_gate, w_up=w_up, w_down=w_down,
             ws_gate=ws_gate, ws_up=ws_up, ws_down=ws_down, ln2_g=ln2_g, ln2_b=ln2_b)
    depth = w_in.shape[0]
    alpha = (2.0 * depth) ** 0.25
    bsz = x_prompt.shape[0]
    zero_conv = jnp.zeros((bsz,) + state_conv.shape[2:], x_prompt.dtype)
    zero_ssm = jnp.zeros((bsz,) + state_ssm_re.shape[2:], F32)
    y_p, y_s = x_prompt, x_sample
    st_p, st_s = [], []
    for l in range(depth):
        p = _layer_params(l, w)
        y_s, st, w_bf16 = _trunk_layer(y_s, state_conv[l], state_ssm_re[l], state_ssm_im[l],
                                       p, alpha, None)
        st_s.append(st)
        y_p, st, _ = _trunk_layer(y_p, zero_conv, zero_ssm, zero_ssm, p, alpha, w_bf16)
        st_p.append(st)
    conv_p, re_p, im_p = (jnp.stack(v) for v in zip(*st_p))
    conv_s, re_s, im_s = (jnp.stack(v) for v in zip(*st_s))
    return (y_p, y_s, conv_p, re_p, im_p, conv_s, re_s, im_s)
```

```python
import functools
import math

import jax
import jax.numpy as jnp
from jax import lax
from jax.experimental import pallas as pl
from jax.experimental.pallas import tpu as pltpu

F32 = jnp.float32
BF16 = jnp.bfloat16

LN_EPS = 1e-5
ROUTED_SCALE = 2.5
N_ROUTE_GROUPS = 8
TOPK_GROUPS = 4
TOP_K = 8

SUBLANES = 8
LANES = 128
SSM_BLOCKS = 4
SCAN_UNROLL = 4
POST_ROWS = 512
DENSE_EXPERTS_PER_STEP = 2
VMEM_LIMIT = 60 * 1024 * 1024


def _const_spec(shape):
    nd = len(shape)
    return pl.BlockSpec(shape, lambda *_: (0,) * nd, pipeline_mode=pl.Buffered(1))


def _ssm_prep_kernel(are_ref, aim_ref, ldt_ref, br_ref, bi_ref,
                     lr_ref, li_ref, bbr_ref, bbi_ref):
    dt = jnp.exp(ldt_ref[...])
    ar = are_ref[...]
    ai = aim_ref[...]
    mag = jnp.exp(ar * dt)
    lr = mag * jnp.cos(ai * dt)
    li = mag * jnp.sin(ai * dt)
    den = ar * ar + ai * ai
    fr = ((lr - 1.0) * ar + li * ai) / den
    fi = (li * ar - (lr - 1.0) * ai) / den
    lr_ref[...] = lr
    li_ref[...] = li
    br = br_ref[...]
    bi = bi_ref[...]
    bbr_ref[...] = fr * br - fi * bi
    bbi_ref[...] = fr * bi + fi * br


def _ssm_prep(a_re, a_im, log_dt, b_re, b_im):
    g, n = a_re.shape
    h = b_re.shape[-1]
    vec = jax.ShapeDtypeStruct((g, 1, n), F32)
    mat = jax.ShapeDtypeStruct((g, h, n), F32)
    return pl.pallas_call(
        _ssm_prep_kernel,
        out_shape=(vec, vec, mat, mat),
        name="ssm_prep",
    )(a_re.reshape(g, 1, n), a_im.reshape(g, 1, n), log_dt.reshape(g, 1, 1),
      b_re.transpose(0, 2, 1), b_im.transpose(0, 2, 1))


def _block_diag(m, nblk):
    g, p, q = m.shape
    gl = g // nblk
    eye = jnp.eye(gl, dtype=m.dtype)
    out = jnp.einsum('kapq,ab->kapbq', m.reshape(nblk, gl, p, q), eye)
    return out.reshape(nblk, gl * p, gl * q)


def _layer_norm(r, g, b):
    mu = jnp.mean(r, axis=-1, keepdims=True)
    d = r - mu
    var = jnp.mean(d * d, axis=-1, keepdims=True)
    return d * lax.rsqrt(var + LN_EPS) * g + b


def _split_bf16(v):
    hi = v.astype(BF16)
    return hi, (v - hi.astype(F32)).astype(BF16)


def _route(scores, biased):
    n_exp, r = scores.shape
    gsz = n_exp // N_ROUTE_GROUPS
    neg = jnp.float32(-jnp.inf)
    rows = []
    for g in range(N_ROUTE_GROUPS):
        v = biased[g * gsz:(g + 1) * gsz, :]
        m1 = jnp.max(v, axis=0, keepdims=True)
        is_max = v == m1
        n_max = jnp.sum(is_max.astype(F32), axis=0, keepdims=True)
        rest = jnp.max(jnp.where(is_max, neg, v), axis=0, keepdims=True)
        rows.append(m1 + jnp.where(n_max >= 2.0, m1, rest))
    gscore = jnp.concatenate(rows, axis=0)
    gidx = lax.broadcasted_iota(jnp.int32, gscore.shape, 0)
    grank = jnp.zeros(gscore.shape, F32)
    for g in range(N_ROUTE_GROUPS):
        sg = gscore[g:g + 1, :]
        beats = (sg > gscore) | ((sg == gscore) & (gidx > g))
        grank = grank + beats.astype(F32)
    gkeep = grank < float(TOPK_GROUPS)
    masked = jnp.concatenate(
        [jnp.where(gkeep[g:g + 1, :], biased[g * gsz:(g + 1) * gsz, :], neg)
         for g in range(N_ROUTE_GROUPS)], axis=0)
    eidx = lax.broadcasted_iota(jnp.int32, masked.shape, 0).astype(F32)
    left = masked
    for _ in range(TOP_K):
        top = jnp.max(left, axis=0, keepdims=True)
        first = jnp.min(jnp.where(left == top, eidx, float(n_exp)), axis=0, keepdims=True)
        left = jnp.where(eidx == first, neg, left)
    w = jnp.where(left != masked, scores, 0.0)
    return w / jnp.sum(w, axis=0, keepdims=True) * ROUTED_SCALE


def _mixer_kernel(alpha, batch, steps, sub,
                  x_ref, cprev_ref, h0r_ref, h0i_ref,
                  win_ref, bin_ref, convw_ref, wco_ref,
                  lamr_ref, lami_ref, bbd_ref, cbd_ref, dskip_ref,
                  wglu_ref, bglu_ref, wso_ref, wo_ref, ln1g_ref, ln1b_ref,
                  wrt_ref, rbias_ref,
                  x1_ref, comb_ref, cnew_ref, hr_ref, hi_ref, *rest):
    if sub:
        combt_ref, cnt_ref, *rest = rest
    ubuf, xk_ref, ys_ref, *rest = rest
    rows = batch * steps
    d = x_ref.shape[-1]
    d_blk = d // SSM_BLOCKS
    n_blk = lamr_ref.shape[1] // SSM_BLOCKS
    i = pl.program_id(0)

    @pl.when(i == 0)
    def _init():
        ubuf[0:2 * batch, :] = cprev_ref[...]
        hr_ref[...] = h0r_ref[...]
        hi_ref[...] = h0i_ref[...]

    if len(x_ref.shape) == 3:
        xs_ref, = rest
        for b in range(batch):
            for c in range(d // LANES):
                xs_ref[c, pl.ds(b, steps, stride=batch), :] = x_ref[b, :, c * LANES:(c + 1) * LANES]
        x = jnp.concatenate([xs_ref[c] for c in range(d // LANES)], axis=1)
    else:
        x = x_ref[...]
    xb = x.astype(BF16)

    def proj(c):
        cols = slice(c * d, (c + 1) * d)
        return (jnp.dot(xb, win_ref[:, cols], preferred_element_type=F32)
                + bin_ref[:, cols])

    u = proj(1) * proj(2)
    ubuf[2 * batch:2 * batch + rows, :] = u
    conv = (convw_ref[0:1, :] * ubuf[0:rows, :]
            + convw_ref[1:2, :] * ubuf[batch:batch + rows, :]
            + convw_ref[2:3, :] * u)
    ya = jnp.dot((proj(0) * conv).astype(BF16), wco_ref[...],
                 preferred_element_type=F32)
    tail = ubuf[rows:rows + 2 * batch, :]
    ubuf[0:2 * batch, :] = tail
    cnew_ref[...] = tail
    ya_rows = slice(2 * batch, 2 * batch + rows)
    ubuf[ya_rows, :] = ya

    us = proj(3)
    half = n_blk // 2
    unroll = SCAN_UNROLL if steps % SCAN_UNROLL == 0 else 1
    for k in range(SSM_BLOCKS):
        usk = us[:, k * d_blk:(k + 1) * d_blk].astype(BF16)
        xk_ref[...] = jnp.dot(usk, bbd_ref[k], preferred_element_type=F32)
        for hf in range(2):
            st = slice(k * n_blk + hf * half, k * n_blk + (hf + 1) * half)
            re = slice(hf * half, (hf + 1) * half)
            im = slice(n_blk + hf * half, n_blk + (hf + 1) * half)
            if steps == 1:
                lr = lamr_ref[:, st]
                li = lami_ref[:, st]
                hr = hr_ref[:, st]
                hi = hi_ref[:, st]
                nhr = lr * hr - li * hi + xk_ref[:, re]
                nhi = lr * hi + li * hr + xk_ref[:, im]
                xk_ref[:, re] = nhr
                xk_ref[:, im] = nhi
                hr_ref[:, st] = nhr
                hi_ref[:, st] = nhi
            else:
                lr = jnp.broadcast_to(lamr_ref[:, st], (SUBLANES, half))
                li = jnp.broadcast_to(lami_ref[:, st], (SUBLANES, half))
                for s in range(batch // SUBLANES):
                    grp = slice(s * SUBLANES, (s + 1) * SUBLANES)

                    def step(tt, carry, s=s, re=re, im=im, lr=lr, li=li):
                        hr, hi = carry
                        for k_un in range(unroll):
                            row = pl.multiple_of((tt * unroll + k_un) * batch + s * SUBLANES,
                                                 SUBLANES)
                            rs = pl.ds(row, SUBLANES)
                            hr, hi = (lr * hr - li * hi + xk_ref[rs, re],
                                      lr * hi + li * hr + xk_ref[rs, im])
                            xk_ref[rs, re] = hr
                            xk_ref[rs, im] = hi
                        return hr, hi

                    hr, hi = lax.fori_loop(0, steps // unroll, step,
                                           (hr_ref[grp, st], hi_ref[grp, st]))
                    hr_ref[grp, st] = hr
                    hi_ref[grp, st] = hi
        ys_ref[:, k * d_blk:(k + 1) * d_blk] = jnp.dot(
            xk_ref[...].astype(BF16), cbd_ref[k], preferred_element_type=F32)
    ys_ref[...] = ys_ref[...] + dskip_ref[...] * us

    chunk = min(rows, POST_ROWS)
    for q in range(rows // chunk):
        rq = slice(q * chunk, (q + 1) * chunk)
        if len(x_ref.shape) == 3:
            xq = jnp.concatenate([xs_ref[c, rq, :] for c in range(d // LANES)], axis=1)
        else:
            xq = x_ref[rq, :]
        xbq = xq.astype(BF16)

        def projq(c, xbq=xbq):
            cols = slice(c * d, (c + 1) * d)
            return (jnp.dot(xbq, win_ref[:, cols], preferred_element_type=F32)
                    + bin_ref[:, cols])

        z = jax.nn.gelu(ys_ref[rq, :])
        gate = (jnp.dot(z.astype(BF16), wglu_ref[...], preferred_element_type=F32)
                + bglu_ref[...])
        glu = z * jax.nn.sigmoid(gate)
        yb = jnp.dot(glu.astype(BF16), wso_ref[...], preferred_element_type=F32)

        m = (jax.nn.sigmoid(projq(4)) * ubuf[2 * batch + q * chunk:2 * batch + (q + 1) * chunk, :]
             + jax.nn.sigmoid(projq(5)) * yb)
        o = jnp.dot(m.astype(BF16), wo_ref[...], preferred_element_type=F32)
        x1 = _layer_norm(alpha * xq + o, ln1g_ref[...], ln1b_ref[...])
        x1_ref[rq, :] = x1

        nt = (((1,), (1,)), ((), ()))
        w_hi, w_lo = _split_bf16(wrt_ref[...])
        x_hi, x_lo = _split_bf16(x1)
        logits = (lax.dot_general(w_hi, x_hi, nt, preferred_element_type=F32)
                  + lax.dot_general(w_hi, x_lo, nt, preferred_element_type=F32)
                  + lax.dot_general(w_lo, x_hi, nt, preferred_element_type=F32))
        scores = jax.nn.sigmoid(logits)
        comb = _route(scores, scores + rbias_ref[...])
        comb_ref[rq, :] = comb.T
        if sub:
            combt_ref[:, rq] = comb
    if sub:
        ones = jnp.ones((SUBLANES, sub), BF16)
        for s in range(rows // sub):
            sel = jnp.where(combt_ref[:, s * sub:(s + 1) * sub] != 0.0, 1.0, 0.0).astype(BF16)
            cnt_ref[s] = lax.dot_general(ones, sel, (((1,), (1,)), ((), ())),
                                         preferred_element_type=F32)


def _mixer(x, cprev, h0r, h0i, p, *, alpha, batch, steps, sub=0):
    d = x.shape[-1]
    total = x.size // d
    rows = batch * steps
    assert not sub or rows % sub == 0
    n_state = h0r.shape[1]
    n_exp = p['w_router_t'].shape[0]
    grid = (total // rows,)
    if x.ndim == 3:
        x_spec = pl.BlockSpec((batch, steps, d), lambda i: (0, i, 0))
    else:
        x_spec = pl.BlockSpec((rows, d), lambda i: (i, 0))
    consts = [p['w_in'], p['b_in'], p['conv_w'], p['w_conv_out'],
              p['lam_r'], p['lam_i'], p['bbd'], p['cbd'], p['ssm_d'],
              p['w_glu'], p['b_glu'], p['w_ssm_out'], p['w_o'], p['ln1_g'], p['ln1_b'],
              p['w_router_t'], p['router_bias']]
    in_specs = ([x_spec,
                 _const_spec(cprev.shape), _const_spec(h0r.shape), _const_spec(h0i.shape)]
                + [_const_spec(c.shape) for c in consts])
    out_shape = (jax.ShapeDtypeStruct((total, d), F32),
                 jax.ShapeDtypeStruct((total, n_exp), F32),
                 jax.ShapeDtypeStruct((2 * batch, d), F32),
                 jax.ShapeDtypeStruct((batch, n_state), F32),
                 jax.ShapeDtypeStruct((batch, n_state), F32))
    out_specs = (pl.BlockSpec((rows, d), lambda i: (i, 0)),
                 pl.BlockSpec((rows, n_exp), lambda i: (i, 0)),
                 pl.BlockSpec((2 * batch, d), lambda i: (0, 0)),
                 pl.BlockSpec((batch, n_state), lambda i: (0, 0)),
                 pl.BlockSpec((batch, n_state), lambda i: (0, 0)))
    if sub:
        out_shape += (jax.ShapeDtypeStruct((n_exp, total), F32),
                      jax.ShapeDtypeStruct((total // sub, SUBLANES, n_exp), F32))
        out_specs += (pl.BlockSpec((n_exp, rows), lambda i: (0, i)),
                      pl.BlockSpec((rows // sub, SUBLANES, n_exp), lambda i: (i, 0, 0)))
    scratch = [pltpu.VMEM((rows + 2 * batch, d), F32),
               pltpu.VMEM((rows, 2 * n_state // SSM_BLOCKS), F32),
               pltpu.VMEM((rows, d), F32)]
    if x.ndim == 3:
        scratch.append(pltpu.VMEM((d // LANES, rows, LANES), F32))
    return pl.pallas_call(
        functools.partial(_mixer_kernel, alpha, batch, steps, sub),
        grid=grid, in_specs=in_specs, out_specs=out_specs, out_shape=out_shape,
        scratch_shapes=scratch,
        compiler_params=pltpu.CompilerParams(
            dimension_semantics=("arbitrary",), vmem_limit_bytes=VMEM_LIMIT),
        name="mixer",
    )(x, cprev, h0r, h0i, *consts)


def _swiglu(xb, wg, wu):
    g = jnp.dot(xb, wg.astype(BF16), preferred_element_type=F32)
    u = jnp.dot(xb, wu.astype(BF16), preferred_element_type=F32)
    return jax.nn.silu(g) * u


def _moe_kernel(alpha, emit_bf16, x1_ref, comb_ref, wg_ref, wu_ref, wd_ref,
                wsg_ref, wsu_ref, wsd_ref, g_ref, b_ref, out_ref, *rest):
    if emit_bf16:
        wgub_ref, wdb_ref, xb_ref, acc_ref = rest
    else:
        xb_ref, acc_ref = rest
    e = pl.program_id(1)

    @pl.when(e == 0)
    def _shared():
        xb = x1_ref[...].astype(BF16)
        xb_ref[...] = xb
        hs = _swiglu(xb, wsg_ref[...], wsu_ref[...])
        acc_ref[...] = jnp.dot(hs.astype(BF16), wsd_ref[...].astype(BF16),
                               preferred_element_type=F32)

    comb = comb_ref[...]
    lane = lax.broadcasted_iota(jnp.int32, comb.shape, 1)
    xb = xb_ref[...]
    acc = acc_ref[...]
    for k in range(wg_ref.shape[0]):
        wg = wg_ref[k].astype(BF16)
        wu = wu_ref[k].astype(BF16)
        wd = wd_ref[k].astype(BF16)
        if emit_bf16:
            f = wg.shape[1]
            wgub_ref[k, :, 0:f] = wg
            wgub_ref[k, :, f:2 * f] = wu
            wdb_ref[k] = wd
        c = jnp.sum(jnp.where(lane == e * wg_ref.shape[0] + k, comb, 0.0), axis=1, keepdims=True)
        h = _swiglu(xb, wg, wu) * c
        acc = acc + jnp.dot(h.astype(BF16), wd, preferred_element_type=F32)
    acc_ref[...] = acc

    @pl.when(e == pl.num_programs(1) - 1)
    def _finish():
        out_ref[...] = _layer_norm(alpha * x1_ref[...] + acc_ref[...],
                                   g_ref[...], b_ref[...])


def _bf16_experts(p):
    return (jnp.concatenate([p['w_gate'], p['w_up']], axis=-1).astype(BF16),
            p['w_down'].astype(BF16))


def _moe(x1, comb, p, *, alpha, tile):
    total, d = x1.shape
    n_exp, _, f = p['w_gate'].shape
    fs = p['ws_gate'].shape[1]
    eg = DENSE_EXPERTS_PER_STEP
    grid = (total // tile, n_exp // eg)
    emit_bf16 = grid[0] == 1
    up_spec = pl.BlockSpec((eg, d, f), lambda i, e: (e, 0, 0))
    down_spec = pl.BlockSpec((eg, f, d), lambda i, e: (e, 0, 0))
    in_specs = [pl.BlockSpec((tile, d), lambda i, e: (i, 0)),
                pl.BlockSpec((tile, n_exp), lambda i, e: (i, 0)),
                up_spec, up_spec, down_spec,
                _const_spec((d, fs)), _const_spec((d, fs)), _const_spec((fs, d)),
                _const_spec((1, d)), _const_spec((1, d))]
    out_specs = [pl.BlockSpec((tile, d), lambda i, e: (i, 0))]
    out_shape = [jax.ShapeDtypeStruct((total, d), F32)]
    if emit_bf16:
        out_specs += [pl.BlockSpec((eg, d, 2 * f), lambda i, e: (e, 0, 0)), down_spec]
        out_shape += [jax.ShapeDtypeStruct((n_exp, d, 2 * f), BF16),
                      jax.ShapeDtypeStruct(p['w_down'].shape, BF16)]
    res = pl.pallas_call(
        functools.partial(_moe_kernel, alpha, emit_bf16),
        grid=grid, in_specs=in_specs, out_specs=out_specs, out_shape=out_shape,
        scratch_shapes=[pltpu.VMEM((tile, d), BF16), pltpu.VMEM((tile, d), F32)],
        compiler_params=pltpu.CompilerParams(
            dimension_semantics=("arbitrary", "arbitrary"), vmem_limit_bytes=VMEM_LIMIT),
        name="moe",
    )(x1, comb, p['w_gate'], p['w_up'], p['w_down'],
      p['ws_gate'], p['ws_up'], p['ws_down'], p['ln2_g'], p['ln2_b'])
    if emit_bf16:
        return res[0], tuple(res[1:])
    return res[0], _bf16_experts(p)


SUB = 256
RUN_ALIGN = 16
FFN_CHUNK = 192
SEL_CHUNK = 512
PACK_STATIC = 4
PACK_FULL = 2
DISPATCH_SUBTILES = 4
EXPERTS_PER_STEP = 4


def _expert_onehot(r, off, cnt):
    hit = (r >= off) & (r < off + cnt)
    return hit, jnp.where(hit, off + 1.0, 0.0)


def _moe_sparse_kernel(alpha, batch, n_sub, cap, eg,
                       off_s, pc_s,
                       x1_ref, comb_ref, combt_ref, offr_ref, cntr_ref, offc_ref, cntc_ref,
                       wgu_ref, wd_ref, wsg_ref, wsu_ref, wsd_ref, g_ref, b_ref,
                       out_ref, gbuf, xe, zbuf, *xq):
    i = pl.program_id(0)
    g = pl.program_id(1)
    n_exp = comb_ref.shape[1]
    tri_r = lax.broadcasted_iota(jnp.int32, (SUB, SUB), 0)
    tri_c = lax.broadcasted_iota(jnp.int32, (SUB, SUB), 1)

    def run(j, e):
        idx = (i * n_sub + j) * n_exp + e
        return off_s[idx], pc_s[idx]

    def used_rows(j):
        o, p = run(j, n_exp - 1)
        return o + p

    def rows_at(start, chunks):
        return pl.ds(pl.multiple_of(start, RUN_ALIGN), chunks * RUN_ALIGN)

    rows16 = functools.partial(rows_at, chunks=1)

    @pl.when((i == 0) & (g == 0))
    def _zero():
        xe[...] = jnp.zeros(xe.shape, xe.dtype)
        for buf in xq:
            buf[...] = jnp.zeros(buf.shape, buf.dtype)

    @pl.when(g == 0)
    def _dispatch():
        before = jnp.where(tri_r < tri_c, 1.0, 0.0).astype(BF16)
        for j in range(n_sub):
            tok = slice(j * SUB, (j + 1) * SUB)
            xj = x1_ref[tok, :].astype(BF16)
            sel = combt_ref[:, tok] != 0.0
            pos = jnp.dot(jnp.where(sel, 1.0, 0.0).astype(BF16), before,
                          preferred_element_type=F32)
            posm = jnp.where(sel, pos, -1.0).astype(BF16)
            off = offr_ref[j]
            cnt = cntr_ref[j]
            used = used_rows(j)

            def sort_rows(rc, j=j, xj=xj, posm=posm, off=off, cnt=cnt):
                r = (lax.broadcasted_iota(jnp.int32, (SEL_CHUNK, 1), 0)
                     + rc * SEL_CHUNK).astype(F32)
                hit, start1 = _expert_onehot(r, off, cnt)
                s = jnp.sum(start1, axis=1, keepdims=True)
                q = jnp.where(s > 0.0, r - (s - 1.0), -2.0)
                rank = jnp.dot(jnp.where(hit, 1.0, 0.0).astype(BF16), posm,
                               preferred_element_type=F32)
                pick = jnp.where(rank == q, 1.0, 0.0).astype(BF16)
                gbuf[j, rc * SEL_CHUNK:(rc + 1) * SEL_CHUNK, :] = jnp.dot(
                    pick, xj, preferred_element_type=F32).astype(BF16)

            last = cap // SEL_CHUNK - 1
            for rc in range(last):
                sort_rows(rc)
            pl.when(last * SEL_CHUNK < used)(functools.partial(sort_rows, last))

            @pl.when(last * SEL_CHUNK >= used)
            def _blank(j=j):
                gbuf[j, last * SEL_CHUNK:, :] = jnp.zeros((SEL_CHUNK, gbuf.shape[2]), BF16)

    def gated(xc, ee):
        gu = jnp.dot(xc, wgu_ref[ee], preferred_element_type=F32)
        f = gu.shape[1] // 2
        return (jax.nn.silu(gu[:, :f]) * gu[:, f:]).astype(BF16)

    def ffn_rows(rs, ee):
        xc = xe[rs, :]
        xe[rs, :] = jnp.dot(gated(xc, ee), wd_ref[ee], preferred_element_type=F32).astype(BF16)

    def pack(e, buf, static):
        n = jnp.int32(0)
        for j in range(n_sub):
            o, p = run(j, e)
            if static:
                buf[rows_at(n, PACK_STATIC), :] = gbuf[j, rows_at(o, PACK_STATIC), :]
            else:
                def chunk(c, carry, j=j, o=o, n=n):
                    buf[rows16(n + c * RUN_ALIGN), :] = gbuf[j, rows16(o + c * RUN_ALIGN), :]
                    return carry

                lax.fori_loop(0, lax.div(p, RUN_ALIGN), chunk, 0)
            n = n + p
        return n

    def unpack(e, buf, static, keep_next=True):
        n = jnp.int32(0)
        for j in range(n_sub):
            o, p = run(j, e)
            if static and not keep_next:
                gbuf[j, rows_at(o, PACK_STATIC), :] = buf[rows_at(n, PACK_STATIC), :]
            elif static:
                full = PACK_FULL * RUN_ALIGN
                gbuf[j, rows_at(o, PACK_FULL), :] = buf[rows_at(n, PACK_FULL), :]
                rest = PACK_STATIC - PACK_FULL
                dst = rows_at(o + full, rest)
                row = lax.broadcasted_iota(jnp.int32, (rest * RUN_ALIGN, gbuf.shape[2]), 0)
                gbuf[j, dst, :] = jnp.where(row < p - full, buf[rows_at(n + full, rest), :],
                                            gbuf[j, dst, :])
            else:
                def chunk(c, carry, j=j, o=o, n=n):
                    gbuf[j, rows16(o + c * RUN_ALIGN), :] = buf[rows16(n + c * RUN_ALIGN), :]
                    return carry

                lax.fori_loop(0, lax.div(p, RUN_ALIGN), chunk, 0)
            n = n + p

    usual = jnp.bool_(True)
    for ee in range(eg):
        n = jnp.int32(0)
        for j in range(n_sub):
            p = run(j, g * eg + ee)[1]
            usual = usual & (p >= PACK_FULL * RUN_ALIGN) & (p <= PACK_STATIC * RUN_ALIGN)
            n = n + p
        usual = usual & (n <= FFN_CHUNK)

    @pl.when(usual)
    def _together():
        for ee in range(eg):
            pack(g * eg + ee, xq[ee], True)
        xcs = [xq[ee][0:FFN_CHUNK, :] for ee in range(eg)]
        hs = [gated(xc, ee) for ee, xc in enumerate(xcs)]
        outs = [jnp.dot(h, wd_ref[ee], preferred_element_type=F32).astype(BF16)
                for ee, h in enumerate(hs)]
        for ee, o in enumerate(outs):
            xq[ee][0:FFN_CHUNK, :] = o
        for ee in range(eg):
            unpack(g * eg + ee, xq[ee], True, keep_next=ee == eg - 1)

    @pl.when(jnp.logical_not(usual))
    def _one_by_one():
        for ee in range(eg):
            e = g * eg + ee
            n = pack(e, xe, False)

            def ffn(k, carry, ee=ee):
                ffn_rows(pl.ds(pl.multiple_of(k * FFN_CHUNK, RUN_ALIGN), FFN_CHUNK), ee)
                return carry

            lax.fori_loop(0, lax.div(n + (FFN_CHUNK - 1), FFN_CHUNK), ffn, 0)
            unpack(e, xe, False)

    @pl.when(g == pl.num_programs(1) - 1)
    def _combine():
        earlier = jnp.where(tri_c < tri_r, 1.0, 0.0).astype(BF16)
        for j in range(n_sub):
            tok = slice(j * SUB, (j + 1) * SUB)
            x = x1_ref[tok, :]
            xb = x.astype(BF16)
            hs = (jax.nn.silu(jnp.dot(xb, wsg_ref[...], preferred_element_type=F32))
                  * jnp.dot(xb, wsu_ref[...], preferred_element_type=F32))
            y = jnp.dot(hs.astype(BF16), wsd_ref[...], preferred_element_type=F32)
            comb = comb_ref[tok, :]
            sel = comb != 0.0
            pos = jnp.dot(earlier, jnp.where(sel, 1.0, 0.0).astype(BF16),
                          preferred_element_type=F32)
            posm = jnp.where(sel, pos, -1.0).astype(BF16)
            c_bf = comb.astype(BF16)
            off = offc_ref[j]
            cnt = cntc_ref[j]
            used = used_rows(j)

            def gather_rows(rc, j=j, posm=posm, c_bf=c_bf, off=off, cnt=cnt):
                r = (lax.broadcasted_iota(jnp.int32, (1, SEL_CHUNK), 1)
                     + rc * SEL_CHUNK).astype(F32)
                hit, start1 = _expert_onehot(r, off, cnt)
                s = jnp.sum(start1, axis=0, keepdims=True)
                q = jnp.where(s > 0.0, r - (s - 1.0), -2.0)
                hb = jnp.where(hit, 1.0, 0.0).astype(BF16)
                pick = jnp.dot(posm, hb, preferred_element_type=F32) == q
                weight = jnp.dot(c_bf, hb, preferred_element_type=F32)
                w = jnp.where(pick, weight, 0.0).astype(BF16)
                rows = gbuf[j, rc * SEL_CHUNK:(rc + 1) * SEL_CHUNK, :]
                return jnp.dot(w, rows, preferred_element_type=F32)

            last = cap // SEL_CHUNK - 1
            for rc in range(last):
                y = y + gather_rows(rc)
            n_lt = x.shape[1] // LANES
            lane_tiles = lambda v: [v[:, c * LANES:(c + 1) * LANES] for c in range(n_lt)]

            def put(v):
                for c, t in enumerate(lane_tiles(v)):
                    zbuf[c] = t

            get = lambda: jnp.concatenate([zbuf[c] for c in range(n_lt)], axis=1)
            put(alpha * x + y)

            @pl.when(last * SEL_CHUNK < used)
            def _tail(gather_rows=gather_rows, put=put, get=get):
                put(get() + gather_rows(last))

            put(_layer_norm(get(), g_ref[...], b_ref[...]))
            t_sub = SUB // batch
            for b in range(batch):
                for c in range(n_lt):
                    out_ref[b, j * t_sub:(j + 1) * t_sub, c * LANES:(c + 1) * LANES] = (
                        zbuf[c, pl.ds(b, t_sub, stride=batch), :])


def _moe_sparse(x1, comb, combt, cnt, p, w_bf16, *, alpha, batch, n_sub, eg=EXPERTS_PER_STEP):
    total, d = x1.shape
    w_gate_up, w_down = w_bf16
    n_exp, f, _ = w_down.shape
    fs = p['ws_gate'].shape[1]
    tile = n_sub * SUB
    n_tiles = total // tile
    assert SUB % batch == 0
    cap = -(-(SUB * TOP_K + n_exp * (RUN_ALIGN - 1)) // SEL_CHUNK) * SEL_CHUNK
    xe_rows = -(-(tile + n_sub * (RUN_ALIGN - 1)) // FFN_CHUNK) * FFN_CHUNK

    cnt = cnt[:, 0, :]
    pc = jnp.ceil(cnt / RUN_ALIGN) * RUN_ALIGN
    off = jnp.cumsum(pc, axis=1) - pc
    off_s = off.astype(jnp.int32).reshape(-1)
    pc_s = pc.astype(jnp.int32).reshape(-1)
    offr, cntr = off[:, None, :], cnt[:, None, :]
    offc, cntc = off[:, :, None], cnt[:, :, None]

    row_spec = pl.BlockSpec((n_sub, 1, n_exp), lambda i, g, *_: (i, 0, 0))
    col_spec = pl.BlockSpec((n_sub, n_exp, 1), lambda i, g, *_: (i, 0, 0))
    const = lambda shape: pl.BlockSpec(shape, lambda i, g, *_: (0,) * len(shape),
                                       pipeline_mode=pl.Buffered(1))
    grid_spec = pltpu.PrefetchScalarGridSpec(
        num_scalar_prefetch=2,
        grid=(n_tiles, n_exp // eg),
        in_specs=[pl.BlockSpec((tile, d), lambda i, g, *_: (i, 0), pipeline_mode=pl.Buffered(1)),
                  pl.BlockSpec((tile, n_exp), lambda i, g, *_: (i, 0)),
                  pl.BlockSpec((n_exp, tile), lambda i, g, *_: (0, i)),
                  row_spec, row_spec, col_spec, col_spec,
                  pl.BlockSpec((eg, d, 2 * f), lambda i, g, *_: (g, 0, 0)),
                  pl.BlockSpec((eg, f, d), lambda i, g, *_: (g, 0, 0)),
                  const((d, fs)), const((d, fs)), const((fs, d)),
                  const((1, d)), const((1, d))],
        out_specs=pl.BlockSpec((batch, tile // batch, d), lambda i, g, *_: (0, i, 0),
                               pipeline_mode=pl.Buffered(1)),
        scratch_shapes=[pltpu.VMEM((n_sub, cap, d), BF16), pltpu.VMEM((xe_rows, d), BF16),
                        pltpu.VMEM((d // LANES, SUB, LANES), F32)]
        + [pltpu.VMEM((FFN_CHUNK + PACK_STATIC * RUN_ALIGN, d), BF16)] * eg)
    return pl.pallas_call(
        functools.partial(_moe_sparse_kernel, alpha, batch, n_sub, cap, eg),
        grid_spec=grid_spec,
        out_shape=jax.ShapeDtypeStruct((batch, total // batch, d), F32),
        compiler_params=pltpu.CompilerParams(
            dimension_semantics=("arbitrary", "arbitrary"), vmem_limit_bytes=VMEM_LIMIT),
        name="moe_sparse",
    )(off_s, pc_s, x1, comb, combt, offr, cntr, offc, cntc, w_gate_up, w_down,
      p['ws_gate'].astype(BF16), p['ws_up'].astype(BF16), p['ws_down'].astype(BF16),
      p['ln2_g'], p['ln2_b'])


def _layer_params(l, w):
    g, n = w['a_re'].shape[1:]
    lr, li, bbr, bbi = _ssm_prep(w['a_re'][l], w['a_im'][l], w['log_dt'][l],
                                 w['ssm_b_re'][l], w['ssm_b_im'][l])
    bbd = jnp.concatenate([_block_diag(bbr, SSM_BLOCKS), _block_diag(bbi, SSM_BLOCKS)],
                          axis=-1).astype(BF16)
    c_re = w['ssm_c_re'][l].transpose(0, 2, 1)
    c_im = w['ssm_c_im'][l].transpose(0, 2, 1)
    cbd = jnp.concatenate([_block_diag(c_re, SSM_BLOCKS), _block_diag(-c_im, SSM_BLOCKS)],
                          axis=1).astype(BF16)
    row = lambda v: v.reshape(1, -1)
    return {
        'w_in': w['w_in'][l].astype(BF16), 'b_in': row(w['b_in'][l]),
        'conv_w': w['conv_w'][l], 'w_conv_out': w['w_conv_out'][l].astype(BF16),
        'lam_r': lr.reshape(1, g * n), 'lam_i': li.reshape(1, g * n),
        'bbd': bbd, 'cbd': cbd, 'ssm_d': row(w['ssm_d'][l]),
        'w_glu': w['w_glu'][l].astype(BF16), 'b_glu': row(w['b_glu'][l]),
        'w_ssm_out': w['w_ssm_out'][l].astype(BF16), 'w_o': w['w_o'][l].astype(BF16),
        'ln1_g': row(w['ln1_g'][l]), 'ln1_b': row(w['ln1_b'][l]),
        'w_router_t': w['w_router'][l].T, 'router_bias': w['router_bias'][l].reshape(-1, 1),
        'w_gate': w['w_gate'][l], 'w_up': w['w_up'][l], 'w_down': w['w_down'][l],
        'ws_gate': w['ws_gate'][l], 'ws_up': w['ws_up'][l], 'ws_down': w['ws_down'][l],
        'ln2_g': row(w['ln2_g'][l]), 'ln2_b': row(w['ln2_b'][l]),
    }


def _pick_steps(batch, seq, max_rows):
    steps = max(1, min(seq, max_rows // batch))
    while seq % steps:
        steps -= 1
    return steps


def _pick_tile(total, max_tile):
    tile = min(total, max_tile)
    while total % tile or tile % 16:
        tile -= 16
    return tile


MIXER_ROWS = 512
MOE_TILE = 1024


def _trunk_layer(x, conv_l, re_l, im_l, p, alpha, w_bf16):
    bsz, seq, d = x.shape
    assert bsz % SUBLANES == 0
    steps = _pick_steps(bsz, seq, MIXER_ROWS)
    kw = conv_l.shape[1]
    assert kw == 2
    cprev = conv_l.astype(F32).transpose(1, 0, 2).reshape(kw * bsz, d)
    h0r = re_l.astype(F32).reshape(bsz, -1)
    h0i = im_l.astype(F32).reshape(bsz, -1)
    if ((bsz * steps) % SUB == 0 and (bsz * seq) % (DISPATCH_SUBTILES * SUB) == 0
            and SUB % bsz == 0):
        if w_bf16 is None:
            w_bf16 = _bf16_experts(p)
        x1, comb, cnew, hr, hi, combt, cnt = _mixer(
            x, cprev, h0r, h0i, p, alpha=alpha, batch=bsz, steps=steps, sub=SUB)
        y = _moe_sparse(x1, comb, combt, cnt, p, w_bf16, alpha=alpha, batch=bsz,
                        n_sub=DISPATCH_SUBTILES)
    else:
        rows = x.transpose(1, 0, 2).reshape(seq * bsz, d)
        x1, comb, cnew, hr, hi = _mixer(rows, cprev, h0r, h0i, p,
                                        alpha=alpha, batch=bsz, steps=steps)
        out, w_bf16 = _moe(x1, comb, p, alpha=alpha, tile=_pick_tile(bsz * seq, MOE_TILE))
        y = out.reshape(seq, bsz, d).transpose(1, 0, 2)
    states = (cnew.reshape(kw, bsz, d).transpose(1, 0, 2),
              hr.reshape(re_l.shape), hi.reshape(im_l.shape))
    return y, states, w_bf16


def kernel(x_prompt, x_sample, state_conv, state_ssm_re, state_ssm_im,
           w_in, b_in, conv_w, w_conv_out, a_re, a_im, log_dt,
           ssm_b_re, ssm_b_im, ssm_c_re, ssm_c_im, ssm_d, w_glu, b_glu, w_ssm_out, w_o,
           ln1_g, ln1_b, w_router, router_bias, w_gate, w_up, w_down,
           ws_gate, ws_up, ws_down, ln2_g, ln2_b):
    w = dict(w_in=w_in, b_in=b_in, conv_w=conv_w, w_conv_out=w_conv_out,
             a_re=a_re, a_im=a_im, log_dt=log_dt,
             ssm_b_re=ssm_b_re, ssm_b_im=ssm_b_im, ssm_c_re=ssm_c_re, ssm_c_im=ssm_c_im,
             ssm_d=ssm_d, w_glu=w_glu, b_glu=b_glu, w_ssm_out=w_ssm_out, w_o=w_o,
             ln1_g=ln1_g, ln1_b=ln1_b, w_router=w_router, router_bias=router_bias,
             w_gate=w_gate, w_up=w_up, w_down=w_down,
             ws_gate=ws_gate, ws_up=ws_up, ws_down=ws_down, ln2_g=ln2_g, ln2_b=ln2_b)
    depth = w_in.shape[0]
    alpha = (2.0 * depth) ** 0.25
    bsz = x_prompt.shape[0]
    zero_conv = jnp.zeros((bsz,) + state_conv.shape[2:], x_prompt.dtype)
    zero_ssm = jnp.zeros((bsz,) + state_ssm_re.shape[2:], F32)
    y_p, y_s = x_prompt, x_sample
    st_p, st_s = [], []
    for l in range(depth):
        p = _layer_params(l, w)
        y_s, st, w_bf16 = _trunk_layer(y_s, state_conv[l], state_ssm_re[l], state_ssm_im[l],
                                       p, alpha, None)
        st_s.append(st)
        y_p, st, _ = _trunk_layer(y_p, zero_conv, zero_ssm, zero_ssm, p, alpha, w_bf16)
        st_p.append(st)
    conv_p, re_p, im_p = (jnp.stack(v) for v in zip(*st_p))
    conv_s, re_s, im_s = (jnp.stack(v) for v in zip(*st_s))
    return (y_p, y_s, conv_p, re_p, im_p, conv_s, re_s, im_s)
```

```python
import functools
import math

import jax
import jax.numpy as jnp
from jax import lax
from jax.experimental import pallas as pl
from jax.experimental.pallas import tpu as pltpu

F32 = jnp.float32
BF16 = jnp.bfloat16

LN_EPS = 1e-5
ROUTED_SCALE = 2.5
N_ROUTE_GROUPS = 8
TOPK_GROUPS = 4
TOP_K = 8

SUBLANES = 8
LANES = 128
SSM_BLOCKS = 4
SCAN_UNROLL = 4
SCAN_SPLIT = 1
POST_ROWS = 512
DENSE_EXPERTS_PER_STEP = 2
VMEM_LIMIT = 60 * 1024 * 1024


def _const_spec(shape):
    nd = len(shape)
    return pl.BlockSpec(shape, lambda *_: (0,) * nd, pipeline_mode=pl.Buffered(1))


def _ssm_prep_kernel(are_ref, aim_ref, ldt_ref, br_ref, bi_ref,
                     lr_ref, li_ref, bbr_ref, bbi_ref):
    dt = jnp.exp(ldt_ref[...])
    ar = are_ref[...]
    ai = aim_ref[...]
    mag = jnp.exp(ar * dt)
    lr = mag * jnp.cos(ai * dt)
    li = mag * jnp.sin(ai * dt)
    den = ar * ar + ai * ai
    fr = ((lr - 1.0) * ar + li * ai) / den
    fi = (li * ar - (lr - 1.0) * ai) / den
    lr_ref[...] = lr
    li_ref[...] = li
    br = br_ref[...]
    bi = bi_ref[...]
    bbr_ref[...] = fr * br - fi * bi
    bbi_ref[...] = fr * bi + fi * br


def _ssm_prep(a_re, a_im, log_dt, b_re, b_im):
    g, n = a_re.shape
    h = b_re.shape[-1]
    vec = jax.ShapeDtypeStruct((g, 1, n), F32)
    mat = jax.ShapeDtypeStruct((g, h, n), F32)
    return pl.pallas_call(
        _ssm_prep_kernel,
        out_shape=(vec, vec, mat, mat),
        name="ssm_prep",
    )(a_re.reshape(g, 1, n), a_im.reshape(g, 1, n), log_dt.reshape(g, 1, 1),
      b_re.transpose(0, 2, 1), b_im.transpose(0, 2, 1))


def _block_diag(m, nblk):
    g, p, q = m.shape
    gl = g // nblk
    tiled = jnp.tile(m.reshape(nblk, gl * p, q), (1, 1, gl))
    row = lax.broadcasted_iota(jnp.int32, (gl * p, gl * q), 0) // p
    col = lax.broadcasted_iota(jnp.int32, (gl * p, gl * q), 1) // q
    return jnp.where(row == col, tiled, 0.0)


def _layer_norm(r, g, b):
    mu = jnp.mean(r, axis=-1, keepdims=True)
    d = r - mu
    var = jnp.mean(d * d, axis=-1, keepdims=True)
    return d * lax.rsqrt(var + LN_EPS) * g + b


def _split_bf16(v):
    hi = v.astype(BF16)
    return hi, (v - hi.astype(F32)).astype(BF16)


def _route(scores, biased):
    n_exp, r = scores.shape
    gsz = n_exp // N_ROUTE_GROUPS
    neg = jnp.float32(-jnp.inf)
    rows = []
    for g in range(N_ROUTE_GROUPS):
        v = biased[g * gsz:(g + 1) * gsz, :]
        m1 = jnp.max(v, axis=0, keepdims=True)
        is_max = v == m1
        n_max = jnp.sum(is_max.astype(F32), axis=0, keepdims=True)
        rest = jnp.max(jnp.where(is_max, neg, v), axis=0, keepdims=True)
        rows.append(m1 + jnp.where(n_max >= 2.0, m1, rest))
    gscore = jnp.concatenate(rows, axis=0)
    gidx = lax.broadcasted_iota(jnp.int32, gscore.shape, 0)
    grank = jnp.zeros(gscore.shape, F32)
    for g in range(N_ROUTE_GROUPS):
        sg = gscore[g:g + 1, :]
        beats = (sg > gscore) | ((sg == gscore) & (gidx > g))
        grank = grank + beats.astype(F32)
    gkeep = grank < float(TOPK_GROUPS)
    masked = jnp.concatenate(
        [jnp.where(gkeep[g:g + 1, :], biased[g * gsz:(g + 1) * gsz, :], neg)
         for g in range(N_ROUTE_GROUPS)], axis=0)
    eidx = lax.broadcasted_iota(jnp.int32, masked.shape, 0).astype(F32)
    left = masked
    for _ in range(TOP_K):
        top = jnp.max(left, axis=0, keepdims=True)
        first = jnp.min(jnp.where(left == top, eidx, float(n_exp)), axis=0, keepdims=True)
        left = jnp.where(eidx == first, neg, left)
    w = jnp.where(left != masked, scores, 0.0)
    return w / jnp.sum(w, axis=0, keepdims=True) * ROUTED_SCALE


def _mixer_kernel(alpha, batch, steps, sub,
                  x_ref, cprev_ref, h0r_ref, h0i_ref,
                  win_ref, bin_ref, convw_ref, wco_ref,
                  lamr_ref, lami_ref, bbd_ref, cbd_ref, dskip_ref,
                  wglu_ref, bglu_ref, wso_ref, wo_ref, ln1g_ref, ln1b_ref,
                  wrt_ref, rbias_ref,
                  x1_ref, comb_ref, cnew_ref, hr_ref, hi_ref, *rest):
    if sub:
        combt_ref, cnt_ref, *rest = rest
    ubuf, xk_ref, ys_ref, *rest = rest
    rows = batch * steps
    d = x_ref.shape[-1]
    d_blk = d // SSM_BLOCKS
    n_blk = lamr_ref.shape[1] // SSM_BLOCKS
    i = pl.program_id(0)

    @pl.when(i == 0)
    def _init():
        ubuf[0:2 * batch, :] = cprev_ref[...]
        hr_ref[...] = h0r_ref[...]
        hi_ref[...] = h0i_ref[...]

    if len(x_ref.shape) == 3:
        xs_ref, = rest
        for b in range(batch):
            for c in range(d // LANES):
                xs_ref[c, pl.ds(b, steps, stride=batch), :] = x_ref[b, :, c * LANES:(c + 1) * LANES]
        x = jnp.concatenate([xs_ref[c] for c in range(d // LANES)], axis=1)
    else:
        x = x_ref[...]
    xb = x.astype(BF16)

    def proj(c):
        cols = slice(c * d, (c + 1) * d)
        return (jnp.dot(xb, win_ref[:, cols], preferred_element_type=F32)
                + bin_ref[:, cols])

    u = proj(1) * proj(2)
    ubuf[2 * batch:2 * batch + rows, :] = u
    conv = (convw_ref[0:1, :] * ubuf[0:rows, :]
            + convw_ref[1:2, :] * ubuf[batch:batch + rows, :]
            + convw_ref[2:3, :] * u)
    ya = jnp.dot((proj(0) * conv).astype(BF16), wco_ref[...],
                 preferred_element_type=F32)
    tail = ubuf[rows:rows + 2 * batch, :]
    ubuf[0:2 * batch, :] = tail
    cnew_ref[...] = tail
    ya_rows = slice(2 * batch, 2 * batch + rows)
    ubuf[ya_rows, :] = ya

    us = proj(3)
    half = n_blk // SCAN_SPLIT
    unroll = SCAN_UNROLL if steps % SCAN_UNROLL == 0 else 1
    for k in range(SSM_BLOCKS):
        usk = us[:, k * d_blk:(k + 1) * d_blk].astype(BF16)
        xk_ref[...] = jnp.dot(usk, bbd_ref[k], preferred_element_type=F32)
        for hf in range(SCAN_SPLIT):
            st = slice(k * n_blk + hf * half, k * n_blk + (hf + 1) * half)
            re = slice(hf * half, (hf + 1) * half)
            im = slice(n_blk + hf * half, n_blk + (hf + 1) * half)
            if steps == 1:
                lr = lamr_ref[:, st]
                li = lami_ref[:, st]
                hr = hr_ref[:, st]
                hi = hi_ref[:, st]
                nhr = lr * hr - li * hi + xk_ref[:, re]
                nhi = lr * hi + li * hr + xk_ref[:, im]
                xk_ref[:, re] = nhr
                xk_ref[:, im] = nhi
                hr_ref[:, st] = nhr
                hi_ref[:, st] = nhi
            else:
                lr = jnp.broadcast_to(lamr_ref[:, st], (SUBLANES, half))
                li = jnp.broadcast_to(lami_ref[:, st], (SUBLANES, half))
                for s in range(batch // SUBLANES):
                    grp = slice(s * SUBLANES, (s + 1) * SUBLANES)

                    def step(tt, carry, s=s, re=re, im=im, lr=lr, li=li):
                        hr, hi = carry
                        for k_un in range(unroll):
                            row = pl.multiple_of((tt * unroll + k_un) * batch + s * SUBLANES,
                                                 SUBLANES)
                            rs = pl.ds(row, SUBLANES)
                            hr, hi = (lr * hr - li * hi + xk_ref[rs, re],
                                      lr * hi + li * hr + xk_ref[rs, im])
                            xk_ref[rs, re] = hr
                            xk_ref[rs, im] = hi
                        return hr, hi

                    hr, hi = lax.fori_loop(0, steps // unroll, step,
                                           (hr_ref[grp, st], hi_ref[grp, st]))
                    hr_ref[grp, st] = hr
                    hi_ref[grp, st] = hi
        ys_ref[:, k * d_blk:(k + 1) * d_blk] = jnp.dot(
            xk_ref[...].astype(BF16), cbd_ref[k], preferred_element_type=F32)
    ys_ref[...] = ys_ref[...] + dskip_ref[...] * us

    chunk = min(rows, POST_ROWS)
    for q in range(rows // chunk):
        rq = slice(q * chunk, (q + 1) * chunk)
        if len(x_ref.shape) == 3:
            xq = jnp.concatenate([xs_ref[c, rq, :] for c in range(d // LANES)], axis=1)
        else:
            xq = x_ref[rq, :]
        xbq = xq.astype(BF16)

        def projq(c, xbq=xbq):
            cols = slice(c * d, (c + 1) * d)
            return (jnp.dot(xbq, win_ref[:, cols], preferred_element_type=F32)
                    + bin_ref[:, cols])

        z = jax.nn.gelu(ys_ref[rq, :])
        gate = (jnp.dot(z.astype(BF16), wglu_ref[...], preferred_element_type=F32)
                + bglu_ref[...])
        glu = z * jax.nn.sigmoid(gate)
        yb = jnp.dot(glu.astype(BF16), wso_ref[...], preferred_element_type=F32)

        m = (jax.nn.sigmoid(projq(4)) * ubuf[2 * batch + q * chunk:2 * batch + (q + 1) * chunk, :]
             + jax.nn.sigmoid(projq(5)) * yb)
        o = jnp.dot(m.astype(BF16), wo_ref[...], preferred_element_type=F32)
        x1 = _layer_norm(alpha * xq + o, ln1g_ref[...], ln1b_ref[...])
        x1_ref[rq, :] = x1

        nt = (((1,), (1,)), ((), ()))
        w_hi, w_lo = _split_bf16(wrt_ref[...])
        x_hi, x_lo = _split_bf16(x1)
        logits = (lax.dot_general(w_hi, x_hi, nt, preferred_element_type=F32)
                  + lax.dot_general(w_hi, x_lo, nt, preferred_element_type=F32)
                  + lax.dot_general(w_lo, x_hi, nt, preferred_element_type=F32))
        scores = jax.nn.sigmoid(logits)
        comb = _route(scores, scores + rbias_ref[...])
        comb_ref[rq, :] = comb.T
        if sub:
            combt_ref[:, rq] = comb
    if sub:
        ones = jnp.ones((SUBLANES, sub), BF16)
        for s in range(rows // sub):
            sel = jnp.where(combt_ref[:, s * sub:(s + 1) * sub] != 0.0, 1.0, 0.0).astype(BF16)
            cnt_ref[s] = lax.dot_general(ones, sel, (((1,), (1,)), ((), ())),
                                         preferred_element_type=F32)


def _mixer(x, cprev, h0r, h0i, p, *, alpha, batch, steps, sub=0):
    d = x.shape[-1]
    total = x.size // d
    rows = batch * steps
    assert not sub or rows % sub == 0
    n_state = h0r.shape[1]
    n_exp = p['w_router_t'].shape[0]
    grid = (total // rows,)
    if x.ndim == 3:
        x_spec = pl.BlockSpec((batch, steps, d), lambda i: (0, i, 0))
    else:
        x_spec = pl.BlockSpec((rows, d), lambda i: (i, 0))
    consts = [p['w_in'], p['b_in'], p['conv_w'], p['w_conv_out'],
              p['lam_r'], p['lam_i'], p['bbd'], p['cbd'], p['ssm_d'],
              p['w_glu'], p['b_glu'], p['w_ssm_out'], p['w_o'], p['ln1_g'], p['ln1_b'],
              p['w_router_t'], p['router_bias']]
    in_specs = ([x_spec,
                 _const_spec(cprev.shape), _const_spec(h0r.shape), _const_spec(h0i.shape)]
                + [_const_spec(c.shape) for c in consts])
    out_shape = (jax.ShapeDtypeStruct((total, d), F32),
                 jax.ShapeDtypeStruct((total, n_exp), F32),
                 jax.ShapeDtypeStruct((2 * batch, d), F32),
                 jax.ShapeDtypeStruct((batch, n_state), F32),
                 jax.ShapeDtypeStruct((batch, n_state), F32))
    out_specs = (pl.BlockSpec((rows, d), lambda i: (i, 0)),
                 pl.BlockSpec((rows, n_exp), lambda i: (i, 0)),
                 pl.BlockSpec((2 * batch, d), lambda i: (0, 0)),
                 pl.BlockSpec((batch, n_state), lambda i: (0, 0)),
                 pl.BlockSpec((batch, n_state), lambda i: (0, 0)))
    if sub:
        out_shape += (jax.ShapeDtypeStruct((n_exp, total), F32),
                      jax.ShapeDtypeStruct((total // sub, SUBLANES, n_exp), F32))
        out_specs += (pl.BlockSpec((n_exp, rows), lambda i: (0, i)),
                      pl.BlockSpec((rows // sub, SUBLANES, n_exp), lambda i: (i, 0, 0)))
    scratch = [pltpu.VMEM((rows + 2 * batch, d), F32),
               pltpu.VMEM((rows, 2 * n_state // SSM_BLOCKS), F32),
               pltpu.VMEM((rows, d), F32)]
    if x.ndim == 3:
        scratch.append(pltpu.VMEM((d // LANES, rows, LANES), F32))
    return pl.pallas_call(
        functools.partial(_mixer_kernel, alpha, batch, steps, sub),
        grid=grid, in_specs=in_specs, out_specs=out_specs, out_shape=out_shape,
        scratch_shapes=scratch,
        compiler_params=pltpu.CompilerParams(
            dimension_semantics=("arbitrary",), vmem_limit_bytes=VMEM_LIMIT),
        name="mixer",
    )(x, cprev, h0r, h0i, *consts)


def _swiglu(xb, wg, wu):
    g = jnp.dot(xb, wg.astype(BF16), preferred_element_type=F32)
    u = jnp.dot(xb, wu.astype(BF16), preferred_element_type=F32)
    return jax.nn.silu(g) * u


def _moe_kernel(alpha, emit_bf16, x1_ref, comb_ref, wg_ref, wu_ref, wd_ref,
                wsg_ref, wsu_ref, wsd_ref, g_ref, b_ref, out_ref, *rest):
    if emit_bf16:
        wgb_ref, wub_ref, wdb_ref, xb_ref, acc_ref = rest
    else:
        xb_ref, acc_ref = rest
    e = pl.program_id(1)

    @pl.when(e == 0)
    def _shared():
        xb = x1_ref[...].astype(BF16)
        xb_ref[...] = xb
        hs = _swiglu(xb, wsg_ref[...], wsu_ref[...])
        acc_ref[...] = jnp.dot(hs.astype(BF16), wsd_ref[...].astype(BF16),
                               preferred_element_type=F32)

    comb = comb_ref[...]
    lane = lax.broadcasted_iota(jnp.int32, comb.shape, 1)
    xb = xb_ref[...]
    acc = acc_ref[...]
    for k in range(wg_ref.shape[0]):
        wg = wg_ref[k].astype(BF16)
        wu = wu_ref[k].astype(BF16)
        wd = wd_ref[k].astype(BF16)
        if emit_bf16:
            wgb_ref[k] = wg
            wub_ref[k] = wu
            wdb_ref[k] = wd
        c = jnp.sum(jnp.where(lane == e * wg_ref.shape[0] + k, comb, 0.0), axis=1, keepdims=True)
        h = _swiglu(xb, wg, wu) * c
        acc = acc + jnp.dot(h.astype(BF16), wd, preferred_element_type=F32)
    acc_ref[...] = acc

    @pl.when(e == pl.num_programs(1) - 1)
    def _finish():
        out_ref[...] = _layer_norm(alpha * x1_ref[...] + acc_ref[...],
                                   g_ref[...], b_ref[...])


def _moe(x1, comb, p, *, alpha, tile):
    total, d = x1.shape
    n_exp, _, f = p['w_gate'].shape
    fs = p['ws_gate'].shape[1]
    eg = DENSE_EXPERTS_PER_STEP
    grid = (total // tile, n_exp // eg)
    emit_bf16 = grid[0] == 1
    up_spec = pl.BlockSpec((eg, d, f), lambda i, e: (e, 0, 0))
    down_spec = pl.BlockSpec((eg, f, d), lambda i, e: (e, 0, 0))
    in_specs = [pl.BlockSpec((tile, d), lambda i, e: (i, 0)),
                pl.BlockSpec((tile, n_exp), lambda i, e: (i, 0)),
                up_spec, up_spec, down_spec,
                _const_spec((d, fs)), _const_spec((d, fs)), _const_spec((fs, d)),
                _const_spec((1, d)), _const_spec((1, d))]
    out_specs = [pl.BlockSpec((tile, d), lambda i, e: (i, 0))]
    out_shape = [jax.ShapeDtypeStruct((total, d), F32)]
    if emit_bf16:
        out_specs += [up_spec, up_spec, down_spec]
        out_shape += [jax.ShapeDtypeStruct(p[k].shape, BF16)
                      for k in ('w_gate', 'w_up', 'w_down')]
    res = pl.pallas_call(
        functools.partial(_moe_kernel, alpha, emit_bf16),
        grid=grid, in_specs=in_specs, out_specs=out_specs, out_shape=out_shape,
        scratch_shapes=[pltpu.VMEM((tile, d), BF16), pltpu.VMEM((tile, d), F32)],
        compiler_params=pltpu.CompilerParams(
            dimension_semantics=("arbitrary", "arbitrary"), vmem_limit_bytes=VMEM_LIMIT),
        name="moe",
    )(x1, comb, p['w_gate'], p['w_up'], p['w_down'],
      p['ws_gate'], p['ws_up'], p['ws_down'], p['ln2_g'], p['ln2_b'])
    if emit_bf16:
        return res[0], tuple(res[1:])
    return res[0], tuple(p[k].astype(BF16) for k in ('w_gate', 'w_up', 'w_down'))


SUB = 256
RUN_ALIGN = 16
FFN_CHUNK = 192
SEL_CHUNK = 512
PACK_STATIC = 4
PACK_FULL = 2
DISPATCH_SUBTILES = 4
EXPERTS_PER_STEP = 4


def _expert_onehot(r, off, cnt):
    hit = (r >= off) & (r < off + cnt)
    return hit, jnp.where(hit, off + 1.0, 0.0)


def _moe_sparse_kernel(alpha, batch, n_sub, cap, eg,
                       off_s, pc_s,
                       x1_ref, comb_ref, combt_ref, offr_ref, cntr_ref, offc_ref, cntc_ref,
                       wg_ref, wu_ref, wd_ref, wsg_ref, wsu_ref, wsd_ref, g_ref, b_ref,
                       out_ref, gbuf, xe, zbuf, *xq):
    i = pl.program_id(0)
    g = pl.program_id(1)
    n_exp = comb_ref.shape[1]
    tri_r = lax.broadcasted_iota(jnp.int32, (SUB, SUB), 0)
    tri_c = lax.broadcasted_iota(jnp.int32, (SUB, SUB), 1)

    def run(j, e):
        idx = (i * n_sub + j) * n_exp + e
        return off_s[idx], pc_s[idx]

    def used_rows(j):
        o, p = run(j, n_exp - 1)
        return o + p

    def rows_at(start, chunks):
        return pl.ds(pl.multiple_of(start, RUN_ALIGN), chunks * RUN_ALIGN)

    rows16 = functools.partial(rows_at, chunks=1)

    @pl.when((i == 0) & (g == 0))
    def _zero():
        xe[...] = jnp.zeros(xe.shape, xe.dtype)
        for buf in xq:
            buf[...] = jnp.zeros(buf.shape, buf.dtype)

    @pl.when(g == 0)
    def _dispatch():
        before = jnp.where(tri_r < tri_c, 1.0, 0.0).astype(BF16)
        for j in range(n_sub):
            tok = slice(j * SUB, (j + 1) * SUB)
            xj = x1_ref[tok, :].astype(BF16)
            sel = combt_ref[:, tok] != 0.0
            pos = jnp.dot(jnp.where(sel, 1.0, 0.0).astype(BF16), before,
                          preferred_element_type=F32)
            posm = jnp.where(sel, pos, -1.0).astype(BF16)
            off = offr_ref[j]
            cnt = cntr_ref[j]
            used = used_rows(j)

            def sort_rows(rc, j=j, xj=xj, posm=posm, off=off, cnt=cnt):
                r = (lax.broadcasted_iota(jnp.int32, (SEL_CHUNK, 1), 0)
                     + rc * SEL_CHUNK).astype(F32)
                hit, start1 = _expert_onehot(r, off, cnt)
                s = jnp.sum(start1, axis=1, keepdims=True)
                q = jnp.where(s > 0.0, r - (s - 1.0), -2.0)
                rank = jnp.dot(jnp.where(hit, 1.0, 0.0).astype(BF16), posm,
                               preferred_element_type=F32)
                pick = jnp.where(rank == q, 1.0, 0.0).astype(BF16)
                gbuf[j, rc * SEL_CHUNK:(rc + 1) * SEL_CHUNK, :] = jnp.dot(
                    pick, xj, preferred_element_type=F32).astype(BF16)

            last = cap // SEL_CHUNK - 1
            for rc in range(last):
                sort_rows(rc)
            pl.when(last * SEL_CHUNK < used)(functools.partial(sort_rows, last))

            @pl.when(last * SEL_CHUNK >= used)
            def _blank(j=j):
                gbuf[j, last * SEL_CHUNK:, :] = jnp.zeros((SEL_CHUNK, gbuf.shape[2]), BF16)

    def ffn_rows(rs, ee):
        xc = xe[rs, :]
        h = (jax.nn.silu(jnp.dot(xc, wg_ref[ee], preferred_element_type=F32))
             * jnp.dot(xc, wu_ref[ee], preferred_element_type=F32))
        xe[rs, :] = jnp.dot(h.astype(BF16), wd_ref[ee], preferred_element_type=F32).astype(BF16)

    def pack(e, buf, static):
        n = jnp.int32(0)
        for j in range(n_sub):
            o, p = run(j, e)
            if static:
                buf[rows_at(n, PACK_STATIC), :] = gbuf[j, rows_at(o, PACK_STATIC), :]
            else:
                def chunk(c, carry, j=j, o=o, n=n):
                    buf[rows16(n + c * RUN_ALIGN), :] = gbuf[j, rows16(o + c * RUN_ALIGN), :]
                    return carry

                lax.fori_loop(0, lax.div(p, RUN_ALIGN), chunk, 0)
            n = n + p
        return n

    def unpack(e, buf, static, keep_next=True):
        n = jnp.int32(0)
        for j in range(n_sub):
            o, p = run(j, e)
            if static and not keep_next:
                gbuf[j, rows_at(o, PACK_STATIC), :] = buf[rows_at(n, PACK_STATIC), :]
            elif static:
                full = PACK_FULL * RUN_ALIGN
                gbuf[j, rows_at(o, PACK_FULL), :] = buf[rows_at(n, PACK_FULL), :]
                rest = PACK_STATIC - PACK_FULL
                dst = rows_at(o + full, rest)
                row = lax.broadcasted_iota(jnp.int32, (rest * RUN_ALIGN, gbuf.shape[2]), 0)
                gbuf[j, dst, :] = jnp.where(row < p - full, buf[rows_at(n + full, rest), :],
                                            gbuf[j, dst, :])
            else:
                def chunk(c, carry, j=j, o=o, n=n):
                    gbuf[j, rows16(o + c * RUN_ALIGN), :] = buf[rows16(n + c * RUN_ALIGN), :]
                    return carry

                lax.fori_loop(0, lax.div(p, RUN_ALIGN), chunk, 0)
            n = n + p

    usual = jnp.bool_(True)
    for ee in range(eg):
        n = jnp.int32(0)
        for j in range(n_sub):
            p = run(j, g * eg + ee)[1]
            usual = usual & (p >= PACK_FULL * RUN_ALIGN) & (p <= PACK_STATIC * RUN_ALIGN)
            n = n + p
        usual = usual & (n <= FFN_CHUNK)

    @pl.when(usual)
    def _together():
        for ee in range(eg):
            pack(g * eg + ee, xq[ee], True)
        xcs = [xq[ee][0:FFN_CHUNK, :] for ee in range(eg)]
        hs = [(jax.nn.silu(jnp.dot(xc, wg_ref[ee], preferred_element_type=F32))
               * jnp.dot(xc, wu_ref[ee], preferred_element_type=F32)).astype(BF16)
              for ee, xc in enumerate(xcs)]
        outs = [jnp.dot(h, wd_ref[ee], preferred_element_type=F32).astype(BF16)
                for ee, h in enumerate(hs)]
        for ee, o in enumerate(outs):
            xq[ee][0:FFN_CHUNK, :] = o
        for ee in range(eg):
            unpack(g * eg + ee, xq[ee], True, keep_next=ee == eg - 1)

    @pl.when(jnp.logical_not(usual))
    def _one_by_one():
        for ee in range(eg):
            e = g * eg + ee
            n = pack(e, xe, False)

            def ffn(k, carry, ee=ee):
                ffn_rows(pl.ds(pl.multiple_of(k * FFN_CHUNK, RUN_ALIGN), FFN_CHUNK), ee)
                return carry

            lax.fori_loop(0, lax.div(n + (FFN_CHUNK - 1), FFN_CHUNK), ffn, 0)
            unpack(e, xe, False)

    @pl.when(g == pl.num_programs(1) - 1)
    def _combine():
        earlier = jnp.where(tri_c < tri_r, 1.0, 0.0).astype(BF16)
        for j in range(n_sub):
            tok = slice(j * SUB, (j + 1) * SUB)
            x = x1_ref[tok, :]
            xb = x.astype(BF16)
            hs = (jax.nn.silu(jnp.dot(xb, wsg_ref[...], preferred_element_type=F32))
                  * jnp.dot(xb, wsu_ref[...], preferred_element_type=F32))
            y = jnp.dot(hs.astype(BF16), wsd_ref[...], preferred_element_type=F32)
            comb = comb_ref[tok, :]
            sel = comb != 0.0
            pos = jnp.dot(earlier, jnp.where(sel, 1.0, 0.0).astype(BF16),
                          preferred_element_type=F32)
            posm = jnp.where(sel, pos, -1.0).astype(BF16)
            c_bf = comb.astype(BF16)
            off = offc_ref[j]
            cnt = cntc_ref[j]
            used = used_rows(j)

            def gather_rows(rc, j=j, posm=posm, c_bf=c_bf, off=off, cnt=cnt):
                r = (lax.broadcasted_iota(jnp.int32, (1, SEL_CHUNK), 1)
                     + rc * SEL_CHUNK).astype(F32)
                hit, start1 = _expert_onehot(r, off, cnt)
                s = jnp.sum(start1, axis=0, keepdims=True)
                q = jnp.where(s > 0.0, r - (s - 1.0), -2.0)
                hb = jnp.where(hit, 1.0, 0.0).astype(BF16)
                pick = jnp.dot(posm, hb, preferred_element_type=F32) == q
                weight = jnp.dot(c_bf, hb, preferred_element_type=F32)
                w = jnp.where(pick, weight, 0.0).astype(BF16)
                rows = gbuf[j, rc * SEL_CHUNK:(rc + 1) * SEL_CHUNK, :]
                return jnp.dot(w, rows, preferred_element_type=F32)

            last = cap // SEL_CHUNK - 1
            for rc in range(last):
                y = y + gather_rows(rc)
            n_lt = x.shape[1] // LANES
            lane_tiles = lambda v: [v[:, c * LANES:(c + 1) * LANES] for c in range(n_lt)]

            def put(v):
                for c, t in enumerate(lane_tiles(v)):
                    zbuf[c] = t

            get = lambda: jnp.concatenate([zbuf[c] for c in range(n_lt)], axis=1)
            put(alpha * x + y)

            @pl.when(last * SEL_CHUNK < used)
            def _tail(gather_rows=gather_rows, put=put, get=get):
                put(get() + gather_rows(last))

            put(_layer_norm(get(), g_ref[...], b_ref[...]))
            t_sub = SUB // batch
            for b in range(batch):
                for c in range(n_lt):
                    out_ref[b, j * t_sub:(j + 1) * t_sub, c * LANES:(c + 1) * LANES] = (
                        zbuf[c, pl.ds(b, t_sub, stride=batch), :])


def _moe_sparse(x1, comb, combt, cnt, p, w_bf16, *, alpha, batch, n_sub, eg=EXPERTS_PER_STEP):
    total, d = x1.shape
    w_gate, w_up, w_down = w_bf16
    n_exp, _, f = w_gate.shape
    fs = p['ws_gate'].shape[1]
    tile = n_sub * SUB
    n_tiles = total // tile
    assert SUB % batch == 0
    cap = -(-(SUB * TOP_K + n_exp * (RUN_ALIGN - 1)) // SEL_CHUNK) * SEL_CHUNK
    xe_rows = -(-(tile + n_sub * (RUN_ALIGN - 1)) // FFN_CHUNK) * FFN_CHUNK

    cnt = cnt[:, 0, :]
    pc = jnp.ceil(cnt / RUN_ALIGN) * RUN_ALIGN
    off = jnp.cumsum(pc, axis=1) - pc
    off_s = off.astype(jnp.int32).reshape(-1)
    pc_s = pc.astype(jnp.int32).reshape(-1)
    offr, cntr = off[:, None, :], cnt[:, None, :]
    offc, cntc = off[:, :, None], cnt[:, :, None]

    row_spec = pl.BlockSpec((n_sub, 1, n_exp), lambda i, g, *_: (i, 0, 0))
    col_spec = pl.BlockSpec((n_sub, n_exp, 1), lambda i, g, *_: (i, 0, 0))
    const = lambda shape: pl.BlockSpec(shape, lambda i, g, *_: (0,) * len(shape),
                                       pipeline_mode=pl.Buffered(1))
    grid_spec = pltpu.PrefetchScalarGridSpec(
        num_scalar_prefetch=2,
        grid=(n_tiles, n_exp // eg),
        in_specs=[pl.BlockSpec((tile, d), lambda i, g, *_: (i, 0), pipeline_mode=pl.Buffered(1)),
                  pl.BlockSpec((tile, n_exp), lambda i, g, *_: (i, 0)),
                  pl.BlockSpec((n_exp, tile), lambda i, g, *_: (0, i)),
                  row_spec, row_spec, col_spec, col_spec,
                  pl.BlockSpec((eg, d, f), lambda i, g, *_: (g, 0, 0)),
                  pl.BlockSpec((eg, d, f), lambda i, g, *_: (g, 0, 0)),
                  pl.BlockSpec((eg, f, d), lambda i, g, *_: (g, 0, 0)),
                  const((d, fs)), const((d, fs)), const((fs, d)),
                  const((1, d)), const((1, d))],
        out_specs=pl.BlockSpec((batch, tile // batch, d), lambda i, g, *_: (0, i, 0),
                               pipeline_mode=pl.Buffered(1)),
        scratch_shapes=[pltpu.VMEM((n_sub, cap, d), BF16), pltpu.VMEM((xe_rows, d), BF16),
                        pltpu.VMEM((d // LANES, SUB, LANES), F32)]
        + [pltpu.VMEM((FFN_CHUNK + PACK_STATIC * RUN_ALIGN, d), BF16)] * eg)
    return pl.pallas_call(
        functools.partial(_moe_sparse_kernel, alpha, batch, n_sub, cap, eg),
        grid_spec=grid_spec,
        out_shape=jax.ShapeDtypeStruct((batch, total // batch, d), F32),
        compiler_params=pltpu.CompilerParams(
            dimension_semantics=("arbitrary", "arbitrary"), vmem_limit_bytes=VMEM_LIMIT),
        name="moe_sparse",
    )(off_s, pc_s, x1, comb, combt, offr, cntr, offc, cntc, w_gate, w_up, w_down,
      p['ws_gate'].astype(BF16), p['ws_up'].astype(BF16), p['ws_down'].astype(BF16),
      p['ln2_g'], p['ln2_b'])


def _layer_params(l, w):
    g, n = w['a_re'].shape[1:]
    lr, li, bbr, bbi = _ssm_prep(w['a_re'][l], w['a_im'][l], w['log_dt'][l],
                                 w['ssm_b_re'][l], w['ssm_b_im'][l])
    bbd = jnp.concatenate([_block_diag(bbr, SSM_BLOCKS), _block_diag(bbi, SSM_BLOCKS)],
                          axis=-1).astype(BF16)
    c_re = w['ssm_c_re'][l].transpose(0, 2, 1)
    c_im = w['ssm_c_im'][l].transpose(0, 2, 1)
    cbd = jnp.concatenate([_block_diag(c_re, SSM_BLOCKS), _block_diag(-c_im, SSM_BLOCKS)],
                          axis=1).astype(BF16)
    row = lambda v: v.reshape(1, -1)
    return {
        'w_in': w['w_in'][l].astype(BF16), 'b_in': row(w['b_in'][l]),
        'conv_w': w['conv_w'][l], 'w_conv_out': w['w_conv_out'][l].astype(BF16),
        'lam_r': lr.reshape(1, g * n), 'lam_i': li.reshape(1, g * n),
        'bbd': bbd, 'cbd': cbd, 'ssm_d': row(w['ssm_d'][l]),
        'w_glu': w['w_glu'][l].astype(BF16), 'b_glu': row(w['b_glu'][l]),
        'w_ssm_out': w['w_ssm_out'][l].astype(BF16), 'w_o': w['w_o'][l].astype(BF16),
        'ln1_g': row(w['ln1_g'][l]), 'ln1_b': row(w['ln1_b'][l]),
        'w_router_t': w['w_router'][l].T, 'router_bias': w['router_bias'][l].reshape(-1, 1),
        'w_gate': w['w_gate'][l], 'w_up': w['w_up'][l], 'w_down': w['w_down'][l],
        'ws_gate': w['ws_gate'][l], 'ws_up': w['ws_up'][l], 'ws_down': w['ws_down'][l],
        'ln2_g': row(w['ln2_g'][l]), 'ln2_b': row(w['ln2_b'][l]),
    }


def _pick_steps(batch, seq, max_rows):
    steps = max(1, min(seq, max_rows // batch))
    while seq % steps:
        steps -= 1
    return steps


def _pick_tile(total, max_tile):
    tile = min(total, max_tile)
    while total % tile or tile % 16:
        tile -= 16
    return tile


MIXER_ROWS = 512
MOE_TILE = 1024


def _trunk_layer(x, conv_l, re_l, im_l, p, alpha, w_bf16):
    bsz, seq, d = x.shape
    assert bsz % SUBLANES == 0
    steps = _pick_steps(bsz, seq, MIXER_ROWS)
    kw = conv_l.shape[1]
    assert kw == 2
    cprev = conv_l.astype(F32).transpose(1, 0, 2).reshape(kw * bsz, d)
    h0r = re_l.astype(F32).reshape(bsz, -1)
    h0i = im_l.astype(F32).reshape(bsz, -1)
    if ((bsz * steps) % SUB == 0 and (bsz * seq) % (DISPATCH_SUBTILES * SUB) == 0
            and SUB % bsz == 0):
        if w_bf16 is None:
            w_bf16 = tuple(p[k].astype(BF16) for k in ('w_gate', 'w_up', 'w_down'))
        x1, comb, cnew, hr, hi, combt, cnt = _mixer(
            x, cprev, h0r, h0i, p, alpha=alpha, batch=bsz, steps=steps, sub=SUB)
        y = _moe_sparse(x1, comb, combt, cnt, p, w_bf16, alpha=alpha, batch=bsz,
                        n_sub=DISPATCH_SUBTILES)
    else:
        rows = x.transpose(1, 0, 2).reshape(seq * bsz, d)
        x1, comb, cnew, hr, hi = _mixer(rows, cprev, h0r, h0i, p,
                                        alpha=alpha, batch=bsz, steps=steps)
        out, w_bf16 = _moe(x1, comb, p, alpha=alpha, tile=_pick_tile(bsz * seq, MOE_TILE))
        y = out.reshape(seq, bsz, d).transpose(1, 0, 2)
    states = (cnew.reshape(kw, bsz, d).transpose(1, 0, 2),
              hr.reshape(re_l.shape), hi.reshape(im_l.shape))
    return y, states, w_bf16


def kernel(x_prompt, x_sample, state_conv, state_ssm_re, state_ssm_im,
           w_in, b_in, conv_w, w_conv_out, a_re, a_im, log_dt,
           ssm_b_re, ssm_b_im, ssm_c_re, ssm_c_im, ssm_d, w_glu, b_glu, w_ssm_out, w_o,
           ln1_g, ln1_b, w_router, router_bias, w_gate, w_up, w_down,
           ws_gate, ws_up, ws_down, ln2_g, ln2_b):
    w = dict(w_in=w_in, b_in=b_in, conv_w=conv_w, w_conv_out=w_conv_out,
             a_re=a_re, a_im=a_im, log_dt=log_dt,
             ssm_b_re=ssm_b_re, ssm_b_im=ssm_b_im, ssm_c_re=ssm_c_re, ssm_c_im=ssm_c_im,
             ssm_d=ssm_d, w_glu=w_glu, b_glu=b_glu, w_ssm_out=w_ssm_out, w_o=w_o,
             ln1_g=ln1_g, ln1_b=ln1_b, w_router=w_router, router_bias=router_bias,
             w_gate=w_gate, w_up=w_up, w_down=w_down,
             ws_gate=ws_gate, ws_up=ws_up, ws_down=ws_down, ln2_g=ln2_g, ln2_b=ln2_b)
    depth = w_in.shape[0]
    alpha = (2.0 * depth) ** 0.25
    bsz = x_prompt.shape[0]
    zero_conv = jnp.zeros((bsz,) + state_conv.shape[2:], x_prompt.dtype)
    zero_ssm = jnp.zeros((bsz,) + state_ssm_re.shape[2:], F32)
    y_p, y_s = x_prompt, x_sample
    st_p, st_s = [], []
    for l in range(depth):
        p = _layer_params(l, w)
        y_s, st, w_bf16 = _trunk_layer(y_s, state_conv[l], state_ssm_re[l], state_ssm_im[l],
                                       p, alpha, None)
        st_s.append(st)
        y_p, st, _ = _trunk_layer(y_p, zero_conv, zero_ssm, zero_ssm, p, alpha, w_bf16)
        st_p.append(st)
    conv_p, re_p, im_p = (jnp.stack(v) for v in zip(*st_p))
    conv_s, re_s, im_s = (jnp.stack(v) for v in zip(*st_s))
    return (y_p, y_s, conv_p, re_p, im_p, conv_s, re_s, im_s)
```

```python
import functools
import math

import jax
import jax.numpy as jnp
from jax import lax
from jax.experimental import pallas as pl
from jax.experimental.pallas import tpu as pltpu

F32 = jnp.float32
BF16 = jnp.bfloat16

LN_EPS = 1e-5
ROUTED_SCALE = 2.5
N_ROUTE_GROUPS = 8
TOPK_GROUPS = 4
TOP_K = 8

SUBLANES = 8
LANES = 128
SSM_BLOCKS = 4
SCAN_UNROLL = 4
SCAN_SPLIT = 1
POST_ROWS = 512
DENSE_EXPERTS_PER_STEP = 2
VMEM_LIMIT = 60 * 1024 * 1024


def _const_spec(shape):
    nd = len(shape)
    return pl.BlockSpec(shape, lambda *_: (0,) * nd, pipeline_mode=pl.Buffered(1))


def _ssm_prep_kernel(are_ref, aim_ref, ldt_ref, br_ref, bi_ref,
                     lr_ref, li_ref, bbr_ref, bbi_ref):
    dt = jnp.exp(ldt_ref[...])
    ar = are_ref[...]
    ai = aim_ref[...]
    mag = jnp.exp(ar * dt)
    lr = mag * jnp.cos(ai * dt)
    li = mag * jnp.sin(ai * dt)
    den = ar * ar + ai * ai
    fr = ((lr - 1.0) * ar + li * ai) / den
    fi = (li * ar - (lr - 1.0) * ai) / den
    lr_ref[...] = lr
    li_ref[...] = li
    br = br_ref[...]
    bi = bi_ref[...]
    bbr_ref[...] = fr * br - fi * bi
    bbi_ref[...] = fr * bi + fi * br


def _ssm_prep(a_re, a_im, log_dt, b_re, b_im):
    g, n = a_re.shape
    h = b_re.shape[-1]
    vec = jax.ShapeDtypeStruct((g, 1, n), F32)
    mat = jax.ShapeDtypeStruct((g, h, n), F32)
    return pl.pallas_call(
        _ssm_prep_kernel,
        out_shape=(vec, vec, mat, mat),
        name="ssm_prep",
    )(a_re.reshape(g, 1, n), a_im.reshape(g, 1, n), log_dt.reshape(g, 1, 1),
      b_re.transpose(0, 2, 1), b_im.transpose(0, 2, 1))


def _block_diag(m, nblk):
    g, p, q = m.shape
    gl = g // nblk
    tiled = jnp.tile(m.reshape(nblk, gl * p, q), (1, 1, gl))
    row = lax.broadcasted_iota(jnp.int32, (gl * p, gl * q), 0) // p
    col = lax.broadcasted_iota(jnp.int32, (gl * p, gl * q), 1) // q
    return jnp.where(row == col, tiled, 0.0)


def _layer_norm(r, g, b):
    mu = jnp.mean(r, axis=-1, keepdims=True)
    d = r - mu
    var = jnp.mean(d * d, axis=-1, keepdims=True)
    return d * lax.rsqrt(var + LN_EPS) * g + b


def _split_bf16(v):
    hi = v.astype(BF16)
    return hi, (v - hi.astype(F32)).astype(BF16)


def _route(scores, biased):
    n_exp, r = scores.shape
    gsz = n_exp // N_ROUTE_GROUPS
    neg = jnp.float32(-jnp.inf)
    rows = []
    for g in range(N_ROUTE_GROUPS):
        v = biased[g * gsz:(g + 1) * gsz, :]
        m1 = jnp.max(v, axis=0, keepdims=True)
        is_max = v == m1
        n_max = jnp.sum(is_max.astype(F32), axis=0, keepdims=True)
        rest = jnp.max(jnp.where(is_max, neg, v), axis=0, keepdims=True)
        rows.append(m1 + jnp.where(n_max >= 2.0, m1, rest))
    gscore = jnp.concatenate(rows, axis=0)
    gidx = lax.broadcasted_iota(jnp.int32, gscore.shape, 0)
    grank = jnp.zeros(gscore.shape, F32)
    for g in range(N_ROUTE_GROUPS):
        sg = gscore[g:g + 1, :]
        beats = (sg > gscore) | ((sg == gscore) & (gidx > g))
        grank = grank + beats.astype(F32)
    gkeep = grank < float(TOPK_GROUPS)
    masked = jnp.concatenate(
        [jnp.where(gkeep[g:g + 1, :], biased[g * gsz:(g + 1) * gsz, :], neg)
         for g in range(N_ROUTE_GROUPS)], axis=0)
    eidx = lax.broadcasted_iota(jnp.int32, masked.shape, 0).astype(F32)
    left = masked
    for _ in range(TOP_K):
        top = jnp.max(left, axis=0, keepdims=True)
        first = jnp.min(jnp.where(left == top, eidx, float(n_exp)), axis=0, keepdims=True)
        left = jnp.where(eidx == first, neg, left)
    w = jnp.where(left != masked, scores, 0.0)
    return w / jnp.sum(w, axis=0, keepdims=True) * ROUTED_SCALE


def _mixer_kernel(alpha, batch, steps, sub,
                  x_ref, cprev_ref, h0r_ref, h0i_ref,
                  win_ref, bin_ref, convw_ref, wco_ref,
                  lamr_ref, lami_ref, bbd_ref, cbd_ref, dskip_ref,
                  wglu_ref, bglu_ref, wso_ref, wo_ref, ln1g_ref, ln1b_ref,
                  wrt_ref, rbias_ref,
                  x1_ref, comb_ref, cnew_ref, hr_ref, hi_ref, *rest):
    if sub:
        combt_ref, cnt_ref, *rest = rest
    ubuf, xk_ref, ys_ref, *rest = rest
    rows = batch * steps
    d = x_ref.shape[-1]
    d_blk = d // SSM_BLOCKS
    n_blk = lamr_ref.shape[1] // SSM_BLOCKS
    i = pl.program_id(0)

    @pl.when(i == 0)
    def _init():
        ubuf[0:2 * batch, :] = cprev_ref[...]
        hr_ref[...] = h0r_ref[...]
        hi_ref[...] = h0i_ref[...]

    if len(x_ref.shape) == 3:
        xs_ref, = rest
        for b in range(batch):
            for c in range(d // LANES):
                xs_ref[c, pl.ds(b, steps, stride=batch), :] = x_ref[b, :, c * LANES:(c + 1) * LANES]
        x = jnp.concatenate([xs_ref[c] for c in range(d // LANES)], axis=1)
    else:
        x = x_ref[...]
    xb = x.astype(BF16)

    def proj(c):
        cols = slice(c * d, (c + 1) * d)
        return (jnp.dot(xb, win_ref[:, cols], preferred_element_type=F32)
                + bin_ref[:, cols])

    u = proj(1) * proj(2)
    ubuf[2 * batch:2 * batch + rows, :] = u
    conv = (convw_ref[0:1, :] * ubuf[0:rows, :]
            + convw_ref[1:2, :] * ubuf[batch:batch + rows, :]
            + convw_ref[2:3, :] * u)
    ya = jnp.dot((proj(0) * conv).astype(BF16), wco_ref[...],
                 preferred_element_type=F32)
    tail = ubuf[rows:rows + 2 * batch, :]
    ubuf[0:2 * batch, :] = tail
    cnew_ref[...] = tail
    ya_rows = slice(2 * batch, 2 * batch + rows)
    ubuf[ya_rows, :] = ya

    us = proj(3)
    half = n_blk // SCAN_SPLIT
    unroll = SCAN_UNROLL if steps % SCAN_UNROLL == 0 else 1
    for k in range(SSM_BLOCKS):
        usk = us[:, k * d_blk:(k + 1) * d_blk].astype(BF16)
        xk_ref[...] = jnp.dot(usk, bbd_ref[k], preferred_element_type=F32)
        for hf in range(SCAN_SPLIT):
            st = slice(k * n_blk + hf * half, k * n_blk + (hf + 1) * half)
            re = slice(hf * half, (hf + 1) * half)
            im = slice(n_blk + hf * half, n_blk + (hf + 1) * half)
            if steps == 1:
                lr = lamr_ref[:, st]
                li = lami_ref[:, st]
                hr = hr_ref[:, st]
                hi = hi_ref[:, st]
                nhr = lr * hr - li * hi + xk_ref[:, re]
                nhi = lr * hi + li * hr + xk_ref[:, im]
                xk_ref[:, re] = nhr
                xk_ref[:, im] = nhi
                hr_ref[:, st] = nhr
                hi_ref[:, st] = nhi
            else:
                lr = jnp.broadcast_to(lamr_ref[:, st], (SUBLANES, half))
                li = jnp.broadcast_to(lami_ref[:, st], (SUBLANES, half))
                for s in range(batch // SUBLANES):
                    grp = slice(s * SUBLANES, (s + 1) * SUBLANES)

                    def step(tt, carry, s=s, re=re, im=im, lr=lr, li=li):
                        hr, hi = carry
                        for k_un in range(unroll):
                            row = pl.multiple_of((tt * unroll + k_un) * batch + s * SUBLANES,
                                                 SUBLANES)
                            rs = pl.ds(row, SUBLANES)
                            hr, hi = (lr * hr - li * hi + xk_ref[rs, re],
                                      lr * hi + li * hr + xk_ref[rs, im])
                            xk_ref[rs, re] = hr
                            xk_ref[rs, im] = hi
                        return hr, hi

                    hr, hi = lax.fori_loop(0, steps // unroll, step,
                                           (hr_ref[grp, st], hi_ref[grp, st]))
                    hr_ref[grp, st] = hr
                    hi_ref[grp, st] = hi
        ys_ref[:, k * d_blk:(k + 1) * d_blk] = jnp.dot(
            xk_ref[...].astype(BF16), cbd_ref[k], preferred_element_type=F32)
    ys_ref[...] = ys_ref[...] + dskip_ref[...] * us

    chunk = min(rows, POST_ROWS)
    for q in range(rows // chunk):
        rq = slice(q * chunk, (q + 1) * chunk)
        if len(x_ref.shape) == 3:
            xq = jnp.concatenate([xs_ref[c, rq, :] for c in range(d // LANES)], axis=1)
        else:
            xq = x_ref[rq, :]
        xbq = xq.astype(BF16)

        def projq(c, xbq=xbq):
            cols = slice(c * d, (c + 1) * d)
            return (jnp.dot(xbq, win_ref[:, cols], preferred_element_type=F32)
                    + bin_ref[:, cols])

        z = jax.nn.gelu(ys_ref[rq, :])
        gate = (jnp.dot(z.astype(BF16), wglu_ref[...], preferred_element_type=F32)
                + bglu_ref[...])
        glu = z * jax.nn.sigmoid(gate)
        yb = jnp.dot(glu.astype(BF16), wso_ref[...], preferred_element_type=F32)

        m = (jax.nn.sigmoid(projq(4)) * ubuf[2 * batch + q * chunk:2 * batch + (q + 1) * chunk, :]
             + jax.nn.sigmoid(projq(5)) * yb)
        o = jnp.dot(m.astype(BF16), wo_ref[...], preferred_element_type=F32)
        x1 = _layer_norm(alpha * xq + o, ln1g_ref[...], ln1b_ref[...])
        x1_ref[rq, :] = x1

        nt = (((1,), (1,)), ((), ()))
        w_hi, w_lo = _split_bf16(wrt_ref[...])
        x_hi, x_lo = _split_bf16(x1)
        logits = (lax.dot_general(w_hi, x_hi, nt, preferred_element_type=F32)
                  + lax.dot_general(w_hi, x_lo, nt, preferred_element_type=F32)
                  + lax.dot_general(w_lo, x_hi, nt, preferred_element_type=F32))
        scores = jax.nn.sigmoid(logits)
        comb = _route(scores, scores + rbias_ref[...])
        comb_ref[rq, :] = comb.T
        if sub:
            combt_ref[:, rq] = comb
    if sub:
        ones = jnp.ones((SUBLANES, sub), BF16)
        for s in range(rows // sub):
            sel = jnp.where(combt_ref[:, s * sub:(s + 1) * sub] != 0.0, 1.0, 0.0).astype(BF16)
            cnt_ref[s] = lax.dot_general(ones, sel, (((1,), (1,)), ((), ())),
                                         preferred_element_type=F32)


def _mixer(x, cprev, h0r, h0i, p, *, alpha, batch, steps, sub=0):
    d = x.shape[-1]
    total = x.size // d
    rows = batch * steps
    assert not sub or rows % sub == 0
    n_state = h0r.shape[1]
    n_exp = p['w_router_t'].shape[0]
    grid = (total // rows,)
    if x.ndim == 3:
        x_spec = pl.BlockSpec((batch, steps, d), lambda i: (0, i, 0))
    else:
        x_spec = pl.BlockSpec((rows, d), lambda i: (i, 0))
    consts = [p['w_in'], p['b_in'], p['conv_w'], p['w_conv_out'],
              p['lam_r'], p['lam_i'], p['bbd'], p['cbd'], p['ssm_d'],
              p['w_glu'], p['b_glu'], p['w_ssm_out'], p['w_o'], p['ln1_g'], p['ln1_b'],
              p['w_router_t'], p['router_bias']]
    in_specs = ([x_spec,
                 _const_spec(cprev.shape), _const_spec(h0r.shape), _const_spec(h0i.shape)]
                + [_const_spec(c.shape) for c in consts])
    out_shape = (jax.ShapeDtypeStruct((total, d), F32),
                 jax.ShapeDtypeStruct((total, n_exp), F32),
                 jax.ShapeDtypeStruct((2 * batch, d), F32),
                 jax.ShapeDtypeStruct((batch, n_state), F32),
                 jax.ShapeDtypeStruct((batch, n_state), F32))
    out_specs = (pl.BlockSpec((rows, d), lambda i: (i, 0)),
                 pl.BlockSpec((rows, n_exp), lambda i: (i, 0)),
                 pl.BlockSpec((2 * batch, d), lambda i: (0, 0)),
                 pl.BlockSpec((batch, n_state), lambda i: (0, 0)),
                 pl.BlockSpec((batch, n_state), lambda i: (0, 0)))
    if sub:
        out_shape += (jax.ShapeDtypeStruct((n_exp, total), F32),
                      jax.ShapeDtypeStruct((total // sub, SUBLANES, n_exp), F32))
        out_specs += (pl.BlockSpec((n_exp, rows), lambda i: (0, i)),
                      pl.BlockSpec((rows // sub, SUBLANES, n_exp), lambda i: (i, 0, 0)))
    scratch = [pltpu.VMEM((rows + 2 * batch, d), F32),
               pltpu.VMEM((rows, 2 * n_state // SSM_BLOCKS), F32),
               pltpu.VMEM((rows, d), F32)]
    if x.ndim == 3:
        scratch.append(pltpu.VMEM((d // LANES, rows, LANES), F32))
    return pl.pallas_call(
        functools.partial(_mixer_kernel, alpha, batch, steps, sub),
        grid=grid, in_specs=in_specs, out_specs=out_specs, out_shape=out_shape,
        scratch_shapes=scratch,
        compiler_params=pltpu.CompilerParams(
            dimension_semantics=("arbitrary",), vmem_limit_bytes=VMEM_LIMIT),
        name="mixer",
    )(x, cprev, h0r, h0i, *consts)


def _swiglu(xb, wg, wu):
    g = jnp.dot(xb, wg.astype(BF16), preferred_element_type=F32)
    u = jnp.dot(xb, wu.astype(BF16), preferred_element_type=F32)
    return jax.nn.silu(g) * u


def _moe_kernel(alpha, emit_bf16, x1_ref, comb_ref, wg_ref, wu_ref, wd_ref,
                wsg_ref, wsu_ref, wsd_ref, g_ref, b_ref, out_ref, *rest):
    if emit_bf16:
        wgb_ref, wub_ref, wdb_ref, xb_ref, acc_ref = rest
    else:
        xb_ref, acc_ref = rest
    e = pl.program_id(1)

    @pl.when(e == 0)
    def _shared():
        xb = x1_ref[...].astype(BF16)
        xb_ref[...] = xb
        hs = _swiglu(xb, wsg_ref[...], wsu_ref[...])
        acc_ref[...] = jnp.dot(hs.astype(BF16), wsd_ref[...].astype(BF16),
                               preferred_element_type=F32)

    comb = comb_ref[...]
    lane = lax.broadcasted_iota(jnp.int32, comb.shape, 1)
    xb = xb_ref[...]
    acc = acc_ref[...]
    for k in range(wg_ref.shape[0]):
        wg = wg_ref[k].astype(BF16)
        wu = wu_ref[k].astype(BF16)
        wd = wd_ref[k].astype(BF16)
        if emit_bf16:
            wgb_ref[k] = wg
            wub_ref[k] = wu
            wdb_ref[k] = wd
        c = jnp.sum(jnp.where(lane == e * wg_ref.shape[0] + k, comb, 0.0), axis=1, keepdims=True)
        h = _swiglu(xb, wg, wu) * c
        acc = acc + jnp.dot(h.astype(BF16), wd, preferred_element_type=F32)
    acc_ref[...] = acc

    @pl.when(e == pl.num_programs(1) - 1)
    def _finish():
        out_ref[...] = _layer_norm(alpha * x1_ref[...] + acc_ref[...],
                                   g_ref[...], b_ref[...])


def _moe(x1, comb, p, *, alpha, tile):
    total, d = x1.shape
    n_exp, _, f = p['w_gate'].shape
    fs = p['ws_gate'].shape[1]
    eg = DENSE_EXPERTS_PER_STEP
    grid = (total // tile, n_exp // eg)
    emit_bf16 = grid[0] == 1
    up_spec = pl.BlockSpec((eg, d, f), lambda i, e: (e, 0, 0))
    down_spec = pl.BlockSpec((eg, f, d), lambda i, e: (e, 0, 0))
    in_specs = [pl.BlockSpec((tile, d), lambda i, e: (i, 0)),
                pl.BlockSpec((tile, n_exp), lambda i, e: (i, 0)),
                up_spec, up_spec, down_spec,
                _const_spec((d, fs)), _const_spec((d, fs)), _const_spec((fs, d)),
                _const_spec((1, d)), _const_spec((1, d))]
    out_specs = [pl.BlockSpec((tile, d), lambda i, e: (i, 0))]
    out_shape = [jax.ShapeDtypeStruct((total, d), F32)]
    if emit_bf16:
        out_specs += [up_spec, up_spec, down_spec]
        out_shape += [jax.ShapeDtypeStruct(p[k].shape, BF16)
                      for k in ('w_gate', 'w_up', 'w_down')]
    res = pl.pallas_call(
        functools.partial(_moe_kernel, alpha, emit_bf16),
        grid=grid, in_specs=in_specs, out_specs=out_specs, out_shape=out_shape,
        scratch_shapes=[pltpu.VMEM((tile, d), BF16), pltpu.VMEM((tile, d), F32)],
        compiler_params=pltpu.CompilerParams(
            dimension_semantics=("arbitrary", "arbitrary"), vmem_limit_bytes=VMEM_LIMIT),
        name="moe",
    )(x1, comb, p['w_gate'], p['w_up'], p['w_down'],
      p['ws_gate'], p['ws_up'], p['ws_down'], p['ln2_g'], p['ln2_b'])
    if emit_bf16:
        return res[0], tuple(res[1:])
    return res[0], tuple(p[k].astype(BF16) for k in ('w_gate', 'w_up', 'w_down'))


SUB = 256
RUN_ALIGN = 16
FFN_CHUNK = 192
SEL_CHUNK = 512
PACK_STATIC = 4
PACK_FULL = 2
DISPATCH_SUBTILES = 4
EXPERTS_PER_STEP = 4
BUILD_AHEAD = 2


def _expert_onehot(r, off, cnt):
    hit = (r >= off) & (r < off + cnt)
    return hit, jnp.where(hit, off + 1.0, 0.0)


def _moe_sparse_kernel(alpha, batch, n_sub, cap, eg,
                       off_s, pc_s,
                       x1_ref, comb_ref, combt_ref, offr_ref, cntr_ref, offc_ref, cntc_ref,
                       wg_ref, wu_ref, wd_ref, wsg_ref, wsu_ref, wsd_ref, g_ref, b_ref,
                       out_ref, gbuf, xe, zbuf, built, *xq):
    i = pl.program_id(0)
    g = pl.program_id(1)
    n_exp = comb_ref.shape[1]
    tri_r = lax.broadcasted_iota(jnp.int32, (SUB, SUB), 0)
    tri_c = lax.broadcasted_iota(jnp.int32, (SUB, SUB), 1)

    def run(j, e):
        idx = (i * n_sub + j) * n_exp + e
        return off_s[idx], pc_s[idx]

    def used_rows(j):
        o, p = run(j, n_exp - 1)
        return o + p

    def rows_at(start, chunks):
        return pl.ds(pl.multiple_of(start, RUN_ALIGN), chunks * RUN_ALIGN)

    rows16 = functools.partial(rows_at, chunks=1)

    @pl.when((i == 0) & (g == 0))
    def _zero():
        xe[...] = jnp.zeros(xe.shape, xe.dtype)
        for buf in xq:
            buf[...] = jnp.zeros(buf.shape, buf.dtype)

    n_chunks = cap // SEL_CHUNK

    @pl.when(g == 0)
    def _start():
        built[0] = 0

    def build(rc, j):
        before = jnp.where(tri_r < tri_c, 1.0, 0.0).astype(BF16)
        tok = slice(j * SUB, (j + 1) * SUB)
        xj = x1_ref[tok, :].astype(BF16)
        sel = combt_ref[:, tok] != 0.0
        pos = jnp.dot(jnp.where(sel, 1.0, 0.0).astype(BF16), before,
                      preferred_element_type=F32)
        posm = jnp.where(sel, pos, -1.0).astype(BF16)
        off = offr_ref[j]
        cnt = cntr_ref[j]
        r = (lax.broadcasted_iota(jnp.int32, (SEL_CHUNK, 1), 0)
             + rc * SEL_CHUNK).astype(F32)
        hit, start1 = _expert_onehot(r, off, cnt)
        s = jnp.sum(start1, axis=1, keepdims=True)
        q = jnp.where(s > 0.0, r - (s - 1.0), -2.0)
        rank = jnp.dot(jnp.where(hit, 1.0, 0.0).astype(BF16), posm,
                       preferred_element_type=F32)
        pick = jnp.where(rank == q, 1.0, 0.0).astype(BF16)
        gbuf[j, rc * SEL_CHUNK:(rc + 1) * SEL_CHUNK, :] = jnp.dot(
            pick, xj, preferred_element_type=F32).astype(BF16)

    def chunks_for(rows):
        return jnp.minimum(lax.div(rows + (SEL_CHUNK - 1), SEL_CHUNK), n_chunks)

    reach = jnp.int32(0)
    final = jnp.int32((n_chunks - 1) * SEL_CHUNK)
    for j in range(n_sub):
        o, p = run(j, g * eg + eg - 1)
        reach = jnp.maximum(reach, o + p)
        final = jnp.maximum(final, used_rows(j))
    slack = PACK_STATIC * RUN_ALIGN
    reach = jnp.where(g == pl.num_programs(1) - 1, final, reach)
    done = built[0]
    target = jnp.maximum(chunks_for(reach + slack) * n_sub,
                         jnp.minimum(done + BUILD_AHEAD, chunks_for(final + slack) * n_sub))
    for u in range(n_chunks * n_sub):
        pl.when((done <= u) & (u < target))(functools.partial(build, u // n_sub, u % n_sub))
    built[0] = jnp.maximum(done, target)

    def ffn_rows(rs, ee):
        xc = xe[rs, :]
        h = (jax.nn.silu(jnp.dot(xc, wg_ref[ee], preferred_element_type=F32))
             * jnp.dot(xc, wu_ref[ee], preferred_element_type=F32))
        xe[rs, :] = jnp.dot(h.astype(BF16), wd_ref[ee], preferred_element_type=F32).astype(BF16)

    def pack(e, buf, static):
        n = jnp.int32(0)
        for j in range(n_sub):
            o, p = run(j, e)
            if static:
                buf[rows_at(n, PACK_STATIC), :] = gbuf[j, rows_at(o, PACK_STATIC), :]
            else:
                def chunk(c, carry, j=j, o=o, n=n):
                    buf[rows16(n + c * RUN_ALIGN), :] = gbuf[j, rows16(o + c * RUN_ALIGN), :]
                    return carry

                lax.fori_loop(0, lax.div(p, RUN_ALIGN), chunk, 0)
            n = n + p
        return n

    def unpack(e, buf, static, keep_next=True):
        n = jnp.int32(0)
        for j in range(n_sub):
            o, p = run(j, e)
            if static and not keep_next:
                gbuf[j, rows_at(o, PACK_STATIC), :] = buf[rows_at(n, PACK_STATIC), :]
            elif static:
                full = PACK_FULL * RUN_ALIGN
                gbuf[j, rows_at(o, PACK_FULL), :] = buf[rows_at(n, PACK_FULL), :]
                rest = PACK_STATIC - PACK_FULL
                dst = rows_at(o + full, rest)
                row = lax.broadcasted_iota(jnp.int32, (rest * RUN_ALIGN, gbuf.shape[2]), 0)
                gbuf[j, dst, :] = jnp.where(row < p - full, buf[rows_at(n + full, rest), :],
                                            gbuf[j, dst, :])
            else:
                def chunk(c, carry, j=j, o=o, n=n):
                    gbuf[j, rows16(o + c * RUN_ALIGN), :] = buf[rows16(n + c * RUN_ALIGN), :]
                    return carry

                lax.fori_loop(0, lax.div(p, RUN_ALIGN), chunk, 0)
            n = n + p

    usual = jnp.bool_(True)
    for ee in range(eg):
        n = jnp.int32(0)
        for j in range(n_sub):
            p = run(j, g * eg + ee)[1]
            usual = usual & (p >= PACK_FULL * RUN_ALIGN) & (p <= PACK_STATIC * RUN_ALIGN)
            n = n + p
        usual = usual & (n <= FFN_CHUNK)

    @pl.when(usual)
    def _together():
        for ee in range(eg):
            pack(g * eg + ee, xq[ee], True)
        xcs = [xq[ee][0:FFN_CHUNK, :] for ee in range(eg)]
        hs = [(jax.nn.silu(jnp.dot(xc, wg_ref[ee], preferred_element_type=F32))
               * jnp.dot(xc, wu_ref[ee], preferred_element_type=F32)).astype(BF16)
              for ee, xc in enumerate(xcs)]
        outs = [jnp.dot(h, wd_ref[ee], preferred_element_type=F32).astype(BF16)
                for ee, h in enumerate(hs)]
        for ee, o in enumerate(outs):
            xq[ee][0:FFN_CHUNK, :] = o
        for ee in range(eg):
            unpack(g * eg + ee, xq[ee], True, keep_next=ee == eg - 1)

    @pl.when(jnp.logical_not(usual))
    def _one_by_one():
        for ee in range(eg):
            e = g * eg + ee
            n = pack(e, xe, False)

            def ffn(k, carry, ee=ee):
                ffn_rows(pl.ds(pl.multiple_of(k * FFN_CHUNK, RUN_ALIGN), FFN_CHUNK), ee)
                return carry

            lax.fori_loop(0, lax.div(n + (FFN_CHUNK - 1), FFN_CHUNK), ffn, 0)
            unpack(e, xe, False)

    @pl.when(g == pl.num_programs(1) - 1)
    def _combine():
        earlier = jnp.where(tri_c < tri_r, 1.0, 0.0).astype(BF16)
        for j in range(n_sub):
            tok = slice(j * SUB, (j + 1) * SUB)
            x = x1_ref[tok, :]
            xb = x.astype(BF16)
            hs = (jax.nn.silu(jnp.dot(xb, wsg_ref[...], preferred_element_type=F32))
                  * jnp.dot(xb, wsu_ref[...], preferred_element_type=F32))
            y = jnp.dot(hs.astype(BF16), wsd_ref[...], preferred_element_type=F32)
            comb = comb_ref[tok, :]
            sel = comb != 0.0
            pos = jnp.dot(earlier, jnp.where(sel, 1.0, 0.0).astype(BF16),
                          preferred_element_type=F32)
            posm = jnp.where(sel, pos, -1.0).astype(BF16)
            c_bf = comb.astype(BF16)
            off = offc_ref[j]
            cnt = cntc_ref[j]
            used = used_rows(j)

            def gather_rows(rc, j=j, posm=posm, c_bf=c_bf, off=off, cnt=cnt):
                r = (lax.broadcasted_iota(jnp.int32, (1, SEL_CHUNK), 1)
                     + rc * SEL_CHUNK).astype(F32)
                hit, start1 = _expert_onehot(r, off, cnt)
                s = jnp.sum(start1, axis=0, keepdims=True)
                q = jnp.where(s > 0.0, r - (s - 1.0), -2.0)
                hb = jnp.where(hit, 1.0, 0.0).astype(BF16)
                pick = jnp.dot(posm, hb, preferred_element_type=F32) == q
                weight = jnp.dot(c_bf, hb, preferred_element_type=F32)
                w = jnp.where(pick, weight, 0.0).astype(BF16)
                rows = gbuf[j, rc * SEL_CHUNK:(rc + 1) * SEL_CHUNK, :]
                return jnp.dot(w, rows, preferred_element_type=F32)

            last = cap // SEL_CHUNK - 1
            for rc in range(last):
                y = y + gather_rows(rc)
            n_lt = x.shape[1] // LANES
            lane_tiles = lambda v: [v[:, c * LANES:(c + 1) * LANES] for c in range(n_lt)]

            def put(v):
                for c, t in enumerate(lane_tiles(v)):
                    zbuf[c] = t

            get = lambda: jnp.concatenate([zbuf[c] for c in range(n_lt)], axis=1)
            put(alpha * x + y)

            @pl.when(last * SEL_CHUNK < used)
            def _tail(gather_rows=gather_rows, put=put, get=get):
                put(get() + gather_rows(last))

            put(_layer_norm(get(), g_ref[...], b_ref[...]))
            t_sub = SUB // batch
            for b in range(batch):
                for c in range(n_lt):
                    out_ref[b, j * t_sub:(j + 1) * t_sub, c * LANES:(c + 1) * LANES] = (
                        zbuf[c, pl.ds(b, t_sub, stride=batch), :])


def _moe_sparse(x1, comb, combt, cnt, p, w_bf16, *, alpha, batch, n_sub, eg=EXPERTS_PER_STEP):
    total, d = x1.shape
    w_gate, w_up, w_down = w_bf16
    n_exp, _, f = w_gate.shape
    fs = p['ws_gate'].shape[1]
    tile = n_sub * SUB
    n_tiles = total // tile
    assert SUB % batch == 0
    cap = -(-(SUB * TOP_K + n_exp * (RUN_ALIGN - 1)) // SEL_CHUNK) * SEL_CHUNK
    xe_rows = -(-(tile + n_sub * (RUN_ALIGN - 1)) // FFN_CHUNK) * FFN_CHUNK

    cnt = cnt[:, 0, :]
    pc = jnp.ceil(cnt / RUN_ALIGN) * RUN_ALIGN
    off = jnp.cumsum(pc, axis=1) - pc
    off_s = off.astype(jnp.int32).reshape(-1)
    pc_s = pc.astype(jnp.int32).reshape(-1)
    offr, cntr = off[:, None, :], cnt[:, None, :]
    offc, cntc = off[:, :, None], cnt[:, :, None]

    row_spec = pl.BlockSpec((n_sub, 1, n_exp), lambda i, g, *_: (i, 0, 0))
    col_spec = pl.BlockSpec((n_sub, n_exp, 1), lambda i, g, *_: (i, 0, 0))
    const = lambda shape: pl.BlockSpec(shape, lambda i, g, *_: (0,) * len(shape),
                                       pipeline_mode=pl.Buffered(1))
    grid_spec = pltpu.PrefetchScalarGridSpec(
        num_scalar_prefetch=2,
        grid=(n_tiles, n_exp // eg),
        in_specs=[pl.BlockSpec((tile, d), lambda i, g, *_: (i, 0), pipeline_mode=pl.Buffered(1)),
                  pl.BlockSpec((tile, n_exp), lambda i, g, *_: (i, 0)),
                  pl.BlockSpec((n_exp, tile), lambda i, g, *_: (0, i)),
                  row_spec, row_spec, col_spec, col_spec,
                  pl.BlockSpec((eg, d, f), lambda i, g, *_: (g, 0, 0)),
                  pl.BlockSpec((eg, d, f), lambda i, g, *_: (g, 0, 0)),
                  pl.BlockSpec((eg, f, d), lambda i, g, *_: (g, 0, 0)),
                  const((d, fs)), const((d, fs)), const((fs, d)),
                  const((1, d)), const((1, d))],
        out_specs=pl.BlockSpec((batch, tile // batch, d), lambda i, g, *_: (0, i, 0),
                               pipeline_mode=pl.Buffered(1)),
        scratch_shapes=[pltpu.VMEM((n_sub, cap, d), BF16), pltpu.VMEM((xe_rows, d), BF16),
                        pltpu.VMEM((d // LANES, SUB, LANES), F32),
                        pltpu.SMEM((1,), jnp.int32)]
        + [pltpu.VMEM((FFN_CHUNK + PACK_STATIC * RUN_ALIGN, d), BF16)] * eg)
    return pl.pallas_call(
        functools.partial(_moe_sparse_kernel, alpha, batch, n_sub, cap, eg),
        grid_spec=grid_spec,
        out_shape=jax.ShapeDtypeStruct((batch, total // batch, d), F32),
        compiler_params=pltpu.CompilerParams(
            dimension_semantics=("arbitrary", "arbitrary"), vmem_limit_bytes=VMEM_LIMIT),
        name="moe_sparse",
    )(off_s, pc_s, x1, comb, combt, offr, cntr, offc, cntc, w_gate, w_up, w_down,
      p['ws_gate'].astype(BF16), p['ws_up'].astype(BF16), p['ws_down'].astype(BF16),
      p['ln2_g'], p['ln2_b'])


def _layer_params(l, w):
    g, n = w['a_re'].shape[1:]
    lr, li, bbr, bbi = _ssm_prep(w['a_re'][l], w['a_im'][l], w['log_dt'][l],
                                 w['ssm_b_re'][l], w['ssm_b_im'][l])
    bbd = jnp.concatenate([_block_diag(bbr, SSM_BLOCKS), _block_diag(bbi, SSM_BLOCKS)],
                          axis=-1).astype(BF16)
    c_re = w['ssm_c_re'][l].transpose(0, 2, 1)
    c_im = w['ssm_c_im'][l].transpose(0, 2, 1)
    cbd = jnp.concatenate([_block_diag(c_re, SSM_BLOCKS), _block_diag(-c_im, SSM_BLOCKS)],
                          axis=1).astype(BF16)
    row = lambda v: v.reshape(1, -1)
    return {
        'w_in': w['w_in'][l].astype(BF16), 'b_in': row(w['b_in'][l]),
        'conv_w': w['conv_w'][l], 'w_conv_out': w['w_conv_out'][l].astype(BF16),
        'lam_r': lr.reshape(1, g * n), 'lam_i': li.reshape(1, g * n),
        'bbd': bbd, 'cbd': cbd, 'ssm_d': row(w['ssm_d'][l]),
        'w_glu': w['w_glu'][l].astype(BF16), 'b_glu': row(w['b_glu'][l]),
        'w_ssm_out': w['w_ssm_out'][l].astype(BF16), 'w_o': w['w_o'][l].astype(BF16),
        'ln1_g': row(w['ln1_g'][l]), 'ln1_b': row(w['ln1_b'][l]),
        'w_router_t': w['w_router'][l].T, 'router_bias': w['router_bias'][l].reshape(-1, 1),
        'w_gate': w['w_gate'][l], 'w_up': w['w_up'][l], 'w_down': w['w_down'][l],
        'ws_gate': w['ws_gate'][l], 'ws_up': w['ws_up'][l], 'ws_down': w['ws_down'][l],
        'ln2_g': row(w['ln2_g'][l]), 'ln2_b': row(w['ln2_b'][l]),
    }


def _pick_steps(batch, seq, max_rows):
    steps = max(1, min(seq, max_rows // batch))
    while seq % steps:
        steps -= 1
    return steps


def _pick_tile(total, max_tile):
    tile = min(total, max_tile)
    while total % tile or tile % 16:
        tile -= 16
    return tile


MIXER_ROWS = 512
MOE_TILE = 1024


def _trunk_layer(x, conv_l, re_l, im_l, p, alpha, w_bf16):
    bsz, seq, d = x.shape
    assert bsz % SUBLANES == 0
    steps = _pick_steps(bsz, seq, MIXER_ROWS)
    kw = conv_l.shape[1]
    assert kw == 2
    cprev = conv_l.astype(F32).transpose(1, 0, 2).reshape(kw * bsz, d)
    h0r = re_l.astype(F32).reshape(bsz, -1)
    h0i = im_l.astype(F32).reshape(bsz, -1)
    if ((bsz * steps) % SUB == 0 and (bsz * seq) % (DISPATCH_SUBTILES * SUB) == 0
            and SUB % bsz == 0):
        if w_bf16 is None:
            w_bf16 = tuple(p[k].astype(BF16) for k in ('w_gate', 'w_up', 'w_down'))
        x1, comb, cnew, hr, hi, combt, cnt = _mixer(
            x, cprev, h0r, h0i, p, alpha=alpha, batch=bsz, steps=steps, sub=SUB)
        y = _moe_sparse(x1, comb, combt, cnt, p, w_bf16, alpha=alpha, batch=bsz,
                        n_sub=DISPATCH_SUBTILES)
    else:
        rows = x.transpose(1, 0, 2).reshape(seq * bsz, d)
        x1, comb, cnew, hr, hi = _mixer(rows, cprev, h0r, h0i, p,
                                        alpha=alpha, batch=bsz, steps=steps)
        out, w_bf16 = _moe(x1, comb, p, alpha=alpha, tile=_pick_tile(bsz * seq, MOE_TILE))
        y = out.reshape(seq, bsz, d).transpose(1, 0, 2)
    states = (cnew.reshape(kw, bsz, d).transpose(1, 0, 2),
              hr.reshape(re_l.shape), hi.reshape(im_l.shape))
    return y, states, w_bf16


def kernel(x_prompt, x_sample, state_conv, state_ssm_re, state_ssm_im,
           w_in, b_in, conv_w, w_conv_out, a_re, a_im, log_dt,
           ssm_b_re, ssm_b_im, ssm_c_re, ssm_c_im, ssm_d, w_glu, b_glu, w_ssm_out, w_o,
           ln1_g, ln1_b, w_router, router_bias, w_gate, w_up, w_down,
           ws_gate, ws_up, ws_down, ln2_g, ln2_b):
    w = dict(w_in=w_in, b_in=b_in, conv_w=conv_w, w_conv_out=w_conv_out,
             a_re=a_re, a_im=a_im, log_dt=log_dt,
             ssm_b_re=ssm_b_re, ssm_b_im=ssm_b_im, ssm_c_re=ssm_c_re, ssm_c_im=ssm_c_im,
             ssm_d=ssm_d, w_glu=w_glu, b_glu=b_glu, w_ssm_out=w_ssm_out, w_o=w_o,
             ln1_g=ln1_g, ln1_b=ln1_b, w_router=w_router, router_bias=router_bias,
             w_gate=w_gate, w_up=w_up, w_down=w_down,
             ws_gate=ws_gate, ws_up=ws_up, ws_down=ws_down, ln2_g=ln2_g, ln2_b=ln2_b)
    depth = w_in.shape[0]
    alpha = (2.0 * depth) ** 0.25
    bsz = x_prompt.shape[0]
    zero_conv = jnp.zeros((bsz,) + state_conv.shape[2:], x_prompt.dtype)
    zero_ssm = jnp.zeros((bsz,) + state_ssm_re.shape[2:], F32)
    y_p, y_s = x_prompt, x_sample
    st_p, st_s = [], []
    for l in range(depth):
        p = _layer_params(l, w)
        y_s, st, w_bf16 = _trunk_layer(y_s, state_conv[l], state_ssm_re[l], state_ssm_im[l],
                                       p, alpha, None)
        st_s.append(st)
        y_p, st, _ = _trunk_layer(y_p, zero_conv, zero_ssm, zero_ssm, p, alpha, w_bf16)
        st_p.append(st)
    conv_p, re_p, im_p = (jnp.stack(v) for v in zip(*st_p))
    conv_s, re_s, im_s = (jnp.stack(v) for v in zip(*st_s))
    return (y_p, y_s, conv_p, re_p, im_p, conv_s, re_s, im_s)
```

```python
import functools
import math

import jax
import jax.numpy as jnp
from jax import lax
from jax.experimental import pallas as pl
from jax.experimental.pallas import tpu as pltpu

F32 = jnp.float32
BF16 = jnp.bfloat16

LN_EPS = 1e-5
ROUTED_SCALE = 2.5
N_ROUTE_GROUPS = 8
TOPK_GROUPS = 4
TOP_K = 8

SUBLANES = 8
LANES = 128
SSM_BLOCKS = 4
SCAN_UNROLL = 4
SCAN_SPLIT = 1
POST_ROWS = 512
DENSE_EXPERTS_PER_STEP = 2
VMEM_LIMIT = 60 * 1024 * 1024


def _const_spec(shape):
    nd = len(shape)
    return pl.BlockSpec(shape, lambda *_: (0,) * nd, pipeline_mode=pl.Buffered(1))


def _ssm_prep_kernel(are_ref, aim_ref, ldt_ref, br_ref, bi_ref,
                     lr_ref, li_ref, bbr_ref, bbi_ref):
    dt = jnp.exp(ldt_ref[...])
    ar = are_ref[...]
    ai = aim_ref[...]
    mag = jnp.exp(ar * dt)
    lr = mag * jnp.cos(ai * dt)
    li = mag * jnp.sin(ai * dt)
    den = ar * ar + ai * ai
    fr = ((lr - 1.0) * ar + li * ai) / den
    fi = (li * ar - (lr - 1.0) * ai) / den
    lr_ref[...] = lr
    li_ref[...] = li
    br = br_ref[...]
    bi = bi_ref[...]
    bbr_ref[...] = fr * br - fi * bi
    bbi_ref[...] = fr * bi + fi * br


def _ssm_prep(a_re, a_im, log_dt, b_re, b_im):
    g, n = a_re.shape
    h = b_re.shape[-1]
    vec = jax.ShapeDtypeStruct((g, 1, n), F32)
    mat = jax.ShapeDtypeStruct((g, h, n), F32)
    return pl.pallas_call(
        _ssm_prep_kernel,
        out_shape=(vec, vec, mat, mat),
        name="ssm_prep",
    )(a_re.reshape(g, 1, n), a_im.reshape(g, 1, n), log_dt.reshape(g, 1, 1),
      b_re.transpose(0, 2, 1), b_im.transpose(0, 2, 1))


def _block_diag(m, nblk):
    g, p, q = m.shape
    gl = g // nblk
    tiled = jnp.tile(m.reshape(nblk, gl * p, q), (1, 1, gl))
    row = lax.broadcasted_iota(jnp.int32, (gl * p, gl * q), 0) // p
    col = lax.broadcasted_iota(jnp.int32, (gl * p, gl * q), 1) // q
    return jnp.where(row == col, tiled, 0.0)


def _layer_norm(r, g, b):
    mu = jnp.mean(r, axis=-1, keepdims=True)
    d = r - mu
    var = jnp.mean(d * d, axis=-1, keepdims=True)
    return d * lax.rsqrt(var + LN_EPS) * g + b


def _split_bf16(v):
    hi = v.astype(BF16)
    return hi, (v - hi.astype(F32)).astype(BF16)


def _route(scores, biased):
    n_exp, r = scores.shape
    gsz = n_exp // N_ROUTE_GROUPS
    neg = jnp.float32(-jnp.inf)
    rows = []
    for g in range(N_ROUTE_GROUPS):
        v = biased[g * gsz:(g + 1) * gsz, :]
        m1 = jnp.max(v, axis=0, keepdims=True)
        is_max = v == m1
        n_max = jnp.sum(is_max.astype(F32), axis=0, keepdims=True)
        rest = jnp.max(jnp.where(is_max, neg, v), axis=0, keepdims=True)
        rows.append(m1 + jnp.where(n_max >= 2.0, m1, rest))
    gscore = jnp.concatenate(rows, axis=0)
    gidx = lax.broadcasted_iota(jnp.int32, gscore.shape, 0)
    grank = jnp.zeros(gscore.shape, F32)
    for g in range(N_ROUTE_GROUPS):
        sg = gscore[g:g + 1, :]
        beats = (sg > gscore) | ((sg == gscore) & (gidx > g))
        grank = grank + beats.astype(F32)
    gkeep = grank < float(TOPK_GROUPS)
    masked = jnp.concatenate(
        [jnp.where(gkeep[g:g + 1, :], biased[g * gsz:(g + 1) * gsz, :], neg)
         for g in range(N_ROUTE_GROUPS)], axis=0)
    eidx = lax.broadcasted_iota(jnp.int32, masked.shape, 0).astype(F32)
    left = masked
    for _ in range(TOP_K):
        top = jnp.max(left, axis=0, keepdims=True)
        first = jnp.min(jnp.where(left == top, eidx, float(n_exp)), axis=0, keepdims=True)
        left = jnp.where(eidx == first, neg, left)
    w = jnp.where(left != masked, scores, 0.0)
    return w / jnp.sum(w, axis=0, keepdims=True) * ROUTED_SCALE


def _mixer_kernel(alpha, batch, steps, sub,
                  x_ref, cprev_ref, h0r_ref, h0i_ref,
                  win_ref, bin_ref, convw_ref, wco_ref,
                  lamr_ref, lami_ref, bbd_ref, cbd_ref, dskip_ref,
                  wglu_ref, bglu_ref, wso_ref, wo_ref, ln1g_ref, ln1b_ref,
                  wrt_ref, rbias_ref,
                  x1_ref, comb_ref, cnew_ref, hr_ref, hi_ref, *rest):
    if sub:
        combt_ref, cnt_ref, *rest = rest
    ubuf, xk_ref, ys_ref, *rest = rest
    rows = batch * steps
    d = x_ref.shape[-1]
    d_blk = d // SSM_BLOCKS
    n_blk = lamr_ref.shape[1] // SSM_BLOCKS
    i = pl.program_id(0)

    @pl.when(i == 0)
    def _init():
        ubuf[0:2 * batch, :] = cprev_ref[...]
        hr_ref[...] = h0r_ref[...]
        hi_ref[...] = h0i_ref[...]

    if len(x_ref.shape) == 3:
        xs_ref, = rest
        for b in range(batch):
            for c in range(d // LANES):
                xs_ref[c, pl.ds(b, steps, stride=batch), :] = x_ref[b, :, c * LANES:(c + 1) * LANES]
        x = jnp.concatenate([xs_ref[c] for c in range(d // LANES)], axis=1)
    else:
        x = x_ref[...]
    xb = x.astype(BF16)

    def proj(c):
        cols = slice(c * d, (c + 1) * d)
        return (jnp.dot(xb, win_ref[:, cols], preferred_element_type=F32)
                + bin_ref[:, cols])

    u = proj(1) * proj(2)
    ubuf[2 * batch:2 * batch + rows, :] = u
    conv = (convw_ref[0:1, :] * ubuf[0:rows, :]
            + convw_ref[1:2, :] * ubuf[batch:batch + rows, :]
            + convw_ref[2:3, :] * u)
    ya = jnp.dot((proj(0) * conv).astype(BF16), wco_ref[...],
                 preferred_element_type=F32)
    tail = ubuf[rows:rows + 2 * batch, :]
    ubuf[0:2 * batch, :] = tail
    cnew_ref[...] = tail
    ya_rows = slice(2 * batch, 2 * batch + rows)
    ubuf[ya_rows, :] = ya

    us = proj(3)
    half = n_blk // SCAN_SPLIT
    unroll = SCAN_UNROLL if steps % SCAN_UNROLL == 0 else 1
    for k in range(SSM_BLOCKS):
        usk = us[:, k * d_blk:(k + 1) * d_blk].astype(BF16)
        xk_ref[...] = jnp.dot(usk, bbd_ref[k], preferred_element_type=F32)
        for hf in range(SCAN_SPLIT):
            st = slice(k * n_blk + hf * half, k * n_blk + (hf + 1) * half)
            re = slice(hf * half, (hf + 1) * half)
            im = slice(n_blk + hf * half, n_blk + (hf + 1) * half)
            if steps == 1:
                lr = lamr_ref[:, st]
                li = lami_ref[:, st]
                hr = hr_ref[:, st]
                hi = hi_ref[:, st]
                nhr = lr * hr - li * hi + xk_ref[:, re]
                nhi = lr * hi + li * hr + xk_ref[:, im]
                xk_ref[:, re] = nhr
                xk_ref[:, im] = nhi
                hr_ref[:, st] = nhr
                hi_ref[:, st] = nhi
            else:
                lr = jnp.broadcast_to(lamr_ref[:, st], (SUBLANES, half))
                li = jnp.broadcast_to(lami_ref[:, st], (SUBLANES, half))
                for s in range(batch // SUBLANES):
                    grp = slice(s * SUBLANES, (s + 1) * SUBLANES)

                    def step(tt, carry, s=s, re=re, im=im, lr=lr, li=li):
                        hr, hi = carry
                        for k_un in range(unroll):
                            row = pl.multiple_of((tt * unroll + k_un) * batch + s * SUBLANES,
                                                 SUBLANES)
                            rs = pl.ds(row, SUBLANES)
                            hr, hi = (lr * hr - li * hi + xk_ref[rs, re],
                                      lr * hi + li * hr + xk_ref[rs, im])
                            xk_ref[rs, re] = hr
                            xk_ref[rs, im] = hi
                        return hr, hi

                    hr, hi = lax.fori_loop(0, steps // unroll, step,
                                           (hr_ref[grp, st], hi_ref[grp, st]))
                    hr_ref[grp, st] = hr
                    hi_ref[grp, st] = hi
        ys_ref[:, k * d_blk:(k + 1) * d_blk] = jnp.dot(
            xk_ref[...].astype(BF16), cbd_ref[k], preferred_element_type=F32)
    ys_ref[...] = ys_ref[...] + dskip_ref[...] * us

    chunk = min(rows, POST_ROWS)
    for q in range(rows // chunk):
        rq = slice(q * chunk, (q + 1) * chunk)
        if len(x_ref.shape) == 3:
            xq = jnp.concatenate([xs_ref[c, rq, :] for c in range(d // LANES)], axis=1)
        else:
            xq = x_ref[rq, :]
        xbq = xq.astype(BF16)

        def projq(c, xbq=xbq):
            cols = slice(c * d, (c + 1) * d)
            return (jnp.dot(xbq, win_ref[:, cols], preferred_element_type=F32)
                    + bin_ref[:, cols])

        z = jax.nn.gelu(ys_ref[rq, :])
        gate = (jnp.dot(z.astype(BF16), wglu_ref[...], preferred_element_type=F32)
                + bglu_ref[...])
        glu = z * jax.nn.sigmoid(gate)
        yb = jnp.dot(glu.astype(BF16), wso_ref[...], preferred_element_type=F32)

        m = (jax.nn.sigmoid(projq(4)) * ubuf[2 * batch + q * chunk:2 * batch + (q + 1) * chunk, :]
             + jax.nn.sigmoid(projq(5)) * yb)
        o = jnp.dot(m.astype(BF16), wo_ref[...], preferred_element_type=F32)
        x1 = _layer_norm(alpha * xq + o, ln1g_ref[...], ln1b_ref[...])
        x1_ref[rq, :] = x1

        nt = (((1,), (1,)), ((), ()))
        w_hi, w_lo = _split_bf16(wrt_ref[...])
        x_hi, x_lo = _split_bf16(x1)
        logits = (lax.dot_general(w_hi, x_hi, nt, preferred_element_type=F32)
                  + lax.dot_general(w_hi, x_lo, nt, preferred_element_type=F32)
                  + lax.dot_general(w_lo, x_hi, nt, preferred_element_type=F32))
        scores = jax.nn.sigmoid(logits)
        comb = _route(scores, scores + rbias_ref[...])
        comb_ref[rq, :] = comb.T
        if sub:
            combt_ref[:, rq] = comb
    if sub:
        ones = jnp.ones((SUBLANES, sub), BF16)
        for s in range(rows // sub):
            sel = jnp.where(combt_ref[:, s * sub:(s + 1) * sub] != 0.0, 1.0, 0.0).astype(BF16)
            cnt_ref[s] = lax.dot_general(ones, sel, (((1,), (1,)), ((), ())),
                                         preferred_element_type=F32)


def _mixer(x, cprev, h0r, h0i, p, *, alpha, batch, steps, sub=0):
    d = x.shape[-1]
    total = x.size // d
    rows = batch * steps
    assert not sub or rows % sub == 0
    n_state = h0r.shape[1]
    n_exp = p['w_router_t'].shape[0]
    grid = (total // rows,)
    if x.ndim == 3:
        x_spec = pl.BlockSpec((batch, steps, d), lambda i: (0, i, 0))
    else:
        x_spec = pl.BlockSpec((rows, d), lambda i: (i, 0))
    consts = [p['w_in'], p['b_in'], p['conv_w'], p['w_conv_out'],
              p['lam_r'], p['lam_i'], p['bbd'], p['cbd'], p['ssm_d'],
              p['w_glu'], p['b_glu'], p['w_ssm_out'], p['w_o'], p['ln1_g'], p['ln1_b'],
              p['w_router_t'], p['router_bias']]
    in_specs = ([x_spec,
                 _const_spec(cprev.shape), _const_spec(h0r.shape), _const_spec(h0i.shape)]
                + [_const_spec(c.shape) for c in consts])
    out_shape = (jax.ShapeDtypeStruct((total, d), F32),
                 jax.ShapeDtypeStruct((total, n_exp), F32),
                 jax.ShapeDtypeStruct((2 * batch, d), F32),
                 jax.ShapeDtypeStruct((batch, n_state), F32),
                 jax.ShapeDtypeStruct((batch, n_state), F32))
    out_specs = (pl.BlockSpec((rows, d), lambda i: (i, 0)),
                 pl.BlockSpec((rows, n_exp), lambda i: (i, 0)),
                 pl.BlockSpec((2 * batch, d), lambda i: (0, 0)),
                 pl.BlockSpec((batch, n_state), lambda i: (0, 0)),
                 pl.BlockSpec((batch, n_state), lambda i: (0, 0)))
    if sub:
        out_shape += (jax.ShapeDtypeStruct((n_exp, total), F32),
                      jax.ShapeDtypeStruct((total // sub, SUBLANES, n_exp), F32))
        out_specs += (pl.BlockSpec((n_exp, rows), lambda i: (0, i)),
                      pl.BlockSpec((rows // sub, SUBLANES, n_exp), lambda i: (i, 0, 0)))
    scratch = [pltpu.VMEM((rows + 2 * batch, d), F32),
               pltpu.VMEM((rows, 2 * n_state // SSM_BLOCKS), F32),
               pltpu.VMEM((rows, d), F32)]
    if x.ndim == 3:
        scratch.append(pltpu.VMEM((d // LANES, rows, LANES), F32))
    return pl.pallas_call(
        functools.partial(_mixer_kernel, alpha, batch, steps, sub),
        grid=grid, in_specs=in_specs, out_specs=out_specs, out_shape=out_shape,
        scratch_shapes=scratch,
        compiler_params=pltpu.CompilerParams(
            dimension_semantics=("arbitrary",), vmem_limit_bytes=VMEM_LIMIT),
        name="mixer",
    )(x, cprev, h0r, h0i, *consts)


def _swiglu(xb, wg, wu):
    g = jnp.dot(xb, wg.astype(BF16), preferred_element_type=F32)
    u = jnp.dot(xb, wu.astype(BF16), preferred_element_type=F32)
    return jax.nn.silu(g) * u


def _moe_kernel(alpha, emit_bf16, x1_ref, comb_ref, wg_ref, wu_ref, wd_ref,
                wsg_ref, wsu_ref, wsd_ref, g_ref, b_ref, out_ref, *rest):
    if emit_bf16:
        wgb_ref, wub_ref, wdb_ref, xb_ref, acc_ref = rest
    else:
        xb_ref, acc_ref = rest
    e = pl.program_id(1)

    @pl.when(e == 0)
    def _shared():
        xb = x1_ref[...].astype(BF16)
        xb_ref[...] = xb
        hs = _swiglu(xb, wsg_ref[...], wsu_ref[...])
        acc_ref[...] = jnp.dot(hs.astype(BF16), wsd_ref[...].astype(BF16),
                               preferred_element_type=F32)

    comb = comb_ref[...]
    lane = lax.broadcasted_iota(jnp.int32, comb.shape, 1)
    xb = xb_ref[...]
    acc = acc_ref[...]
    for k in range(wg_ref.shape[0]):
        wg = wg_ref[k].astype(BF16)
        wu = wu_ref[k].astype(BF16)
        wd = wd_ref[k].astype(BF16)
        if emit_bf16:
            wgb_ref[k] = wg
            wub_ref[k] = wu
            wdb_ref[k] = wd
        c = jnp.sum(jnp.where(lane == e * wg_ref.shape[0] + k, comb, 0.0), axis=1, keepdims=True)
        h = _swiglu(xb, wg, wu) * c
        acc = acc + jnp.dot(h.astype(BF16), wd, preferred_element_type=F32)
    acc_ref[...] = acc

    @pl.when(e == pl.num_programs(1) - 1)
    def _finish():
        out_ref[...] = _layer_norm(alpha * x1_ref[...] + acc_ref[...],
                                   g_ref[...], b_ref[...])


def _moe(x1, comb, p, *, alpha, tile):
    total, d = x1.shape
    n_exp, _, f = p['w_gate'].shape
    fs = p['ws_gate'].shape[1]
    eg = DENSE_EXPERTS_PER_STEP
    grid = (total // tile, n_exp // eg)
    emit_bf16 = grid[0] == 1
    up_spec = pl.BlockSpec((eg, d, f), lambda i, e: (e, 0, 0))
    down_spec = pl.BlockSpec((eg, f, d), lambda i, e: (e, 0, 0))
    in_specs = [pl.BlockSpec((tile, d), lambda i, e: (i, 0)),
                pl.BlockSpec((tile, n_exp), lambda i, e: (i, 0)),
                up_spec, up_spec, down_spec,
                _const_spec((d, fs)), _const_spec((d, fs)), _const_spec((fs, d)),
                _const_spec((1, d)), _const_spec((1, d))]
    out_specs = [pl.BlockSpec((tile, d), lambda i, e: (i, 0))]
    out_shape = [jax.ShapeDtypeStruct((total, d), F32)]
    if emit_bf16:
        out_specs += [up_spec, up_spec, down_spec]
        out_shape += [jax.ShapeDtypeStruct(p[k].shape, BF16)
                      for k in ('w_gate', 'w_up', 'w_down')]
    res = pl.pallas_call(
        functools.partial(_moe_kernel, alpha, emit_bf16),
        grid=grid, in_specs=in_specs, out_specs=out_specs, out_shape=out_shape,
        scratch_shapes=[pltpu.VMEM((tile, d), BF16), pltpu.VMEM((tile, d), F32)],
        compiler_params=pltpu.CompilerParams(
            dimension_semantics=("arbitrary", "arbitrary"), vmem_limit_bytes=VMEM_LIMIT),
        name="moe",
    )(x1, comb, p['w_gate'], p['w_up'], p['w_down'],
      p['ws_gate'], p['ws_up'], p['ws_down'], p['ln2_g'], p['ln2_b'])
    if emit_bf16:
        return res[0], tuple(res[1:])
    return res[0], tuple(p[k].astype(BF16) for k in ('w_gate', 'w_up', 'w_down'))


SUB = 256
RUN_ALIGN = 16
FFN_CHUNK = 192
SEL_CHUNK = 512
PACK_STATIC = 4
PACK_FULL = 2
DISPATCH_SUBTILES = 4
EXPERTS_PER_STEP = 4
BUILD_AHEAD = 1


def _expert_onehot(r, off, cnt):
    hit = (r >= off) & (r < off + cnt)
    return hit, jnp.where(hit, off + 1.0, 0.0)


def _moe_sparse_kernel(alpha, batch, n_sub, cap, eg,
                       off_s, pc_s,
                       x1_ref, comb_ref, combt_ref, offr_ref, cntr_ref, offc_ref, cntc_ref,
                       wg_ref, wu_ref, wd_ref, wsg_ref, wsu_ref, wsd_ref, g_ref, b_ref,
                       out_ref, gbuf, xe, zbuf, built, *xq):
    i = pl.program_id(0)
    g = pl.program_id(1)
    n_exp = comb_ref.shape[1]
    tri_r = lax.broadcasted_iota(jnp.int32, (SUB, SUB), 0)
    tri_c = lax.broadcasted_iota(jnp.int32, (SUB, SUB), 1)

    def run(j, e):
        idx = (i * n_sub + j) * n_exp + e
        return off_s[idx], pc_s[idx]

    def used_rows(j):
        o, p = run(j, n_exp - 1)
        return o + p

    def rows_at(start, chunks):
        return pl.ds(pl.multiple_of(start, RUN_ALIGN), chunks * RUN_ALIGN)

    rows16 = functools.partial(rows_at, chunks=1)

    @pl.when((i == 0) & (g == 0))
    def _zero():
        xe[...] = jnp.zeros(xe.shape, xe.dtype)
        for buf in xq:
            buf[...] = jnp.zeros(buf.shape, buf.dtype)

    n_chunks = cap // SEL_CHUNK

    @pl.when(g == 0)
    def _start():
        built[0] = 0

    def build(rc, j):
        before = jnp.where(tri_r < tri_c, 1.0, 0.0).astype(BF16)
        tok = slice(j * SUB, (j + 1) * SUB)
        xj = x1_ref[tok, :].astype(BF16)
        sel = combt_ref[:, tok] != 0.0
        pos = jnp.dot(jnp.where(sel, 1.0, 0.0).astype(BF16), before,
                      preferred_element_type=F32)
        posm = jnp.where(sel, pos, -1.0).astype(BF16)
        off = offr_ref[j]
        cnt = cntr_ref[j]
        r = (lax.broadcasted_iota(jnp.int32, (SEL_CHUNK, 1), 0)
             + rc * SEL_CHUNK).astype(F32)
        hit, start1 = _expert_onehot(r, off, cnt)
        s = jnp.sum(start1, axis=1, keepdims=True)
        q = jnp.where(s > 0.0, r - (s - 1.0), -2.0)
        rank = jnp.dot(jnp.where(hit, 1.0, 0.0).astype(BF16), posm,
                       preferred_element_type=F32)
        pick = jnp.where(rank == q, 1.0, 0.0).astype(BF16)
        gbuf[j, rc * SEL_CHUNK:(rc + 1) * SEL_CHUNK, :] = jnp.dot(
            pick, xj, preferred_element_type=F32).astype(BF16)

    def chunks_for(rows):
        return jnp.minimum(lax.div(rows + (SEL_CHUNK - 1), SEL_CHUNK), n_chunks)

    reach = jnp.int32(0)
    final = jnp.int32((n_chunks - 1) * SEL_CHUNK)
    for j in range(n_sub):
        o, p = run(j, g * eg + eg - 1)
        reach = jnp.maximum(reach, o + p)
        final = jnp.maximum(final, used_rows(j))
    slack = PACK_STATIC * RUN_ALIGN
    reach = jnp.where(g == pl.num_programs(1) - 1, final, reach)
    done = built[0]
    target = jnp.maximum(chunks_for(reach + slack) * n_sub,
                         jnp.minimum(done + BUILD_AHEAD, chunks_for(final + slack) * n_sub))
    for u in range(n_chunks * n_sub):
        pl.when((done <= u) & (u < target))(functools.partial(build, u // n_sub, u % n_sub))
    built[0] = jnp.maximum(done, target)

    def ffn_rows(rs, ee):
        xc = xe[rs, :]
        h = (jax.nn.silu(jnp.dot(xc, wg_ref[ee], preferred_element_type=F32))
             * jnp.dot(xc, wu_ref[ee], preferred_element_type=F32))
        xe[rs, :] = jnp.dot(h.astype(BF16), wd_ref[ee], preferred_element_type=F32).astype(BF16)

    def pack(e, buf, static):
        n = jnp.int32(0)
        for j in range(n_sub):
            o, p = run(j, e)
            if static:
                buf[rows_at(n, PACK_STATIC), :] = gbuf[j, rows_at(o, PACK_STATIC), :]
            else:
                def chunk(c, carry, j=j, o=o, n=n):
                    buf[rows16(n + c * RUN_ALIGN), :] = gbuf[j, rows16(o + c * RUN_ALIGN), :]
                    return carry

                lax.fori_loop(0, lax.div(p, RUN_ALIGN), chunk, 0)
            n = n + p
        return n

    def unpack(e, buf, static, keep_next=True):
        n = jnp.int32(0)
        for j in range(n_sub):
            o, p = run(j, e)
            if static and not keep_next:
                gbuf[j, rows_at(o, PACK_STATIC), :] = buf[rows_at(n, PACK_STATIC), :]
            elif static:
                full = PACK_FULL * RUN_ALIGN
                gbuf[j, rows_at(o, PACK_FULL), :] = buf[rows_at(n, PACK_FULL), :]
                rest = PACK_STATIC - PACK_FULL
                dst = rows_at(o + full, rest)
                row = lax.broadcasted_iota(jnp.int32, (rest * RUN_ALIGN, gbuf.shape[2]), 0)
                gbuf[j, dst, :] = jnp.where(row < p - full, buf[rows_at(n + full, rest), :],
                                            gbuf[j, dst, :])
            else:
                def chunk(c, carry, j=j, o=o, n=n):
                    gbuf[j, rows16(o + c * RUN_ALIGN), :] = buf[rows16(n + c * RUN_ALIGN), :]
                    return carry

                lax.fori_loop(0, lax.div(p, RUN_ALIGN), chunk, 0)
            n = n + p

    usual = jnp.bool_(True)
    for ee in range(eg):
        n = jnp.int32(0)
        for j in range(n_sub):
            p = run(j, g * eg + ee)[1]
            usual = usual & (p >= PACK_FULL * RUN_ALIGN) & (p <= PACK_STATIC * RUN_ALIGN)
            n = n + p
        usual = usual & (n <= FFN_CHUNK)

    @pl.when(usual)
    def _together():
        for ee in range(eg):
            pack(g * eg + ee, xq[ee], True)
        xcs = [xq[ee][0:FFN_CHUNK, :] for ee in range(eg)]
        hs = [(jax.nn.silu(jnp.dot(xc, wg_ref[ee], preferred_element_type=F32))
               * jnp.dot(xc, wu_ref[ee], preferred_element_type=F32)).astype(BF16)
              for ee, xc in enumerate(xcs)]
        outs = [jnp.dot(h, wd_ref[ee], preferred_element_type=F32).astype(BF16)
                for ee, h in enumerate(hs)]
        for ee, o in enumerate(outs):
            xq[ee][0:FFN_CHUNK, :] = o
        for ee in range(eg):
            unpack(g * eg + ee, xq[ee], True, keep_next=ee == eg - 1)

    @pl.when(jnp.logical_not(usual))
    def _one_by_one():
        for ee in range(eg):
            e = g * eg + ee
            n = pack(e, xe, False)

            def ffn(k, carry, ee=ee):
                ffn_rows(pl.ds(pl.multiple_of(k * FFN_CHUNK, RUN_ALIGN), FFN_CHUNK), ee)
                return carry

            lax.fori_loop(0, lax.div(n + (FFN_CHUNK - 1), FFN_CHUNK), ffn, 0)
            unpack(e, xe, False)

    @pl.when(g == pl.num_programs(1) - 1)
    def _combine():
        earlier = jnp.where(tri_c < tri_r, 1.0, 0.0).astype(BF16)
        for j in range(n_sub):
            tok = slice(j * SUB, (j + 1) * SUB)
            x = x1_ref[tok, :]
            xb = x.astype(BF16)
            hs = (jax.nn.silu(jnp.dot(xb, wsg_ref[...], preferred_element_type=F32))
                  * jnp.dot(xb, wsu_ref[...], preferred_element_type=F32))
            y = jnp.dot(hs.astype(BF16), wsd_ref[...], preferred_element_type=F32)
            comb = comb_ref[tok, :]
            sel = comb != 0.0
            pos = jnp.dot(earlier, jnp.where(sel, 1.0, 0.0).astype(BF16),
                          preferred_element_type=F32)
            posm = jnp.where(sel, pos, -1.0).astype(BF16)
            c_bf = comb.astype(BF16)
            off = offc_ref[j]
            cnt = cntc_ref[j]
            used = used_rows(j)

            def gather_rows(rc, j=j, posm=posm, c_bf=c_bf, off=off, cnt=cnt):
                r = (lax.broadcasted_iota(jnp.int32, (1, SEL_CHUNK), 1)
                     + rc * SEL_CHUNK).astype(F32)
                hit, start1 = _expert_onehot(r, off, cnt)
                s = jnp.sum(start1, axis=0, keepdims=True)
                q = jnp.where(s > 0.0, r - (s - 1.0), -2.0)
                hb = jnp.where(hit, 1.0, 0.0).astype(BF16)
                pick = jnp.dot(posm, hb, preferred_element_type=F32) == q
                weight = jnp.dot(c_bf, hb, preferred_element_type=F32)
                w = jnp.where(pick, weight, 0.0).astype(BF16)
                rows = gbuf[j, rc * SEL_CHUNK:(rc + 1) * SEL_CHUNK, :]
                return jnp.dot(w, rows, preferred_element_type=F32)

            last = cap // SEL_CHUNK - 1
            for rc in range(last):
                y = y + gather_rows(rc)
            n_lt = x.shape[1] // LANES
            lane_tiles = lambda v: [v[:, c * LANES:(c + 1) * LANES] for c in range(n_lt)]

            def put(v):
                for c, t in enumerate(lane_tiles(v)):
                    zbuf[c] = t

            get = lambda: jnp.concatenate([zbuf[c] for c in range(n_lt)], axis=1)
            put(alpha * x + y)

            @pl.when(last * SEL_CHUNK < used)
            def _tail(gather_rows=gather_rows, put=put, get=get):
                put(get() + gather_rows(last))

            put(_layer_norm(get(), g_ref[...], b_ref[...]))
            t_sub = SUB // batch
            for b in range(batch):
                for c in range(n_lt):
                    out_ref[b, j * t_sub:(j + 1) * t_sub, c * LANES:(c + 1) * LANES] = (
                        zbuf[c, pl.ds(b, t_sub, stride=batch), :])


def _moe_sparse(x1, comb, combt, cnt, p, w_bf16, *, alpha, batch, n_sub, eg=EXPERTS_PER_STEP):
    total, d = x1.shape
    w_gate, w_up, w_down = w_bf16
    n_exp, _, f = w_gate.shape
    fs = p['ws_gate'].shape[1]
    tile = n_sub * SUB
    n_tiles = total // tile
    assert SUB % batch == 0
    cap = -(-(SUB * TOP_K + n_exp * (RUN_ALIGN - 1)) // SEL_CHUNK) * SEL_CHUNK
    xe_rows = -(-(tile + n_sub * (RUN_ALIGN - 1)) // FFN_CHUNK) * FFN_CHUNK

    cnt = cnt[:, 0, :]
    pc = jnp.ceil(cnt / RUN_ALIGN) * RUN_ALIGN
    off = jnp.cumsum(pc, axis=1) - pc
    off_s = off.astype(jnp.int32).reshape(-1)
    pc_s = pc.astype(jnp.int32).reshape(-1)
    offr, cntr = off[:, None, :], cnt[:, None, :]
    offc, cntc = off[:, :, None], cnt[:, :, None]

    row_spec = pl.BlockSpec((n_sub, 1, n_exp), lambda i, g, *_: (i, 0, 0))
    col_spec = pl.BlockSpec((n_sub, n_exp, 1), lambda i, g, *_: (i, 0, 0))
    const = lambda shape: pl.BlockSpec(shape, lambda i, g, *_: (0,) * len(shape),
                                       pipeline_mode=pl.Buffered(1))
    grid_spec = pltpu.PrefetchScalarGridSpec(
        num_scalar_prefetch=2,
        grid=(n_tiles, n_exp // eg),
        in_specs=[pl.BlockSpec((tile, d), lambda i, g, *_: (i, 0), pipeline_mode=pl.Buffered(1)),
                  pl.BlockSpec((tile, n_exp), lambda i, g, *_: (i, 0)),
                  pl.BlockSpec((n_exp, tile), lambda i, g, *_: (0, i)),
                  row_spec, row_spec, col_spec, col_spec,
                  pl.BlockSpec((eg, d, f), lambda i, g, *_: (g, 0, 0)),
                  pl.BlockSpec((eg, d, f), lambda i, g, *_: (g, 0, 0)),
                  pl.BlockSpec((eg, f, d), lambda i, g, *_: (g, 0, 0)),
                  const((d, fs)), const((d, fs)), const((fs, d)),
                  const((1, d)), const((1, d))],
        out_specs=pl.BlockSpec((batch, tile // batch, d), lambda i, g, *_: (0, i, 0),
                               pipeline_mode=pl.Buffered(1)),
        scratch_shapes=[pltpu.VMEM((n_sub, cap, d), BF16), pltpu.VMEM((xe_rows, d), BF16),
                        pltpu.VMEM((d // LANES, SUB, LANES), F32),
                        pltpu.SMEM((1,), jnp.int32)]
        + [pltpu.VMEM((FFN_CHUNK + PACK_STATIC * RUN_ALIGN, d), BF16)] * eg)
    return pl.pallas_call(
        functools.partial(_moe_sparse_kernel, alpha, batch, n_sub, cap, eg),
        grid_spec=grid_spec,
        out_shape=jax.ShapeDtypeStruct((batch, total // batch, d), F32),
        compiler_params=pltpu.CompilerParams(
            dimension_semantics=("arbitrary", "arbitrary"), vmem_limit_bytes=VMEM_LIMIT),
        name="moe_sparse",
    )(off_s, pc_s, x1, comb, combt, offr, cntr, offc, cntc, w_gate, w_up, w_down,
      p['ws_gate'].astype(BF16), p['ws_up'].astype(BF16), p['ws_down'].astype(BF16),
      p['ln2_g'], p['ln2_b'])


def _layer_params(l, w):
    g, n = w['a_re'].shape[1:]
    lr, li, bbr, bbi = _ssm_prep(w['a_re'][l], w['a_im'][l], w['log_dt'][l],
                                 w['ssm_b_re'][l], w['ssm_b_im'][l])
    bbd = jnp.concatenate([_block_diag(bbr, SSM_BLOCKS), _block_diag(bbi, SSM_BLOCKS)],
                          axis=-1).astype(BF16)
    c_re = w['ssm_c_re'][l].transpose(0, 2, 1)
    c_im = w['ssm_c_im'][l].transpose(0, 2, 1)
    cbd = jnp.concatenate([_block_diag(c_re, SSM_BLOCKS), _block_diag(-c_im, SSM_BLOCKS)],
                          axis=1).astype(BF16)
    row = lambda v: v.reshape(1, -1)
    return {
        'w_in': w['w_in'][l].astype(BF16), 'b_in': row(w['b_in'][l]),
        'conv_w': w['conv_w'][l], 'w_conv_out': w['w_conv_out'][l].astype(BF16),
        'lam_r': lr.reshape(1, g * n), 'lam_i': li.reshape(1, g * n),
        'bbd': bbd, 'cbd': cbd, 'ssm_d': row(w['ssm_d'][l]),
        'w_glu': w['w_glu'][l].astype(BF16), 'b_glu': row(w['b_glu'][l]),
        'w_ssm_out': w['w_ssm_out'][l].astype(BF16), 'w_o': w['w_o'][l].astype(BF16),
        'ln1_g': row(w['ln1_g'][l]), 'ln1_b': row(w['ln1_b'][l]),
        'w_router_t': w['w_router'][l].T, 'router_bias': w['router_bias'][l].reshape(-1, 1),
        'w_gate': w['w_gate'][l], 'w_up': w['w_up'][l], 'w_down': w['w_down'][l],
        'ws_gate': w['ws_gate'][l], 'ws_up': w['ws_up'][l], 'ws_down': w['ws_down'][l],
        'ln2_g': row(w['ln2_g'][l]), 'ln2_b': row(w['ln2_b'][l]),
    }


def _pick_steps(batch, seq, max_rows):
    steps = max(1, min(seq, max_rows // batch))
    while seq % steps:
        steps -= 1
    return steps


def _pick_tile(total, max_tile):
    tile = min(total, max_tile)
    while total % tile or tile % 16:
        tile -= 16
    return tile


MIXER_ROWS = 512
MOE_TILE = 1024


def _trunk_layer(x, conv_l, re_l, im_l, p, alpha, w_bf16):
    bsz, seq, d = x.shape
    assert bsz % SUBLANES == 0
    steps = _pick_steps(bsz, seq, MIXER_ROWS)
    kw = conv_l.shape[1]
    assert kw == 2
    cprev = conv_l.astype(F32).transpose(1, 0, 2).reshape(kw * bsz, d)
    h0r = re_l.astype(F32).reshape(bsz, -1)
    h0i = im_l.astype(F32).reshape(bsz, -1)
    if ((bsz * steps) % SUB == 0 and (bsz * seq) % (DISPATCH_SUBTILES * SUB) == 0
            and SUB % bsz == 0):
        if w_bf16 is None:
            w_bf16 = tuple(p[k].astype(BF16) for k in ('w_gate', 'w_up', 'w_down'))
        x1, comb, cnew, hr, hi, combt, cnt = _mixer(
            x, cprev, h0r, h0i, p, alpha=alpha, batch=bsz, steps=steps, sub=SUB)
        y = _moe_sparse(x1, comb, combt, cnt, p, w_bf16, alpha=alpha, batch=bsz,
                        n_sub=DISPATCH_SUBTILES)
    else:
        rows = x.transpose(1, 0, 2).reshape(seq * bsz, d)
        x1, comb, cnew, hr, hi = _mixer(rows, cprev, h0r, h0i, p,
                                        alpha=alpha, batch=bsz, steps=steps)
        out, w_bf16 = _moe(x1, comb, p, alpha=alpha, tile=_pick_tile(bsz * seq, MOE_TILE))
        y = out.reshape(seq, bsz, d).transpose(1, 0, 2)
    states = (cnew.reshape(kw, bsz, d).transpose(1, 0, 2),
              hr.reshape(re_l.shape), hi.reshape(im_l.shape))
    return y, states, w_bf16


def kernel(x_prompt, x_sample, state_conv, state_ssm_re, state_ssm_im,
           w_in, b_in, conv_w, w_conv_out, a_re, a_im, log_dt,
           ssm_b_re, ssm_b_im, ssm_c_re, ssm_c_im, ssm_d, w_glu, b_glu, w_ssm_out, w_o,
           ln1_g, ln1_b, w_router, router_bias, w_gate, w_up, w_down,
           ws_gate, ws_up, ws_down, ln2_g, ln2_b):
    w = dict(w_in=w_in, b_in=b_in, conv_w=conv_w, w_conv_out=w_conv_out,
             a_re=a_re, a_im=a_im, log_dt=log_dt,
             ssm_b_re=ssm_b_re, ssm_b_im=ssm_b_im, ssm_c_re=ssm_c_re, ssm_c_im=ssm_c_im,
             ssm_d=ssm_d, w_glu=w_glu, b_glu=b_glu, w_ssm_out=w_ssm_out, w_o=w_o,
             ln1_g=ln1_g, ln1_b=ln1_b, w_router=w_router, router_bias=router_bias,
             w_gate=w_gate, w_up=w_up, w_down=w_down,
             ws_gate=ws_gate, ws_up=ws_up, ws_down=ws_down, ln2_g=ln2_g, ln2_b=ln2_b)
    depth = w_in.shape[0]
    alpha = (2.0 * depth) ** 0.25
    bsz = x_prompt.shape[0]
    zero_conv = jnp.zeros((bsz,) + state_conv.shape[2:], x_prompt.dtype)
    zero_ssm = jnp.zeros((bsz,) + state_ssm_re.shape[2:], F32)
    y_p, y_s = x_prompt, x_sample
    st_p, st_s = [], []
    for l in range(depth):
        p = _layer_params(l, w)
        y_s, st, w_bf16 = _trunk_layer(y_s, state_conv[l], state_ssm_re[l], state_ssm_im[l],
                                       p, alpha, None)
        st_s.append(st)
        y_p, st, _ = _trunk_layer(y_p, zero_conv, zero_ssm, zero_ssm, p, alpha, w_bf16)
        st_p.append(st)
    conv_p, re_p, im_p = (jnp.stack(v) for v in zip(*st_p))
    conv_s, re_s, im_s = (jnp.stack(v) for v in zip(*st_s))
    return (y_p, y_s, conv_p, re_p, im_p, conv_s, re_s, im_s)
```

```python
import functools

import jax
import jax.numpy as jnp
from jax import lax
from jax.experimental import pallas as pl
from jax.experimental.pallas import tpu as pltpu

F32 = jnp.float32
BF16 = jnp.bfloat16

LN_EPS = 1e-5
ROUTED_SCALE = 2.5
N_ROUTE_GROUPS = 8
TOPK_GROUPS = 4
TOP_K = 8

SUBLANES = 8
LANES = 128
SSM_BLOCKS = 4
SCAN_UNROLL = 4
SCAN_SPLIT = 1
POST_ROWS = 512
DENSE_EXPERTS_PER_STEP = 4
VMEM_LIMIT = 60 * 1024 * 1024


def _const_spec(shape):
    nd = len(shape)
    return pl.BlockSpec(shape, lambda *_: (0,) * nd, pipeline_mode=pl.Buffered(1))


def _ssm_prep_kernel(are_ref, aim_ref, ldt_ref, br_ref, bi_ref,
                     lr_ref, li_ref, bbr_ref, bbi_ref):
    dt = jnp.exp(ldt_ref[...])
    ar = are_ref[...]
    ai = aim_ref[...]
    mag = jnp.exp(ar * dt)
    lr = mag * jnp.cos(ai * dt)
    li = mag * jnp.sin(ai * dt)
    den = ar * ar + ai * ai
    fr = ((lr - 1.0) * ar + li * ai) / den
    fi = (li * ar - (lr - 1.0) * ai) / den
    lr_ref[...] = lr
    li_ref[...] = li
    br = br_ref[...]
    bi = bi_ref[...]
    bbr_ref[...] = fr * br - fi * bi
    bbi_ref[...] = fr * bi + fi * br


def _ssm_prep(a_re, a_im, log_dt, b_re, b_im):
    g, n = a_re.shape
    h = b_re.shape[-1]
    vec = jax.ShapeDtypeStruct((g, 1, n), F32)
    mat = jax.ShapeDtypeStruct((g, h, n), F32)
    return pl.pallas_call(
        _ssm_prep_kernel,
        out_shape=(vec, vec, mat, mat),
        name="ssm_prep",
    )(a_re.reshape(g, 1, n), a_im.reshape(g, 1, n), log_dt.reshape(g, 1, 1),
      b_re.transpose(0, 2, 1), b_im.transpose(0, 2, 1))


def _block_diag(m, nblk):
    g, p, q = m.shape
    gl = g // nblk
    tiled = jnp.tile(m.reshape(nblk, gl * p, q), (1, 1, gl))
    row = lax.broadcasted_iota(jnp.int32, (gl * p, gl * q), 0) // p
    col = lax.broadcasted_iota(jnp.int32, (gl * p, gl * q), 1) // q
    return jnp.where(row == col, tiled, 0.0)


def _layer_norm(r, g, b):
    mu = jnp.mean(r, axis=-1, keepdims=True)
    d = r - mu
    var = jnp.mean(d * d, axis=-1, keepdims=True)
    return d * lax.rsqrt(var + LN_EPS) * g + b


def _split_bf16(v):
    hi = v.astype(BF16)
    return hi, (v - hi.astype(F32)).astype(BF16)


def _route(scores, biased):
    n_exp, r = scores.shape
    gsz = n_exp // N_ROUTE_GROUPS
    neg = jnp.float32(-jnp.inf)
    rows = []
    for g in range(N_ROUTE_GROUPS):
        v = biased[g * gsz:(g + 1) * gsz, :]
        m1 = jnp.max(v, axis=0, keepdims=True)
        is_max = v == m1
        n_max = jnp.sum(is_max.astype(F32), axis=0, keepdims=True)
        rest = jnp.max(jnp.where(is_max, neg, v), axis=0, keepdims=True)
        rows.append(m1 + jnp.where(n_max >= 2.0, m1, rest))
    gscore = jnp.concatenate(rows, axis=0)
    gidx = lax.broadcasted_iota(jnp.int32, gscore.shape, 0)
    grank = jnp.zeros(gscore.shape, F32)
    for g in range(N_ROUTE_GROUPS):
        sg = gscore[g:g + 1, :]
        beats = (sg > gscore) | ((sg == gscore) & (gidx > g))
        grank = grank + beats.astype(F32)
    gkeep = grank < float(TOPK_GROUPS)
    masked = jnp.concatenate(
        [jnp.where(gkeep[g:g + 1, :], biased[g * gsz:(g + 1) * gsz, :], neg)
         for g in range(N_ROUTE_GROUPS)], axis=0)
    eidx = lax.broadcasted_iota(jnp.int32, masked.shape, 0).astype(F32)
    left = masked
    for _ in range(TOP_K):
        top = jnp.max(left, axis=0, keepdims=True)
        first = jnp.min(jnp.where(left == top, eidx, float(n_exp)), axis=0, keepdims=True)
        left = jnp.where(eidx == first, neg, left)
    w = jnp.where(left != masked, scores, 0.0)
    return w / jnp.sum(w, axis=0, keepdims=True) * ROUTED_SCALE


def _mixer_kernel(alpha, batch, steps, sub,
                  x_ref, cprev_ref, h0r_ref, h0i_ref,
                  win_ref, bin_ref, convw_ref, wco_ref,
                  lamr_ref, lami_ref, bbd_ref, cbd_ref, dskip_ref,
                  wglu_ref, bglu_ref, wso_ref, wo_ref, ln1g_ref, ln1b_ref,
                  wrt_ref, rbias_ref,
                  x1_ref, comb_ref, cnew_ref, hr_ref, hi_ref, *rest):
    if sub:
        combt_ref, cnt_ref, *rest = rest
    ubuf, xk_ref, ys_ref, *rest = rest
    rows = batch * steps
    d = x_ref.shape[-1]
    d_blk = d // SSM_BLOCKS
    n_blk = lamr_ref.shape[1] // SSM_BLOCKS
    i = pl.program_id(0)

    @pl.when(i == 0)
    def _init():
        ubuf[0:2 * batch, :] = cprev_ref[...]
        hr_ref[...] = h0r_ref[...]
        hi_ref[...] = h0i_ref[...]

    if len(x_ref.shape) == 3:
        xs_ref, = rest
        for b in range(batch):
            for c in range(d // LANES):
                xs_ref[c, pl.ds(b, steps, stride=batch), :] = x_ref[b, :, c * LANES:(c + 1) * LANES]
        x = jnp.concatenate([xs_ref[c] for c in range(d // LANES)], axis=1)
    else:
        x = x_ref[...]
    xb = x.astype(BF16)

    def proj(c):
        cols = slice(c * d, (c + 1) * d)
        return (jnp.dot(xb, win_ref[:, cols], preferred_element_type=F32)
                + bin_ref[:, cols])

    u = proj(1) * proj(2)
    ubuf[2 * batch:2 * batch + rows, :] = u
    conv = (convw_ref[0:1, :] * ubuf[0:rows, :]
            + convw_ref[1:2, :] * ubuf[batch:batch + rows, :]
            + convw_ref[2:3, :] * u)
    ya = jnp.dot((proj(0) * conv).astype(BF16), wco_ref[...],
                 preferred_element_type=F32)
    tail = ubuf[rows:rows + 2 * batch, :]
    ubuf[0:2 * batch, :] = tail
    cnew_ref[...] = tail
    ya_rows = slice(2 * batch, 2 * batch + rows)
    ubuf[ya_rows, :] = ya

    us = proj(3)
    half = n_blk // SCAN_SPLIT
    unroll = SCAN_UNROLL if steps % SCAN_UNROLL == 0 else 1
    for k in range(SSM_BLOCKS):
        usk = us[:, k * d_blk:(k + 1) * d_blk].astype(BF16)
        xk_ref[...] = jnp.dot(usk, bbd_ref[k], preferred_element_type=F32)
        for hf in range(SCAN_SPLIT):
            st = slice(k * n_blk + hf * half, k * n_blk + (hf + 1) * half)
            re = slice(hf * half, (hf + 1) * half)
            im = slice(n_blk + hf * half, n_blk + (hf + 1) * half)
            if steps == 1:
                lr = lamr_ref[:, st]
                li = lami_ref[:, st]
                hr = hr_ref[:, st]
                hi = hi_ref[:, st]
                nhr = lr * hr - li * hi + xk_ref[:, re]
                nhi = lr * hi + li * hr + xk_ref[:, im]
                xk_ref[:, re] = nhr
                xk_ref[:, im] = nhi
                hr_ref[:, st] = nhr
                hi_ref[:, st] = nhi
            else:
                lr = jnp.broadcast_to(lamr_ref[:, st], (SUBLANES, half))
                li = jnp.broadcast_to(lami_ref[:, st], (SUBLANES, half))
                for s in range(batch // SUBLANES):
                    grp = slice(s * SUBLANES, (s + 1) * SUBLANES)

                    def step(tt, carry, s=s, re=re, im=im, lr=lr, li=li):
                        hr, hi = carry
                        for k_un in range(unroll):
                            row = pl.multiple_of((tt * unroll + k_un) * batch + s * SUBLANES,
                                                 SUBLANES)
                            rs = pl.ds(row, SUBLANES)
                            hr, hi = (lr * hr - li * hi + xk_ref[rs, re],
                                      lr * hi + li * hr + xk_ref[rs, im])
                            xk_ref[rs, re] = hr
                            xk_ref[rs, im] = hi
                        return hr, hi

                    hr, hi = lax.fori_loop(0, steps // unroll, step,
                                           (hr_ref[grp, st], hi_ref[grp, st]))
                    hr_ref[grp, st] = hr
                    hi_ref[grp, st] = hi
        ys_ref[:, k * d_blk:(k + 1) * d_blk] = jnp.dot(
            xk_ref[...].astype(BF16), cbd_ref[k], preferred_element_type=F32)
    ys_ref[...] = ys_ref[...] + dskip_ref[...] * us

    chunk = min(rows, POST_ROWS)
    for q in range(rows // chunk):
        rq = slice(q * chunk, (q + 1) * chunk)
        if len(x_ref.shape) == 3:
            xq = jnp.concatenate([xs_ref[c, rq, :] for c in range(d // LANES)], axis=1)
        else:
            xq = x_ref[rq, :]
        xbq = xq.astype(BF16)

        def projq(c, xbq=xbq):
            cols = slice(c * d, (c + 1) * d)
            return (jnp.dot(xbq, win_ref[:, cols], preferred_element_type=F32)
                    + bin_ref[:, cols])

        z = jax.nn.gelu(ys_ref[rq, :])
        gate = (jnp.dot(z.astype(BF16), wglu_ref[...], preferred_element_type=F32)
                + bglu_ref[...])
        glu = z * jax.nn.sigmoid(gate)
        yb = jnp.dot(glu.astype(BF16), wso_ref[...], preferred_element_type=F32)

        m = (jax.nn.sigmoid(projq(4)) * ubuf[2 * batch + q * chunk:2 * batch + (q + 1) * chunk, :]
             + jax.nn.sigmoid(projq(5)) * yb)
        o = jnp.dot(m.astype(BF16), wo_ref[...], preferred_element_type=F32)
        x1 = _layer_norm(alpha * xq + o, ln1g_ref[...], ln1b_ref[...])
        x1_ref[rq, :] = x1

        nt = (((1,), (1,)), ((), ()))
        w_hi, w_lo = _split_bf16(wrt_ref[...])
        x_hi, x_lo = _split_bf16(x1)
        logits = (lax.dot_general(w_hi, x_hi, nt, preferred_element_type=F32)
                  + lax.dot_general(w_hi, x_lo, nt, preferred_element_type=F32)
                  + lax.dot_general(w_lo, x_hi, nt, preferred_element_type=F32))
        scores = jax.nn.sigmoid(logits)
        comb = _route(scores, scores + rbias_ref[...])
        comb_ref[rq, :] = comb.T
        if sub:
            combt_ref[:, rq] = comb
    if sub:
        ones = jnp.ones((SUBLANES, sub), BF16)
        for s in range(rows // sub):
            sel = jnp.where(combt_ref[:, s * sub:(s + 1) * sub] != 0.0, 1.0, 0.0).astype(BF16)
            cnt_ref[s] = lax.dot_general(ones, sel, (((1,), (1,)), ((), ())),
                                         preferred_element_type=F32)


def _mixer(x, cprev, h0r, h0i, p, *, alpha, batch, steps, sub=0):
    d = x.shape[-1]
    total = x.size // d
    rows = batch * steps
    assert not sub or rows % sub == 0
    n_state = h0r.shape[1]
    n_exp = p['w_router_t'].shape[0]
    grid = (total // rows,)
    if x.ndim == 3:
        x_spec = pl.BlockSpec((batch, steps, d), lambda i: (0, i, 0))
    else:
        x_spec = pl.BlockSpec((rows, d), lambda i: (i, 0))
    consts = [p['w_in'], p['b_in'], p['conv_w'], p['w_conv_out'],
              p['lam_r'], p['lam_i'], p['bbd'], p['cbd'], p['ssm_d'],
              p['w_glu'], p['b_glu'], p['w_ssm_out'], p['w_o'], p['ln1_g'], p['ln1_b'],
              p['w_router_t'], p['router_bias']]
    in_specs = ([x_spec,
                 _const_spec(cprev.shape), _const_spec(h0r.shape), _const_spec(h0i.shape)]
                + [_const_spec(c.shape) for c in consts])
    out_shape = (jax.ShapeDtypeStruct((total, d), F32),
                 jax.ShapeDtypeStruct((total, n_exp), F32),
                 jax.ShapeDtypeStruct((2 * batch, d), F32),
                 jax.ShapeDtypeStruct((batch, n_state), F32),
                 jax.ShapeDtypeStruct((batch, n_state), F32))
    out_specs = (pl.BlockSpec((rows, d), lambda i: (i, 0)),
                 pl.BlockSpec((rows, n_exp), lambda i: (i, 0)),
                 pl.BlockSpec((2 * batch, d), lambda i: (0, 0)),
                 pl.BlockSpec((batch, n_state), lambda i: (0, 0)),
                 pl.BlockSpec((batch, n_state), lambda i: (0, 0)))
    if sub:
        out_shape += (jax.ShapeDtypeStruct((n_exp, total), F32),
                      jax.ShapeDtypeStruct((total // sub, SUBLANES, n_exp), F32))
        out_specs += (pl.BlockSpec((n_exp, rows), lambda i: (0, i)),
                      pl.BlockSpec((rows // sub, SUBLANES, n_exp), lambda i: (i, 0, 0)))
    scratch = [pltpu.VMEM((rows + 2 * batch, d), F32),
               pltpu.VMEM((rows, 2 * n_state // SSM_BLOCKS), F32),
               pltpu.VMEM((rows, d), F32)]
    if x.ndim == 3:
        scratch.append(pltpu.VMEM((d // LANES, rows, LANES), F32))
    return pl.pallas_call(
        functools.partial(_mixer_kernel, alpha, batch, steps, sub),
        grid=grid, in_specs=in_specs, out_specs=out_specs, out_shape=out_shape,
        scratch_shapes=scratch,
        compiler_params=pltpu.CompilerParams(
            dimension_semantics=("arbitrary",), vmem_limit_bytes=VMEM_LIMIT),
        name="mixer",
    )(x, cprev, h0r, h0i, *consts)


def _swiglu(xb, wg, wu):
    g = jnp.dot(xb, wg.astype(BF16), preferred_element_type=F32)
    u = jnp.dot(xb, wu.astype(BF16), preferred_element_type=F32)
    return jax.nn.silu(g) * u


def _moe_kernel(alpha, emit_bf16, x1_ref, comb_ref, wg_ref, wu_ref, wd_ref,
                wsg_ref, wsu_ref, wsd_ref, g_ref, b_ref, out_ref, *rest):
    if emit_bf16:
        wgb_ref, wub_ref, wdb_ref, xb_ref, acc_ref = rest
    else:
        xb_ref, acc_ref = rest
    e = pl.program_id(1)

    @pl.when(e == 0)
    def _shared():
        xb = x1_ref[...].astype(BF16)
        xb_ref[...] = xb
        hs = _swiglu(xb, wsg_ref[...], wsu_ref[...])
        acc_ref[...] = jnp.dot(hs.astype(BF16), wsd_ref[...].astype(BF16),
                               preferred_element_type=F32)

    comb = comb_ref[...]
    lane = lax.broadcasted_iota(jnp.int32, comb.shape, 1)
    xb = xb_ref[...]
    acc = acc_ref[...]
    for k in range(wg_ref.shape[0]):
        wg = wg_ref[k].astype(BF16)
        wu = wu_ref[k].astype(BF16)
        wd = wd_ref[k].astype(BF16)
        if emit_bf16:
            wgb_ref[k] = wg
            wub_ref[k] = wu
            wdb_ref[k] = wd
        c = jnp.sum(jnp.where(lane == e * wg_ref.shape[0] + k, comb, 0.0), axis=1, keepdims=True)
        h = _swiglu(xb, wg, wu) * c
        acc = acc + jnp.dot(h.astype(BF16), wd, preferred_element_type=F32)
    acc_ref[...] = acc

    @pl.when(e == pl.num_programs(1) - 1)
    def _finish():
        out_ref[...] = _layer_norm(alpha * x1_ref[...] + acc_ref[...],
                                   g_ref[...], b_ref[...])


def _moe(x1, comb, p, *, alpha, tile):
    total, d = x1.shape
    n_exp, _, f = p['w_gate'].shape
    fs = p['ws_gate'].shape[1]
    eg = DENSE_EXPERTS_PER_STEP
    grid = (total // tile, n_exp // eg)
    emit_bf16 = grid[0] == 1
    up_spec = pl.BlockSpec((eg, d, f), lambda i, e: (e, 0, 0))
    down_spec = pl.BlockSpec((eg, f, d), lambda i, e: (e, 0, 0))
    in_specs = [pl.BlockSpec((tile, d), lambda i, e: (i, 0)),
                pl.BlockSpec((tile, n_exp), lambda i, e: (i, 0)),
                up_spec, up_spec, down_spec,
                _const_spec((d, fs)), _const_spec((d, fs)), _const_spec((fs, d)),
                _const_spec((1, d)), _const_spec((1, d))]
    out_specs = [pl.BlockSpec((tile, d), lambda i, e: (i, 0))]
    out_shape = [jax.ShapeDtypeStruct((total, d), F32)]
    if emit_bf16:
        out_specs += [up_spec, up_spec, down_spec]
        out_shape += [jax.ShapeDtypeStruct(p[k].shape, BF16)
                      for k in ('w_gate', 'w_up', 'w_down')]
    res = pl.pallas_call(
        functools.partial(_moe_kernel, alpha, emit_bf16),
        grid=grid, in_specs=in_specs, out_specs=out_specs, out_shape=out_shape,
        scratch_shapes=[pltpu.VMEM((tile, d), BF16), pltpu.VMEM((tile, d), F32)],
        compiler_params=pltpu.CompilerParams(
            dimension_semantics=("arbitrary", "arbitrary"), vmem_limit_bytes=VMEM_LIMIT),
        name="moe",
    )(x1, comb, p['w_gate'], p['w_up'], p['w_down'],
      p['ws_gate'], p['ws_up'], p['ws_down'], p['ln2_g'], p['ln2_b'])
    if emit_bf16:
        return res[0], tuple(res[1:])
    return res[0], tuple(p[k].astype(BF16) for k in ('w_gate', 'w_up', 'w_down'))


SUB = 256
RUN_ALIGN = 16
FFN_CHUNK = 192
FFN_TIGHT = 176
SEL_CHUNK = 512
PACK_STATIC = 4
PACK_FULL = 2
DISPATCH_SUBTILES = 4
EXPERTS_PER_STEP = 4


def _expert_onehot(r, off, cnt):
    hit = (r >= off) & (r < off + cnt)
    return hit, jnp.where(hit, off + 1.0, 0.0)


def _moe_sparse_kernel(alpha, batch, n_sub, cap, eg,
                       off_s, pc_s,
                       x1_ref, comb_ref, combt_ref, offr_ref, cntr_ref, offc_ref, cntc_ref,
                       wg_ref, wu_ref, wd_ref, wsg_ref, wsu_ref, wsd_ref, g_ref, b_ref,
                       out_ref, gbuf, xe, zbuf, *xq):
    i = pl.program_id(0)
    g = pl.program_id(1)
    n_exp = comb_ref.shape[1]
    tri_r = lax.broadcasted_iota(jnp.int32, (SUB, SUB), 0)
    tri_c = lax.broadcasted_iota(jnp.int32, (SUB, SUB), 1)

    def run(j, e):
        idx = (i * n_sub + j) * n_exp + e
        return off_s[idx], pc_s[idx]

    def used_rows(j):
        o, p = run(j, n_exp - 1)
        return o + p

    def rows_at(start, chunks):
        return pl.ds(pl.multiple_of(start, RUN_ALIGN), chunks * RUN_ALIGN)

    rows16 = functools.partial(rows_at, chunks=1)

    @pl.when((i == 0) & (g == 0))
    def _zero():
        xe[...] = jnp.zeros(xe.shape, xe.dtype)
        for buf in xq:
            buf[...] = jnp.zeros(buf.shape, buf.dtype)

    @pl.when(g == 0)
    def _dispatch():
        before = jnp.where(tri_r < tri_c, 1.0, 0.0).astype(BF16)
        for j in range(n_sub):
            tok = slice(j * SUB, (j + 1) * SUB)
            xj = x1_ref[tok, :].astype(BF16)
            sel = combt_ref[:, tok] != 0.0
            pos = jnp.dot(jnp.where(sel, 1.0, 0.0).astype(BF16), before,
                          preferred_element_type=F32)
            posm = jnp.where(sel, pos, -1.0).astype(BF16)
            off = offr_ref[j]
            cnt = cntr_ref[j]
            used = used_rows(j)

            def sort_rows(rc, j=j, xj=xj, posm=posm, off=off, cnt=cnt):
                r = (lax.broadcasted_iota(jnp.int32, (SEL_CHUNK, 1), 0)
                     + rc * SEL_CHUNK).astype(F32)
                hit, start1 = _expert_onehot(r, off, cnt)
                s = jnp.sum(start1, axis=1, keepdims=True)
                q = jnp.where(s > 0.0, r - (s - 1.0), -2.0)
                rank = jnp.dot(jnp.where(hit, 1.0, 0.0).astype(BF16), posm,
                               preferred_element_type=F32)
                pick = jnp.where(rank == q, 1.0, 0.0).astype(BF16)
                gbuf[j, rc * SEL_CHUNK:(rc + 1) * SEL_CHUNK, :] = jnp.dot(
                    pick, xj, preferred_element_type=F32).astype(BF16)

            last = cap // SEL_CHUNK - 1
            for rc in range(last):
                sort_rows(rc)
            pl.when(last * SEL_CHUNK < used)(functools.partial(sort_rows, last))

            @pl.when(last * SEL_CHUNK >= used)
            def _blank(j=j):
                gbuf[j, last * SEL_CHUNK:, :] = jnp.zeros((SEL_CHUNK, gbuf.shape[2]), BF16)

    def ffn_rows(rs, ee):
        xc = xe[rs, :]
        h = (jax.nn.silu(jnp.dot(xc, wg_ref[ee], preferred_element_type=F32))
             * jnp.dot(xc, wu_ref[ee], preferred_element_type=F32))
        xe[rs, :] = jnp.dot(h.astype(BF16), wd_ref[ee], preferred_element_type=F32).astype(BF16)

    def pack(e, buf, static):
        n = jnp.int32(0)
        for j in range(n_sub):
            o, p = run(j, e)
            if static:
                buf[rows_at(n, PACK_STATIC), :] = gbuf[j, rows_at(o, PACK_STATIC), :]
            else:
                def chunk(c, carry, j=j, o=o, n=n):
                    buf[rows16(n + c * RUN_ALIGN), :] = gbuf[j, rows16(o + c * RUN_ALIGN), :]
                    return carry

                lax.fori_loop(0, lax.div(p, RUN_ALIGN), chunk, 0)
            n = n + p
        return n

    def unpack(e, buf, static, keep_next=True):
        n = jnp.int32(0)
        for j in range(n_sub):
            o, p = run(j, e)
            if static and not keep_next:
                gbuf[j, rows_at(o, PACK_STATIC), :] = buf[rows_at(n, PACK_STATIC), :]
            elif static:
                full = PACK_FULL * RUN_ALIGN
                gbuf[j, rows_at(o, PACK_FULL), :] = buf[rows_at(n, PACK_FULL), :]
                rest = PACK_STATIC - PACK_FULL
                dst = rows_at(o + full, rest)
                row = lax.broadcasted_iota(jnp.int32, (rest * RUN_ALIGN, gbuf.shape[2]), 0)
                gbuf[j, dst, :] = jnp.where(row < p - full, buf[rows_at(n + full, rest), :],
                                            gbuf[j, dst, :])
            else:
                def chunk(c, carry, j=j, o=o, n=n):
                    gbuf[j, rows16(o + c * RUN_ALIGN), :] = buf[rows16(n + c * RUN_ALIGN), :]
                    return carry

                lax.fori_loop(0, lax.div(p, RUN_ALIGN), chunk, 0)
            n = n + p

    usual = jnp.bool_(True)
    n_max = jnp.int32(0)
    for ee in range(eg):
        n = jnp.int32(0)
        for j in range(n_sub):
            p = run(j, g * eg + ee)[1]
            usual = usual & (p >= PACK_FULL * RUN_ALIGN) & (p <= PACK_STATIC * RUN_ALIGN)
            n = n + p
        n_max = jnp.maximum(n_max, n)
    tight = usual & (n_max <= FFN_TIGHT)
    usual = usual & (n_max <= FFN_CHUNK)

    def together(m):
        for ee in range(eg):
            pack(g * eg + ee, xq[ee], True)
        xcs = [xq[ee][0:m, :] for ee in range(eg)]
        hs = [(jax.nn.silu(jnp.dot(xc, wg_ref[ee], preferred_element_type=F32))
               * jnp.dot(xc, wu_ref[ee], preferred_element_type=F32)).astype(BF16)
              for ee, xc in enumerate(xcs)]
        outs = [jnp.dot(h, wd_ref[ee], preferred_element_type=F32).astype(BF16)
                for ee, h in enumerate(hs)]
        for ee, o in enumerate(outs):
            xq[ee][0:m, :] = o
        for ee in range(eg):
            unpack(g * eg + ee, xq[ee], True, keep_next=ee == eg - 1)

    pl.when(tight)(functools.partial(together, FFN_TIGHT))
    pl.when(usual & jnp.logical_not(tight))(functools.partial(together, FFN_CHUNK))

    @pl.when(jnp.logical_not(usual))
    def _one_by_one():
        for ee in range(eg):
            e = g * eg + ee
            n = pack(e, xe, False)

            def ffn(k, carry, ee=ee):
                ffn_rows(pl.ds(pl.multiple_of(k * FFN_CHUNK, RUN_ALIGN), FFN_CHUNK), ee)
                return carry

            lax.fori_loop(0, lax.div(n + (FFN_CHUNK - 1), FFN_CHUNK), ffn, 0)
            unpack(e, xe, False)

    @pl.when(g == pl.num_programs(1) - 1)
    def _combine():
        earlier = jnp.where(tri_c < tri_r, 1.0, 0.0).astype(BF16)
        for j in range(n_sub):
            tok = slice(j * SUB, (j + 1) * SUB)
            x = x1_ref[tok, :]
            xb = x.astype(BF16)
            hs = (jax.nn.silu(jnp.dot(xb, wsg_ref[...], preferred_element_type=F32))
                  * jnp.dot(xb, wsu_ref[...], preferred_element_type=F32))
            y = jnp.dot(hs.astype(BF16), wsd_ref[...], preferred_element_type=F32)
            comb = comb_ref[tok, :]
            sel = comb != 0.0
            pos = jnp.dot(earlier, jnp.where(sel, 1.0, 0.0).astype(BF16),
                          preferred_element_type=F32)
            posm = jnp.where(sel, pos, -1.0).astype(BF16)
            c_bf = comb.astype(BF16)
            off = offc_ref[j]
            cnt = cntc_ref[j]
            used = used_rows(j)

            def gather_rows(rc, j=j, posm=posm, c_bf=c_bf, off=off, cnt=cnt):
                r = (lax.broadcasted_iota(jnp.int32, (1, SEL_CHUNK), 1)
                     + rc * SEL_CHUNK).astype(F32)
                hit, start1 = _expert_onehot(r, off, cnt)
                s = jnp.sum(start1, axis=0, keepdims=True)
                q = jnp.where(s > 0.0, r - (s - 1.0), -2.0)
                hb = jnp.where(hit, 1.0, 0.0).astype(BF16)
                pick = jnp.dot(posm, hb, preferred_element_type=F32) == q
                weight = jnp.dot(c_bf, hb, preferred_element_type=F32)
                w = jnp.where(pick, weight, 0.0).astype(BF16)
                rows = gbuf[j, rc * SEL_CHUNK:(rc + 1) * SEL_CHUNK, :]
                return jnp.dot(w, rows, preferred_element_type=F32)

            last = cap // SEL_CHUNK - 1
            for rc in range(last):
                y = y + gather_rows(rc)
            n_lt = x.shape[1] // LANES
            lane_tiles = lambda v: [v[:, c * LANES:(c + 1) * LANES] for c in range(n_lt)]

            def put(v):
                for c, t in enumerate(lane_tiles(v)):
                    zbuf[c] = t

            get = lambda: jnp.concatenate([zbuf[c] for c in range(n_lt)], axis=1)
            put(alpha * x + y)

            @pl.when(last * SEL_CHUNK < used)
            def _tail(gather_rows=gather_rows, put=put, get=get):
                put(get() + gather_rows(last))

            put(_layer_norm(get(), g_ref[...], b_ref[...]))
            t_sub = SUB // batch
            for b in range(batch):
                for c in range(n_lt):
                    out_ref[b, j * t_sub:(j + 1) * t_sub, c * LANES:(c + 1) * LANES] = (
                        zbuf[c, pl.ds(b, t_sub, stride=batch), :])


def _moe_sparse(x1, comb, combt, cnt, p, w_bf16, *, alpha, batch, n_sub, eg=EXPERTS_PER_STEP):
    total, d = x1.shape
    w_gate, w_up, w_down = w_bf16
    n_exp, _, f = w_gate.shape
    fs = p['ws_gate'].shape[1]
    tile = n_sub * SUB
    n_tiles = total // tile
    assert SUB % batch == 0
    cap = -(-(SUB * TOP_K + n_exp * (RUN_ALIGN - 1)) // SEL_CHUNK) * SEL_CHUNK
    xe_rows = -(-(tile + n_sub * (RUN_ALIGN - 1)) // FFN_CHUNK) * FFN_CHUNK

    cnt = cnt[:, 0, :]
    pc = jnp.ceil(cnt / RUN_ALIGN) * RUN_ALIGN
    off = jnp.cumsum(pc, axis=1) - pc
    off_s = off.astype(jnp.int32).reshape(-1)
    pc_s = pc.astype(jnp.int32).reshape(-1)
    offr, cntr = off[:, None, :], cnt[:, None, :]
    offc, cntc = off[:, :, None], cnt[:, :, None]

    row_spec = pl.BlockSpec((n_sub, 1, n_exp), lambda i, g, *_: (i, 0, 0))
    col_spec = pl.BlockSpec((n_sub, n_exp, 1), lambda i, g, *_: (i, 0, 0))
    const = lambda shape: pl.BlockSpec(shape, lambda i, g, *_: (0,) * len(shape),
                                       pipeline_mode=pl.Buffered(1))
    grid_spec = pltpu.PrefetchScalarGridSpec(
        num_scalar_prefetch=2,
        grid=(n_tiles, n_exp // eg),
        in_specs=[pl.BlockSpec((tile, d), lambda i, g, *_: (i, 0), pipeline_mode=pl.Buffered(1)),
                  pl.BlockSpec((tile, n_exp), lambda i, g, *_: (i, 0)),
                  pl.BlockSpec((n_exp, tile), lambda i, g, *_: (0, i)),
                  row_spec, row_spec, col_spec, col_spec,
                  pl.BlockSpec((eg, d, f), lambda i, g, *_: (g, 0, 0)),
                  pl.BlockSpec((eg, d, f), lambda i, g, *_: (g, 0, 0)),
                  pl.BlockSpec((eg, f, d), lambda i, g, *_: (g, 0, 0)),
                  const((d, fs)), const((d, fs)), const((fs, d)),
                  const((1, d)), const((1, d))],
        out_specs=pl.BlockSpec((batch, tile // batch, d), lambda i, g, *_: (0, i, 0),
                               pipeline_mode=pl.Buffered(1)),
        scratch_shapes=[pltpu.VMEM((n_sub, cap, d), BF16), pltpu.VMEM((xe_rows, d), BF16),
                        pltpu.VMEM((d // LANES, SUB, LANES), F32)]
        + [pltpu.VMEM((FFN_CHUNK + PACK_STATIC * RUN_ALIGN, d), BF16)] * eg)
    return pl.pallas_call(
        functools.partial(_moe_sparse_kernel, alpha, batch, n_sub, cap, eg),
        grid_spec=grid_spec,
        out_shape=jax.ShapeDtypeStruct((batch, total // batch, d), F32),
        compiler_params=pltpu.CompilerParams(
            dimension_semantics=("arbitrary", "arbitrary"), vmem_limit_bytes=VMEM_LIMIT),
        name="moe_sparse",
    )(off_s, pc_s, x1, comb, combt, offr, cntr, offc, cntc, w_gate, w_up, w_down,
      p['ws_gate'].astype(BF16), p['ws_up'].astype(BF16), p['ws_down'].astype(BF16),
      p['ln2_g'], p['ln2_b'])


def _layer_params(l, w):
    g, n = w['a_re'].shape[1:]
    lr, li, bbr, bbi = _ssm_prep(w['a_re'][l], w['a_im'][l], w['log_dt'][l],
                                 w['ssm_b_re'][l], w['ssm_b_im'][l])
    bbd = jnp.concatenate([_block_diag(bbr, SSM_BLOCKS), _block_diag(bbi, SSM_BLOCKS)],
                          axis=-1).astype(BF16)
    c_re = w['ssm_c_re'][l].transpose(0, 2, 1)
    c_im = w['ssm_c_im'][l].transpose(0, 2, 1)
    cbd = jnp.concatenate([_block_diag(c_re, SSM_BLOCKS), _block_diag(-c_im, SSM_BLOCKS)],
                          axis=1).astype(BF16)
    row = lambda v: v.reshape(1, -1)
    return {
        'w_in': w['w_in'][l].astype(BF16), 'b_in': row(w['b_in'][l]),
        'conv_w': w['conv_w'][l], 'w_conv_out': w['w_conv_out'][l].astype(BF16),
        'lam_r': lr.reshape(1, g * n), 'lam_i': li.reshape(1, g * n),
        'bbd': bbd, 'cbd': cbd, 'ssm_d': row(w['ssm_d'][l]),
        'w_glu': w['w_glu'][l].astype(BF16), 'b_glu': row(w['b_glu'][l]),
        'w_ssm_out': w['w_ssm_out'][l].astype(BF16), 'w_o': w['w_o'][l].astype(BF16),
        'ln1_g': row(w['ln1_g'][l]), 'ln1_b': row(w['ln1_b'][l]),
        'w_router_t': w['w_router'][l].T, 'router_bias': w['router_bias'][l].reshape(-1, 1),
        'w_gate': w['w_gate'][l], 'w_up': w['w_up'][l], 'w_down': w['w_down'][l],
        'ws_gate': w['ws_gate'][l], 'ws_up': w['ws_up'][l], 'ws_down': w['ws_down'][l],
        'ln2_g': row(w['ln2_g'][l]), 'ln2_b': row(w['ln2_b'][l]),
    }


def _pick_steps(batch, seq, max_rows):
    steps = max(1, min(seq, max_rows // batch))
    while seq % steps:
        steps -= 1
    return steps


def _pick_tile(total, max_tile):
    tile = min(total, max_tile)
    while total % tile or tile % 16:
        tile -= 16
    return tile


MIXER_ROWS = 512
MOE_TILE = 1024


def _trunk_layer(x, conv_l, re_l, im_l, p, alpha, w_bf16):
    bsz, seq, d = x.shape
    assert bsz % SUBLANES == 0
    steps = _pick_steps(bsz, seq, MIXER_ROWS)
    kw = conv_l.shape[1]
    assert kw == 2
    cprev = conv_l.astype(F32).transpose(1, 0, 2).reshape(kw * bsz, d)
    h0r = re_l.astype(F32).reshape(bsz, -1)
    h0i = im_l.astype(F32).reshape(bsz, -1)
    if ((bsz * steps) % SUB == 0 and (bsz * seq) % (DISPATCH_SUBTILES * SUB) == 0
            and SUB % bsz == 0):
        if w_bf16 is None:
            w_bf16 = tuple(p[k].astype(BF16) for k in ('w_gate', 'w_up', 'w_down'))
        x1, comb, cnew, hr, hi, combt, cnt = _mixer(
            x, cprev, h0r, h0i, p, alpha=alpha, batch=bsz, steps=steps, sub=SUB)
        y = _moe_sparse(x1, comb, combt, cnt, p, w_bf16, alpha=alpha, batch=bsz,
                        n_sub=DISPATCH_SUBTILES)
    else:
        rows = x.transpose(1, 0, 2).reshape(seq * bsz, d)
        x1, comb, cnew, hr, hi = _mixer(rows, cprev, h0r, h0i, p,
                                        alpha=alpha, batch=bsz, steps=steps)
        out, w_bf16 = _moe(x1, comb, p, alpha=alpha, tile=_pick_tile(bsz * seq, MOE_TILE))
        y = out.reshape(seq, bsz, d).transpose(1, 0, 2)
    states = (cnew.reshape(kw, bsz, d).transpose(1, 0, 2),
              hr.reshape(re_l.shape), hi.reshape(im_l.shape))
    return y, states, w_bf16


def kernel(x_prompt, x_sample, state_conv, state_ssm_re, state_ssm_im,
           w_in, b_in, conv_w, w_conv_out, a_re, a_im, log_dt,
           ssm_b_re, ssm_b_im, ssm_c_re, ssm_c_im, ssm_d, w_glu, b_glu, w_ssm_out, w_o,
           ln1_g, ln1_b, w_router, router_bias, w_gate, w_up, w_down,
           ws_gate, ws_up, ws_down, ln2_g, ln2_b):
    w = dict(w_in=w_in, b_in=b_in, conv_w=conv_w, w_conv_out=w_conv_out,
             a_re=a_re, a_im=a_im, log_dt=log_dt,
             ssm_b_re=ssm_b_re, ssm_b_im=ssm_b_im, ssm_c_re=ssm_c_re, ssm_c_im=ssm_c_im,
             ssm_d=ssm_d, w_glu=w_glu, b_glu=b_glu, w_ssm_out=w_ssm_out, w_o=w_o,
             ln1_g=ln1_g, ln1_b=ln1_b, w_router=w_router, router_bias=router_bias,
             w_gate=w_gate, w_up=w_up, w_down=w_down,
             ws_gate=ws_gate, ws_up=ws_up, ws_down=ws_down, ln2_g=ln2_g, ln2_b=ln2_b)
    depth = w_in.shape[0]
    alpha = (2.0 * depth) ** 0.25
    bsz = x_prompt.shape[0]
    zero_conv = jnp.zeros((bsz,) + state_conv.shape[2:], x_prompt.dtype)
    zero_ssm = jnp.zeros((bsz,) + state_ssm_re.shape[2:], F32)
    y_p, y_s = x_prompt, x_sample
    st_p, st_s = [], []
    for l in range(depth):
        p = _layer_params(l, w)
        y_s, st, w_bf16 = _trunk_layer(y_s, state_conv[l], state_ssm_re[l], state_ssm_im[l],
                                       p, alpha, None)
        st_s.append(st)
        y_p, st, _ = _trunk_layer(y_p, zero_conv, zero_ssm, zero_ssm, p, alpha, w_bf16)
        st_p.append(st)
    conv_p, re_p, im_p = (jnp.stack(v) for v in zip(*st_p))
    conv_s, re_s, im_s = (jnp.stack(v) for v in zip(*st_s))
    return (y_p, y_s, conv_p, re_p, im_p, conv_s, re_s, im_s)
```

```python
import functools

import jax
import jax.numpy as jnp
from jax import lax
from jax.experimental import pallas as pl
from jax.experimental.pallas import tpu as pltpu

F32 = jnp.float32
BF16 = jnp.bfloat16

LN_EPS = 1e-5
ROUTED_SCALE = 2.5
N_ROUTE_GROUPS = 8
TOPK_GROUPS = 4
TOP_K = 8

SUBLANES = 8
LANES = 128
SSM_BLOCKS = 4
SCAN_UNROLL = 4
SCAN_SPLIT = 1
POST_ROWS = 512
DENSE_EXPERTS_PER_STEP = 4
VMEM_LIMIT = 60 * 1024 * 1024


def _const_spec(shape):
    nd = len(shape)
    return pl.BlockSpec(shape, lambda *_: (0,) * nd, pipeline_mode=pl.Buffered(1))


def _ssm_prep_kernel(are_ref, aim_ref, ldt_ref, br_ref, bi_ref,
                     lr_ref, li_ref, bbr_ref, bbi_ref):
    dt = jnp.exp(ldt_ref[...])
    ar = are_ref[...]
    ai = aim_ref[...]
    mag = jnp.exp(ar * dt)
    lr = mag * jnp.cos(ai * dt)
    li = mag * jnp.sin(ai * dt)
    den = ar * ar + ai * ai
    fr = ((lr - 1.0) * ar + li * ai) / den
    fi = (li * ar - (lr - 1.0) * ai) / den
    lr_ref[...] = lr
    li_ref[...] = li
    br = br_ref[...]
    bi = bi_ref[...]
    bbr_ref[...] = fr * br - fi * bi
    bbi_ref[...] = fr * bi + fi * br


def _ssm_prep(a_re, a_im, log_dt, b_re, b_im):
    g, n = a_re.shape
    h = b_re.shape[-1]
    vec = jax.ShapeDtypeStruct((g, 1, n), F32)
    mat = jax.ShapeDtypeStruct((g, h, n), F32)
    return pl.pallas_call(
        _ssm_prep_kernel,
        out_shape=(vec, vec, mat, mat),
        name="ssm_prep",
    )(a_re.reshape(g, 1, n), a_im.reshape(g, 1, n), log_dt.reshape(g, 1, 1),
      b_re.transpose(0, 2, 1), b_im.transpose(0, 2, 1))


def _block_diag(m, nblk):
    g, p, q = m.shape
    gl = g // nblk
    tiled = jnp.tile(m.reshape(nblk, gl * p, q), (1, 1, gl))
    row = lax.broadcasted_iota(jnp.int32, (gl * p, gl * q), 0) // p
    col = lax.broadcasted_iota(jnp.int32, (gl * p, gl * q), 1) // q
    return jnp.where(row == col, tiled, 0.0)


def _layer_norm(r, g, b):
    mu = jnp.mean(r, axis=-1, keepdims=True)
    d = r - mu
    var = jnp.mean(d * d, axis=-1, keepdims=True)
    return d * lax.rsqrt(var + LN_EPS) * g + b


def _split_bf16(v):
    hi = v.astype(BF16)
    return hi, (v - hi.astype(F32)).astype(BF16)


def _route(scores, biased):
    n_exp, r = scores.shape
    gsz = n_exp // N_ROUTE_GROUPS
    neg = jnp.float32(-jnp.inf)
    rows = []
    for g in range(N_ROUTE_GROUPS):
        v = biased[g * gsz:(g + 1) * gsz, :]
        m1 = jnp.max(v, axis=0, keepdims=True)
        is_max = v == m1
        n_max = jnp.sum(is_max.astype(F32), axis=0, keepdims=True)
        rest = jnp.max(jnp.where(is_max, neg, v), axis=0, keepdims=True)
        rows.append(m1 + jnp.where(n_max >= 2.0, m1, rest))
    gscore = jnp.concatenate(rows, axis=0)
    gidx = lax.broadcasted_iota(jnp.int32, gscore.shape, 0)
    grank = jnp.zeros(gscore.shape, F32)
    for g in range(N_ROUTE_GROUPS):
        sg = gscore[g:g + 1, :]
        beats = (sg > gscore) | ((sg == gscore) & (gidx > g))
        grank = grank + beats.astype(F32)
    gkeep = grank < float(TOPK_GROUPS)
    masked = jnp.concatenate(
        [jnp.where(gkeep[g:g + 1, :], biased[g * gsz:(g + 1) * gsz, :], neg)
         for g in range(N_ROUTE_GROUPS)], axis=0)
    eidx = lax.broadcasted_iota(jnp.int32, masked.shape, 0).astype(F32)
    left = masked
    for _ in range(TOP_K):
        top = jnp.max(left, axis=0, keepdims=True)
        first = jnp.min(jnp.where(left == top, eidx, float(n_exp)), axis=0, keepdims=True)
        left = jnp.where(eidx == first, neg, left)
    w = jnp.where(left != masked, scores, 0.0)
    return w / jnp.sum(w, axis=0, keepdims=True) * ROUTED_SCALE


def _mixer_kernel(alpha, batch, steps, sub,
                  x_ref, cprev_ref, h0r_ref, h0i_ref,
                  win_ref, bin_ref, convw_ref, wco_ref,
                  lamr_ref, lami_ref, bbd_ref, cbd_ref, dskip_ref,
                  wglu_ref, bglu_ref, wso_ref, wo_ref, ln1g_ref, ln1b_ref,
                  wrt_ref, rbias_ref,
                  x1_ref, comb_ref, cnew_ref, hr_ref, hi_ref, *rest):
    if sub:
        combt_ref, cnt_ref, *rest = rest
    ubuf, xk_ref, ys_ref, *rest = rest
    rows = batch * steps
    d = x_ref.shape[-1]
    d_blk = d // SSM_BLOCKS
    n_blk = lamr_ref.shape[1] // SSM_BLOCKS
    i = pl.program_id(0)

    @pl.when(i == 0)
    def _init():
        ubuf[0:2 * batch, :] = cprev_ref[...]
        hr_ref[...] = h0r_ref[...]
        hi_ref[...] = h0i_ref[...]

    if len(x_ref.shape) == 3:
        xs_ref, = rest
        for b in range(batch):
            for c in range(d // LANES):
                xs_ref[c, pl.ds(b, steps, stride=batch), :] = x_ref[b, :, c * LANES:(c + 1) * LANES]
        x = jnp.concatenate([xs_ref[c] for c in range(d // LANES)], axis=1)
    else:
        x = x_ref[...]
    xb = x.astype(BF16)

    def proj(c):
        cols = slice(c * d, (c + 1) * d)
        return (jnp.dot(xb, win_ref[:, cols], preferred_element_type=F32)
                + bin_ref[:, cols])

    u = proj(1) * proj(2)
    ubuf[2 * batch:2 * batch + rows, :] = u
    conv = (convw_ref[0:1, :] * ubuf[0:rows, :]
            + convw_ref[1:2, :] * ubuf[batch:batch + rows, :]
            + convw_ref[2:3, :] * u)
    ya = jnp.dot((proj(0) * conv).astype(BF16), wco_ref[...],
                 preferred_element_type=F32)
    tail = ubuf[rows:rows + 2 * batch, :]
    ubuf[0:2 * batch, :] = tail
    cnew_ref[...] = tail
    ya_rows = slice(2 * batch, 2 * batch + rows)
    ubuf[ya_rows, :] = ya

    us = proj(3)
    half = n_blk // SCAN_SPLIT
    unroll = SCAN_UNROLL if steps % SCAN_UNROLL == 0 else 1
    for k in range(SSM_BLOCKS):
        usk = us[:, k * d_blk:(k + 1) * d_blk].astype(BF16)
        xk_ref[...] = jnp.dot(usk, bbd_ref[k], preferred_element_type=F32)
        for hf in range(SCAN_SPLIT):
            st = slice(k * n_blk + hf * half, k * n_blk + (hf + 1) * half)
            re = slice(hf * half, (hf + 1) * half)
            im = slice(n_blk + hf * half, n_blk + (hf + 1) * half)
            if steps == 1:
                lr = lamr_ref[:, st]
                li = lami_ref[:, st]
                hr = hr_ref[:, st]
                hi = hi_ref[:, st]
                nhr = lr * hr - li * hi + xk_ref[:, re]
                nhi = lr * hi + li * hr + xk_ref[:, im]
                xk_ref[:, re] = nhr
                xk_ref[:, im] = nhi
                hr_ref[:, st] = nhr
                hi_ref[:, st] = nhi
            else:
                lr = jnp.broadcast_to(lamr_ref[:, st], (SUBLANES, half))
                li = jnp.broadcast_to(lami_ref[:, st], (SUBLANES, half))
                for s in range(batch // SUBLANES):
                    grp = slice(s * SUBLANES, (s + 1) * SUBLANES)

                    def step(tt, carry, s=s, re=re, im=im, lr=lr, li=li):
                        hr, hi = carry
                        for k_un in range(unroll):
                            row = pl.multiple_of((tt * unroll + k_un) * batch + s * SUBLANES,
                                                 SUBLANES)
                            rs = pl.ds(row, SUBLANES)
                            hr, hi = (lr * hr - li * hi + xk_ref[rs, re],
                                      lr * hi + li * hr + xk_ref[rs, im])
                            xk_ref[rs, re] = hr
                            xk_ref[rs, im] = hi
                        return hr, hi

                    hr, hi = lax.fori_loop(0, steps // unroll, step,
                                           (hr_ref[grp, st], hi_ref[grp, st]))
                    hr_ref[grp, st] = hr
                    hi_ref[grp, st] = hi
        ys_ref[:, k * d_blk:(k + 1) * d_blk] = jnp.dot(
            xk_ref[...].astype(BF16), cbd_ref[k], preferred_element_type=F32)
    ys_ref[...] = ys_ref[...] + dskip_ref[...] * us

    chunk = min(rows, POST_ROWS)
    for q in range(rows // chunk):
        rq = slice(q * chunk, (q + 1) * chunk)
        if len(x_ref.shape) == 3:
            xq = jnp.concatenate([xs_ref[c, rq, :] for c in range(d // LANES)], axis=1)
        else:
            xq = x_ref[rq, :]
        xbq = xq.astype(BF16)

        def projq(c, xbq=xbq):
            cols = slice(c * d, (c + 1) * d)
            return (jnp.dot(xbq, win_ref[:, cols], preferred_element_type=F32)
                    + bin_ref[:, cols])

        z = jax.nn.gelu(ys_ref[rq, :])
        gate = (jnp.dot(z.astype(BF16), wglu_ref[...], preferred_element_type=F32)
                + bglu_ref[...])
        glu = z * jax.nn.sigmoid(gate)
        yb = jnp.dot(glu.astype(BF16), wso_ref[...], preferred_element_type=F32)

        m = (jax.nn.sigmoid(projq(4)) * ubuf[2 * batch + q * chunk:2 * batch + (q + 1) * chunk, :]
             + jax.nn.sigmoid(projq(5)) * yb)
        o = jnp.dot(m.astype(BF16), wo_ref[...], preferred_element_type=F32)
        x1 = _layer_norm(alpha * xq + o, ln1g_ref[...], ln1b_ref[...])
        x1_ref[rq, :] = x1

        nt = (((1,), (1,)), ((), ()))
        w_hi, w_lo = _split_bf16(wrt_ref[...])
        x_hi, x_lo = _split_bf16(x1)
        logits = (lax.dot_general(w_hi, x_hi, nt, preferred_element_type=F32)
                  + lax.dot_general(w_hi, x_lo, nt, preferred_element_type=F32)
                  + lax.dot_general(w_lo, x_hi, nt, preferred_element_type=F32))
        scores = jax.nn.sigmoid(logits)
        comb = _route(scores, scores + rbias_ref[...])
        comb_ref[rq, :] = comb.T
        if sub:
            combt_ref[:, rq] = comb
    if sub:
        ones = jnp.ones((SUBLANES, sub), BF16)
        for s in range(rows // sub):
            sel = jnp.where(combt_ref[:, s * sub:(s + 1) * sub] != 0.0, 1.0, 0.0).astype(BF16)
            cnt_ref[s] = lax.dot_general(ones, sel, (((1,), (1,)), ((), ())),
                                         preferred_element_type=F32)


def _mixer(x, cprev, h0r, h0i, p, *, alpha, batch, steps, sub=0):
    d = x.shape[-1]
    total = x.size // d
    rows = batch * steps
    assert not sub or rows % sub == 0
    n_state = h0r.shape[1]
    n_exp = p['w_router_t'].shape[0]
    grid = (total // rows,)
    if x.ndim == 3:
        x_spec = pl.BlockSpec((batch, steps, d), lambda i: (0, i, 0))
    else:
        x_spec = pl.BlockSpec((rows, d), lambda i: (i, 0))
    consts = [p['w_in'], p['b_in'], p['conv_w'], p['w_conv_out'],
              p['lam_r'], p['lam_i'], p['bbd'], p['cbd'], p['ssm_d'],
              p['w_glu'], p['b_glu'], p['w_ssm_out'], p['w_o'], p['ln1_g'], p['ln1_b'],
              p['w_router_t'], p['router_bias']]
    in_specs = ([x_spec,
                 _const_spec(cprev.shape), _const_spec(h0r.shape), _const_spec(h0i.shape)]
                + [_const_spec(c.shape) for c in consts])
    out_shape = (jax.ShapeDtypeStruct((total, d), F32),
                 jax.ShapeDtypeStruct((total, n_exp), F32),
                 jax.ShapeDtypeStruct((2 * batch, d), F32),
                 jax.ShapeDtypeStruct((batch, n_state), F32),
                 jax.ShapeDtypeStruct((batch, n_state), F32))
    out_specs = (pl.BlockSpec((rows, d), lambda i: (i, 0)),
                 pl.BlockSpec((rows, n_exp), lambda i: (i, 0)),
                 pl.BlockSpec((2 * batch, d), lambda i: (0, 0)),
                 pl.BlockSpec((batch, n_state), lambda i: (0, 0)),
                 pl.BlockSpec((batch, n_state), lambda i: (0, 0)))
    if sub:
        out_shape += (jax.ShapeDtypeStruct((n_exp, total), F32),
                      jax.ShapeDtypeStruct((total // sub, SUBLANES, n_exp), F32))
        out_specs += (pl.BlockSpec((n_exp, rows), lambda i: (0, i)),
                      pl.BlockSpec((rows // sub, SUBLANES, n_exp), lambda i: (i, 0, 0)))
    scratch = [pltpu.VMEM((rows + 2 * batch, d), F32),
               pltpu.VMEM((rows, 2 * n_state // SSM_BLOCKS), F32),
               pltpu.VMEM((rows, d), F32)]
    if x.ndim == 3:
        scratch.append(pltpu.VMEM((d // LANES, rows, LANES), F32))
    return pl.pallas_call(
        functools.partial(_mixer_kernel, alpha, batch, steps, sub),
        grid=grid, in_specs=in_specs, out_specs=out_specs, out_shape=out_shape,
        scratch_shapes=scratch,
        compiler_params=pltpu.CompilerParams(
            dimension_semantics=("arbitrary",), vmem_limit_bytes=VMEM_LIMIT),
        name="mixer",
    )(x, cprev, h0r, h0i, *consts)


def _swiglu(xb, wg, wu):
    g = jnp.dot(xb, wg.astype(BF16), preferred_element_type=F32)
    u = jnp.dot(xb, wu.astype(BF16), preferred_element_type=F32)
    return jax.nn.silu(g) * u


def _moe_kernel(alpha, emit_bf16, x1_ref, comb_ref, wg_ref, wu_ref, wd_ref,
                wsg_ref, wsu_ref, wsd_ref, g_ref, b_ref, out_ref, *rest):
    if emit_bf16:
        wgb_ref, wub_ref, wdb_ref, xb_ref, acc_ref = rest
    else:
        xb_ref, acc_ref = rest
    e = pl.program_id(1)

    @pl.when(e == 0)
    def _shared():
        xb = x1_ref[...].astype(BF16)
        xb_ref[...] = xb
        hs = _swiglu(xb, wsg_ref[...], wsu_ref[...])
        acc_ref[...] = jnp.dot(hs.astype(BF16), wsd_ref[...].astype(BF16),
                               preferred_element_type=F32)

    comb = comb_ref[...]
    lane = lax.broadcasted_iota(jnp.int32, comb.shape, 1)
    xb = xb_ref[...]
    acc = acc_ref[...]
    for k in range(wg_ref.shape[0]):
        wg = wg_ref[k].astype(BF16)
        wu = wu_ref[k].astype(BF16)
        wd = wd_ref[k].astype(BF16)
        if emit_bf16:
            wgb_ref[k] = wg
            wub_ref[k] = wu
            wdb_ref[k] = wd
        c = jnp.sum(jnp.where(lane == e * wg_ref.shape[0] + k, comb, 0.0), axis=1, keepdims=True)
        h = _swiglu(xb, wg, wu) * c
        acc = acc + jnp.dot(h.astype(BF16), wd, preferred_element_type=F32)
    acc_ref[...] = acc

    @pl.when(e == pl.num_programs(1) - 1)
    def _finish():
        out_ref[...] = _layer_norm(alpha * x1_ref[...] + acc_ref[...],
                                   g_ref[...], b_ref[...])


def _moe(x1, comb, p, *, alpha, tile):
    total, d = x1.shape
    n_exp, _, f = p['w_gate'].shape
    fs = p['ws_gate'].shape[1]
    eg = DENSE_EXPERTS_PER_STEP
    grid = (total // tile, n_exp // eg)
    emit_bf16 = grid[0] == 1
    up_spec = pl.BlockSpec((eg, d, f), lambda i, e: (e, 0, 0))
    down_spec = pl.BlockSpec((eg, f, d), lambda i, e: (e, 0, 0))
    in_specs = [pl.BlockSpec((tile, d), lambda i, e: (i, 0)),
                pl.BlockSpec((tile, n_exp), lambda i, e: (i, 0)),
                up_spec, up_spec, down_spec,
                _const_spec((d, fs)), _const_spec((d, fs)), _const_spec((fs, d)),
                _const_spec((1, d)), _const_spec((1, d))]
    out_specs = [pl.BlockSpec((tile, d), lambda i, e: (i, 0))]
    out_shape = [jax.ShapeDtypeStruct((total, d), F32)]
    if emit_bf16:
        out_specs += [up_spec, up_spec, down_spec]
        out_shape += [jax.ShapeDtypeStruct(p[k].shape, BF16)
                      for k in ('w_gate', 'w_up', 'w_down')]
    res = pl.pallas_call(
        functools.partial(_moe_kernel, alpha, emit_bf16),
        grid=grid, in_specs=in_specs, out_specs=out_specs, out_shape=out_shape,
        scratch_shapes=[pltpu.VMEM((tile, d), BF16), pltpu.VMEM((tile, d), F32)],
        compiler_params=pltpu.CompilerParams(
            dimension_semantics=("arbitrary", "arbitrary"), vmem_limit_bytes=VMEM_LIMIT),
        name="moe",
    )(x1, comb, p['w_gate'], p['w_up'], p['w_down'],
      p['ws_gate'], p['ws_up'], p['ws_down'], p['ln2_g'], p['ln2_b'])
    if emit_bf16:
        return res[0], tuple(res[1:])
    return res[0], tuple(p[k].astype(BF16) for k in ('w_gate', 'w_up', 'w_down'))


SUB = 256
RUN_ALIGN = 16
FFN_CHUNK = 192
FFN_TIGHT = 176
SEL_CHUNK = 512
PACK_STATIC = 3
PACK_FULL = 2
DISPATCH_SUBTILES = 4
EXPERTS_PER_STEP = 4


def _expert_onehot(r, off, cnt):
    hit = (r >= off) & (r < off + cnt)
    return hit, jnp.where(hit, off + 1.0, 0.0)


def _moe_sparse_kernel(alpha, batch, n_sub, cap, eg,
                       off_s, pc_s,
                       x1_ref, comb_ref, combt_ref, offr_ref, cntr_ref, offc_ref, cntc_ref,
                       wg_ref, wu_ref, wd_ref, wsg_ref, wsu_ref, wsd_ref, g_ref, b_ref,
                       out_ref, gbuf, xe, zbuf, *xq):
    i = pl.program_id(0)
    g = pl.program_id(1)
    n_exp = comb_ref.shape[1]
    tri_r = lax.broadcasted_iota(jnp.int32, (SUB, SUB), 0)
    tri_c = lax.broadcasted_iota(jnp.int32, (SUB, SUB), 1)

    def run(j, e):
        idx = (i * n_sub + j) * n_exp + e
        return off_s[idx], pc_s[idx]

    def used_rows(j):
        o, p = run(j, n_exp - 1)
        return o + p

    def rows_at(start, chunks):
        return pl.ds(pl.multiple_of(start, RUN_ALIGN), chunks * RUN_ALIGN)

    rows16 = functools.partial(rows_at, chunks=1)

    @pl.when((i == 0) & (g == 0))
    def _zero():
        xe[...] = jnp.zeros(xe.shape, xe.dtype)
        for buf in xq:
            buf[...] = jnp.zeros(buf.shape, buf.dtype)

    @pl.when(g == 0)
    def _dispatch():
        before = jnp.where(tri_r < tri_c, 1.0, 0.0).astype(BF16)
        for j in range(n_sub):
            tok = slice(j * SUB, (j + 1) * SUB)
            xj = x1_ref[tok, :].astype(BF16)
            sel = combt_ref[:, tok] != 0.0
            pos = jnp.dot(jnp.where(sel, 1.0, 0.0).astype(BF16), before,
                          preferred_element_type=F32)
            posm = jnp.where(sel, pos, -1.0).astype(BF16)
            off = offr_ref[j]
            cnt = cntr_ref[j]
            used = used_rows(j)

            def sort_rows(rc, j=j, xj=xj, posm=posm, off=off, cnt=cnt):
                r = (lax.broadcasted_iota(jnp.int32, (SEL_CHUNK, 1), 0)
                     + rc * SEL_CHUNK).astype(F32)
                hit, start1 = _expert_onehot(r, off, cnt)
                s = jnp.sum(start1, axis=1, keepdims=True)
                q = jnp.where(s > 0.0, r - (s - 1.0), -2.0)
                rank = jnp.dot(jnp.where(hit, 1.0, 0.0).astype(BF16), posm,
                               preferred_element_type=F32)
                pick = jnp.where(rank == q, 1.0, 0.0).astype(BF16)
                gbuf[j, rc * SEL_CHUNK:(rc + 1) * SEL_CHUNK, :] = jnp.dot(
                    pick, xj, preferred_element_type=F32).astype(BF16)

            last = cap // SEL_CHUNK - 1
            for rc in range(last):
                sort_rows(rc)
            pl.when(last * SEL_CHUNK < used)(functools.partial(sort_rows, last))

            @pl.when(last * SEL_CHUNK >= used)
            def _blank(j=j):
                gbuf[j, last * SEL_CHUNK:, :] = jnp.zeros((SEL_CHUNK, gbuf.shape[2]), BF16)

    def ffn_rows(rs, ee):
        xc = xe[rs, :]
        h = (jax.nn.silu(jnp.dot(xc, wg_ref[ee], preferred_element_type=F32))
             * jnp.dot(xc, wu_ref[ee], preferred_element_type=F32))
        xe[rs, :] = jnp.dot(h.astype(BF16), wd_ref[ee], preferred_element_type=F32).astype(BF16)

    def pack(e, buf, static):
        n = jnp.int32(0)
        for j in range(n_sub):
            o, p = run(j, e)
            if static:
                buf[rows_at(n, PACK_STATIC), :] = gbuf[j, rows_at(o, PACK_STATIC), :]
            else:
                def chunk(c, carry, j=j, o=o, n=n):
                    buf[rows16(n + c * RUN_ALIGN), :] = gbuf[j, rows16(o + c * RUN_ALIGN), :]
                    return carry

                lax.fori_loop(0, lax.div(p, RUN_ALIGN), chunk, 0)
            n = n + p
        return n

    def unpack(e, buf, static, keep_next=True):
        n = jnp.int32(0)
        for j in range(n_sub):
            o, p = run(j, e)
            if static and not keep_next:
                gbuf[j, rows_at(o, PACK_STATIC), :] = buf[rows_at(n, PACK_STATIC), :]
            elif static:
                full = PACK_FULL * RUN_ALIGN
                gbuf[j, rows_at(o, PACK_FULL), :] = buf[rows_at(n, PACK_FULL), :]
                rest = PACK_STATIC - PACK_FULL
                dst = rows_at(o + full, rest)
                row = lax.broadcasted_iota(jnp.int32, (rest * RUN_ALIGN, gbuf.shape[2]), 0)
                gbuf[j, dst, :] = jnp.where(row < p - full, buf[rows_at(n + full, rest), :],
                                            gbuf[j, dst, :])
            else:
                def chunk(c, carry, j=j, o=o, n=n):
                    gbuf[j, rows16(o + c * RUN_ALIGN), :] = buf[rows16(n + c * RUN_ALIGN), :]
                    return carry

                lax.fori_loop(0, lax.div(p, RUN_ALIGN), chunk, 0)
            n = n + p

    usual = jnp.bool_(True)
    n_max = jnp.int32(0)
    for ee in range(eg):
        n = jnp.int32(0)
        for j in range(n_sub):
            p = run(j, g * eg + ee)[1]
            usual = usual & (p >= PACK_FULL * RUN_ALIGN) & (p <= PACK_STATIC * RUN_ALIGN)
            n = n + p
        n_max = jnp.maximum(n_max, n)
    tight = usual & (n_max <= FFN_TIGHT)
    usual = usual & (n_max <= FFN_CHUNK)

    def together(m):
        for ee in range(eg):
            pack(g * eg + ee, xq[ee], True)
        xcs = [xq[ee][0:m, :] for ee in range(eg)]
        hs = [(jax.nn.silu(jnp.dot(xc, wg_ref[ee], preferred_element_type=F32))
               * jnp.dot(xc, wu_ref[ee], preferred_element_type=F32)).astype(BF16)
              for ee, xc in enumerate(xcs)]
        outs = [jnp.dot(h, wd_ref[ee], preferred_element_type=F32).astype(BF16)
                for ee, h in enumerate(hs)]
        for ee, o in enumerate(outs):
            xq[ee][0:m, :] = o
        for ee in range(eg):
            unpack(g * eg + ee, xq[ee], True, keep_next=ee == eg - 1)

    pl.when(tight)(functools.partial(together, FFN_TIGHT))
    pl.when(usual & jnp.logical_not(tight))(functools.partial(together, FFN_CHUNK))

    @pl.when(jnp.logical_not(usual))
    def _one_by_one():
        for ee in range(eg):
            e = g * eg + ee
            n = pack(e, xe, False)

            def ffn(k, carry, ee=ee):
                ffn_rows(pl.ds(pl.multiple_of(k * FFN_CHUNK, RUN_ALIGN), FFN_CHUNK), ee)
                return carry

            lax.fori_loop(0, lax.div(n + (FFN_CHUNK - 1), FFN_CHUNK), ffn, 0)
            unpack(e, xe, False)

    @pl.when(g == pl.num_programs(1) - 1)
    def _combine():
        earlier = jnp.where(tri_c < tri_r, 1.0, 0.0).astype(BF16)
        for j in range(n_sub):
            tok = slice(j * SUB, (j + 1) * SUB)
            x = x1_ref[tok, :]
            xb = x.astype(BF16)
            hs = (jax.nn.silu(jnp.dot(xb, wsg_ref[...], preferred_element_type=F32))
                  * jnp.dot(xb, wsu_ref[...], preferred_element_type=F32))
            y = jnp.dot(hs.astype(BF16), wsd_ref[...], preferred_element_type=F32)
            comb = comb_ref[tok, :]
            sel = comb != 0.0
            pos = jnp.dot(earlier, jnp.where(sel, 1.0, 0.0).astype(BF16),
                          preferred_element_type=F32)
            posm = jnp.where(sel, pos, -1.0).astype(BF16)
            c_bf = comb.astype(BF16)
            off = offc_ref[j]
            cnt = cntc_ref[j]
            used = used_rows(j)

            def gather_rows(rc, j=j, posm=posm, c_bf=c_bf, off=off, cnt=cnt):
                r = (lax.broadcasted_iota(jnp.int32, (1, SEL_CHUNK), 1)
                     + rc * SEL_CHUNK).astype(F32)
                hit, start1 = _expert_onehot(r, off, cnt)
                s = jnp.sum(start1, axis=0, keepdims=True)
                q = jnp.where(s > 0.0, r - (s - 1.0), -2.0)
                hb = jnp.where(hit, 1.0, 0.0).astype(BF16)
                pick = jnp.dot(posm, hb, preferred_element_type=F32) == q
                weight = jnp.dot(c_bf, hb, preferred_element_type=F32)
                w = jnp.where(pick, weight, 0.0).astype(BF16)
                rows = gbuf[j, rc * SEL_CHUNK:(rc + 1) * SEL_CHUNK, :]
                return jnp.dot(w, rows, preferred_element_type=F32)

            last = cap // SEL_CHUNK - 1
            for rc in range(last):
                y = y + gather_rows(rc)
            n_lt = x.shape[1] // LANES
            lane_tiles = lambda v: [v[:, c * LANES:(c + 1) * LANES] for c in range(n_lt)]

            def put(v):
                for c, t in enumerate(lane_tiles(v)):
                    zbuf[c] = t

            get = lambda: jnp.concatenate([zbuf[c] for c in range(n_lt)], axis=1)
            put(alpha * x + y)

            @pl.when(last * SEL_CHUNK < used)
            def _tail(gather_rows=gather_rows, put=put, get=get):
                put(get() + gather_rows(last))

            put(_layer_norm(get(), g_ref[...], b_ref[...]))
            t_sub = SUB // batch
            for b in range(batch):
                for c in range(n_lt):
                    out_ref[b, j * t_sub:(j + 1) * t_sub, c * LANES:(c + 1) * LANES] = (
                        zbuf[c, pl.ds(b, t_sub, stride=batch), :])


def _moe_sparse(x1, comb, combt, cnt, p, w_bf16, *, alpha, batch, n_sub, eg=EXPERTS_PER_STEP):
    total, d = x1.shape
    w_gate, w_up, w_down = w_bf16
    n_exp, _, f = w_gate.shape
    fs = p['ws_gate'].shape[1]
    tile = n_sub * SUB
    n_tiles = total // tile
    assert SUB % batch == 0
    cap = -(-(SUB * TOP_K + n_exp * (RUN_ALIGN - 1)) // SEL_CHUNK) * SEL_CHUNK
    xe_rows = -(-(tile + n_sub * (RUN_ALIGN - 1)) // FFN_CHUNK) * FFN_CHUNK

    cnt = cnt[:, 0, :]
    pc = jnp.ceil(cnt / RUN_ALIGN) * RUN_ALIGN
    off = jnp.cumsum(pc, axis=1) - pc
    off_s = off.astype(jnp.int32).reshape(-1)
    pc_s = pc.astype(jnp.int32).reshape(-1)
    offr, cntr = off[:, None, :], cnt[:, None, :]
    offc, cntc = off[:, :, None], cnt[:, :, None]

    row_spec = pl.BlockSpec((n_sub, 1, n_exp), lambda i, g, *_: (i, 0, 0))
    col_spec = pl.BlockSpec((n_sub, n_exp, 1), lambda i, g, *_: (i, 0, 0))
    const = lambda shape: pl.BlockSpec(shape, lambda i, g, *_: (0,) * len(shape),
                                       pipeline_mode=pl.Buffered(1))
    grid_spec = pltpu.PrefetchScalarGridSpec(
        num_scalar_prefetch=2,
        grid=(n_tiles, n_exp // eg),
        in_specs=[pl.BlockSpec((tile, d), lambda i, g, *_: (i, 0), pipeline_mode=pl.Buffered(1)),
                  pl.BlockSpec((tile, n_exp), lambda i, g, *_: (i, 0)),
                  pl.BlockSpec((n_exp, tile), lambda i, g, *_: (0, i)),
                  row_spec, row_spec, col_spec, col_spec,
                  pl.BlockSpec((eg, d, f), lambda i, g, *_: (g, 0, 0)),
                  pl.BlockSpec((eg, d, f), lambda i, g, *_: (g, 0, 0)),
                  pl.BlockSpec((eg, f, d), lambda i, g, *_: (g, 0, 0)),
                  const((d, fs)), const((d, fs)), const((fs, d)),
                  const((1, d)), const((1, d))],
        out_specs=pl.BlockSpec((batch, tile // batch, d), lambda i, g, *_: (0, i, 0),
                               pipeline_mode=pl.Buffered(1)),
        scratch_shapes=[pltpu.VMEM((n_sub, cap, d), BF16), pltpu.VMEM((xe_rows, d), BF16),
                        pltpu.VMEM((d // LANES, SUB, LANES), F32)]
        + [pltpu.VMEM((FFN_CHUNK + PACK_STATIC * RUN_ALIGN, d), BF16)] * eg)
    return pl.pallas_call(
        functools.partial(_moe_sparse_kernel, alpha, batch, n_sub, cap, eg),
        grid_spec=grid_spec,
        out_shape=jax.ShapeDtypeStruct((batch, total // batch, d), F32),
        compiler_params=pltpu.CompilerParams(
            dimension_semantics=("arbitrary", "arbitrary"), vmem_limit_bytes=VMEM_LIMIT),
        name="moe_sparse",
    )(off_s, pc_s, x1, comb, combt, offr, cntr, offc, cntc, w_gate, w_up, w_down,
      p['ws_gate'].astype(BF16), p['ws_up'].astype(BF16), p['ws_down'].astype(BF16),
      p['ln2_g'], p['ln2_b'])


def _layer_params(l, w):
    g, n = w['a_re'].shape[1:]
    lr, li, bbr, bbi = _ssm_prep(w['a_re'][l], w['a_im'][l], w['log_dt'][l],
                                 w['ssm_b_re'][l], w['ssm_b_im'][l])
    bbd = jnp.concatenate([_block_diag(bbr, SSM_BLOCKS), _block_diag(bbi, SSM_BLOCKS)],
                          axis=-1).astype(BF16)
    c_re = w['ssm_c_re'][l].transpose(0, 2, 1)
    c_im = w['ssm_c_im'][l].transpose(0, 2, 1)
    cbd = jnp.concatenate([_block_diag(c_re, SSM_BLOCKS), _block_diag(-c_im, SSM_BLOCKS)],
                          axis=1).astype(BF16)
    row = lambda v: v.reshape(1, -1)
    return {
        'w_in': w['w_in'][l].astype(BF16), 'b_in': row(w['b_in'][l]),
        'conv_w': w['conv_w'][l], 'w_conv_out': w['w_conv_out'][l].astype(BF16),
        'lam_r': lr.reshape(1, g * n), 'lam_i': li.reshape(1, g * n),
        'bbd': bbd, 'cbd': cbd, 'ssm_d': row(w['ssm_d'][l]),
        'w_glu': w['w_glu'][l].astype(BF16), 'b_glu': row(w['b_glu'][l]),
        'w_ssm_out': w['w_ssm_out'][l].astype(BF16), 'w_o': w['w_o'][l].astype(BF16),
        'ln1_g': row(w['ln1_g'][l]), 'ln1_b': row(w['ln1_b'][l]),
        'w_router_t': w['w_router'][l].T, 'router_bias': w['router_bias'][l].reshape(-1, 1),
        'w_gate': w['w_gate'][l], 'w_up': w['w_up'][l], 'w_down': w['w_down'][l],
        'ws_gate': w['ws_gate'][l], 'ws_up': w['ws_up'][l], 'ws_down': w['ws_down'][l],
        'ln2_g': row(w['ln2_g'][l]), 'ln2_b': row(w['ln2_b'][l]),
    }


def _pick_steps(batch, seq, max_rows):
    steps = max(1, min(seq, max_rows // batch))
    while seq % steps:
        steps -= 1
    return steps


def _pick_tile(total, max_tile):
    tile = min(total, max_tile)
    while total % tile or tile % 16:
        tile -= 16
    return tile


MIXER_ROWS = 512
MOE_TILE = 1024


def _trunk_layer(x, conv_l, re_l, im_l, p, alpha, w_bf16):
    bsz, seq, d = x.shape
    assert bsz % SUBLANES == 0
    steps = _pick_steps(bsz, seq, MIXER_ROWS)
    kw = conv_l.shape[1]
    assert kw == 2
    cprev = conv_l.astype(F32).transpose(1, 0, 2).reshape(kw * bsz, d)
    h0r = re_l.astype(F32).reshape(bsz, -1)
    h0i = im_l.astype(F32).reshape(bsz, -1)
    if ((bsz * steps) % SUB == 0 and (bsz * seq) % (DISPATCH_SUBTILES * SUB) == 0
            and SUB % bsz == 0):
        if w_bf16 is None:
            w_bf16 = tuple(p[k].astype(BF16) for k in ('w_gate', 'w_up', 'w_down'))
        x1, comb, cnew, hr, hi, combt, cnt = _mixer(
            x, cprev, h0r, h0i, p, alpha=alpha, batch=bsz, steps=steps, sub=SUB)
        y = _moe_sparse(x1, comb, combt, cnt, p, w_bf16, alpha=alpha, batch=bsz,
                        n_sub=DISPATCH_SUBTILES)
    else:
        rows = x.transpose(1, 0, 2).reshape(seq * bsz, d)
        x1, comb, cnew, hr, hi = _mixer(rows, cprev, h0r, h0i, p,
                                        alpha=alpha, batch=bsz, steps=steps)
        out, w_bf16 = _moe(x1, comb, p, alpha=alpha, tile=_pick_tile(bsz * seq, MOE_TILE))
        y = out.reshape(seq, bsz, d).transpose(1, 0, 2)
    states = (cnew.reshape(kw, bsz, d).transpose(1, 0, 2),
              hr.reshape(re_l.shape), hi.reshape(im_l.shape))
    return y, states, w_bf16


def kernel(x_prompt, x_sample, state_conv, state_ssm_re, state_ssm_im,
           w_in, b_in, conv_w, w_conv_out, a_re, a_im, log_dt,
           ssm_b_re, ssm_b_im, ssm_c_re, ssm_c_im, ssm_d, w_glu, b_glu, w_ssm_out, w_o,
           ln1_g, ln1_b, w_router, router_bias, w_gate, w_up, w_down,
           ws_gate, ws_up, ws_down, ln2_g, ln2_b):
    w = dict(w_in=w_in, b_in=b_in, conv_w=conv_w, w_conv_out=w_conv_out,
             a_re=a_re, a_im=a_im, log_dt=log_dt,
             ssm_b_re=ssm_b_re, ssm_b_im=ssm_b_im, ssm_c_re=ssm_c_re, ssm_c_im=ssm_c_im,
             ssm_d=ssm_d, w_glu=w_glu, b_glu=b_glu, w_ssm_out=w_ssm_out, w_o=w_o,
             ln1_g=ln1_g, ln1_b=ln1_b, w_router=w_router, router_bias=router_bias,
             w_gate=w_gate, w_up=w_up, w_down=w_down,
             ws_gate=ws_gate, ws_up=ws_up, ws_down=ws_down, ln2_g=ln2_g, ln2_b=ln2_b)
    depth = w_in.shape[0]
    alpha = (2.0 * depth) ** 0.25
    bsz = x_prompt.shape[0]
    zero_conv = jnp.zeros((bsz,) + state_conv.shape[2:], x_prompt.dtype)
    zero_ssm = jnp.zeros((bsz,) + state_ssm_re.shape[2:], F32)
    y_p, y_s = x_prompt, x_sample
    st_p, st_s = [], []
    for l in range(depth):
        p = _layer_params(l, w)
        y_s, st, w_bf16 = _trunk_layer(y_s, state_conv[l], state_ssm_re[l], state_ssm_im[l],
                                       p, alpha, None)
        st_s.append(st)
        y_p, st, _ = _trunk_layer(y_p, zero_conv, zero_ssm, zero_ssm, p, alpha, w_bf16)
        st_p.append(st)
    conv_p, re_p, im_p = (jnp.stack(v) for v in zip(*st_p))
    conv_s, re_s, im_s = (jnp.stack(v) for v in zip(*st_s))
    return (y_p, y_s, conv_p, re_p, im_p, conv_s, re_s, im_s)
```

```python
import functools

import jax
import jax.numpy as jnp
from jax import lax
from jax.experimental import pallas as pl
from jax.experimental.pallas import tpu as pltpu

F32 = jnp.float32
BF16 = jnp.bfloat16

LN_EPS = 1e-5
ROUTED_SCALE = 2.5
N_ROUTE_GROUPS = 8
TOPK_GROUPS = 4
TOP_K = 8

SUBLANES = 8
LANES = 128
SSM_BLOCKS = 4
SCAN_UNROLL = 4
SCAN_SPLIT = 1
POST_ROWS = 512
DENSE_EXPERTS_PER_STEP = 4
VMEM_LIMIT = 60 * 1024 * 1024


def _const_spec(shape):
    nd = len(shape)
    return pl.BlockSpec(shape, lambda *_: (0,) * nd, pipeline_mode=pl.Buffered(1))


def _ssm_prep_kernel(are_ref, aim_ref, ldt_ref, br_ref, bi_ref,
                     lr_ref, li_ref, bbr_ref, bbi_ref):
    dt = jnp.exp(ldt_ref[...])
    ar = are_ref[...]
    ai = aim_ref[...]
    mag = jnp.exp(ar * dt)
    lr = mag * jnp.cos(ai * dt)
    li = mag * jnp.sin(ai * dt)
    den = ar * ar + ai * ai
    fr = ((lr - 1.0) * ar + li * ai) / den
    fi = (li * ar - (lr - 1.0) * ai) / den
    lr_ref[...] = lr
    li_ref[...] = li
    br = br_ref[...]
    bi = bi_ref[...]
    bbr_ref[...] = fr * br - fi * bi
    bbi_ref[...] = fr * bi + fi * br


def _ssm_prep(a_re, a_im, log_dt, b_re, b_im):
    g, n = a_re.shape
    h = b_re.shape[-1]
    vec = jax.ShapeDtypeStruct((g, 1, n), F32)
    mat = jax.ShapeDtypeStruct((g, h, n), F32)
    return pl.pallas_call(
        _ssm_prep_kernel,
        out_shape=(vec, vec, mat, mat),
        name="ssm_prep",
    )(a_re.reshape(g, 1, n), a_im.reshape(g, 1, n), log_dt.reshape(g, 1, 1),
      b_re.transpose(0, 2, 1), b_im.transpose(0, 2, 1))


def _block_diag(m, nblk):
    g, p, q = m.shape
    gl = g // nblk
    tiled = jnp.tile(m.reshape(nblk, gl * p, q), (1, 1, gl))
    row = lax.broadcasted_iota(jnp.int32, (gl * p, gl * q), 0) // p
    col = lax.broadcasted_iota(jnp.int32, (gl * p, gl * q), 1) // q
    return jnp.where(row == col, tiled, 0.0)


def _layer_norm(r, g, b):
    mu = jnp.mean(r, axis=-1, keepdims=True)
    d = r - mu
    var = jnp.mean(d * d, axis=-1, keepdims=True)
    return d * lax.rsqrt(var + LN_EPS) * g + b


def _split_bf16(v):
    hi = v.astype(BF16)
    return hi, (v - hi.astype(F32)).astype(BF16)


def _route(scores, biased):
    n_exp, r = scores.shape
    gsz = n_exp // N_ROUTE_GROUPS
    neg = jnp.float32(-jnp.inf)
    rows = []
    for g in range(N_ROUTE_GROUPS):
        v = biased[g * gsz:(g + 1) * gsz, :]
        m1 = jnp.max(v, axis=0, keepdims=True)
        is_max = v == m1
        n_max = jnp.sum(is_max.astype(F32), axis=0, keepdims=True)
        rest = jnp.max(jnp.where(is_max, neg, v), axis=0, keepdims=True)
        rows.append(m1 + jnp.where(n_max >= 2.0, m1, rest))
    gscore = jnp.concatenate(rows, axis=0)
    gidx = lax.broadcasted_iota(jnp.int32, gscore.shape, 0)
    grank = jnp.zeros(gscore.shape, F32)
    for g in range(N_ROUTE_GROUPS):
        sg = gscore[g:g + 1, :]
        beats = (sg > gscore) | ((sg == gscore) & (gidx > g))
        grank = grank + beats.astype(F32)
    gkeep = grank < float(TOPK_GROUPS)
    masked = jnp.concatenate(
        [jnp.where(gkeep[g:g + 1, :], biased[g * gsz:(g + 1) * gsz, :], neg)
         for g in range(N_ROUTE_GROUPS)], axis=0)
    eidx = lax.broadcasted_iota(jnp.int32, masked.shape, 0).astype(F32)
    left = masked
    for _ in range(TOP_K):
        top = jnp.max(left, axis=0, keepdims=True)
        first = jnp.min(jnp.where(left == top, eidx, float(n_exp)), axis=0, keepdims=True)
        left = jnp.where(eidx == first, neg, left)
    w = jnp.where(left != masked, scores, 0.0)
    return w / jnp.sum(w, axis=0, keepdims=True) * ROUTED_SCALE


def _mixer_kernel(alpha, batch, steps, sub,
                  x_ref, cprev_ref, h0r_ref, h0i_ref,
                  win_ref, bin_ref, convw_ref, wco_ref,
                  lamr_ref, lami_ref, bbd_ref, cbd_ref, dskip_ref,
                  wglu_ref, bglu_ref, wso_ref, wo_ref, ln1g_ref, ln1b_ref,
                  wrt_ref, rbias_ref,
                  x1_ref, comb_ref, cnew_ref, hr_ref, hi_ref, *rest):
    if sub:
        combt_ref, cnt_ref, *rest = rest
    ubuf, xk_ref, ys_ref, *rest = rest
    rows = batch * steps
    d = x_ref.shape[-1]
    d_blk = d // SSM_BLOCKS
    n_blk = lamr_ref.shape[1] // SSM_BLOCKS
    i = pl.program_id(0)

    @pl.when(i == 0)
    def _init():
        ubuf[0:2 * batch, :] = cprev_ref[...]
        hr_ref[...] = h0r_ref[...]
        hi_ref[...] = h0i_ref[...]

    if len(x_ref.shape) == 3:
        xs_ref, = rest
        for b in range(batch):
            for c in range(d // LANES):
                xs_ref[c, pl.ds(b, steps, stride=batch), :] = x_ref[b, :, c * LANES:(c + 1) * LANES]
        x = jnp.concatenate([xs_ref[c] for c in range(d // LANES)], axis=1)
    else:
        x = x_ref[...]
    xb = x.astype(BF16)

    def proj(c):
        cols = slice(c * d, (c + 1) * d)
        return (jnp.dot(xb, win_ref[:, cols], preferred_element_type=F32)
                + bin_ref[:, cols])

    u = proj(1) * proj(2)
    ubuf[2 * batch:2 * batch + rows, :] = u
    conv = (convw_ref[0:1, :] * ubuf[0:rows, :]
            + convw_ref[1:2, :] * ubuf[batch:batch + rows, :]
            + convw_ref[2:3, :] * u)
    ya = jnp.dot((proj(0) * conv).astype(BF16), wco_ref[...],
                 preferred_element_type=F32)
    tail = ubuf[rows:rows + 2 * batch, :]
    ubuf[0:2 * batch, :] = tail
    cnew_ref[...] = tail
    ya_rows = slice(2 * batch, 2 * batch + rows)
    ubuf[ya_rows, :] = ya

    us = proj(3)
    half = n_blk // SCAN_SPLIT
    unroll = SCAN_UNROLL if steps % SCAN_UNROLL == 0 else 1
    for k in range(SSM_BLOCKS):
        xk = xk_ref.at[k % 2]
        usk = us[:, k * d_blk:(k + 1) * d_blk].astype(BF16)
        xk[...] = jnp.dot(usk, bbd_ref[k], preferred_element_type=F32)
        for hf in range(SCAN_SPLIT):
            st = slice(k * n_blk + hf * half, k * n_blk + (hf + 1) * half)
            re = slice(hf * half, (hf + 1) * half)
            im = slice(n_blk + hf * half, n_blk + (hf + 1) * half)
            if steps == 1:
                lr = lamr_ref[:, st]
                li = lami_ref[:, st]
                hr = hr_ref[:, st]
                hi = hi_ref[:, st]
                nhr = lr * hr - li * hi + xk[:, re]
                nhi = lr * hi + li * hr + xk[:, im]
                xk[:, re] = nhr
                xk[:, im] = nhi
                hr_ref[:, st] = nhr
                hi_ref[:, st] = nhi
            else:
                lr = jnp.broadcast_to(lamr_ref[:, st], (SUBLANES, half))
                li = jnp.broadcast_to(lami_ref[:, st], (SUBLANES, half))
                for s in range(batch // SUBLANES):
                    grp = slice(s * SUBLANES, (s + 1) * SUBLANES)

                    def step(tt, carry, s=s, re=re, im=im, lr=lr, li=li, xk=xk):
                        hr, hi = carry
                        for k_un in range(unroll):
                            row = pl.multiple_of((tt * unroll + k_un) * batch + s * SUBLANES,
                                                 SUBLANES)
                            rs = pl.ds(row, SUBLANES)
                            hr, hi = (lr * hr - li * hi + xk[rs, re],
                                      lr * hi + li * hr + xk[rs, im])
                            xk[rs, re] = hr
                            xk[rs, im] = hi
                        return hr, hi

                    hr, hi = lax.fori_loop(0, steps // unroll, step,
                                           (hr_ref[grp, st], hi_ref[grp, st]), unroll=True)
                    hr_ref[grp, st] = hr
                    hi_ref[grp, st] = hi
        ys_ref[:, k * d_blk:(k + 1) * d_blk] = jnp.dot(
            xk[...].astype(BF16), cbd_ref[k], preferred_element_type=F32)
    ys_ref[...] = ys_ref[...] + dskip_ref[...] * us

    chunk = min(rows, POST_ROWS)
    for q in range(rows // chunk):
        rq = slice(q * chunk, (q + 1) * chunk)
        if len(x_ref.shape) == 3:
            xq = jnp.concatenate([xs_ref[c, rq, :] for c in range(d // LANES)], axis=1)
        else:
            xq = x_ref[rq, :]
        xbq = xq.astype(BF16)

        def projq(c, xbq=xbq):
            cols = slice(c * d, (c + 1) * d)
            return (jnp.dot(xbq, win_ref[:, cols], preferred_element_type=F32)
                    + bin_ref[:, cols])

        z = jax.nn.gelu(ys_ref[rq, :])
        gate = (jnp.dot(z.astype(BF16), wglu_ref[...], preferred_element_type=F32)
                + bglu_ref[...])
        glu = z * jax.nn.sigmoid(gate)
        yb = jnp.dot(glu.astype(BF16), wso_ref[...], preferred_element_type=F32)

        m = (jax.nn.sigmoid(projq(4)) * ubuf[2 * batch + q * chunk:2 * batch + (q + 1) * chunk, :]
             + jax.nn.sigmoid(projq(5)) * yb)
        o = jnp.dot(m.astype(BF16), wo_ref[...], preferred_element_type=F32)
        x1 = _layer_norm(alpha * xq + o, ln1g_ref[...], ln1b_ref[...])
        x1_ref[rq, :] = x1

        nt = (((1,), (1,)), ((), ()))
        w_hi, w_lo = _split_bf16(wrt_ref[...])
        x_hi, x_lo = _split_bf16(x1)
        logits = (lax.dot_general(w_hi, x_hi, nt, preferred_element_type=F32)
                  + lax.dot_general(w_hi, x_lo, nt, preferred_element_type=F32)
                  + lax.dot_general(w_lo, x_hi, nt, preferred_element_type=F32))
        scores = jax.nn.sigmoid(logits)
        comb = _route(scores, scores + rbias_ref[...])
        comb_ref[rq, :] = comb.T
        if sub:
            combt_ref[:, rq] = comb
    if sub:
        ones = jnp.ones((SUBLANES, sub), BF16)
        for s in range(rows // sub):
            sel = jnp.where(combt_ref[:, s * sub:(s + 1) * sub] != 0.0, 1.0, 0.0).astype(BF16)
            cnt_ref[s] = lax.dot_general(ones, sel, (((1,), (1,)), ((), ())),
                                         preferred_element_type=F32)


def _mixer(x, cprev, h0r, h0i, p, *, alpha, batch, steps, sub=0):
    d = x.shape[-1]
    total = x.size // d
    rows = batch * steps
    assert not sub or rows % sub == 0
    n_state = h0r.shape[1]
    n_exp = p['w_router_t'].shape[0]
    grid = (total // rows,)
    if x.ndim == 3:
        x_spec = pl.BlockSpec((batch, steps, d), lambda i: (0, i, 0))
    else:
        x_spec = pl.BlockSpec((rows, d), lambda i: (i, 0))
    consts = [p['w_in'], p['b_in'], p['conv_w'], p['w_conv_out'],
              p['lam_r'], p['lam_i'], p['bbd'], p['cbd'], p['ssm_d'],
              p['w_glu'], p['b_glu'], p['w_ssm_out'], p['w_o'], p['ln1_g'], p['ln1_b'],
              p['w_router_t'], p['router_bias']]
    in_specs = ([x_spec,
                 _const_spec(cprev.shape), _const_spec(h0r.shape), _const_spec(h0i.shape)]
                + [_const_spec(c.shape) for c in consts])
    out_shape = (jax.ShapeDtypeStruct((total, d), F32),
                 jax.ShapeDtypeStruct((total, n_exp), F32),
                 jax.ShapeDtypeStruct((2 * batch, d), F32),
                 jax.ShapeDtypeStruct((batch, n_state), F32),
                 jax.ShapeDtypeStruct((batch, n_state), F32))
    out_specs = (pl.BlockSpec((rows, d), lambda i: (i, 0)),
                 pl.BlockSpec((rows, n_exp), lambda i: (i, 0)),
                 pl.BlockSpec((2 * batch, d), lambda i: (0, 0)),
                 pl.BlockSpec((batch, n_state), lambda i: (0, 0)),
                 pl.BlockSpec((batch, n_state), lambda i: (0, 0)))
    if sub:
        out_shape += (jax.ShapeDtypeStruct((n_exp, total), F32),
                      jax.ShapeDtypeStruct((total // sub, SUBLANES, n_exp), F32))
        out_specs += (pl.BlockSpec((n_exp, rows), lambda i: (0, i)),
                      pl.BlockSpec((rows // sub, SUBLANES, n_exp), lambda i: (i, 0, 0)))
    scratch = [pltpu.VMEM((rows + 2 * batch, d), F32),
               pltpu.VMEM((2, rows, 2 * n_state // SSM_BLOCKS), F32),
               pltpu.VMEM((rows, d), F32)]
    if x.ndim == 3:
        scratch.append(pltpu.VMEM((d // LANES, rows, LANES), F32))
    return pl.pallas_call(
        functools.partial(_mixer_kernel, alpha, batch, steps, sub),
        grid=grid, in_specs=in_specs, out_specs=out_specs, out_shape=out_shape,
        scratch_shapes=scratch,
        compiler_params=pltpu.CompilerParams(
            dimension_semantics=("arbitrary",), vmem_limit_bytes=VMEM_LIMIT),
        name="mixer",
    )(x, cprev, h0r, h0i, *consts)


def _swiglu(xb, wg, wu):
    g = jnp.dot(xb, wg.astype(BF16), preferred_element_type=F32)
    u = jnp.dot(xb, wu.astype(BF16), preferred_element_type=F32)
    return jax.nn.silu(g) * u


def _moe_kernel(alpha, emit_bf16, x1_ref, comb_ref, wg_ref, wu_ref, wd_ref,
                wsg_ref, wsu_ref, wsd_ref, g_ref, b_ref, out_ref, *rest):
    if emit_bf16:
        wgb_ref, wub_ref, wdb_ref, xb_ref, acc_ref = rest
    else:
        xb_ref, acc_ref = rest
    e = pl.program_id(1)

    @pl.when(e == 0)
    def _shared():
        xb = x1_ref[...].astype(BF16)
        xb_ref[...] = xb
        hs = _swiglu(xb, wsg_ref[...], wsu_ref[...])
        acc_ref[...] = jnp.dot(hs.astype(BF16), wsd_ref[...].astype(BF16),
                               preferred_element_type=F32)

    comb = comb_ref[...]
    lane = lax.broadcasted_iota(jnp.int32, comb.shape, 1)
    xb = xb_ref[...]
    acc = acc_ref[...]
    for k in range(wg_ref.shape[0]):
        wg = wg_ref[k].astype(BF16)
        wu = wu_ref[k].astype(BF16)
        wd = wd_ref[k].astype(BF16)
        if emit_bf16:
            wgb_ref[k] = wg
            wub_ref[k] = wu
            wdb_ref[k] = wd
        c = jnp.sum(jnp.where(lane == e * wg_ref.shape[0] + k, comb, 0.0), axis=1, keepdims=True)
        h = _swiglu(xb, wg, wu) * c
        acc = acc + jnp.dot(h.astype(BF16), wd, preferred_element_type=F32)
    acc_ref[...] = acc

    @pl.when(e == pl.num_programs(1) - 1)
    def _finish():
        out_ref[...] = _layer_norm(alpha * x1_ref[...] + acc_ref[...],
                                   g_ref[...], b_ref[...])


def _moe(x1, comb, p, *, alpha, tile):
    total, d = x1.shape
    n_exp, _, f = p['w_gate'].shape
    fs = p['ws_gate'].shape[1]
    eg = DENSE_EXPERTS_PER_STEP
    grid = (total // tile, n_exp // eg)
    emit_bf16 = grid[0] == 1
    up_spec = pl.BlockSpec((eg, d, f), lambda i, e: (e, 0, 0))
    down_spec = pl.BlockSpec((eg, f, d), lambda i, e: (e, 0, 0))
    in_specs = [pl.BlockSpec((tile, d), lambda i, e: (i, 0)),
                pl.BlockSpec((tile, n_exp), lambda i, e: (i, 0)),
                up_spec, up_spec, down_spec,
                _const_spec((d, fs)), _const_spec((d, fs)), _const_spec((fs, d)),
                _const_spec((1, d)), _const_spec((1, d))]
    out_specs = [pl.BlockSpec((tile, d), lambda i, e: (i, 0))]
    out_shape = [jax.ShapeDtypeStruct((total, d), F32)]
    if emit_bf16:
        out_specs += [up_spec, up_spec, down_spec]
        out_shape += [jax.ShapeDtypeStruct(p[k].shape, BF16)
                      for k in ('w_gate', 'w_up', 'w_down')]
    res = pl.pallas_call(
        functools.partial(_moe_kernel, alpha, emit_bf16),
        grid=grid, in_specs=in_specs, out_specs=out_specs, out_shape=out_shape,
        scratch_shapes=[pltpu.VMEM((tile, d), BF16), pltpu.VMEM((tile, d), F32)],
        compiler_params=pltpu.CompilerParams(
            dimension_semantics=("arbitrary", "arbitrary"), vmem_limit_bytes=VMEM_LIMIT),
        name="moe",
    )(x1, comb, p['w_gate'], p['w_up'], p['w_down'],
      p['ws_gate'], p['ws_up'], p['ws_down'], p['ln2_g'], p['ln2_b'])
    if emit_bf16:
        return res[0], tuple(res[1:])
    return res[0], tuple(p[k].astype(BF16) for k in ('w_gate', 'w_up', 'w_down'))


SUB = 256
RUN_ALIGN = 16
FFN_CHUNK = 192
FFN_TIGHT = 176
SEL_CHUNK = 512
PACK_STATIC = 4
PACK_FULL = 2
DISPATCH_SUBTILES = 4
EXPERTS_PER_STEP = 4


def _expert_onehot(r, off, cnt):
    hit = (r >= off) & (r < off + cnt)
    return hit, jnp.where(hit, off + 1.0, 0.0)


def _moe_sparse_kernel(alpha, batch, n_sub, cap, eg,
                       off_s, pc_s,
                       x1_ref, comb_ref, combt_ref, offr_ref, cntr_ref, offc_ref, cntc_ref,
                       wg_ref, wu_ref, wd_ref, wsg_ref, wsu_ref, wsd_ref, g_ref, b_ref,
                       out_ref, gbuf, xe, zbuf, *xq):
    i = pl.program_id(0)
    g = pl.program_id(1)
    n_exp = comb_ref.shape[1]
    tri_r = lax.broadcasted_iota(jnp.int32, (SUB, SUB), 0)
    tri_c = lax.broadcasted_iota(jnp.int32, (SUB, SUB), 1)

    def run(j, e):
        idx = (i * n_sub + j) * n_exp + e
        return off_s[idx], pc_s[idx]

    def used_rows(j):
        o, p = run(j, n_exp - 1)
        return o + p

    def rows_at(start, chunks):
        return pl.ds(pl.multiple_of(start, RUN_ALIGN), chunks * RUN_ALIGN)

    rows16 = functools.partial(rows_at, chunks=1)

    @pl.when((i == 0) & (g == 0))
    def _zero():
        xe[...] = jnp.zeros(xe.shape, xe.dtype)
        for buf in xq:
            buf[...] = jnp.zeros(buf.shape, buf.dtype)

    @pl.when(g == 0)
    def _dispatch():
        before = jnp.where(tri_r < tri_c, 1.0, 0.0).astype(BF16)
        for j in range(n_sub):
            tok = slice(j * SUB, (j + 1) * SUB)
            xj = x1_ref[tok, :].astype(BF16)
            sel = combt_ref[:, tok] != 0.0
            pos = jnp.dot(jnp.where(sel, 1.0, 0.0).astype(BF16), before,
                          preferred_element_type=F32)
            posm = jnp.where(sel, pos, -1.0).astype(BF16)
            off = offr_ref[j]
            cnt = cntr_ref[j]
            used = used_rows(j)

            def sort_rows(rc, j=j, xj=xj, posm=posm, off=off, cnt=cnt):
                r = (lax.broadcasted_iota(jnp.int32, (SEL_CHUNK, 1), 0)
                     + rc * SEL_CHUNK).astype(F32)
                hit, start1 = _expert_onehot(r, off, cnt)
                s = jnp.sum(start1, axis=1, keepdims=True)
                q = jnp.where(s > 0.0, r - (s - 1.0), -2.0)
                rank = jnp.dot(jnp.where(hit, 1.0, 0.0).astype(BF16), posm,
                               preferred_element_type=F32)
                pick = jnp.where(rank == q, 1.0, 0.0).astype(BF16)
                gbuf[j, rc * SEL_CHUNK:(rc + 1) * SEL_CHUNK, :] = jnp.dot(
                    pick, xj, preferred_element_type=F32).astype(BF16)

            last = cap // SEL_CHUNK - 1
            for rc in range(last):
                sort_rows(rc)
            pl.when(last * SEL_CHUNK < used)(functools.partial(sort_rows, last))

            @pl.when(last * SEL_CHUNK >= used)
            def _blank(j=j):
                gbuf[j, last * SEL_CHUNK:, :] = jnp.zeros((SEL_CHUNK, gbuf.shape[2]), BF16)

    def ffn_rows(rs, ee):
        xc = xe[rs, :]
        h = (jax.nn.silu(jnp.dot(xc, wg_ref[ee], preferred_element_type=F32))
             * jnp.dot(xc, wu_ref[ee], preferred_element_type=F32))
        xe[rs, :] = jnp.dot(h.astype(BF16), wd_ref[ee], preferred_element_type=F32).astype(BF16)

    def pack(e, buf, static):
        n = jnp.int32(0)
        for j in range(n_sub):
            o, p = run(j, e)
            if static:
                buf[rows_at(n, PACK_STATIC), :] = gbuf[j, rows_at(o, PACK_STATIC), :]
            else:
                def chunk(c, carry, j=j, o=o, n=n):
                    buf[rows16(n + c * RUN_ALIGN), :] = gbuf[j, rows16(o + c * RUN_ALIGN), :]
                    return carry

                lax.fori_loop(0, lax.div(p, RUN_ALIGN), chunk, 0)
            n = n + p
        return n

    def unpack(e, buf, static, keep_next=True):
        n = jnp.int32(0)
        for j in range(n_sub):
            o, p = run(j, e)
            if static and not keep_next:
                gbuf[j, rows_at(o, PACK_STATIC), :] = buf[rows_at(n, PACK_STATIC), :]
            elif static:
                full = PACK_FULL * RUN_ALIGN
                gbuf[j, rows_at(o, PACK_FULL), :] = buf[rows_at(n, PACK_FULL), :]
                rest = PACK_STATIC - PACK_FULL
                dst = rows_at(o + full, rest)
                row = lax.broadcasted_iota(jnp.int32, (rest * RUN_ALIGN, gbuf.shape[2]), 0)
                gbuf[j, dst, :] = jnp.where(row < p - full, buf[rows_at(n + full, rest), :],
                                            gbuf[j, dst, :])
            else:
                def chunk(c, carry, j=j, o=o, n=n):
                    gbuf[j, rows16(o + c * RUN_ALIGN), :] = buf[rows16(n + c * RUN_ALIGN), :]
                    return carry

                lax.fori_loop(0, lax.div(p, RUN_ALIGN), chunk, 0)
            n = n + p

    usual = jnp.bool_(True)
    n_max = jnp.int32(0)
    for ee in range(eg):
        n = jnp.int32(0)
        for j in range(n_sub):
            p = run(j, g * eg + ee)[1]
            usual = usual & (p >= PACK_FULL * RUN_ALIGN) & (p <= PACK_STATIC * RUN_ALIGN)
            n = n + p
        n_max = jnp.maximum(n_max, n)
    tight = usual & (n_max <= FFN_TIGHT)
    usual = usual & (n_max <= FFN_CHUNK)

    def together(m):
        for ee in range(eg):
            pack(g * eg + ee, xq[ee], True)
        xcs = [xq[ee][0:m, :] for ee in range(eg)]
        hs = [(jax.nn.silu(jnp.dot(xc, wg_ref[ee], preferred_element_type=F32))
               * jnp.dot(xc, wu_ref[ee], preferred_element_type=F32)).astype(BF16)
              for ee, xc in enumerate(xcs)]
        outs = [jnp.dot(h, wd_ref[ee], preferred_element_type=F32).astype(BF16)
                for ee, h in enumerate(hs)]
        for ee, o in enumerate(outs):
            xq[ee][0:m, :] = o
        for ee in range(eg):
            unpack(g * eg + ee, xq[ee], True, keep_next=ee == eg - 1)

    pl.when(tight)(functools.partial(together, FFN_TIGHT))
    pl.when(usual & jnp.logical_not(tight))(functools.partial(together, FFN_CHUNK))

    @pl.when(jnp.logical_not(usual))
    def _one_by_one():
        for ee in range(eg):
            e = g * eg + ee
            n = pack(e, xe, False)

            def ffn(k, carry, ee=ee):
                ffn_rows(pl.ds(pl.multiple_of(k * FFN_CHUNK, RUN_ALIGN), FFN_CHUNK), ee)
                return carry

            lax.fori_loop(0, lax.div(n + (FFN_CHUNK - 1), FFN_CHUNK), ffn, 0)
            unpack(e, xe, False)

    @pl.when(g == pl.num_programs(1) - 1)
    def _combine():
        earlier = jnp.where(tri_c < tri_r, 1.0, 0.0).astype(BF16)
        for j in range(n_sub):
            tok = slice(j * SUB, (j + 1) * SUB)
            x = x1_ref[tok, :]
            xb = x.astype(BF16)
            hs = (jax.nn.silu(jnp.dot(xb, wsg_ref[...], preferred_element_type=F32))
                  * jnp.dot(xb, wsu_ref[...], preferred_element_type=F32))
            y = jnp.dot(hs.astype(BF16), wsd_ref[...], preferred_element_type=F32)
            comb = comb_ref[tok, :]
            sel = comb != 0.0
            pos = jnp.dot(earlier, jnp.where(sel, 1.0, 0.0).astype(BF16),
                          preferred_element_type=F32)
            posm = jnp.where(sel, pos, -1.0).astype(BF16)
            c_bf = comb.astype(BF16)
            off = offc_ref[j]
            cnt = cntc_ref[j]
            used = used_rows(j)

            def gather_rows(rc, j=j, posm=posm, c_bf=c_bf, off=off, cnt=cnt):
                r = (lax.broadcasted_iota(jnp.int32, (1, SEL_CHUNK), 1)
                     + rc * SEL_CHUNK).astype(F32)
                hit, start1 = _expert_onehot(r, off, cnt)
                s = jnp.sum(start1, axis=0, keepdims=True)
                q = jnp.where(s > 0.0, r - (s - 1.0), -2.0)
                hb = jnp.where(hit, 1.0, 0.0).astype(BF16)
                pick = jnp.dot(posm, hb, preferred_element_type=F32) == q
                weight = jnp.dot(c_bf, hb, preferred_element_type=F32)
                w = jnp.where(pick, weight, 0.0).astype(BF16)
                rows = gbuf[j, rc * SEL_CHUNK:(rc + 1) * SEL_CHUNK, :]
                return jnp.dot(w, rows, preferred_element_type=F32)

            last = cap // SEL_CHUNK - 1
            for rc in range(last):
                y = y + gather_rows(rc)
            n_lt = x.shape[1] // LANES
            lane_tiles = lambda v: [v[:, c * LANES:(c + 1) * LANES] for c in range(n_lt)]

            def put(v):
                for c, t in enumerate(lane_tiles(v)):
                    zbuf[c] = t

            get = lambda: jnp.concatenate([zbuf[c] for c in range(n_lt)], axis=1)
            put(alpha * x + y)

            @pl.when(last * SEL_CHUNK < used)
            def _tail(gather_rows=gather_rows, put=put, get=get):
                put(get() + gather_rows(last))

            put(_layer_norm(get(), g_ref[...], b_ref[...]))
            t_sub = SUB // batch
            for b in range(batch):
                for c in range(n_lt):
                    out_ref[b, j * t_sub:(j + 1) * t_sub, c * LANES:(c + 1) * LANES] = (
                        zbuf[c, pl.ds(b, t_sub, stride=batch), :])


def _moe_sparse(x1, comb, combt, cnt, p, w_bf16, *, alpha, batch, n_sub, eg=EXPERTS_PER_STEP):
    total, d = x1.shape
    w_gate, w_up, w_down = w_bf16
    n_exp, _, f = w_gate.shape
    fs = p['ws_gate'].shape[1]
    tile = n_sub * SUB
    n_tiles = total // tile
    assert SUB % batch == 0
    cap = -(-(SUB * TOP_K + n_exp * (RUN_ALIGN - 1)) // SEL_CHUNK) * SEL_CHUNK
    xe_rows = -(-(tile + n_sub * (RUN_ALIGN - 1)) // FFN_CHUNK) * FFN_CHUNK

    cnt = cnt[:, 0, :]
    pc = jnp.ceil(cnt / RUN_ALIGN) * RUN_ALIGN
    off = jnp.cumsum(pc, axis=1) - pc
    off_s = off.astype(jnp.int32).reshape(-1)
    pc_s = pc.astype(jnp.int32).reshape(-1)
    offr, cntr = off[:, None, :], cnt[:, None, :]
    offc, cntc = off[:, :, None], cnt[:, :, None]

    row_spec = pl.BlockSpec((n_sub, 1, n_exp), lambda i, g, *_: (i, 0, 0))
    col_spec = pl.BlockSpec((n_sub, n_exp, 1), lambda i, g, *_: (i, 0, 0))
    const = lambda shape: pl.BlockSpec(shape, lambda i, g, *_: (0,) * len(shape),
                                       pipeline_mode=pl.Buffered(1))
    grid_spec = pltpu.PrefetchScalarGridSpec(
        num_scalar_prefetch=2,
        grid=(n_tiles, n_exp // eg),
        in_specs=[pl.BlockSpec((tile, d), lambda i, g, *_: (i, 0), pipeline_mode=pl.Buffered(1)),
                  pl.BlockSpec((tile, n_exp), lambda i, g, *_: (i, 0)),
                  pl.BlockSpec((n_exp, tile), lambda i, g, *_: (0, i)),
                  row_spec, row_spec, col_spec, col_spec,
                  pl.BlockSpec((eg, d, f), lambda i, g, *_: (g, 0, 0)),
                  pl.BlockSpec((eg, d, f), lambda i, g, *_: (g, 0, 0)),
                  pl.BlockSpec((eg, f, d), lambda i, g, *_: (g, 0, 0)),
                  const((d, fs)), const((d, fs)), const((fs, d)),
                  const((1, d)), const((1, d))],
        out_specs=pl.BlockSpec((batch, tile // batch, d), lambda i, g, *_: (0, i, 0),
                               pipeline_mode=pl.Buffered(1)),
        scratch_shapes=[pltpu.VMEM((n_sub, cap, d), BF16), pltpu.VMEM((xe_rows, d), BF16),
                        pltpu.VMEM((d // LANES, SUB, LANES), F32)]
        + [pltpu.VMEM((FFN_CHUNK + PACK_STATIC * RUN_ALIGN, d), BF16)] * eg)
    return pl.pallas_call(
        functools.partial(_moe_sparse_kernel, alpha, batch, n_sub, cap, eg),
        grid_spec=grid_spec,
        out_shape=jax.ShapeDtypeStruct((batch, total // batch, d), F32),
        compiler_params=pltpu.CompilerParams(
            dimension_semantics=("arbitrary", "arbitrary"), vmem_limit_bytes=VMEM_LIMIT),
        name="moe_sparse",
    )(off_s, pc_s, x1, comb, combt, offr, cntr, offc, cntc, w_gate, w_up, w_down,
      p['ws_gate'].astype(BF16), p['ws_up'].astype(BF16), p['ws_down'].astype(BF16),
      p['ln2_g'], p['ln2_b'])


def _layer_params(l, w):
    g, n = w['a_re'].shape[1:]
    lr, li, bbr, bbi = _ssm_prep(w['a_re'][l], w['a_im'][l], w['log_dt'][l],
                                 w['ssm_b_re'][l], w['ssm_b_im'][l])
    bbd = jnp.concatenate([_block_diag(bbr, SSM_BLOCKS), _block_diag(bbi, SSM_BLOCKS)],
                          axis=-1).astype(BF16)
    c_re = w['ssm_c_re'][l].transpose(0, 2, 1)
    c_im = w['ssm_c_im'][l].transpose(0, 2, 1)
    cbd = jnp.concatenate([_block_diag(c_re, SSM_BLOCKS), _block_diag(-c_im, SSM_BLOCKS)],
                          axis=1).astype(BF16)
    row = lambda v: v.reshape(1, -1)
    return {
        'w_in': w['w_in'][l].astype(BF16), 'b_in': row(w['b_in'][l]),
        'conv_w': w['conv_w'][l], 'w_conv_out': w['w_conv_out'][l].astype(BF16),
        'lam_r': lr.reshape(1, g * n), 'lam_i': li.reshape(1, g * n),
        'bbd': bbd, 'cbd': cbd, 'ssm_d': row(w['ssm_d'][l]),
        'w_glu': w['w_glu'][l].astype(BF16), 'b_glu': row(w['b_glu'][l]),
        'w_ssm_out': w['w_ssm_out'][l].astype(BF16), 'w_o': w['w_o'][l].astype(BF16),
        'ln1_g': row(w['ln1_g'][l]), 'ln1_b': row(w['ln1_b'][l]),
        'w_router_t': w['w_router'][l].T, 'router_bias': w['router_bias'][l].reshape(-1, 1),
        'w_gate': w['w_gate'][l], 'w_up': w['w_up'][l], 'w_down': w['w_down'][l],
        'ws_gate': w['ws_gate'][l], 'ws_up': w['ws_up'][l], 'ws_down': w['ws_down'][l],
        'ln2_g': row(w['ln2_g'][l]), 'ln2_b': row(w['ln2_b'][l]),
    }


def _pick_steps(batch, seq, max_rows):
    steps = max(1, min(seq, max_rows // batch))
    while seq % steps:
        steps -= 1
    return steps


def _pick_tile(total, max_tile):
    tile = min(total, max_tile)
    while total % tile or tile % 16:
        tile -= 16
    return tile


MIXER_ROWS = 512
MOE_TILE = 1024


def _trunk_layer(x, conv_l, re_l, im_l, p, alpha, w_bf16):
    bsz, seq, d = x.shape
    assert bsz % SUBLANES == 0
    steps = _pick_steps(bsz, seq, MIXER_ROWS)
    kw = conv_l.shape[1]
    assert kw == 2
    cprev = conv_l.astype(F32).transpose(1, 0, 2).reshape(kw * bsz, d)
    h0r = re_l.astype(F32).reshape(bsz, -1)
    h0i = im_l.astype(F32).reshape(bsz, -1)
    if ((bsz * steps) % SUB == 0 and (bsz * seq) % (DISPATCH_SUBTILES * SUB) == 0
            and SUB % bsz == 0):
        if w_bf16 is None:
            w_bf16 = tuple(p[k].astype(BF16) for k in ('w_gate', 'w_up', 'w_down'))
        x1, comb, cnew, hr, hi, combt, cnt = _mixer(
            x, cprev, h0r, h0i, p, alpha=alpha, batch=bsz, steps=steps, sub=SUB)
        y = _moe_sparse(x1, comb, combt, cnt, p, w_bf16, alpha=alpha, batch=bsz,
                        n_sub=DISPATCH_SUBTILES)
    else:
        rows = x.transpose(1, 0, 2).reshape(seq * bsz, d)
        x1, comb, cnew, hr, hi = _mixer(rows, cprev, h0r, h0i, p,
                                        alpha=alpha, batch=bsz, steps=steps)
        out, w_bf16 = _moe(x1, comb, p, alpha=alpha, tile=_pick_tile(bsz * seq, MOE_TILE))
        y = out.reshape(seq, bsz, d).transpose(1, 0, 2)
    states = (cnew.reshape(kw, bsz, d).transpose(1, 0, 2),
              hr.reshape(re_l.shape), hi.reshape(im_l.shape))
    return y, states, w_bf16


def kernel(x_prompt, x_sample, state_conv, state_ssm_re, state_ssm_im,
           w_in, b_in, conv_w, w_conv_out, a_re, a_im, log_dt,
           ssm_b_re, ssm_b_im, ssm_c_re, ssm_c_im, ssm_d, w_glu, b_glu, w_ssm_out, w_o,
           ln1_g, ln1_b, w_router, router_bias, w_gate, w_up, w_down,
           ws_gate, ws_up, ws_down, ln2_g, ln2_b):
    w = dict(w_in=w_in, b_in=b_in, conv_w=conv_w, w_conv_out=w_conv_out,
             a_re=a_re, a_im=a_im, log_dt=log_dt,
             ssm_b_re=ssm_b_re, ssm_b_im=ssm_b_im, ssm_c_re=ssm_c_re, ssm_c_im=ssm_c_im,
             ssm_d=ssm_d, w_glu=w_glu, b_glu=b_glu, w_ssm_out=w_ssm_out, w_o=w_o,
             ln1_g=ln1_g, ln1_b=ln1_b, w_router=w_router, router_bias=router_bias,
             w_gate=w_gate, w_up=w_up, w_down=w_down,
             ws_gate=ws_gate, ws_up=ws_up, ws_down=ws_down, ln2_g=ln2_g, ln2_b=ln2_b)
    depth = w_in.shape[0]
    alpha = (2.0 * depth) ** 0.25
    bsz = x_prompt.shape[0]
    zero_conv = jnp.zeros((bsz,) + state_conv.shape[2:], x_prompt.dtype)
    zero_ssm = jnp.zeros((bsz,) + state_ssm_re.shape[2:], F32)
    y_p, y_s = x_prompt, x_sample
    st_p, st_s = [], []
    for l in range(depth):
        p = _layer_params(l, w)
        y_s, st, w_bf16 = _trunk_layer(y_s, state_conv[l], state_ssm_re[l], state_ssm_im[l],
                                       p, alpha, None)
        st_s.append(st)
        y_p, st, _ = _trunk_layer(y_p, zero_conv, zero_ssm, zero_ssm, p, alpha, w_bf16)
        st_p.append(st)
    conv_p, re_p, im_p = (jnp.stack(v) for v in zip(*st_p))
    conv_s, re_s, im_s = (jnp.stack(v) for v in zip(*st_s))
    return (y_p, y_s, conv_p, re_p, im_p, conv_s, re_s, im_s)
```

```python
import functools

import jax
import jax.numpy as jnp
from jax import lax
from jax.experimental import pallas as pl
from jax.experimental.pallas import tpu as pltpu

F32 = jnp.float32
BF16 = jnp.bfloat16

LN_EPS = 1e-5
ROUTED_SCALE = 2.5
N_ROUTE_GROUPS = 8
TOPK_GROUPS = 4
TOP_K = 8

SUBLANES = 8
LANES = 128
SSM_BLOCKS = 4
SCAN_UNROLL = 4
SCAN_SPLIT = 1
POST_ROWS = 512
DENSE_EXPERTS_PER_STEP = 4
VMEM_LIMIT = 60 * 1024 * 1024


def _const_spec(shape):
    nd = len(shape)
    return pl.BlockSpec(shape, lambda *_: (0,) * nd, pipeline_mode=pl.Buffered(1))


def _ssm_prep_kernel(are_ref, aim_ref, ldt_ref, br_ref, bi_ref,
                     lr_ref, li_ref, bbr_ref, bbi_ref):
    dt = jnp.exp(ldt_ref[...])
    ar = are_ref[...]
    ai = aim_ref[...]
    mag = jnp.exp(ar * dt)
    lr = mag * jnp.cos(ai * dt)
    li = mag * jnp.sin(ai * dt)
    den = ar * ar + ai * ai
    fr = ((lr - 1.0) * ar + li * ai) / den
    fi = (li * ar - (lr - 1.0) * ai) / den
    lr_ref[...] = lr
    li_ref[...] = li
    br = br_ref[...]
    bi = bi_ref[...]
    bbr_ref[...] = fr * br - fi * bi
    bbi_ref[...] = fr * bi + fi * br


def _ssm_prep(a_re, a_im, log_dt, b_re, b_im):
    g, n = a_re.shape
    h = b_re.shape[-1]
    vec = jax.ShapeDtypeStruct((g, 1, n), F32)
    mat = jax.ShapeDtypeStruct((g, h, n), F32)
    return pl.pallas_call(
        _ssm_prep_kernel,
        out_shape=(vec, vec, mat, mat),
        name="ssm_prep",
    )(a_re.reshape(g, 1, n), a_im.reshape(g, 1, n), log_dt.reshape(g, 1, 1),
      b_re.transpose(0, 2, 1), b_im.transpose(0, 2, 1))


def _block_diag(m, nblk):
    g, p, q = m.shape
    gl = g // nblk
    tiled = jnp.tile(m.reshape(nblk, gl * p, q), (1, 1, gl))
    row = lax.broadcasted_iota(jnp.int32, (gl * p, gl * q), 0) // p
    col = lax.broadcasted_iota(jnp.int32, (gl * p, gl * q), 1) // q
    return jnp.where(row == col, tiled, 0.0)


def _layer_norm(r, g, b):
    mu = jnp.mean(r, axis=-1, keepdims=True)
    d = r - mu
    var = jnp.mean(d * d, axis=-1, keepdims=True)
    return d * lax.rsqrt(var + LN_EPS) * g + b


def _split_bf16(v):
    hi = v.astype(BF16)
    return hi, (v - hi.astype(F32)).astype(BF16)


def _route(scores, biased):
    n_exp, r = scores.shape
    gsz = n_exp // N_ROUTE_GROUPS
    neg = jnp.float32(-jnp.inf)
    rows = []
    for g in range(N_ROUTE_GROUPS):
        v = biased[g * gsz:(g + 1) * gsz, :]
        m1 = jnp.max(v, axis=0, keepdims=True)
        is_max = v == m1
        n_max = jnp.sum(is_max.astype(F32), axis=0, keepdims=True)
        rest = jnp.max(jnp.where(is_max, neg, v), axis=0, keepdims=True)
        rows.append(m1 + jnp.where(n_max >= 2.0, m1, rest))
    gscore = jnp.concatenate(rows, axis=0)
    gidx = lax.broadcasted_iota(jnp.int32, gscore.shape, 0)
    grank = jnp.zeros(gscore.shape, F32)
    for g in range(N_ROUTE_GROUPS):
        sg = gscore[g:g + 1, :]
        beats = (sg > gscore) | ((sg == gscore) & (gidx > g))
        grank = grank + beats.astype(F32)
    gkeep = grank < float(TOPK_GROUPS)
    masked = jnp.concatenate(
        [jnp.where(gkeep[g:g + 1, :], biased[g * gsz:(g + 1) * gsz, :], neg)
         for g in range(N_ROUTE_GROUPS)], axis=0)
    eidx = lax.broadcasted_iota(jnp.int32, masked.shape, 0).astype(F32)
    left = masked
    for _ in range(TOP_K):
        top = jnp.max(left, axis=0, keepdims=True)
        first = jnp.min(jnp.where(left == top, eidx, float(n_exp)), axis=0, keepdims=True)
        left = jnp.where(eidx == first, neg, left)
    w = jnp.where(left != masked, scores, 0.0)
    return w / jnp.sum(w, axis=0, keepdims=True) * ROUTED_SCALE


def _mixer_kernel(alpha, batch, steps, sub,
                  x_ref, cprev_ref, h0r_ref, h0i_ref,
                  win_ref, bin_ref, convw_ref, wco_ref,
                  lamr_ref, lami_ref, bbd_ref, cbd_ref, dskip_ref,
                  wglu_ref, bglu_ref, wso_ref, wo_ref, ln1g_ref, ln1b_ref,
                  wrt_ref, rbias_ref,
                  x1_ref, comb_ref, cnew_ref, hr_ref, hi_ref, *rest):
    if sub:
        combt_ref, cnt_ref, *rest = rest
    ubuf, xk_ref, ys_ref, *rest = rest
    rows = batch * steps
    d = x_ref.shape[-1]
    d_blk = d // SSM_BLOCKS
    n_blk = lamr_ref.shape[1] // SSM_BLOCKS
    i = pl.program_id(0)

    @pl.when(i == 0)
    def _init():
        ubuf[0:2 * batch, :] = cprev_ref[...]
        hr_ref[...] = h0r_ref[...]
        hi_ref[...] = h0i_ref[...]

    if len(x_ref.shape) == 3:
        xs_ref, = rest
        for b in range(batch):
            for c in range(d // LANES):
                xs_ref[c, pl.ds(b, steps, stride=batch), :] = x_ref[b, :, c * LANES:(c + 1) * LANES]
        x = jnp.concatenate([xs_ref[c] for c in range(d // LANES)], axis=1)
    else:
        x = x_ref[...]
    xb = x.astype(BF16)

    def proj(c):
        cols = slice(c * d, (c + 1) * d)
        return (jnp.dot(xb, win_ref[:, cols], preferred_element_type=F32)
                + bin_ref[:, cols])

    u = proj(1) * proj(2)
    ubuf[2 * batch:2 * batch + rows, :] = u
    conv = (convw_ref[0:1, :] * ubuf[0:rows, :]
            + convw_ref[1:2, :] * ubuf[batch:batch + rows, :]
            + convw_ref[2:3, :] * u)
    ya = jnp.dot((proj(0) * conv).astype(BF16), wco_ref[...],
                 preferred_element_type=F32)
    tail = ubuf[rows:rows + 2 * batch, :]
    ubuf[0:2 * batch, :] = tail
    cnew_ref[...] = tail
    ya_rows = slice(2 * batch, 2 * batch + rows)
    ubuf[ya_rows, :] = ya

    us = proj(3)
    half = n_blk // SCAN_SPLIT
    unroll = SCAN_UNROLL if steps % SCAN_UNROLL == 0 else 1
    for k in range(SSM_BLOCKS):
        xk = xk_ref.at[k % 2]
        usk = us[:, k * d_blk:(k + 1) * d_blk].astype(BF16)
        xk[...] = jnp.dot(usk, bbd_ref[k], preferred_element_type=F32)
        for hf in range(SCAN_SPLIT):
            st = slice(k * n_blk + hf * half, k * n_blk + (hf + 1) * half)
            re = slice(hf * half, (hf + 1) * half)
            im = slice(n_blk + hf * half, n_blk + (hf + 1) * half)
            if steps == 1:
                lr = lamr_ref[:, st]
                li = lami_ref[:, st]
                hr = hr_ref[:, st]
                hi = hi_ref[:, st]
                nhr = lr * hr - li * hi + xk[:, re]
                nhi = lr * hi + li * hr + xk[:, im]
                xk[:, re] = nhr
                xk[:, im] = nhi
                hr_ref[:, st] = nhr
                hi_ref[:, st] = nhi
            else:
                lr = jnp.broadcast_to(lamr_ref[:, st], (SUBLANES, half))
                li = jnp.broadcast_to(lami_ref[:, st], (SUBLANES, half))
                for s in range(batch // SUBLANES):
                    grp = slice(s * SUBLANES, (s + 1) * SUBLANES)

                    def step(tt, carry, s=s, re=re, im=im, lr=lr, li=li, xk=xk):
                        hr, hi = carry
                        for k_un in range(unroll):
                            row = pl.multiple_of((tt * unroll + k_un) * batch + s * SUBLANES,
                                                 SUBLANES)
                            rs = pl.ds(row, SUBLANES)
                            hr, hi = (lr * hr - li * hi + xk[rs, re],
                                      lr * hi + li * hr + xk[rs, im])
                            xk[rs, re] = hr
                            xk[rs, im] = hi
                        return hr, hi

                    hr, hi = lax.fori_loop(0, steps // unroll, step,
                                           (hr_ref[grp, st], hi_ref[grp, st]), unroll=True)
                    hr_ref[grp, st] = hr
                    hi_ref[grp, st] = hi
        ys_ref[:, k * d_blk:(k + 1) * d_blk] = jnp.dot(
            xk[...].astype(BF16), cbd_ref[k], preferred_element_type=F32)
    ys_ref[...] = ys_ref[...] + dskip_ref[...] * us

    chunk = min(rows, POST_ROWS)
    for q in range(rows // chunk):
        rq = slice(q * chunk, (q + 1) * chunk)
        if len(x_ref.shape) == 3:
            xq = jnp.concatenate([xs_ref[c, rq, :] for c in range(d // LANES)], axis=1)
        else:
            xq = x_ref[rq, :]
        xbq = xq.astype(BF16)

        def projq(c, xbq=xbq):
            cols = slice(c * d, (c + 1) * d)
            return (jnp.dot(xbq, win_ref[:, cols], preferred_element_type=F32)
                    + bin_ref[:, cols])

        z = jax.nn.gelu(ys_ref[rq, :])
        gate = (jnp.dot(z.astype(BF16), wglu_ref[...], preferred_element_type=F32)
                + bglu_ref[...])
        glu = z * jax.nn.sigmoid(gate)
        yb = jnp.dot(glu.astype(BF16), wso_ref[...], preferred_element_type=F32)

        m = (jax.nn.sigmoid(projq(4)) * ubuf[2 * batch + q * chunk:2 * batch + (q + 1) * chunk, :]
             + jax.nn.sigmoid(projq(5)) * yb)
        o = jnp.dot(m.astype(BF16), wo_ref[...], preferred_element_type=F32)
        x1 = _layer_norm(alpha * xq + o, ln1g_ref[...], ln1b_ref[...])
        x1_ref[rq, :] = x1

        nt = (((1,), (1,)), ((), ()))
        w_hi, w_lo = _split_bf16(wrt_ref[...])
        x_hi, x_lo = _split_bf16(x1)
        logits = (lax.dot_general(w_hi, x_hi, nt, preferred_element_type=F32)
                  + lax.dot_general(w_hi, x_lo, nt, preferred_element_type=F32)
                  + lax.dot_general(w_lo, x_hi, nt, preferred_element_type=F32))
        scores = jax.nn.sigmoid(logits)
        comb = _route(scores, scores + rbias_ref[...])
        comb_ref[rq, :] = comb.T
        if sub:
            combt_ref[:, rq] = comb
    if sub:
        ones = jnp.ones((SUBLANES, sub), BF16)
        for s in range(rows // sub):
            sel = jnp.where(combt_ref[:, s * sub:(s + 1) * sub] != 0.0, 1.0, 0.0).astype(BF16)
            cnt_ref[s] = lax.dot_general(ones, sel, (((1,), (1,)), ((), ())),
                                         preferred_element_type=F32)


def _mixer(x, cprev, h0r, h0i, p, *, alpha, batch, steps, sub=0):
    d = x.shape[-1]
    total = x.size // d
    rows = batch * steps
    assert not sub or rows % sub == 0
    n_state = h0r.shape[1]
    n_exp = p['w_router_t'].shape[0]
    grid = (total // rows,)
    if x.ndim == 3:
        x_spec = pl.BlockSpec((batch, steps, d), lambda i: (0, i, 0))
    else:
        x_spec = pl.BlockSpec((rows, d), lambda i: (i, 0))
    consts = [p['w_in'], p['b_in'], p['conv_w'], p['w_conv_out'],
              p['lam_r'], p['lam_i'], p['bbd'], p['cbd'], p['ssm_d'],
              p['w_glu'], p['b_glu'], p['w_ssm_out'], p['w_o'], p['ln1_g'], p['ln1_b'],
              p['w_router_t'], p['router_bias']]
    in_specs = ([x_spec,
                 _const_spec(cprev.shape), _const_spec(h0r.shape), _const_spec(h0i.shape)]
                + [_const_spec(c.shape) for c in consts])
    out_shape = (jax.ShapeDtypeStruct((total, d), F32),
                 jax.ShapeDtypeStruct((total, n_exp), F32),
                 jax.ShapeDtypeStruct((2 * batch, d), F32),
                 jax.ShapeDtypeStruct((batch, n_state), F32),
                 jax.ShapeDtypeStruct((batch, n_state), F32))
    out_specs = (pl.BlockSpec((rows, d), lambda i: (i, 0)),
                 pl.BlockSpec((rows, n_exp), lambda i: (i, 0)),
                 pl.BlockSpec((2 * batch, d), lambda i: (0, 0)),
                 pl.BlockSpec((batch, n_state), lambda i: (0, 0)),
                 pl.BlockSpec((batch, n_state), lambda i: (0, 0)))
    if sub:
        out_shape += (jax.ShapeDtypeStruct((n_exp, total), F32),
                      jax.ShapeDtypeStruct((total // sub, SUBLANES, n_exp), F32))
        out_specs += (pl.BlockSpec((n_exp, rows), lambda i: (0, i)),
                      pl.BlockSpec((rows // sub, SUBLANES, n_exp), lambda i: (i, 0, 0)))
    scratch = [pltpu.VMEM((rows + 2 * batch, d), F32),
               pltpu.VMEM((2, rows, 2 * n_state // SSM_BLOCKS), F32),
               pltpu.VMEM((rows, d), F32)]
    if x.ndim == 3:
        scratch.append(pltpu.VMEM((d // LANES, rows, LANES), F32))
    return pl.pallas_call(
        functools.partial(_mixer_kernel, alpha, batch, steps, sub),
        grid=grid, in_specs=in_specs, out_specs=out_specs, out_shape=out_shape,
        scratch_shapes=scratch,
        compiler_params=pltpu.CompilerParams(
            dimension_semantics=("arbitrary",), vmem_limit_bytes=VMEM_LIMIT,
            allow_input_fusion=[False] * 4 + [c.dtype == BF16 for c in consts]),
        name="mixer",
    )(x, cprev, h0r, h0i, *consts)


def _swiglu(xb, wg, wu):
    g = jnp.dot(xb, wg.astype(BF16), preferred_element_type=F32)
    u = jnp.dot(xb, wu.astype(BF16), preferred_element_type=F32)
    return jax.nn.silu(g) * u


def _moe_kernel(alpha, emit_bf16, x1_ref, comb_ref, wg_ref, wu_ref, wd_ref,
                wsg_ref, wsu_ref, wsd_ref, g_ref, b_ref, out_ref, *rest):
    if emit_bf16:
        wgb_ref, wub_ref, wdb_ref, xb_ref, acc_ref = rest
    else:
        xb_ref, acc_ref = rest
    e = pl.program_id(1)

    @pl.when(e == 0)
    def _shared():
        xb = x1_ref[...].astype(BF16)
        xb_ref[...] = xb
        hs = _swiglu(xb, wsg_ref[...], wsu_ref[...])
        acc_ref[...] = jnp.dot(hs.astype(BF16), wsd_ref[...].astype(BF16),
                               preferred_element_type=F32)

    comb = comb_ref[...]
    lane = lax.broadcasted_iota(jnp.int32, comb.shape, 1)
    xb = xb_ref[...]
    acc = acc_ref[...]
    for k in range(wg_ref.shape[0]):
        wg = wg_ref[k].astype(BF16)
        wu = wu_ref[k].astype(BF16)
        wd = wd_ref[k].astype(BF16)
        if emit_bf16:
            wgb_ref[k] = wg
            wub_ref[k] = wu
            wdb_ref[k] = wd
        c = jnp.sum(jnp.where(lane == e * wg_ref.shape[0] + k, comb, 0.0), axis=1, keepdims=True)
        h = _swiglu(xb, wg, wu) * c
        acc = acc + jnp.dot(h.astype(BF16), wd, preferred_element_type=F32)
    acc_ref[...] = acc

    @pl.when(e == pl.num_programs(1) - 1)
    def _finish():
        out_ref[...] = _layer_norm(alpha * x1_ref[...] + acc_ref[...],
                                   g_ref[...], b_ref[...])


def _moe(x1, comb, p, *, alpha, tile):
    total, d = x1.shape
    n_exp, _, f = p['w_gate'].shape
    fs = p['ws_gate'].shape[1]
    eg = DENSE_EXPERTS_PER_STEP
    grid = (total // tile, n_exp // eg)
    emit_bf16 = grid[0] == 1
    up_spec = pl.BlockSpec((eg, d, f), lambda i, e: (e, 0, 0))
    down_spec = pl.BlockSpec((eg, f, d), lambda i, e: (e, 0, 0))
    in_specs = [pl.BlockSpec((tile, d), lambda i, e: (i, 0)),
                pl.BlockSpec((tile, n_exp), lambda i, e: (i, 0)),
                up_spec, up_spec, down_spec,
                _const_spec((d, fs)), _const_spec((d, fs)), _const_spec((fs, d)),
                _const_spec((1, d)), _const_spec((1, d))]
    out_specs = [pl.BlockSpec((tile, d), lambda i, e: (i, 0))]
    out_shape = [jax.ShapeDtypeStruct((total, d), F32)]
    if emit_bf16:
        out_specs += [up_spec, up_spec, down_spec]
        out_shape += [jax.ShapeDtypeStruct(p[k].shape, BF16)
                      for k in ('w_gate', 'w_up', 'w_down')]
    res = pl.pallas_call(
        functools.partial(_moe_kernel, alpha, emit_bf16),
        grid=grid, in_specs=in_specs, out_specs=out_specs, out_shape=out_shape,
        scratch_shapes=[pltpu.VMEM((tile, d), BF16), pltpu.VMEM((tile, d), F32)],
        compiler_params=pltpu.CompilerParams(
            dimension_semantics=("arbitrary", "arbitrary"), vmem_limit_bytes=VMEM_LIMIT),
        name="moe",
    )(x1, comb, p['w_gate'], p['w_up'], p['w_down'],
      p['ws_gate'], p['ws_up'], p['ws_down'], p['ln2_g'], p['ln2_b'])
    if emit_bf16:
        return res[0], tuple(res[1:])
    return res[0], tuple(p[k].astype(BF16) for k in ('w_gate', 'w_up', 'w_down'))


SUB = 256
RUN_ALIGN = 16
FFN_CHUNK = 192
FFN_TIGHT = 176
SEL_CHUNK = 512
PACK_STATIC = 4
PACK_FULL = 2
DISPATCH_SUBTILES = 4
EXPERTS_PER_STEP = 4


def _expert_onehot(r, off, cnt):
    hit = (r >= off) & (r < off + cnt)
    return hit, jnp.where(hit, off + 1.0, 0.0)


def _moe_sparse_kernel(alpha, batch, n_sub, cap, eg,
                       off_s, pc_s,
                       x1_ref, comb_ref, combt_ref, offr_ref, cntr_ref, offc_ref, cntc_ref,
                       wg_ref, wu_ref, wd_ref, wsg_ref, wsu_ref, wsd_ref, g_ref, b_ref,
                       out_ref, gbuf, xe, zbuf, *xq):
    i = pl.program_id(0)
    g = pl.program_id(1)
    n_exp = comb_ref.shape[1]
    tri_r = lax.broadcasted_iota(jnp.int32, (SUB, SUB), 0)
    tri_c = lax.broadcasted_iota(jnp.int32, (SUB, SUB), 1)

    def run(j, e):
        idx = (i * n_sub + j) * n_exp + e
        return off_s[idx], pc_s[idx]

    def used_rows(j):
        o, p = run(j, n_exp - 1)
        return o + p

    def rows_at(start, chunks):
        return pl.ds(pl.multiple_of(start, RUN_ALIGN), chunks * RUN_ALIGN)

    rows16 = functools.partial(rows_at, chunks=1)

    @pl.when((i == 0) & (g == 0))
    def _zero():
        xe[...] = jnp.zeros(xe.shape, xe.dtype)
        for buf in xq:
            buf[...] = jnp.zeros(buf.shape, buf.dtype)

    @pl.when(g == 0)
    def _dispatch():
        before = jnp.where(tri_r < tri_c, 1.0, 0.0).astype(BF16)
        for j in range(n_sub):
            tok = slice(j * SUB, (j + 1) * SUB)
            xj = x1_ref[tok, :].astype(BF16)
            sel = combt_ref[:, tok] != 0.0
            pos = jnp.dot(jnp.where(sel, 1.0, 0.0).astype(BF16), before,
                          preferred_element_type=F32)
            posm = jnp.where(sel, pos, -1.0).astype(BF16)
            off = offr_ref[j]
            cnt = cntr_ref[j]
            used = used_rows(j)

            def sort_rows(rc, j=j, xj=xj, posm=posm, off=off, cnt=cnt):
                r = (lax.broadcasted_iota(jnp.int32, (SEL_CHUNK, 1), 0)
                     + rc * SEL_CHUNK).astype(F32)
                hit, start1 = _expert_onehot(r, off, cnt)
                s = jnp.sum(start1, axis=1, keepdims=True)
                q = jnp.where(s > 0.0, r - (s - 1.0), -2.0)
                rank = jnp.dot(jnp.where(hit, 1.0, 0.0).astype(BF16), posm,
                               preferred_element_type=F32)
                pick = jnp.where(rank == q, 1.0, 0.0).astype(BF16)
                gbuf[j, rc * SEL_CHUNK:(rc + 1) * SEL_CHUNK, :] = jnp.dot(
                    pick, xj, preferred_element_type=F32).astype(BF16)

            last = cap // SEL_CHUNK - 1
            for rc in range(last):
                sort_rows(rc)
            pl.when(last * SEL_CHUNK < used)(functools.partial(sort_rows, last))

            @pl.when(last * SEL_CHUNK >= used)
            def _blank(j=j):
                gbuf[j, last * SEL_CHUNK:, :] = jnp.zeros((SEL_CHUNK, gbuf.shape[2]), BF16)

    def ffn_rows(rs, ee):
        xc = xe[rs, :]
        h = (jax.nn.silu(jnp.dot(xc, wg_ref[ee], preferred_element_type=F32))
             * jnp.dot(xc, wu_ref[ee], preferred_element_type=F32))
        xe[rs, :] = jnp.dot(h.astype(BF16), wd_ref[ee], preferred_element_type=F32).astype(BF16)

    def pack(e, buf, static):
        n = jnp.int32(0)
        for j in range(n_sub):
            o, p = run(j, e)
            if static:
                buf[rows_at(n, PACK_STATIC), :] = gbuf[j, rows_at(o, PACK_STATIC), :]
            else:
                def chunk(c, carry, j=j, o=o, n=n):
                    buf[rows16(n + c * RUN_ALIGN), :] = gbuf[j, rows16(o + c * RUN_ALIGN), :]
                    return carry

                lax.fori_loop(0, lax.div(p, RUN_ALIGN), chunk, 0)
            n = n + p
        return n

    def unpack(e, buf, static, keep_next=True):
        n = jnp.int32(0)
        for j in range(n_sub):
            o, p = run(j, e)
            if static and not keep_next:
                gbuf[j, rows_at(o, PACK_STATIC), :] = buf[rows_at(n, PACK_STATIC), :]
            elif static:
                full = PACK_FULL * RUN_ALIGN
                gbuf[j, rows_at(o, PACK_FULL), :] = buf[rows_at(n, PACK_FULL), :]
                rest = PACK_STATIC - PACK_FULL
                dst = rows_at(o + full, rest)
                row = lax.broadcasted_iota(jnp.int32, (rest * RUN_ALIGN, gbuf.shape[2]), 0)
                gbuf[j, dst, :] = jnp.where(row < p - full, buf[rows_at(n + full, rest), :],
                                            gbuf[j, dst, :])
            else:
                def chunk(c, carry, j=j, o=o, n=n):
                    gbuf[j, rows16(o + c * RUN_ALIGN), :] = buf[rows16(n + c * RUN_ALIGN), :]
                    return carry

                lax.fori_loop(0, lax.div(p, RUN_ALIGN), chunk, 0)
            n = n + p

    usual = jnp.bool_(True)
    n_max = jnp.int32(0)
    for ee in range(eg):
        n = jnp.int32(0)
        for j in range(n_sub):
            p = run(j, g * eg + ee)[1]
            usual = usual & (p >= PACK_FULL * RUN_ALIGN) & (p <= PACK_STATIC * RUN_ALIGN)
            n = n + p
        n_max = jnp.maximum(n_max, n)
    tight = usual & (n_max <= FFN_TIGHT)
    usual = usual & (n_max <= FFN_CHUNK)

    def together(m):
        for ee in range(eg):
            pack(g * eg + ee, xq[ee], True)
        xcs = [xq[ee][0:m, :] for ee in range(eg)]
        hs = [(jax.nn.silu(jnp.dot(xc, wg_ref[ee], preferred_element_type=F32))
               * jnp.dot(xc, wu_ref[ee], preferred_element_type=F32)).astype(BF16)
              for ee, xc in enumerate(xcs)]
        outs = [jnp.dot(h, wd_ref[ee], preferred_element_type=F32).astype(BF16)
                for ee, h in enumerate(hs)]
        for ee, o in enumerate(outs):
            xq[ee][0:m, :] = o
        for ee in range(eg):
            unpack(g * eg + ee, xq[ee], True, keep_next=ee == eg - 1)

    pl.when(tight)(functools.partial(together, FFN_TIGHT))
    pl.when(usual & jnp.logical_not(tight))(functools.partial(together, FFN_CHUNK))

    @pl.when(jnp.logical_not(usual))
    def _one_by_one():
        for ee in range(eg):
            e = g * eg + ee
            n = pack(e, xe, False)

            def ffn(k, carry, ee=ee):
                ffn_rows(pl.ds(pl.multiple_of(k * FFN_CHUNK, RUN_ALIGN), FFN_CHUNK), ee)
                return carry

            lax.fori_loop(0, lax.div(n + (FFN_CHUNK - 1), FFN_CHUNK), ffn, 0)
            unpack(e, xe, False)

    @pl.when(g == pl.num_programs(1) - 1)
    def _combine():
        earlier = jnp.where(tri_c < tri_r, 1.0, 0.0).astype(BF16)
        for j in range(n_sub):
            tok = slice(j * SUB, (j + 1) * SUB)
            x = x1_ref[tok, :]
            xb = x.astype(BF16)
            hs = (jax.nn.silu(jnp.dot(xb, wsg_ref[...], preferred_element_type=F32))
                  * jnp.dot(xb, wsu_ref[...], preferred_element_type=F32))
            y = jnp.dot(hs.astype(BF16), wsd_ref[...], preferred_element_type=F32)
            comb = comb_ref[tok, :]
            sel = comb != 0.0
            pos = jnp.dot(earlier, jnp.where(sel, 1.0, 0.0).astype(BF16),
                          preferred_element_type=F32)
            posm = jnp.where(sel, pos, -1.0).astype(BF16)
            c_bf = comb.astype(BF16)
            off = offc_ref[j]
            cnt = cntc_ref[j]
            used = used_rows(j)

            def gather_rows(rc, j=j, posm=posm, c_bf=c_bf, off=off, cnt=cnt):
                r = (lax.broadcasted_iota(jnp.int32, (1, SEL_CHUNK), 1)
                     + rc * SEL_CHUNK).astype(F32)
                hit, start1 = _expert_onehot(r, off, cnt)
                s = jnp.sum(start1, axis=0, keepdims=True)
                q = jnp.where(s > 0.0, r - (s - 1.0), -2.0)
                hb = jnp.where(hit, 1.0, 0.0).astype(BF16)
                pick = jnp.dot(posm, hb, preferred_element_type=F32) == q
                weight = jnp.dot(c_bf, hb, preferred_element_type=F32)
                w = jnp.where(pick, weight, 0.0).astype(BF16)
                rows = gbuf[j, rc * SEL_CHUNK:(rc + 1) * SEL_CHUNK, :]
                return jnp.dot(w, rows, preferred_element_type=F32)

            last = cap // SEL_CHUNK - 1
            for rc in range(last):
                y = y + gather_rows(rc)
            n_lt = x.shape[1] // LANES
            lane_tiles = lambda v: [v[:, c * LANES:(c + 1) * LANES] for c in range(n_lt)]

            def put(v):
                for c, t in enumerate(lane_tiles(v)):
                    zbuf[c] = t

            get = lambda: jnp.concatenate([zbuf[c] for c in range(n_lt)], axis=1)
            put(alpha * x + y)

            @pl.when(last * SEL_CHUNK < used)
            def _tail(gather_rows=gather_rows, put=put, get=get):
                put(get() + gather_rows(last))

            put(_layer_norm(get(), g_ref[...], b_ref[...]))
            t_sub = SUB // batch
            for b in range(batch):
                for c in range(n_lt):
                    out_ref[b, j * t_sub:(j + 1) * t_sub, c * LANES:(c + 1) * LANES] = (
                        zbuf[c, pl.ds(b, t_sub, stride=batch), :])


def _moe_sparse(x1, comb, combt, cnt, p, w_bf16, *, alpha, batch, n_sub, eg=EXPERTS_PER_STEP):
    total, d = x1.shape
    w_gate, w_up, w_down = w_bf16
    n_exp, _, f = w_gate.shape
    fs = p['ws_gate'].shape[1]
    tile = n_sub * SUB
    n_tiles = total // tile
    assert SUB % batch == 0
    cap = -(-(SUB * TOP_K + n_exp * (RUN_ALIGN - 1)) // SEL_CHUNK) * SEL_CHUNK
    xe_rows = -(-(tile + n_sub * (RUN_ALIGN - 1)) // FFN_CHUNK) * FFN_CHUNK

    cnt = cnt[:, 0, :]
    pc = jnp.ceil(cnt / RUN_ALIGN) * RUN_ALIGN
    off = jnp.cumsum(pc, axis=1) - pc
    off_s = off.astype(jnp.int32).reshape(-1)
    pc_s = pc.astype(jnp.int32).reshape(-1)
    offr, cntr = off[:, None, :], cnt[:, None, :]
    offc, cntc = off[:, :, None], cnt[:, :, None]

    row_spec = pl.BlockSpec((n_sub, 1, n_exp), lambda i, g, *_: (i, 0, 0))
    col_spec = pl.BlockSpec((n_sub, n_exp, 1), lambda i, g, *_: (i, 0, 0))
    const = lambda shape: pl.BlockSpec(shape, lambda i, g, *_: (0,) * len(shape),
                                       pipeline_mode=pl.Buffered(1))
    grid_spec = pltpu.PrefetchScalarGridSpec(
        num_scalar_prefetch=2,
        grid=(n_tiles, n_exp // eg),
        in_specs=[pl.BlockSpec((tile, d), lambda i, g, *_: (i, 0), pipeline_mode=pl.Buffered(1)),
                  pl.BlockSpec((tile, n_exp), lambda i, g, *_: (i, 0)),
                  pl.BlockSpec((n_exp, tile), lambda i, g, *_: (0, i)),
                  row_spec, row_spec, col_spec, col_spec,
                  pl.BlockSpec((eg, d, f), lambda i, g, *_: (g, 0, 0)),
                  pl.BlockSpec((eg, d, f), lambda i, g, *_: (g, 0, 0)),
                  pl.BlockSpec((eg, f, d), lambda i, g, *_: (g, 0, 0)),
                  const((d, fs)), const((d, fs)), const((fs, d)),
                  const((1, d)), const((1, d))],
        out_specs=pl.BlockSpec((batch, tile // batch, d), lambda i, g, *_: (0, i, 0),
                               pipeline_mode=pl.Buffered(1)),
        scratch_shapes=[pltpu.VMEM((n_sub, cap, d), BF16), pltpu.VMEM((xe_rows, d), BF16),
                        pltpu.VMEM((d // LANES, SUB, LANES), F32)]
        + [pltpu.VMEM((FFN_CHUNK + PACK_STATIC * RUN_ALIGN, d), BF16)] * eg)
    return pl.pallas_call(
        functools.partial(_moe_sparse_kernel, alpha, batch, n_sub, cap, eg),
        grid_spec=grid_spec,
        out_shape=jax.ShapeDtypeStruct((batch, total // batch, d), F32),
        compiler_params=pltpu.CompilerParams(
            dimension_semantics=("arbitrary", "arbitrary"), vmem_limit_bytes=VMEM_LIMIT),
        name="moe_sparse",
    )(off_s, pc_s, x1, comb, combt, offr, cntr, offc, cntc, w_gate, w_up, w_down,
      p['ws_gate'].astype(BF16), p['ws_up'].astype(BF16), p['ws_down'].astype(BF16),
      p['ln2_g'], p['ln2_b'])


def _layer_params(l, w):
    g, n = w['a_re'].shape[1:]
    lr, li, bbr, bbi = _ssm_prep(w['a_re'][l], w['a_im'][l], w['log_dt'][l],
                                 w['ssm_b_re'][l], w['ssm_b_im'][l])
    bbd = jnp.concatenate([_block_diag(bbr, SSM_BLOCKS), _block_diag(bbi, SSM_BLOCKS)],
                          axis=-1).astype(BF16)
    c_re = w['ssm_c_re'][l].transpose(0, 2, 1)
    c_im = w['ssm_c_im'][l].transpose(0, 2, 1)
    cbd = jnp.concatenate([_block_diag(c_re, SSM_BLOCKS), _block_diag(-c_im, SSM_BLOCKS)],
                          axis=1).astype(BF16)
    row = lambda v: v.reshape(1, -1)
    return {
        'w_in': w['w_in'][l].astype(BF16), 'b_in': row(w['b_in'][l]),
        'conv_w': w['conv_w'][l], 'w_conv_out': w['w_conv_out'][l].astype(BF16),
        'lam_r': lr.reshape(1, g * n), 'lam_i': li.reshape(1, g * n),
        'bbd': bbd, 'cbd': cbd, 'ssm_d': row(w['ssm_d'][l]),
        'w_glu': w['w_glu'][l].astype(BF16), 'b_glu': row(w['b_glu'][l]),
        'w_ssm_out': w['w_ssm_out'][l].astype(BF16), 'w_o': w['w_o'][l].astype(BF16),
        'ln1_g': row(w['ln1_g'][l]), 'ln1_b': row(w['ln1_b'][l]),
        'w_router_t': w['w_router'][l].T, 'router_bias': w['router_bias'][l].reshape(-1, 1),
        'w_gate': w['w_gate'][l], 'w_up': w['w_up'][l], 'w_down': w['w_down'][l],
        'ws_gate': w['ws_gate'][l], 'ws_up': w['ws_up'][l], 'ws_down': w['ws_down'][l],
        'ln2_g': row(w['ln2_g'][l]), 'ln2_b': row(w['ln2_b'][l]),
    }


def _pick_steps(batch, seq, max_rows):
    steps = max(1, min(seq, max_rows // batch))
    while seq % steps:
        steps -= 1
    return steps


def _pick_tile(total, max_tile):
    tile = min(total, max_tile)
    while total % tile or tile % 16:
        tile -= 16
    return tile


MIXER_ROWS = 512
MOE_TILE = 1024


def _trunk_layer(x, conv_l, re_l, im_l, p, alpha, w_bf16):
    bsz, seq, d = x.shape
    assert bsz % SUBLANES == 0
    steps = _pick_steps(bsz, seq, MIXER_ROWS)
    kw = conv_l.shape[1]
    assert kw == 2
    cprev = conv_l.astype(F32).transpose(1, 0, 2).reshape(kw * bsz, d)
    h0r = re_l.astype(F32).reshape(bsz, -1)
    h0i = im_l.astype(F32).reshape(bsz, -1)
    if ((bsz * steps) % SUB == 0 and (bsz * seq) % (DISPATCH_SUBTILES * SUB) == 0
            and SUB % bsz == 0):
        if w_bf16 is None:
            w_bf16 = tuple(p[k].astype(BF16) for k in ('w_gate', 'w_up', 'w_down'))
        x1, comb, cnew, hr, hi, combt, cnt = _mixer(
            x, cprev, h0r, h0i, p, alpha=alpha, batch=bsz, steps=steps, sub=SUB)
        y = _moe_sparse(x1, comb, combt, cnt, p, w_bf16, alpha=alpha, batch=bsz,
                        n_sub=DISPATCH_SUBTILES)
    else:
        rows = x.transpose(1, 0, 2).reshape(seq * bsz, d)
        x1, comb, cnew, hr, hi = _mixer(rows, cprev, h0r, h0i, p,
                                        alpha=alpha, batch=bsz, steps=steps)
        out, w_bf16 = _moe(x1, comb, p, alpha=alpha, tile=_pick_tile(bsz * seq, MOE_TILE))
        y = out.reshape(seq, bsz, d).transpose(1, 0, 2)
    states = (cnew.reshape(kw, bsz, d).transpose(1, 0, 2),
              hr.reshape(re_l.shape), hi.reshape(im_l.shape))
    return y, states, w_bf16


def kernel(x_prompt, x_sample, state_conv, state_ssm_re, state_ssm_im,
           w_in, b_in, conv_w, w_conv_out, a_re, a_im, log_dt,
           ssm_b_re, ssm_b_im, ssm_c_re, ssm_c_im, ssm_d, w_glu, b_glu, w_ssm_out, w_o,
           ln1_g, ln1_b, w_router, router_bias, w_gate, w_up, w_down,
           ws_gate, ws_up, ws_down, ln2_g, ln2_b):
    w = dict(w_in=w_in, b_in=b_in, conv_w=conv_w, w_conv_out=w_conv_out,
             a_re=a_re, a_im=a_im, log_dt=log_dt,
             ssm_b_re=ssm_b_re, ssm_b_im=ssm_b_im, ssm_c_re=ssm_c_re, ssm_c_im=ssm_c_im,
             ssm_d=ssm_d, w_glu=w_glu, b_glu=b_glu, w_ssm_out=w_ssm_out, w_o=w_o,
             ln1_g=ln1_g, ln1_b=ln1_b, w_router=w_router, router_bias=router_bias,
             w_gate=w_gate, w_up=w_up, w_down=w_down,
             ws_gate=ws_gate, ws_up=ws_up, ws_down=ws_down, ln2_g=ln2_g, ln2_b=ln2_b)
    depth = w_in.shape[0]
    alpha = (2.0 * depth) ** 0.25
    bsz = x_prompt.shape[0]
    zero_conv = jnp.zeros((bsz,) + state_conv.shape[2:], x_prompt.dtype)
    zero_ssm = jnp.zeros((bsz,) + state_ssm_re.shape[2:], F32)
    y_p, y_s = x_prompt, x_sample
    st_p, st_s = [], []
    for l in range(depth):
        p = _layer_params(l, w)
        y_s, st, w_bf16 = _trunk_layer(y_s, state_conv[l], state_ssm_re[l], state_ssm_im[l],
                                       p, alpha, None)
        st_s.append(st)
        y_p, st, _ = _trunk_layer(y_p, zero_conv, zero_ssm, zero_ssm, p, alpha, w_bf16)
        st_p.append(st)
    conv_p, re_p, im_p = (jnp.stack(v) for v in zip(*st_p))
    conv_s, re_s, im_s = (jnp.stack(v) for v in zip(*st_s))
    return (y_p, y_s, conv_p, re_p, im_p, conv_s, re_s, im_s)
```
